```python
import math
import jax, jax.numpy as jnp
from jax import lax
import numpy as np

D_MODEL = 1024
BATCH = 4
SEQ = 8192
DEPTH = 2

D_MIX = D_MODEL
SSM_WIDTH = D_MIX // 2
POOL_WIDTH = D_MIX // 4
CONV_WIDTH = D_MIX - SSM_WIDTH - POOL_WIDTH
SSM_GROUP = 16
SSM_GROUPS = SSM_WIDTH // SSM_GROUP
SSM_STATE = 64
STEP_MIN = 1e-3
STEP_MAX = 1e-1
LAMBDA_RE_MAX = -1e-4
POOL_WINDOWS = (2, 4, 8, 16)
POOL_GROUP = POOL_WIDTH // len(POOL_WINDOWS)
CONV_K = 3
IN_WIDTH = SSM_WIDTH + POOL_WIDTH + 3 * CONV_WIDTH
N_EXPERTS = 32
TOP_K = 4
D_EXPERT = D_MODEL
SWIGLU_LIMIT = 7.0
SWIGLU_ALPHA = 1.702
EXPERT_BLOCK = 256
NORM_EPS = 1e-5

kernel_name = 'hymba_s5_pool_shortconv_moe'


def rmsnorm(x, g):
    xf = x.astype(jnp.float32)
    r = lax.rsqrt(jnp.mean(xf * xf, axis=-1, keepdims=True) + NORM_EPS)
    return (xf * r).astype(x.dtype) * g


def _complex_linear_combine(e1, e2):
    a1r, a1i, b1r, b1i = e1
    a2r, a2i, b2r, b2i = e2
    ar = a2r * a1r - a2i * a1i
    ai = a2r * a1i + a2i * a1r
    br = a2r * b1r - a2i * b1i + b2r
    bi = a2r * b1i + a2i * b1r + b2i
    return (ar, ai, br, bi)


def s5_mixer(u, lam_re, lam_im, b_re, b_im, c_re, c_im, d_skip, log_step, w_glu, b_glu):
    f32 = jnp.float32
    bsz, L, _ = u.shape
    uf = u.astype(f32)
    ug = uf.reshape(bsz, L, SSM_GROUPS, SSM_GROUP)
    lr = jnp.minimum(lam_re.astype(f32), LAMBDA_RE_MAX)
    li = lam_im.astype(f32)
    step = jnp.exp(log_step.astype(f32))[:, None]
    mag = jnp.exp(lr * step)
    ar = mag * jnp.cos(li * step)
    ai = mag * jnp.sin(li * step)
    nr = ar - 1.0
    den = lr * lr + li * li
    kr = (nr * lr + ai * li) / den
    ki = (ai * lr - nr * li) / den
    bre = b_re.astype(f32)
    bim = b_im.astype(f32)
    bbar_r = kr[..., None] * bre - ki[..., None] * bim
    bbar_i = kr[..., None] * bim + ki[..., None] * bre
    bu_r = jnp.einsum('blgh,gph->blgp', ug, bbar_r)
    bu_i = jnp.einsum('blgh,gph->blgp', ug, bbar_i)
    a_r = jnp.broadcast_to(ar[None, None], (1, L, SSM_GROUPS, SSM_STATE))
    a_i = jnp.broadcast_to(ai[None, None], (1, L, SSM_GROUPS, SSM_STATE))
    _, _, h_r, h_i = lax.associative_scan(_complex_linear_combine, (a_r, a_i, bu_r, bu_i), axis=1)
    y = (jnp.einsum('blgp,ghp->blgh', h_r, c_re.astype(f32))
         - jnp.einsum('blgp,ghp->blgh', h_i, c_im.astype(f32)))
    y = y.reshape(bsz, L, SSM_WIDTH) + d_skip.astype(f32) * uf
    z = jax.nn.gelu(y)
    out = z * jax.nn.sigmoid(z @ w_glu.astype(f32) + b_glu.astype(f32))
    return out.astype(u.dtype)


def causal_window_mean(z, w):
    L = z.shape[1]
    cs = jnp.cumsum(z, axis=1)
    cs_pad = jnp.pad(cs, ((0, 0), (w, 0), (0, 0)))
    total = cs_pad[:, w:] - cs_pad[:, :L]
    count = jnp.minimum(jnp.arange(1, L + 1), w).astype(jnp.float32)
    return total / count[None, :, None]


def pool_mixer(p, pool_w, pool_scale):
    f32 = jnp.float32
    bsz, L, _ = p.shape
    pf = p.astype(f32).reshape(bsz, L, len(POOL_WINDOWS), POOL_GROUP)
    pooled = jnp.stack([causal_window_mean(pf[:, :, i], w) - pf[:, :, i]
                        for i, w in enumerate(POOL_WINDOWS)], axis=2)
    mixed = jnp.einsum('blgc,gcd->blgd', pooled, pool_w.astype(f32)).reshape(bsz, L, POOL_WIDTH)
    return (mixed * pool_scale.astype(f32)).astype(p.dtype)


def short_conv_mixer(bg, cg, hv, conv_w):
    L = hv.shape[1]
    z = cg * hv
    zp = jnp.pad(z, ((0, 0), (CONV_K - 1, 0), (0, 0)))
    y = conv_w[0] * zp[:, 0:L]
    for k in range(1, CONV_K):
        y = y + conv_w[k] * zp[:, k:k + L]
    return bg * y


def moe_ffn(h, router_w, router_b, w_gu, b_gu, w_dn, b_dn):
    bsz, L, D = h.shape
    N = bsz * L
    NK = N * TOP_K
    x = h.reshape(N, D)
    logits = (x @ router_w + router_b).astype(jnp.float32)
    top_vals, top_idx = lax.top_k(logits, TOP_K)
    gates = jax.nn.softmax(top_vals, axis=-1)
    flat_e = top_idx.reshape(NK).astype(jnp.int32)
    flat_tok = jnp.arange(NK, dtype=jnp.int32) // TOP_K
    flat_gate = gates.reshape(NK)
    order = jnp.argsort(flat_e)
    sorted_e = flat_e[order]
    counts = jnp.bincount(flat_e, length=N_EXPERTS).astype(jnp.int32)
    padded = (counts + EXPERT_BLOCK - 1) // EXPERT_BLOCK * EXPERT_BLOCK
    padded_end = jnp.cumsum(padded)
    padded_start = padded_end - padded
    start = jnp.cumsum(counts) - counts
    dest = padded_start[sorted_e] + jnp.arange(NK, dtype=jnp.int32) - start[sorted_e]
    n_blocks = -(-(NK + N_EXPERTS * (EXPERT_BLOCK - 1)) // EXPERT_BLOCK)
    R = n_blocks * EXPERT_BLOCK
    row_tok = jnp.full((R,), N, jnp.int32).at[dest].set(flat_tok[order])
    row_gate = jnp.zeros((R,), jnp.float32).at[dest].set(flat_gate[order])
    block_start = jnp.arange(n_blocks, dtype=jnp.int32) * EXPERT_BLOCK
    block_e = jnp.minimum(jnp.searchsorted(padded_end, block_start, side='right'), N_EXPERTS - 1)
    x_pad = jnp.concatenate([x, jnp.zeros((1, D), x.dtype)], axis=0)
    xs = x_pad[row_tok].reshape(n_blocks, EXPERT_BLOCK, D)

    def expert_block(args):
        xb, e = args
        gu = xb @ w_gu[e] + b_gu[e]
        gate = jnp.minimum(gu[:, :D_EXPERT], SWIGLU_LIMIT)
        up = jnp.clip(gu[:, D_EXPERT:], -SWIGLU_LIMIT, SWIGLU_LIMIT)
        act = gate * jax.nn.sigmoid(SWIGLU_ALPHA * gate) * (up + 1.0)
        return act @ w_dn[e] + b_dn[e]

    ys = lax.map(expert_block, (xs, block_e)).reshape(R, D)
    ys = ys * row_gate[:, None].astype(ys.dtype)
    out = jax.ops.segment_sum(ys, row_tok, num_segments=N + 1)[:N]
    return out.reshape(bsz, L, D)


def setup_inputs(seed: int = 0) -> dict:
    key = jax.random.key(seed)
    ks = jax.random.split(key, 28)
    f32 = jnp.float32

    def nrm(k, shape, scale):
        return scale * jax.random.normal(k, shape, f32)

    G, P, H = SSM_GROUPS, SSM_STATE, SSM_GROUP
    return {
        'x': nrm(ks[0], (BATCH, SEQ, D_MODEL), 1.0),
        'norm_mix': 1.0 + nrm(ks[1], (DEPTH, D_MODEL), 0.02),
        'w_in': nrm(ks[2], (DEPTH, D_MODEL, IN_WIDTH), D_MODEL ** -0.5),
        'ssm_lam_re': -0.5 + nrm(ks[3], (DEPTH, G, P), 0.01),
        'ssm_lam_im': math.pi * jnp.arange(P, dtype=f32) + nrm(ks[4], (DEPTH, G, P), 0.01),
        'ssm_b_re': nrm(ks[5], (DEPTH, G, P, H), (2 * H) ** -0.5),
        'ssm_b_im': nrm(ks[6], (DEPTH, G, P, H), (2 * H) ** -0.5),
        'ssm_c_re': nrm(ks[7], (DEPTH, G, H, P), (2 * P) ** -0.5),
        'ssm_c_im': nrm(ks[8], (DEPTH, G, H, P), (2 * P) ** -0.5),
        'ssm_d': nrm(ks[9], (DEPTH, SSM_WIDTH), 1.0),
        'ssm_log_step': jax.random.uniform(ks[10], (DEPTH, G), f32, math.log(STEP_MIN), math.log(STEP_MAX)),
        'ssm_w_glu': nrm(ks[11], (DEPTH, SSM_WIDTH, SSM_WIDTH), SSM_WIDTH ** -0.5),
        'ssm_b_glu': nrm(ks[12], (DEPTH, SSM_WIDTH), 0.02),
        'pool_w': nrm(ks[13], (DEPTH, len(POOL_WINDOWS), POOL_GROUP, POOL_GROUP), POOL_GROUP ** -0.5),
        'pool_scale': 1.0 + nrm(ks[14], (DEPTH, POOL_WIDTH), 0.1),
        'conv_w': nrm(ks[15], (DEPTH, CONV_K, CONV_WIDTH), CONV_K ** -0.5),
        'branch_norm': 1.0 + nrm(ks[16], (DEPTH, D_MIX), 0.02),
        'w_out': nrm(ks[17], (DEPTH, D_MIX, D_MODEL), D_MIX ** -0.5),
        'norm_ffn': 1.0 + nrm(ks[18], (DEPTH, D_MODEL), 0.02),
        'router_w': nrm(ks[19], (DEPTH, D_MODEL, N_EXPERTS), D_MODEL ** -0.5),
        'router_b': nrm(ks[20], (DEPTH, N_EXPERTS), 0.01),
        'w_gate_up': nrm(ks[21], (DEPTH, N_EXPERTS, D_MODEL, 2 * D_EXPERT), D_MODEL ** -0.5),
        'b_gate_up': nrm(ks[22], (DEPTH, N_EXPERTS, 2 * D_EXPERT), 0.02),
        'w_down': nrm(ks[23], (DEPTH, N_EXPERTS, D_EXPERT, D_MODEL), D_EXPERT ** -0.5),
        'b_down': nrm(ks[24], (DEPTH, N_EXPERTS, D_MODEL), 0.02),
        'final_norm': 1.0 + nrm(ks[25], (D_MODEL,), 0.02),
    }


def reference(x, norm_mix, w_in, ssm_lam_re, ssm_lam_im, ssm_b_re, ssm_b_im, ssm_c_re, ssm_c_im,
              ssm_d, ssm_log_step, ssm_w_glu, ssm_b_glu, pool_w, pool_scale, conv_w, branch_norm,
              w_out, norm_ffn, router_w, router_b, w_gate_up, b_gate_up, w_down, b_down, final_norm):
    s0 = SSM_WIDTH
    s1 = s0 + POOL_WIDTH
    splits = [s0, s1, s1 + CONV_WIDTH, s1 + 2 * CONV_WIDTH]
    for l in range(DEPTH):
        hn = rmsnorm(x, norm_mix[l])
        proj = hn @ w_in[l]
        u, p, bg, cg, hv = jnp.split(proj, splits, axis=-1)
        y_ssm = s5_mixer(u, ssm_lam_re[l], ssm_lam_im[l], ssm_b_re[l], ssm_b_im[l],
                         ssm_c_re[l], ssm_c_im[l], ssm_d[l], ssm_log_step[l],
                         ssm_w_glu[l], ssm_b_glu[l])
        y_pool = pool_mixer(p, pool_w[l], pool_scale[l])
        y_conv = short_conv_mixer(bg, cg, hv, conv_w[l])
        g = branch_norm[l]
        mixed = jnp.concatenate([rmsnorm(y_ssm, g[:s0]),
                                 rmsnorm(y_pool, g[s0:s1]),
                                 rmsnorm(y_conv, g[s1:])], axis=-1)
        x = x + mixed @ w_out[l]
        x = x + moe_ffn(rmsnorm(x, norm_ffn[l]), router_w[l], router_b[l],
                        w_gate_up[l], b_gate_up[l], w_down[l], b_down[l])
    return rmsnorm(x, final_norm)
```

```python
import functools
import math

import jax
import jax.numpy as jnp
from jax import lax
from jax.experimental import pallas as pl
from jax.experimental.pallas import tpu as pltpu
from jax.experimental.pallas import tpu_sc as plsc

F32 = jnp.float32
BF16 = jnp.bfloat16
I32 = jnp.int32

D_MODEL = 1024
SSM_WIDTH = 512
POOL_WIDTH = 256
CONV_WIDTH = 256
SSM_GROUP = 16
SSM_GROUPS = 32
SSM_STATE = 64
LAMBDA_RE_MAX = -1e-4
POOL_WINDOWS = (2, 4, 8, 16)
POOL_GROUP = 64
CONV_K = 3
IN_WIDTH = SSM_WIDTH + POOL_WIDTH + 3 * CONV_WIDTH
N_EXPERTS = 32
TOP_K = 4
D_EXPERT = 1024
SWIGLU_LIMIT = 7.0
SWIGLU_ALPHA = 1.702
EXPERT_BLOCK = 256
NORM_EPS = 1e-5

LANES = 128
SUBLANES = 8
TILE_T = 512
SCAN_LEVELS = 3
HALF_STATE = 1024
STATE_COLS = HALF_STATE // LANES
PACKED = D_MODEL // 2
ROUTE_T = 2048
SC_WINDOW = 128
SC_ROW = PACKED // 2
HI_MASK = -65536
VMEM_LIMIT = 56 * 1024 * 1024


def _bf16_round(v):
    return v.astype(BF16).astype(F32)


def _pack_rows(v):
    lo = lax.shift_right_logical(lax.bitcast_convert_type(_bf16_round(v[:, :PACKED]), I32), 16)
    hi = lax.bitcast_convert_type(_bf16_round(v[:, PACKED:]), I32) & HI_MASK
    return hi | lo


def _unpack_lo(w):
    return lax.bitcast_convert_type(lax.shift_left(w, 16), F32)


def _unpack_hi(w):
    return lax.bitcast_convert_type(w & HI_MASK, F32)


def _rms(v, g):
    r = lax.rsqrt(jnp.mean(v * v, axis=-1, keepdims=True) + NORM_EPS)
    return (v * r) * g


def _moe_combine(x, y_ref, slab):
    parts = [x[:, i * SC_ROW:(i + 1) * SC_ROW] for i in range(4)]
    for k in range(TOP_K):
        g = lax.bitcast_convert_type(slab[:, TOP_K + k:TOP_K + k + 1], F32)
        for h in range(2):
            w = y_ref[k, h]
            parts[h] = parts[h] + g * _unpack_lo(w)
            parts[2 + h] = parts[2 + h] + g * _unpack_hi(w)
    return jnp.concatenate(parts, axis=1)


def _cmuladd(xr, xi, cr, ci, vr, vi):
    return xr + cr * vr - ci * vi, xi + cr * vi + ci * vr


def _local_scan(xr, xi, cf):
    for j, d in enumerate((1, 2, 4)):
        rr = pltpu.roll(xr, d, axis=0)
        ri = pltpu.roll(xi, d, axis=0)
        xr, xi = _cmuladd(xr, xi, cf[2 * j], cf[2 * j + 1], rr, ri)
    return xr, xi


def _s5_scan(n, first, s_ref, b2_ref, b3_ref, coef_ref):
    groups1 = TILE_T // SUBLANES
    groups2 = groups1 // SUBLANES

    def coefs(level, c, kinds):
        return [coef_ref[n, level, k, :, c * LANES:(c + 1) * LANES] for k in kinds]

    def bcast(ref, row):
        return jnp.broadcast_to(ref[pl.ds(row, 1), :], (SUBLANES, LANES))

    for c in range(STATE_COLS):
        sr, si = s_ref.at[n, c], s_ref.at[n, STATE_COLS + c]
        b2r, b2i = b2_ref.at[n, c], b2_ref.at[n, STATE_COLS + c]
        b3r, b3i = b3_ref.at[n, c], b3_ref.at[n, STATE_COLS + c]
        cf1, cf2, cf3 = coefs(0, c, range(8)), coefs(1, c, range(8)), coefs(2, c, range(8))

        carry_r = jnp.where(first, 0.0, b3r[15:16, :])
        carry_i = jnp.where(first, 0.0, b3i[15:16, :])
        b3r[7:8, :] = carry_r
        b3i[7:8, :] = carry_i
        b2r[7:8, :] = carry_r
        b2i[7:8, :] = carry_i

        def body1(r, _, sr=sr, si=si, cf=cf1):
            rows = pl.ds(pl.multiple_of(r * SUBLANES, SUBLANES), SUBLANES)
            xr, xi = _local_scan(sr[rows, :], si[rows, :], cf)
            sr[rows, :] = xr
            si[rows, :] = xi
            return 0

        lax.fori_loop(0, groups1, body1, 0, unroll=2)

        x2r = sr[pl.ds(SUBLANES - 1, groups1, stride=SUBLANES), :]
        x2i = si[pl.ds(SUBLANES - 1, groups1, stride=SUBLANES), :]
        for g in range(groups2):
            rows = slice(SUBLANES * (g + 1), SUBLANES * (g + 2))
            lr, li = _local_scan(x2r[g * SUBLANES:(g + 1) * SUBLANES], x2i[g * SUBLANES:(g + 1) * SUBLANES], cf2)
            b2r[rows, :] = lr
            b2i[rows, :] = li

        x3r = b2r[pl.ds(2 * SUBLANES - 1, groups2, stride=SUBLANES), :]
        x3i = b2i[pl.ds(2 * SUBLANES - 1, groups2, stride=SUBLANES), :]
        lr, li = _local_scan(x3r, x3i, cf3)
        hr, hi = _cmuladd(lr, li, cf3[6], cf3[7], bcast(b3r, 7), bcast(b3i, 7))
        b3r[8:16, :] = hr
        b3i[8:16, :] = hi

        for g in range(groups2):
            rows = slice(SUBLANES * (g + 1), SUBLANES * (g + 2))
            hr, hi = _cmuladd(b2r[rows, :], b2i[rows, :], cf2[6], cf2[7], bcast(b3r, 7 + g), bcast(b3i, 7 + g))
            b2r[rows, :] = hr
            b2i[rows, :] = hi

        def body5(r, _, sr=sr, si=si, b2r=b2r, b2i=b2i, cf=cf1):
            rows = pl.ds(pl.multiple_of(r * SUBLANES, SUBLANES), SUBLANES)
            hr, hi = _cmuladd(sr[rows, :], si[rows, :], cf[6], cf[7],
                              bcast(b2r, SUBLANES - 1 + r), bcast(b2i, SUBLANES - 1 + r))
            sr[rows, :] = hr
            si[rows, :] = hi
            return 0

        lax.fori_loop(0, groups1, body5, 0, unroll=2)


def _mixer_kernel(*refs, first_layer):
    if first_layer:
        x_ref = refs[0]
        refs = refs[1:]
    else:
        x_ref, yprev_ref, slabprev_ref = refs[:3]
        refs = refs[3:]
    (gmix_ref, win_ref, wb_ref, coef_ref, wc_ref, dskip_ref, wglu_ref, bglu_ref, poolw_ref, pscale_ref,
     convw_ref, gbr_ref, wout_ref, gffn_ref, rw_ref, rb_ref,
     x1_ref, hp_ref, slab_ref, cnt_ref,
     s_ref, b2_ref, b3_ref, pbuf_ref, zbuf_ref, cntacc_ref) = refs

    b = pl.program_id(0)
    l = pl.program_id(1)
    seq_start = l == 0

    @pl.when(jnp.logical_and(b == 0, l == 0))
    def _():
        cntacc_ref[...] = jnp.zeros_like(cntacc_ref)

    if first_layer:
        x = x_ref[...]
    else:
        x = _moe_combine(x_ref[...], yprev_ref, slabprev_ref[...])

    hn = _rms(x, gmix_ref[...]).astype(BF16)
    proj = jnp.dot(hn, win_ref[...], preferred_element_type=F32)
    u = proj[:, :SSM_WIDTH]
    p = proj[:, SSM_WIDTH:SSM_WIDTH + POOL_WIDTH]
    c0 = SSM_WIDTH + POOL_WIDTH
    bg = proj[:, c0:c0 + CONV_WIDTH]
    cg = proj[:, c0 + CONV_WIDTH:c0 + 2 * CONV_WIDTH]
    hv = proj[:, c0 + 2 * CONV_WIDTH:]

    u_bf = u.astype(BF16)
    halves = []
    for n in range(2):
        bu = jnp.dot(u_bf[:, n * 256:(n + 1) * 256], wb_ref[n], preferred_element_type=F32)
        for c in range(2 * STATE_COLS):
            s_ref[n, c] = bu[:, c * LANES:(c + 1) * LANES]
        _s5_scan(n, seq_start, s_ref, b2_ref, b3_ref, coef_ref)
        h = jnp.concatenate([s_ref[n, c].astype(BF16) for c in range(2 * STATE_COLS)], axis=1)
        halves.append(jnp.dot(h, wc_ref[n], preferred_element_type=F32))
    y = jnp.concatenate(halves, axis=1) + dskip_ref[...] * u
    z = jax.nn.gelu(y)
    glu = jnp.dot(z.astype(BF16), wglu_ref[...], preferred_element_type=F32) + bglu_ref[...]
    y_ssm = z * jax.nn.sigmoid(glu)

    tail = 16
    pbuf_ref[0:tail, :] = jnp.where(seq_start, 0.0, pbuf_ref[TILE_T:TILE_T + tail, :])
    pbuf_ref[tail:, :] = p
    lane_p = lax.broadcasted_iota(I32, (TILE_T, POOL_WIDTH), 1)
    win = jnp.where(lane_p < 64, 2, jnp.where(lane_p < 128, 4, jnp.where(lane_p < 192, 8, 16)))
    total = p
    for d in range(1, 16):
        total = total + jnp.where(win > d, pbuf_ref[pl.ds(tail - d, TILE_T), :], 0.0)
    row_p = lax.broadcasted_iota(I32, (TILE_T, POOL_WIDTH), 0) + l * TILE_T
    count = jnp.minimum(row_p + 1, win).astype(F32)
    pooled = total / count - p
    mixed = jnp.dot(pooled.astype(BF16), poolw_ref[...], preferred_element_type=F32)
    y_pool = mixed * pscale_ref[...]

    zc = cg * hv
    zbuf_ref[0:8, :] = jnp.where(seq_start, 0.0, zbuf_ref[TILE_T:TILE_T + 8, :])
    zbuf_ref[8:, :] = zc
    yc = (convw_ref[0:1, :] * zbuf_ref[pl.ds(6, TILE_T), :]
          + convw_ref[1:2, :] * zbuf_ref[pl.ds(7, TILE_T), :]
          + convw_ref[2:3, :] * zc)
    y_conv = bg * yc

    gbr = gbr_ref[...]
    s1 = SSM_WIDTH + POOL_WIDTH
    mixed_all = jnp.concatenate([_rms(y_ssm, gbr[:, :SSM_WIDTH]).astype(BF16),
                                 _rms(y_pool, gbr[:, SSM_WIDTH:s1]).astype(BF16),
                                 _rms(y_conv, gbr[:, s1:]).astype(BF16)], axis=1)
    x1 = x + jnp.dot(mixed_all, wout_ref[...], preferred_element_type=F32)
    x1_ref[...] = x1

    hn2 = _rms(x1, gffn_ref[...])
    packed = _pack_rows(hn2)
    hp_ref[0] = packed[:, :SC_ROW]
    hp_ref[1] = packed[:, SC_ROW:]
    logits = jnp.dot(hn2.astype(BF16), rw_ref[...], preferred_element_type=F32) + rb_ref[...]
    lane = lax.broadcasted_iota(I32, (TILE_T, LANES), 1)
    neg = jnp.float32(-jnp.inf)
    work = jnp.where(lane < N_EXPERTS, logits, neg)
    vals, idxs = [], []
    for _ in range(TOP_K):
        m = jnp.max(work, axis=1, keepdims=True)
        idx = jnp.min(jnp.where(work == m, lane, LANES), axis=1, keepdims=True)
        vals.append(m)
        idxs.append(idx)
        work = jnp.where(lane == idx, neg, work)
    exps = [jnp.exp(v - vals[0]) for v in vals]
    denom = exps[0] + exps[1] + exps[2] + exps[3]
    gates = [e / denom for e in exps]

    onehot = jnp.zeros((TILE_T, LANES), F32)
    for idx in idxs:
        onehot = jnp.where(lane == idx, 1.0, onehot)
    r_i = lax.broadcasted_iota(I32, (TILE_T, TILE_T), 0)
    c_i = lax.broadcasted_iota(I32, (TILE_T, TILE_T), 1)
    ltri = jnp.where(c_i < r_i, 1.0, 0.0).astype(BF16)
    before = jnp.dot(ltri, onehot.astype(BF16), preferred_element_type=F32) + cntacc_ref[0:1, :]
    slab = jnp.zeros((TILE_T, LANES), I32)
    for k in range(TOP_K):
        rank = jnp.sum(jnp.where(lane == idxs[k], before, 0.0), axis=1, keepdims=True).astype(I32)
        slab = jnp.where(lane == k, idxs[k], slab)
        slab = jnp.where(lane == TOP_K + k, lax.bitcast_convert_type(gates[k], I32), slab)
        slab = jnp.where(lane == 2 * TOP_K + k, rank, slab)
    slab_ref[...] = slab
    newcnt = cntacc_ref[0:1, :] + jnp.sum(onehot, axis=0, keepdims=True)
    cntacc_ref[...] = jnp.broadcast_to(newcnt, cntacc_ref.shape)
    cnt_ref[...] = jnp.broadcast_to(newcnt, cnt_ref.shape).astype(I32)


def _const_spec(shape):
    nd = len(shape)
    return pl.BlockSpec(shape, lambda b, l, nd=nd: (0,) * nd)


def _mixer_call(x, yprev, slabprev, lp, batch, seq):
    first_layer = yprev is None
    n_tok = batch * seq
    steps = seq // TILE_T
    tok_spec = lambda w: pl.BlockSpec((TILE_T, w), lambda b, l: (b * steps + l, 0))
    in_specs = [tok_spec(D_MODEL)]
    args = [x]
    if not first_layer:
        in_specs += [pl.BlockSpec((TOP_K, 2, TILE_T, SC_ROW), lambda b, l: (0, 0, b * steps + l, 0)), tok_spec(LANES)]
        args += [yprev, slabprev]
    weights = [lp['gmix'], lp['win'], lp['wb'], lp['coef'], lp['wc'], lp['dskip'], lp['wglu'], lp['bglu'],
               lp['poolw'], lp['pscale'], lp['convw'], lp['gbr'], lp['wout'], lp['gffn'], lp['rw'], lp['rb']]
    in_specs += [_const_spec(w.shape) for w in weights]
    args += weights
    out_shape = [jax.ShapeDtypeStruct((n_tok, D_MODEL), F32),
                 jax.ShapeDtypeStruct((2, n_tok, SC_ROW), I32),
                 jax.ShapeDtypeStruct((n_tok, LANES), I32),
                 jax.ShapeDtypeStruct((SUBLANES, LANES), I32)]
    out_specs = [tok_spec(D_MODEL), pl.BlockSpec((2, TILE_T, SC_ROW), lambda b, l: (0, b * steps + l, 0)), tok_spec(LANES),
                 pl.BlockSpec((SUBLANES, LANES), lambda b, l: (0, 0))]
    scratch = [pltpu.VMEM((2, 2 * STATE_COLS, TILE_T, LANES), F32),
               pltpu.VMEM((2, 2 * STATE_COLS, SUBLANES + TILE_T // SUBLANES, LANES), F32),
               pltpu.VMEM((2, 2 * STATE_COLS, 2 * SUBLANES, LANES), F32),
               pltpu.VMEM((16 + TILE_T, POOL_WIDTH), F32),
               pltpu.VMEM((8 + TILE_T, CONV_WIDTH), F32),
               pltpu.VMEM((SUBLANES, LANES), F32)]
    return pl.pallas_call(
        functools.partial(_mixer_kernel, first_layer=first_layer),
        grid=(batch, steps),
        in_specs=in_specs, out_specs=out_specs, out_shape=out_shape, scratch_shapes=scratch,
        compiler_params=pltpu.CompilerParams(dimension_semantics=("arbitrary", "arbitrary"),
                                             vmem_limit_bytes=VMEM_LIMIT),
        name="mixer_first" if first_layer else "mixer_next",
    )(*args)


def _route_kernel(slab_ref, cnt_ref, dest_ref, meta_ref, *, n_blocks):
    lane1 = lax.broadcasted_iota(I32, (1, LANES), 1)
    counts = jnp.where(lane1 < N_EXPERTS, cnt_ref[0:1, :], 0)
    padded = ((counts + (EXPERT_BLOCK - 1)) // EXPERT_BLOCK) * EXPERT_BLOCK
    pend = padded
    for sh in (1, 2, 4, 8, 16):
        pend = pend + jnp.where(lane1 >= sh, pltpu.roll(pend, sh, axis=1), 0)
    pstart = pend - padded

    @pl.when(pl.program_id(0) == 0)
    def _():
        rows = meta_ref.shape[0]
        bstart = lax.broadcasted_iota(I32, (rows, LANES), 0) * EXPERT_BLOCK
        lane = lax.broadcasted_iota(I32, (rows, LANES), 1)
        done = jnp.where(jnp.logical_and(lane < N_EXPERTS, pend <= bstart), 1, 0)
        be = jnp.minimum(jnp.sum(done, axis=1, keepdims=True), N_EXPERTS - 1)
        total = jnp.sum(jnp.where(lane1 == N_EXPERTS - 1, pend, 0), axis=1, keepdims=True)
        row = lax.broadcasted_iota(I32, (rows, LANES), 0)
        meta_ref[...] = jnp.where(row == n_blocks, total // EXPERT_BLOCK, jnp.broadcast_to(be, (rows, LANES)))

    slab = slab_ref[...]
    lane = lax.broadcasted_iota(I32, slab.shape, 1)
    out = jnp.zeros(slab.shape, F32)
    for k in range(TOP_K):
        idx = slab[:, k:k + 1]
        rank = slab[:, 2 * TOP_K + k:2 * TOP_K + k + 1]
        base = jnp.sum(jnp.where(lane == idx, pstart, 0), axis=1, keepdims=True)
        for h in range(2):
            out = jnp.where(lane == 2 * k + h, (2 * (base + rank) + h).astype(F32), out)
    dest_ref[...] = out.T[:SUBLANES, :].astype(I32)


def _route_call(slab, counts, n_tok, n_blocks):
    meta_rows = ((n_blocks + 1 + SUBLANES - 1) // SUBLANES) * SUBLANES
    return pl.pallas_call(
        functools.partial(_route_kernel, n_blocks=n_blocks),
        grid=(n_tok // ROUTE_T,),
        in_specs=[pl.BlockSpec((ROUTE_T, LANES), lambda i: (i, 0)),
                  pl.BlockSpec((SUBLANES, LANES), lambda i: (0, 0))],
        out_specs=[pl.BlockSpec((SUBLANES, ROUTE_T), lambda i: (0, i)),
                   pl.BlockSpec((meta_rows, LANES), lambda i: (0, 0))],
        out_shape=[jax.ShapeDtypeStruct((SUBLANES, n_tok), I32),
                   jax.ShapeDtypeStruct((meta_rows, LANES), I32)],
        compiler_params=pltpu.CompilerParams(dimension_semantics=("arbitrary",)),
        name="route",
    )(slab, counts)


def _sc_mesh():
    return plsc.VectorSubcoreMesh(core_axis_name="core", subcore_axis_name="subcore")


def _dispatch(src, idx, n_rows):
    windows = src.shape[0] // SC_WINDOW

    @functools.partial(pl.kernel, out_type=jax.ShapeDtypeStruct((n_rows, SC_ROW), src.dtype),
                       mesh=_sc_mesh(), scratch_types=[])
    def scatter_rows(x_hbm, i_hbm, o_hbm):
        def body(x_vmem, i_vmem):
            pltpu.sync_copy(x_vmem, o_hbm.at[i_vmem.at[0]])

        pltpu.emit_pipeline(
            body,
            grid=(idx.shape[1] // SC_WINDOW,),
            in_specs=[pl.BlockSpec((SC_WINDOW, SC_ROW), lambda i: (lax.rem(i, windows), 0)),
                      pl.BlockSpec((1, SC_WINDOW), lambda i: (0, i))],
            out_specs=[],
            core_axis_name=("core", "subcore"),
            dimension_semantics=(pltpu.PARALLEL,),
        )(x_hbm, i_hbm)

    return scatter_rows(src, idx)


def _combine(src, idx):
    n_idx = idx.shape[1]

    @functools.partial(pl.kernel, out_type=jax.ShapeDtypeStruct((n_idx, SC_ROW), src.dtype),
                       mesh=_sc_mesh(), scratch_types=[])
    def gather_rows(y_hbm, i_hbm, o_hbm):
        def body(i_vmem, o_vmem):
            pltpu.sync_copy(y_hbm.at[i_vmem.at[0]], o_vmem)

        pltpu.emit_pipeline(
            body,
            grid=(n_idx // SC_WINDOW,),
            in_specs=[pl.BlockSpec((1, SC_WINDOW), lambda i: (0, i))],
            out_specs=[pl.BlockSpec((SC_WINDOW, SC_ROW), lambda i: (i, 0))],
            core_axis_name=("core", "subcore"),
            dimension_semantics=(pltpu.PARALLEL,),
        )(i_hbm, o_hbm)

    return gather_rows(src, idx)


def _expert_kernel(be_ref, nv_ref, xs_ref, wgu_ref, bgu_ref, wdn_ref, bdn_ref, ys_ref, wgu_s, wdn_s, act_s):
    blk = pl.program_id(0)
    prev = be_ref[jnp.maximum(blk - 1, 0)]
    changed = jnp.logical_or(blk == 0, be_ref[blk] != prev)
    valid = blk < nv_ref[0]

    @pl.when(jnp.logical_and(valid, changed))
    def _():
        chunk = 128

        def cast_gu(i, _):
            rows = pl.ds(pl.multiple_of(i * chunk, chunk), chunk)
            wgu_s[rows, :] = wgu_ref[rows, :].astype(BF16)
            return 0

        def cast_dn(i, _):
            rows = pl.ds(pl.multiple_of(i * chunk, chunk), chunk)
            wdn_s[rows, :] = wdn_ref[rows, :].astype(BF16)
            return 0

        lax.fori_loop(0, D_MODEL // chunk, cast_gu, 0)
        lax.fori_loop(0, D_EXPERT // chunk, cast_dn, 0)

    @pl.when(valid)
    def _():
        w = xs_ref[...]
        xb = jnp.concatenate([_unpack_lo(w).astype(BF16), _unpack_hi(w).astype(BF16)], axis=1)
        half = D_EXPERT // 2
        for j in range(2):
            cols = slice(j * half, (j + 1) * half)
            ucols = slice(D_EXPERT + j * half, D_EXPERT + (j + 1) * half)
            g = jnp.dot(xb, wgu_s[:, cols], preferred_element_type=F32) + bgu_ref[:, cols]
            up = jnp.dot(xb, wgu_s[:, ucols], preferred_element_type=F32) + bgu_ref[:, ucols]
            g = jnp.minimum(g, SWIGLU_LIMIT)
            up = jnp.clip(up, -SWIGLU_LIMIT, SWIGLU_LIMIT)
            act_s[:, cols] = (g * jax.nn.sigmoid(SWIGLU_ALPHA * g) * (up + 1.0)).astype(BF16)
        y = jnp.dot(act_s[...], wdn_s[...], preferred_element_type=F32) + bdn_ref[...]
        ys_ref[...] = _pack_rows(y)


def _expert_call(xs, block_e, n_valid, w_gu, b_gu, w_dn, b_dn, layer, n_blocks):
    def row_map(i, be, nv):
        return (jnp.minimum(i, jnp.maximum(nv[0] - 1, 0)), 0)

    def w_map(i, be, nv):
        return (layer, be[i], 0, 0)

    grid_spec = pltpu.PrefetchScalarGridSpec(
        num_scalar_prefetch=2,
        grid=(n_blocks,),
        in_specs=[pl.BlockSpec((EXPERT_BLOCK, PACKED), row_map),
                  pl.BlockSpec((None, None, D_MODEL, 2 * D_EXPERT), w_map),
                  pl.BlockSpec((None, None, 1, 2 * D_EXPERT), w_map),
                  pl.BlockSpec((None, None, D_EXPERT, D_MODEL), w_map),
                  pl.BlockSpec((None, None, 1, D_MODEL), w_map)],
        out_specs=pl.BlockSpec((EXPERT_BLOCK, PACKED), row_map),
        scratch_shapes=[pltpu.VMEM((D_MODEL, 2 * D_EXPERT), BF16),
                        pltpu.VMEM((D_EXPERT, D_MODEL), BF16),
                        pltpu.VMEM((EXPERT_BLOCK, D_EXPERT), BF16)])
    return pl.pallas_call(
        _expert_kernel,
        grid_spec=grid_spec,
        out_shape=jax.ShapeDtypeStruct((n_blocks * EXPERT_BLOCK, PACKED), I32),
        compiler_params=pltpu.CompilerParams(dimension_semantics=("arbitrary",),
                                             vmem_limit_bytes=VMEM_LIMIT),
        name="experts",
    )(block_e, n_valid, xs, w_gu, b_gu, w_dn, b_dn)


def _final_kernel(x_ref, y_ref, slab_ref, g_ref, o_ref):
    x = _moe_combine(x_ref[...], y_ref, slab_ref[...])
    o_ref[...] = _rms(x, g_ref[...])


def _final_call(x1, ysg, slab, g, n_tok):
    t = TILE_T
    return pl.pallas_call(
        _final_kernel,
        grid=(n_tok // t,),
        in_specs=[pl.BlockSpec((t, D_MODEL), lambda i: (i, 0)),
                  pl.BlockSpec((TOP_K, 2, t, SC_ROW), lambda i: (0, 0, i, 0)),
                  pl.BlockSpec((t, LANES), lambda i: (i, 0)),
                  pl.BlockSpec((1, D_MODEL), lambda i: (0, 0))],
        out_specs=pl.BlockSpec((t, D_MODEL), lambda i: (i, 0)),
        out_shape=jax.ShapeDtypeStruct((n_tok, D_MODEL), F32),
        compiler_params=pltpu.CompilerParams(dimension_semantics=("arbitrary",)),
        name="final_norm",
    )(x1, ysg, slab, g)


def _s5_tables(lam_re, lam_im, b_re, b_im, c_re, c_im, log_step):
    lr = jnp.minimum(lam_re.astype(F32), LAMBDA_RE_MAX)
    li = lam_im.astype(F32)
    step = jnp.exp(log_step.astype(F32))[:, None]
    mag = jnp.exp(lr * step)
    ar = mag * jnp.cos(li * step)
    ai = mag * jnp.sin(li * step)
    nr = ar - 1.0
    den = lr * lr + li * li
    kr = (nr * lr + ai * li) / den
    ki = (ai * lr - nr * li) / den
    bre = b_re.astype(F32)
    bim = b_im.astype(F32)
    bbar_r = kr[..., None] * bre - ki[..., None] * bim
    bbar_i = kr[..., None] * bim + ki[..., None] * bre
    gl = SSM_GROUPS // 2
    eye = jnp.eye(gl, dtype=F32)

    def wb_half(bb):
        return jnp.einsum('gph,gk->ghkp', bb, eye).reshape(gl * SSM_GROUP, gl * SSM_STATE)

    def wc_half(cc):
        return jnp.einsum('ghp,gk->gpkh', cc, eye).reshape(gl * SSM_STATE, gl * SSM_GROUP)

    wb = jnp.stack([jnp.concatenate([wb_half(bbar_r[n * gl:(n + 1) * gl]), wb_half(bbar_i[n * gl:(n + 1) * gl])], axis=1)
                    for n in range(2)]).astype(BF16)
    cre = c_re.astype(F32)
    cim = c_im.astype(F32)
    wc = jnp.stack([jnp.concatenate([wc_half(cre[n * gl:(n + 1) * gl]), wc_half(-cim[n * gl:(n + 1) * gl])], axis=0)
                    for n in range(2)]).astype(BF16)

    def power(m):
        mg = jnp.exp(m * lr * step)
        return ((mg * jnp.cos(m * li * step)).reshape(2, HALF_STATE),
                (mg * jnp.sin(m * li * step)).reshape(2, HALF_STATE))

    rows = jnp.arange(SUBLANES)[:, None]
    levels = []
    for lev in range(SCAN_LEVELS):
        base = float(SUBLANES ** lev)
        pw = [power(base * j) for j in range(1, SUBLANES + 1)]
        kinds = []
        for d in (1, 2, 4):
            pr, pi = pw[d - 1]
            mask = (rows >= d).astype(F32)[None]
            kinds += [mask * pr[:, None, :], mask * pi[:, None, :]]
        kinds += [jnp.stack([pw[r][0] for r in range(SUBLANES)], axis=1),
                  jnp.stack([pw[r][1] for r in range(SUBLANES)], axis=1)]
        levels.append(jnp.stack(kinds, axis=1))
    coef = jnp.stack(levels, axis=1)
    return wb, coef, wc


def _layer_params(l, norm_mix, w_in, ssm_lam_re, ssm_lam_im, ssm_b_re, ssm_b_im, ssm_c_re, ssm_c_im, ssm_d,
                  ssm_log_step, ssm_w_glu, ssm_b_glu, pool_w, pool_scale, conv_w, branch_norm, w_out, norm_ffn,
                  router_w, router_b):
    wb, coef, wc = _s5_tables(ssm_lam_re[l], ssm_lam_im[l], ssm_b_re[l], ssm_b_im[l], ssm_c_re[l], ssm_c_im[l],
                              ssm_log_step[l])
    eye4 = jnp.eye(len(POOL_WINDOWS), dtype=F32)
    poolw = jnp.einsum('gcd,gk->gckd', pool_w[l].astype(F32), eye4).reshape(POOL_WIDTH, POOL_WIDTH).astype(BF16)
    rw = jnp.zeros((D_MODEL, LANES), F32).at[:, :N_EXPERTS].set(router_w[l]).astype(BF16)
    rb = jnp.zeros((1, LANES), F32).at[0, :N_EXPERTS].set(router_b[l])
    row = lambda v: v.reshape(1, -1).astype(F32)
    return dict(gmix=row(norm_mix[l]), win=w_in[l].astype(BF16), wb=wb, coef=coef, wc=wc, dskip=row(ssm_d[l]),
                wglu=ssm_w_glu[l].astype(BF16), bglu=row(ssm_b_glu[l]), poolw=poolw, pscale=row(pool_scale[l]),
                convw=conv_w[l].astype(F32), gbr=row(branch_norm[l]), wout=w_out[l].astype(BF16),
                gffn=row(norm_ffn[l]), rw=rw, rb=rb)


def kernel(x, norm_mix, w_in, ssm_lam_re, ssm_lam_im, ssm_b_re, ssm_b_im, ssm_c_re, ssm_c_im, ssm_d, ssm_log_step, ssm_w_glu, ssm_b_glu, pool_w, pool_scale, conv_w, branch_norm, w_out, norm_ffn, router_w, router_b, w_gate_up, b_gate_up, w_down, b_down, final_norm):
    batch, seq, d_model = x.shape
    assert d_model == D_MODEL and seq % TILE_T == 0
    depth = w_in.shape[0]
    n_tok = batch * seq
    assert n_tok % ROUTE_T == 0 and n_tok % SC_WINDOW == 0
    n_blocks = -(-(n_tok * TOP_K + N_EXPERTS * (EXPERT_BLOCK - 1)) // EXPERT_BLOCK)
    b_gu = b_gate_up.reshape(depth, N_EXPERTS, 1, 2 * D_EXPERT)
    b_dn = b_down.reshape(depth, N_EXPERTS, 1, D_MODEL)

    xf = x.reshape(n_tok, D_MODEL)
    ysg = None
    slab = None
    for l in range(depth):
        lp = _layer_params(l, norm_mix, w_in, ssm_lam_re, ssm_lam_im, ssm_b_re, ssm_b_im, ssm_c_re, ssm_c_im, ssm_d,
                           ssm_log_step, ssm_w_glu, ssm_b_glu, pool_w, pool_scale, conv_w, branch_norm, w_out,
                           norm_ffn, router_w, router_b)
        xf, hp, slab, counts = _mixer_call(xf, ysg, slab, lp, batch, seq)
        dest, meta = _route_call(slab, counts, n_tok, n_blocks)
        idx = dest.reshape(1, 2 * TOP_K * n_tok)
        block_e = meta[:n_blocks, 0]
        n_valid = meta[n_blocks:n_blocks + 1, 0]
        n_rows = n_blocks * EXPERT_BLOCK
        xs = _dispatch(hp.reshape(2 * n_tok, SC_ROW), idx, 2 * n_rows).reshape(n_rows, PACKED)
        ys = _expert_call(xs, block_e, n_valid, w_gate_up, b_gu, w_down, b_dn, l, n_blocks)
        ysg = _combine(ys.reshape(2 * n_rows, SC_ROW), idx).reshape(TOP_K, 2, n_tok, SC_ROW)
    out = _final_call(xf, ysg, slab, final_norm.reshape(1, D_MODEL).astype(F32), n_tok)
    return out.reshape(batch, seq, D_MODEL)
```

```python
import functools
import math

import jax
import jax.numpy as jnp
from jax import lax
from jax.experimental import pallas as pl
from jax.experimental.pallas import tpu as pltpu
from jax.experimental.pallas import tpu_sc as plsc

F32 = jnp.float32
BF16 = jnp.bfloat16
I32 = jnp.int32

D_MODEL = 1024
SSM_WIDTH = 512
POOL_WIDTH = 256
CONV_WIDTH = 256
SSM_GROUP = 16
SSM_GROUPS = 32
SSM_STATE = 64
LAMBDA_RE_MAX = -1e-4
POOL_WINDOWS = (2, 4, 8, 16)
POOL_GROUP = 64
CONV_K = 3
IN_WIDTH = SSM_WIDTH + POOL_WIDTH + 3 * CONV_WIDTH
N_EXPERTS = 32
TOP_K = 4
D_EXPERT = 1024
SWIGLU_LIMIT = 7.0
SWIGLU_ALPHA = 1.702
EXPERT_BLOCK = 256
NORM_EPS = 1e-5

LANES = 128
SUBLANES = 8
TILE_T = 512
SCAN_LEVELS = 3
HALF_STATE = 1024
STATE_COLS = HALF_STATE // LANES
PACKED = D_MODEL // 2
ROUTE_T = 2048
SC_WINDOW = 128
SC_ROW = PACKED // 2
HI_MASK = -65536
VMEM_LIMIT = 56 * 1024 * 1024


def _bf16_round(v):
    return v.astype(BF16).astype(F32)


def _pack_rows(v):
    lo = lax.shift_right_logical(lax.bitcast_convert_type(_bf16_round(v[:, :PACKED]), I32), 16)
    hi = lax.bitcast_convert_type(_bf16_round(v[:, PACKED:]), I32) & HI_MASK
    return hi | lo


def _unpack_lo(w):
    return lax.bitcast_convert_type(lax.shift_left(w, 16), F32)


def _unpack_hi(w):
    return lax.bitcast_convert_type(w & HI_MASK, F32)


def _rms(v, g):
    r = lax.rsqrt(jnp.mean(v * v, axis=-1, keepdims=True) + NORM_EPS)
    return (v * r) * g


def _moe_combine(x, y_ref, slab):
    parts = [x[:, i * SC_ROW:(i + 1) * SC_ROW] for i in range(4)]
    for k in range(TOP_K):
        g = lax.bitcast_convert_type(slab[:, TOP_K + k:TOP_K + k + 1], F32)
        for h in range(2):
            w = y_ref[k, h]
            parts[h] = parts[h] + g * _unpack_lo(w)
            parts[2 + h] = parts[2 + h] + g * _unpack_hi(w)
    return jnp.concatenate(parts, axis=1)


def _cmuladd(xr, xi, cr, ci, vr, vi):
    return xr + cr * vr - ci * vi, xi + cr * vi + ci * vr


def _local_scan(xr, xi, cf):
    for j, d in enumerate((1, 2, 4)):
        rr = pltpu.roll(xr, d, axis=0)
        ri = pltpu.roll(xi, d, axis=0)
        xr, xi = _cmuladd(xr, xi, cf[2 * j], cf[2 * j + 1], rr, ri)
    return xr, xi


def _s5_scan(n, first, s_ref, b2_ref, b3_ref, coef_ref):
    groups1 = TILE_T // SUBLANES
    groups2 = groups1 // SUBLANES

    def coefs(level, c, kinds):
        return [coef_ref[n, level, k, :, c * LANES:(c + 1) * LANES] for k in kinds]

    def bcast(ref, row):
        return jnp.broadcast_to(ref[pl.ds(row, 1), :], (SUBLANES, LANES))

    for c in range(STATE_COLS):
        sr, si = s_ref.at[n, c], s_ref.at[n, STATE_COLS + c]
        b2r, b2i = b2_ref.at[n, c], b2_ref.at[n, STATE_COLS + c]
        b3r, b3i = b3_ref.at[n, c], b3_ref.at[n, STATE_COLS + c]
        cf1, cf2, cf3 = coefs(0, c, range(8)), coefs(1, c, range(8)), coefs(2, c, range(8))

        carry_r = jnp.where(first, 0.0, b3r[15:16, :])
        carry_i = jnp.where(first, 0.0, b3i[15:16, :])
        b3r[7:8, :] = carry_r
        b3i[7:8, :] = carry_i
        b2r[7:8, :] = carry_r
        b2i[7:8, :] = carry_i

        def body1(r, _, sr=sr, si=si, cf=cf1):
            rows = pl.ds(pl.multiple_of(r * SUBLANES, SUBLANES), SUBLANES)
            xr, xi = _local_scan(sr[rows, :], si[rows, :], cf)
            sr[rows, :] = xr
            si[rows, :] = xi
            return 0

        lax.fori_loop(0, groups1, body1, 0, unroll=2)

        x2r = sr[pl.ds(SUBLANES - 1, groups1, stride=SUBLANES), :]
        x2i = si[pl.ds(SUBLANES - 1, groups1, stride=SUBLANES), :]
        for g in range(groups2):
            rows = slice(SUBLANES * (g + 1), SUBLANES * (g + 2))
            lr, li = _local_scan(x2r[g * SUBLANES:(g + 1) * SUBLANES], x2i[g * SUBLANES:(g + 1) * SUBLANES], cf2)
            b2r[rows, :] = lr
            b2i[rows, :] = li

        x3r = b2r[pl.ds(2 * SUBLANES - 1, groups2, stride=SUBLANES), :]
        x3i = b2i[pl.ds(2 * SUBLANES - 1, groups2, stride=SUBLANES), :]
        lr, li = _local_scan(x3r, x3i, cf3)
        hr, hi = _cmuladd(lr, li, cf3[6], cf3[7], bcast(b3r, 7), bcast(b3i, 7))
        b3r[8:16, :] = hr
        b3i[8:16, :] = hi

        for g in range(groups2):
            rows = slice(SUBLANES * (g + 1), SUBLANES * (g + 2))
            hr, hi = _cmuladd(b2r[rows, :], b2i[rows, :], cf2[6], cf2[7], bcast(b3r, 7 + g), bcast(b3i, 7 + g))
            b2r[rows, :] = hr
            b2i[rows, :] = hi

        def body5(r, _, sr=sr, si=si, b2r=b2r, b2i=b2i, cf=cf1):
            rows = pl.ds(pl.multiple_of(r * SUBLANES, SUBLANES), SUBLANES)
            hr, hi = _cmuladd(sr[rows, :], si[rows, :], cf[6], cf[7],
                              bcast(b2r, SUBLANES - 1 + r), bcast(b2i, SUBLANES - 1 + r))
            sr[rows, :] = hr
            si[rows, :] = hi
            return 0

        lax.fori_loop(0, groups1, body5, 0, unroll=2)


def _mixer_kernel(*refs, first_layer):
    if first_layer:
        x_ref = refs[0]
        refs = refs[1:]
    else:
        x_ref, yprev_ref, slabprev_ref = refs[:3]
        refs = refs[3:]
    (gmix_ref, win_ref, wb_ref, coef_ref, wc_ref, dskip_ref, wglu_ref, bglu_ref, poolw_ref, pscale_ref,
     convw_ref, gbr_ref, wout_ref, gffn_ref, rw_ref, rb_ref,
     x1_ref, hp_ref, slab_ref, cnt_ref,
     s_ref, b2_ref, b3_ref, pbuf_ref, zbuf_ref, cntacc_ref) = refs

    b = pl.program_id(0)
    l = pl.program_id(1)
    seq_start = l == 0

    @pl.when(jnp.logical_and(b == 0, l == 0))
    def _():
        cntacc_ref[...] = jnp.zeros_like(cntacc_ref)

    if first_layer:
        x = x_ref[...]
    else:
        x = _moe_combine(x_ref[...], yprev_ref, slabprev_ref[...])

    hn = _rms(x, gmix_ref[...]).astype(BF16)
    proj = jnp.dot(hn, win_ref[...], preferred_element_type=F32)
    u = proj[:, :SSM_WIDTH]
    p = proj[:, SSM_WIDTH:SSM_WIDTH + POOL_WIDTH]
    c0 = SSM_WIDTH + POOL_WIDTH
    bg = proj[:, c0:c0 + CONV_WIDTH]
    cg = proj[:, c0 + CONV_WIDTH:c0 + 2 * CONV_WIDTH]
    hv = proj[:, c0 + 2 * CONV_WIDTH:]

    u_bf = u.astype(BF16)
    halves = []
    for n in range(2):
        bu = jnp.dot(u_bf[:, n * 256:(n + 1) * 256], wb_ref[n], preferred_element_type=F32)
        for c in range(2 * STATE_COLS):
            s_ref[n, c] = bu[:, c * LANES:(c + 1) * LANES]
        _s5_scan(n, seq_start, s_ref, b2_ref, b3_ref, coef_ref)
        h = jnp.concatenate([s_ref[n, c].astype(BF16) for c in range(2 * STATE_COLS)], axis=1)
        halves.append(jnp.dot(h, wc_ref[n], preferred_element_type=F32))
    y = jnp.concatenate(halves, axis=1) + dskip_ref[...] * u
    z = jax.nn.gelu(y)
    glu = jnp.dot(z.astype(BF16), wglu_ref[...], preferred_element_type=F32) + bglu_ref[...]
    y_ssm = z * jax.nn.sigmoid(glu)

    tail = 16
    pbuf_ref[0:tail, :] = jnp.where(seq_start, 0.0, pbuf_ref[TILE_T:TILE_T + tail, :])
    pbuf_ref[tail:, :] = p
    lane_p = lax.broadcasted_iota(I32, (TILE_T, POOL_WIDTH), 1)
    win = jnp.where(lane_p < 64, 2, jnp.where(lane_p < 128, 4, jnp.where(lane_p < 192, 8, 16)))
    total = p
    for d in range(1, 16):
        total = total + jnp.where(win > d, pbuf_ref[pl.ds(tail - d, TILE_T), :], 0.0)
    row_p = lax.broadcasted_iota(I32, (TILE_T, POOL_WIDTH), 0) + l * TILE_T
    count = jnp.minimum(row_p + 1, win).astype(F32)
    pooled = total / count - p
    mixed = jnp.dot(pooled.astype(BF16), poolw_ref[...], preferred_element_type=F32)
    y_pool = mixed * pscale_ref[...]

    zc = cg * hv
    zbuf_ref[0:8, :] = jnp.where(seq_start, 0.0, zbuf_ref[TILE_T:TILE_T + 8, :])
    zbuf_ref[8:, :] = zc
    yc = (convw_ref[0:1, :] * zbuf_ref[pl.ds(6, TILE_T), :]
          + convw_ref[1:2, :] * zbuf_ref[pl.ds(7, TILE_T), :]
          + convw_ref[2:3, :] * zc)
    y_conv = bg * yc

    gbr = gbr_ref[...]
    s1 = SSM_WIDTH + POOL_WIDTH
    mixed_all = jnp.concatenate([_rms(y_ssm, gbr[:, :SSM_WIDTH]).astype(BF16),
                                 _rms(y_pool, gbr[:, SSM_WIDTH:s1]).astype(BF16),
                                 _rms(y_conv, gbr[:, s1:]).astype(BF16)], axis=1)
    x1 = x + jnp.dot(mixed_all, wout_ref[...], preferred_element_type=F32)
    x1_ref[...] = x1

    hn2 = _rms(x1, gffn_ref[...])
    packed = _pack_rows(hn2)
    hp_ref[0] = packed[:, :SC_ROW]
    hp_ref[1] = packed[:, SC_ROW:]
    logits = jnp.dot(hn2.astype(BF16), rw_ref[...], preferred_element_type=F32) + rb_ref[...]
    lane = lax.broadcasted_iota(I32, (TILE_T, LANES), 1)
    neg = jnp.float32(-jnp.inf)
    work = jnp.where(lane < N_EXPERTS, logits, neg)
    vals, idxs = [], []
    for _ in range(TOP_K):
        m = jnp.max(work, axis=1, keepdims=True)
        idx = jnp.min(jnp.where(work == m, lane, LANES), axis=1, keepdims=True)
        vals.append(m)
        idxs.append(idx)
        work = jnp.where(lane == idx, neg, work)
    exps = [jnp.exp(v - vals[0]) for v in vals]
    denom = exps[0] + exps[1] + exps[2] + exps[3]
    gates = [e / denom for e in exps]

    onehot = jnp.zeros((TILE_T, LANES), F32)
    for idx in idxs:
        onehot = jnp.where(lane == idx, 1.0, onehot)
    r_i = lax.broadcasted_iota(I32, (TILE_T, TILE_T), 0)
    c_i = lax.broadcasted_iota(I32, (TILE_T, TILE_T), 1)
    ltri = jnp.where(c_i < r_i, 1.0, 0.0).astype(BF16)
    before = jnp.dot(ltri, onehot.astype(BF16), preferred_element_type=F32) + cntacc_ref[0:1, :]
    slab = jnp.zeros((TILE_T, LANES), I32)
    for k in range(TOP_K):
        rank = jnp.sum(jnp.where(lane == idxs[k], before, 0.0), axis=1, keepdims=True).astype(I32)
        slab = jnp.where(lane == k, idxs[k], slab)
        slab = jnp.where(lane == TOP_K + k, lax.bitcast_convert_type(gates[k], I32), slab)
        slab = jnp.where(lane == 2 * TOP_K + k, rank, slab)
    slab_ref[...] = slab
    newcnt = cntacc_ref[0:1, :] + jnp.sum(onehot, axis=0, keepdims=True)
    cntacc_ref[...] = jnp.broadcast_to(newcnt, cntacc_ref.shape)
    cnt_ref[...] = jnp.broadcast_to(newcnt, cnt_ref.shape).astype(I32)


def _const_spec(shape):
    nd = len(shape)
    return pl.BlockSpec(shape, lambda b, l, nd=nd: (0,) * nd)


def _mixer_call(x, yprev, slabprev, lp, batch, seq):
    first_layer = yprev is None
    n_tok = batch * seq
    steps = seq // TILE_T
    tok_spec = lambda w: pl.BlockSpec((TILE_T, w), lambda b, l: (b * steps + l, 0))
    in_specs = [tok_spec(D_MODEL)]
    args = [x]
    if not first_layer:
        in_specs += [pl.BlockSpec((TOP_K, 2, TILE_T, SC_ROW), lambda b, l: (0, 0, b * steps + l, 0)), tok_spec(LANES)]
        args += [yprev, slabprev]
    weights = [lp['gmix'], lp['win'], lp['wb'], lp['coef'], lp['wc'], lp['dskip'], lp['wglu'], lp['bglu'],
               lp['poolw'], lp['pscale'], lp['convw'], lp['gbr'], lp['wout'], lp['gffn'], lp['rw'], lp['rb']]
    in_specs += [_const_spec(w.shape) for w in weights]
    args += weights
    out_shape = [jax.ShapeDtypeStruct((n_tok, D_MODEL), F32),
                 jax.ShapeDtypeStruct((2, n_tok, SC_ROW), I32),
                 jax.ShapeDtypeStruct((n_tok, LANES), I32),
                 jax.ShapeDtypeStruct((SUBLANES, LANES), I32)]
    out_specs = [tok_spec(D_MODEL), pl.BlockSpec((2, TILE_T, SC_ROW), lambda b, l: (0, b * steps + l, 0)), tok_spec(LANES),
                 pl.BlockSpec((SUBLANES, LANES), lambda b, l: (0, 0))]
    scratch = [pltpu.VMEM((2, 2 * STATE_COLS, TILE_T, LANES), F32),
               pltpu.VMEM((2, 2 * STATE_COLS, SUBLANES + TILE_T // SUBLANES, LANES), F32),
               pltpu.VMEM((2, 2 * STATE_COLS, 2 * SUBLANES, LANES), F32),
               pltpu.VMEM((16 + TILE_T, POOL_WIDTH), F32),
               pltpu.VMEM((8 + TILE_T, CONV_WIDTH), F32),
               pltpu.VMEM((SUBLANES, LANES), F32)]
    return pl.pallas_call(
        functools.partial(_mixer_kernel, first_layer=first_layer),
        grid=(batch, steps),
        in_specs=in_specs, out_specs=out_specs, out_shape=out_shape, scratch_shapes=scratch,
        compiler_params=pltpu.CompilerParams(dimension_semantics=("arbitrary", "arbitrary"),
                                             vmem_limit_bytes=VMEM_LIMIT),
        name="mixer_first" if first_layer else "mixer_next",
    )(*args)


def _route_kernel(slab_ref, cnt_ref, dest_ref, meta_ref, *, n_blocks):
    lane1 = lax.broadcasted_iota(I32, (1, LANES), 1)
    counts = jnp.where(lane1 < N_EXPERTS, cnt_ref[0:1, :], 0)
    padded = ((counts + (EXPERT_BLOCK - 1)) // EXPERT_BLOCK) * EXPERT_BLOCK
    pend = padded
    for sh in (1, 2, 4, 8, 16):
        pend = pend + jnp.where(lane1 >= sh, pltpu.roll(pend, sh, axis=1), 0)
    pstart = pend - padded

    @pl.when(pl.program_id(0) == 0)
    def _():
        rows = meta_ref.shape[0]
        bstart = lax.broadcasted_iota(I32, (rows, LANES), 0) * EXPERT_BLOCK
        lane = lax.broadcasted_iota(I32, (rows, LANES), 1)
        done = jnp.where(jnp.logical_and(lane < N_EXPERTS, pend <= bstart), 1, 0)
        be = jnp.minimum(jnp.sum(done, axis=1, keepdims=True), N_EXPERTS - 1)
        total = jnp.sum(jnp.where(lane1 == N_EXPERTS - 1, pend, 0), axis=1, keepdims=True)
        row = lax.broadcasted_iota(I32, (rows, LANES), 0)
        meta_ref[...] = jnp.where(row == n_blocks, total // EXPERT_BLOCK, jnp.broadcast_to(be, (rows, LANES)))

    slab = slab_ref[...]
    lane = lax.broadcasted_iota(I32, slab.shape, 1)
    out = jnp.zeros(slab.shape, F32)
    for k in range(TOP_K):
        idx = slab[:, k:k + 1]
        rank = slab[:, 2 * TOP_K + k:2 * TOP_K + k + 1]
        base = jnp.sum(jnp.where(lane == idx, pstart, 0), axis=1, keepdims=True)
        for h in range(2):
            out = jnp.where(lane == 2 * k + h, (base + rank + h * n_blocks * EXPERT_BLOCK).astype(F32), out)
    dest_ref[...] = out.T[:SUBLANES, :].astype(I32)


def _route_call(slab, counts, n_tok, n_blocks):
    meta_rows = ((n_blocks + 1 + SUBLANES - 1) // SUBLANES) * SUBLANES
    return pl.pallas_call(
        functools.partial(_route_kernel, n_blocks=n_blocks),
        grid=(n_tok // ROUTE_T,),
        in_specs=[pl.BlockSpec((ROUTE_T, LANES), lambda i: (i, 0)),
                  pl.BlockSpec((SUBLANES, LANES), lambda i: (0, 0))],
        out_specs=[pl.BlockSpec((SUBLANES, ROUTE_T), lambda i: (0, i)),
                   pl.BlockSpec((meta_rows, LANES), lambda i: (0, 0))],
        out_shape=[jax.ShapeDtypeStruct((SUBLANES, n_tok), I32),
                   jax.ShapeDtypeStruct((meta_rows, LANES), I32)],
        compiler_params=pltpu.CompilerParams(dimension_semantics=("arbitrary",)),
        name="route",
    )(slab, counts)


def _sc_mesh():
    return plsc.VectorSubcoreMesh(core_axis_name="core", subcore_axis_name="subcore")


def _dispatch(src, idx, n_rows):
    windows = src.shape[0] // SC_WINDOW

    @functools.partial(pl.kernel, out_type=jax.ShapeDtypeStruct((n_rows, SC_ROW), src.dtype),
                       mesh=_sc_mesh(), scratch_types=[])
    def scatter_rows(x_hbm, i_hbm, o_hbm):
        def body(x_vmem, i_vmem):
            pltpu.sync_copy(x_vmem, o_hbm.at[i_vmem.at[0]])

        pltpu.emit_pipeline(
            body,
            grid=(idx.shape[1] // SC_WINDOW,),
            in_specs=[pl.BlockSpec((SC_WINDOW, SC_ROW), lambda i: (lax.rem(i, windows), 0)),
                      pl.BlockSpec((1, SC_WINDOW), lambda i: (0, i))],
            out_specs=[],
            core_axis_name=("core", "subcore"),
            dimension_semantics=(pltpu.PARALLEL,),
        )(x_hbm, i_hbm)

    return scatter_rows(src, idx)


def _combine(src, idx):
    n_idx = idx.shape[1]

    @functools.partial(pl.kernel, out_type=jax.ShapeDtypeStruct((n_idx, SC_ROW), src.dtype),
                       mesh=_sc_mesh(), scratch_types=[])
    def gather_rows(y_hbm, i_hbm, o_hbm):
        def body(i_vmem, o_vmem):
            pltpu.sync_copy(y_hbm.at[i_vmem.at[0]], o_vmem)

        pltpu.emit_pipeline(
            body,
            grid=(n_idx // SC_WINDOW,),
            in_specs=[pl.BlockSpec((1, SC_WINDOW), lambda i: (0, i))],
            out_specs=[pl.BlockSpec((SC_WINDOW, SC_ROW), lambda i: (i, 0))],
            core_axis_name=("core", "subcore"),
            dimension_semantics=(pltpu.PARALLEL,),
        )(i_hbm, o_hbm)

    return gather_rows(src, idx)


def _expert_kernel(be_ref, nv_ref, xs_ref, wgu_ref, bgu_ref, wdn_ref, bdn_ref, ys_ref, wgu_s, wdn_s, act_s):
    blk = pl.program_id(0)
    prev = be_ref[jnp.maximum(blk - 1, 0)]
    changed = jnp.logical_or(blk == 0, be_ref[blk] != prev)
    valid = blk < nv_ref[0]

    @pl.when(jnp.logical_and(valid, changed))
    def _():
        chunk = 128

        def cast_gu(i, _):
            rows = pl.ds(pl.multiple_of(i * chunk, chunk), chunk)
            wgu_s[rows, :] = wgu_ref[rows, :].astype(BF16)
            return 0

        def cast_dn(i, _):
            rows = pl.ds(pl.multiple_of(i * chunk, chunk), chunk)
            wdn_s[rows, :] = wdn_ref[rows, :].astype(BF16)
            return 0

        lax.fori_loop(0, D_MODEL // chunk, cast_gu, 0)
        lax.fori_loop(0, D_EXPERT // chunk, cast_dn, 0)

    @pl.when(valid)
    def _():
        w0 = xs_ref[0]
        w1 = xs_ref[1]
        xb = jnp.concatenate([_unpack_lo(w0).astype(BF16), _unpack_lo(w1).astype(BF16),
                              _unpack_hi(w0).astype(BF16), _unpack_hi(w1).astype(BF16)], axis=1)
        half = D_EXPERT // 2
        for j in range(2):
            cols = slice(j * half, (j + 1) * half)
            ucols = slice(D_EXPERT + j * half, D_EXPERT + (j + 1) * half)
            g = jnp.dot(xb, wgu_s[:, cols], preferred_element_type=F32) + bgu_ref[:, cols]
            up = jnp.dot(xb, wgu_s[:, ucols], preferred_element_type=F32) + bgu_ref[:, ucols]
            g = jnp.minimum(g, SWIGLU_LIMIT)
            up = jnp.clip(up, -SWIGLU_LIMIT, SWIGLU_LIMIT)
            act_s[:, cols] = (g * jax.nn.sigmoid(SWIGLU_ALPHA * g) * (up + 1.0)).astype(BF16)
        y = jnp.dot(act_s[...], wdn_s[...], preferred_element_type=F32) + bdn_ref[...]
        packed = _pack_rows(y)
        ys_ref[0] = packed[:, :SC_ROW]
        ys_ref[1] = packed[:, SC_ROW:]


def _expert_call(xs, block_e, n_valid, w_gu, b_gu, w_dn, b_dn, layer, n_blocks):
    def row_map(i, be, nv):
        return (0, jnp.minimum(i, jnp.maximum(nv[0] - 1, 0)), 0)

    def w_map(i, be, nv):
        return (layer, be[i], 0, 0)

    grid_spec = pltpu.PrefetchScalarGridSpec(
        num_scalar_prefetch=2,
        grid=(n_blocks,),
        in_specs=[pl.BlockSpec((2, EXPERT_BLOCK, SC_ROW), row_map),
                  pl.BlockSpec((None, None, D_MODEL, 2 * D_EXPERT), w_map),
                  pl.BlockSpec((None, None, 1, 2 * D_EXPERT), w_map),
                  pl.BlockSpec((None, None, D_EXPERT, D_MODEL), w_map),
                  pl.BlockSpec((None, None, 1, D_MODEL), w_map)],
        out_specs=pl.BlockSpec((2, EXPERT_BLOCK, SC_ROW), row_map),
        scratch_shapes=[pltpu.VMEM((D_MODEL, 2 * D_EXPERT), BF16),
                        pltpu.VMEM((D_EXPERT, D_MODEL), BF16),
                        pltpu.VMEM((EXPERT_BLOCK, D_EXPERT), BF16)])
    return pl.pallas_call(
        _expert_kernel,
        grid_spec=grid_spec,
        out_shape=jax.ShapeDtypeStruct((2, n_blocks * EXPERT_BLOCK, SC_ROW), I32),
        compiler_params=pltpu.CompilerParams(dimension_semantics=("arbitrary",),
                                             vmem_limit_bytes=VMEM_LIMIT),
        name="experts",
    )(block_e, n_valid, xs, w_gu, b_gu, w_dn, b_dn)


def _final_kernel(x_ref, y_ref, slab_ref, g_ref, o_ref):
    x = _moe_combine(x_ref[...], y_ref, slab_ref[...])
    o_ref[...] = _rms(x, g_ref[...])


def _final_call(x1, ysg, slab, g, n_tok):
    t = TILE_T
    return pl.pallas_call(
        _final_kernel,
        grid=(n_tok // t,),
        in_specs=[pl.BlockSpec((t, D_MODEL), lambda i: (i, 0)),
                  pl.BlockSpec((TOP_K, 2, t, SC_ROW), lambda i: (0, 0, i, 0)),
                  pl.BlockSpec((t, LANES), lambda i: (i, 0)),
                  pl.BlockSpec((1, D_MODEL), lambda i: (0, 0))],
        out_specs=pl.BlockSpec((t, D_MODEL), lambda i: (i, 0)),
        out_shape=jax.ShapeDtypeStruct((n_tok, D_MODEL), F32),
        compiler_params=pltpu.CompilerParams(dimension_semantics=("arbitrary",)),
        name="final_norm",
    )(x1, ysg, slab, g)


def _s5_tables(lam_re, lam_im, b_re, b_im, c_re, c_im, log_step):
    lr = jnp.minimum(lam_re.astype(F32), LAMBDA_RE_MAX)
    li = lam_im.astype(F32)
    step = jnp.exp(log_step.astype(F32))[:, None]
    mag = jnp.exp(lr * step)
    ar = mag * jnp.cos(li * step)
    ai = mag * jnp.sin(li * step)
    nr = ar - 1.0
    den = lr * lr + li * li
    kr = (nr * lr + ai * li) / den
    ki = (ai * lr - nr * li) / den
    bre = b_re.astype(F32)
    bim = b_im.astype(F32)
    bbar_r = kr[..., None] * bre - ki[..., None] * bim
    bbar_i = kr[..., None] * bim + ki[..., None] * bre
    gl = SSM_GROUPS // 2
    eye = jnp.eye(gl, dtype=F32)

    def wb_half(bb):
        return jnp.einsum('gph,gk->ghkp', bb, eye).reshape(gl * SSM_GROUP, gl * SSM_STATE)

    def wc_half(cc):
        return jnp.einsum('ghp,gk->gpkh', cc, eye).reshape(gl * SSM_STATE, gl * SSM_GROUP)

    wb = jnp.stack([jnp.concatenate([wb_half(bbar_r[n * gl:(n + 1) * gl]), wb_half(bbar_i[n * gl:(n + 1) * gl])], axis=1)
                    for n in range(2)]).astype(BF16)
    cre = c_re.astype(F32)
    cim = c_im.astype(F32)
    wc = jnp.stack([jnp.concatenate([wc_half(cre[n * gl:(n + 1) * gl]), wc_half(-cim[n * gl:(n + 1) * gl])], axis=0)
                    for n in range(2)]).astype(BF16)

    def power(m):
        mg = jnp.exp(m * lr * step)
        return ((mg * jnp.cos(m * li * step)).reshape(2, HALF_STATE),
                (mg * jnp.sin(m * li * step)).reshape(2, HALF_STATE))

    rows = jnp.arange(SUBLANES)[:, None]
    levels = []
    for lev in range(SCAN_LEVELS):
        base = float(SUBLANES ** lev)
        pw = [power(base * j) for j in range(1, SUBLANES + 1)]
        kinds = []
        for d in (1, 2, 4):
            pr, pi = pw[d - 1]
            mask = (rows >= d).astype(F32)[None]
            kinds += [mask * pr[:, None, :], mask * pi[:, None, :]]
        kinds += [jnp.stack([pw[r][0] for r in range(SUBLANES)], axis=1),
                  jnp.stack([pw[r][1] for r in range(SUBLANES)], axis=1)]
        levels.append(jnp.stack(kinds, axis=1))
    coef = jnp.stack(levels, axis=1)
    return wb, coef, wc


def _layer_params(l, norm_mix, w_in, ssm_lam_re, ssm_lam_im, ssm_b_re, ssm_b_im, ssm_c_re, ssm_c_im, ssm_d,
                  ssm_log_step, ssm_w_glu, ssm_b_glu, pool_w, pool_scale, conv_w, branch_norm, w_out, norm_ffn,
                  router_w, router_b):
    wb, coef, wc = _s5_tables(ssm_lam_re[l], ssm_lam_im[l], ssm_b_re[l], ssm_b_im[l], ssm_c_re[l], ssm_c_im[l],
                              ssm_log_step[l])
    eye4 = jnp.eye(len(POOL_WINDOWS), dtype=F32)
    poolw = jnp.einsum('gcd,gk->gckd', pool_w[l].astype(F32), eye4).reshape(POOL_WIDTH, POOL_WIDTH).astype(BF16)
    rw = jnp.zeros((D_MODEL, LANES), F32).at[:, :N_EXPERTS].set(router_w[l]).astype(BF16)
    rb = jnp.zeros((1, LANES), F32).at[0, :N_EXPERTS].set(router_b[l])
    row = lambda v: v.reshape(1, -1).astype(F32)
    return dict(gmix=row(norm_mix[l]), win=w_in[l].astype(BF16), wb=wb, coef=coef, wc=wc, dskip=row(ssm_d[l]),
                wglu=ssm_w_glu[l].astype(BF16), bglu=row(ssm_b_glu[l]), poolw=poolw, pscale=row(pool_scale[l]),
                convw=conv_w[l].astype(F32), gbr=row(branch_norm[l]), wout=w_out[l].astype(BF16),
                gffn=row(norm_ffn[l]), rw=rw, rb=rb)


def kernel(x, norm_mix, w_in, ssm_lam_re, ssm_lam_im, ssm_b_re, ssm_b_im, ssm_c_re, ssm_c_im, ssm_d, ssm_log_step, ssm_w_glu, ssm_b_glu, pool_w, pool_scale, conv_w, branch_norm, w_out, norm_ffn, router_w, router_b, w_gate_up, b_gate_up, w_down, b_down, final_norm):
    batch, seq, d_model = x.shape
    assert d_model == D_MODEL and seq % TILE_T == 0
    depth = w_in.shape[0]
    n_tok = batch * seq
    assert n_tok % ROUTE_T == 0 and n_tok % SC_WINDOW == 0
    n_blocks = -(-(n_tok * TOP_K + N_EXPERTS * (EXPERT_BLOCK - 1)) // EXPERT_BLOCK)
    b_gu = b_gate_up.reshape(depth, N_EXPERTS, 1, 2 * D_EXPERT)
    b_dn = b_down.reshape(depth, N_EXPERTS, 1, D_MODEL)

    xf = x.reshape(n_tok, D_MODEL)
    ysg = None
    slab = None
    for l in range(depth):
        lp = _layer_params(l, norm_mix, w_in, ssm_lam_re, ssm_lam_im, ssm_b_re, ssm_b_im, ssm_c_re, ssm_c_im, ssm_d,
                           ssm_log_step, ssm_w_glu, ssm_b_glu, pool_w, pool_scale, conv_w, branch_norm, w_out,
                           norm_ffn, router_w, router_b)
        xf, hp, slab, counts = _mixer_call(xf, ysg, slab, lp, batch, seq)
        dest, meta = _route_call(slab, counts, n_tok, n_blocks)
        idx = dest.reshape(1, 2 * TOP_K * n_tok)
        block_e = meta[:n_blocks, 0]
        n_valid = meta[n_blocks:n_blocks + 1, 0]
        n_rows = n_blocks * EXPERT_BLOCK
        xs = _dispatch(hp.reshape(2 * n_tok, SC_ROW), idx, 2 * n_rows).reshape(2, n_rows, SC_ROW)
        ys = _expert_call(xs, block_e, n_valid, w_gate_up, b_gu, w_down, b_dn, l, n_blocks)
        ysg = _combine(ys.reshape(2 * n_rows, SC_ROW), idx).reshape(TOP_K, 2, n_tok, SC_ROW)
    out = _final_call(xf, ysg, slab, final_norm.reshape(1, D_MODEL).astype(F32), n_tok)
    return out.reshape(batch, seq, D_MODEL)
```

```python
import functools
import math

import jax
import jax.numpy as jnp
from jax import lax
from jax.experimental import pallas as pl
from jax.experimental.pallas import tpu as pltpu
from jax.experimental.pallas import tpu_sc as plsc

F32 = jnp.float32
BF16 = jnp.bfloat16
I32 = jnp.int32

D_MODEL = 1024
SSM_WIDTH = 512
POOL_WIDTH = 256
CONV_WIDTH = 256
SSM_GROUP = 16
SSM_GROUPS = 32
SSM_STATE = 64
LAMBDA_RE_MAX = -1e-4
POOL_WINDOWS = (2, 4, 8, 16)
POOL_GROUP = 64
CONV_K = 3
IN_WIDTH = SSM_WIDTH + POOL_WIDTH + 3 * CONV_WIDTH
N_EXPERTS = 32
TOP_K = 4
D_EXPERT = 1024
SWIGLU_LIMIT = 7.0
SWIGLU_ALPHA = 1.702
EXPERT_BLOCK = 256
NORM_EPS = 1e-5

LANES = 128
SUBLANES = 8
TILE_T = 512
CHUNK = 16
N_CHUNK = TILE_T // CHUNK
PACKED = D_MODEL // 2
ROUTE_T = 2048
SC_WINDOW = 128
SC_ROW = PACKED // 2
HI_MASK = -65536
VMEM_LIMIT = 56 * 1024 * 1024


def _bf16_round(v):
    return v.astype(BF16).astype(F32)


def _pack_rows(v):
    lo = lax.shift_right_logical(lax.bitcast_convert_type(_bf16_round(v[:, :PACKED]), I32), 16)
    hi = lax.bitcast_convert_type(_bf16_round(v[:, PACKED:]), I32) & HI_MASK
    return hi | lo


def _unpack_lo(w):
    return lax.bitcast_convert_type(lax.shift_left(w, 16), F32)


def _unpack_hi(w):
    return lax.bitcast_convert_type(w & HI_MASK, F32)


def _rms(v, g):
    r = lax.rsqrt(jnp.mean(v * v, axis=-1, keepdims=True) + NORM_EPS)
    return (v * r) * g


def _moe_combine(x, y_ref, slab):
    parts = [x[:, i * SC_ROW:(i + 1) * SC_ROW] for i in range(4)]
    for k in range(TOP_K):
        g = lax.bitcast_convert_type(slab[:, TOP_K + k:TOP_K + k + 1], F32)
        for h in range(2):
            w = y_ref[k, h]
            parts[h] = parts[h] + g * _unpack_lo(w)
            parts[2 + h] = parts[2 + h] + g * _unpack_hi(w)
    return jnp.concatenate(parts, axis=1)


def _block_transpose8(vs, lane):
    vs = list(vs)
    for dist in (4, 2, 1):
        width = SSM_GROUP * dist
        low = (lane % (2 * width)) < width
        for j in range(8):
            if j & dist:
                continue
            a, b = vs[j], vs[j + dist]
            vs[j] = jnp.where(low, a, pltpu.roll(b, width, axis=1))
            vs[j + dist] = jnp.where(low, pltpu.roll(a, LANES - width, axis=1), b)
    return vs


def _cmul_packed(c1, c2, v):
    return c1 * v + c2 * pltpu.roll(v, SSM_STATE, axis=1)


def _chunk_scan(s, carry_in, cf):
    outs = []
    carry = jnp.broadcast_to(carry_in, (SUBLANES, LANES))
    for rg in range(N_CHUNK // SUBLANES):
        x = s[rg * SUBLANES:(rg + 1) * SUBLANES]
        for j, d in enumerate((1, 2, 4)):
            x = x + _cmul_packed(cf[2 * j], cf[2 * j + 1], pltpu.roll(x, d, axis=0))
        x = x + _cmul_packed(cf[6], cf[7], carry)
        outs.append(x)
        carry = jnp.broadcast_to(x[SUBLANES - 1:SUBLANES], (SUBLANES, LANES))
    return jnp.concatenate(outs, axis=0)


def _s5_mixer(u, first, minw_ref, mcat_ref, coef_ref, u_s, y_s, h_s):
    blocks = SSM_WIDTH // LANES
    for b4 in range(blocks):
        u_s[b4] = u[:, b4 * LANES:(b4 + 1) * LANES]
    lane = lax.broadcasted_iota(I32, (N_CHUNK, LANES), 1)

    halves = [[None, None] for _ in range(SSM_GROUPS)]
    for b4 in range(blocks):
        for hh in range(2):
            xs = [u_s[b4, pl.ds(8 * hh + j, N_CHUNK, stride=CHUNK), :].astype(BF16) for j in range(8)]
            ws = _block_transpose8(xs, lane)
            for gl in range(8):
                halves[8 * b4 + gl][hh] = ws[gl]
    ug = [jnp.concatenate(h2, axis=1) for h2 in halves]

    carry_in = jnp.where(first, 0.0, h_s[N_CHUNK + 7:N_CHUNK + 8, :])
    h_s[7:8, :] = carry_in
    for g in range(SSM_GROUPS):
        cols = slice(g * LANES, (g + 1) * LANES)
        s_g = jnp.dot(ug[g], minw_ref[g], preferred_element_type=F32)
        cf = [coef_ref[k, :, cols] for k in range(8)]
        h_s[8:8 + N_CHUNK, cols] = _chunk_scan(s_g, carry_in[:, cols], cf)

    yg = []
    for g in range(SSM_GROUPS):
        hprev = h_s[7:7 + N_CHUNK, g * LANES:(g + 1) * LANES].astype(BF16)
        lhs = jnp.concatenate([ug[g], hprev], axis=1)
        yg.append(jnp.dot(lhs, mcat_ref[g], preferred_element_type=F32).astype(BF16))

    for b4 in range(blocks):
        for hh in range(2):
            ws = [yg[8 * b4 + gl][:, hh * LANES:(hh + 1) * LANES] for gl in range(8)]
            xs = _block_transpose8(ws, lane)
            for j in range(8):
                y_s[b4, pl.ds(8 * hh + j, N_CHUNK, stride=CHUNK), :] = xs[j].astype(F32)
    return jnp.concatenate([y_s[b4] for b4 in range(blocks)], axis=1)


def _mixer_kernel(*refs, first_layer):
    if first_layer:
        x_ref = refs[0]
        refs = refs[1:]
    else:
        x_ref, yprev_ref, slabprev_ref = refs[:3]
        refs = refs[3:]
    (gmix_ref, win_ref, minw_ref, mcat_ref, coef_ref, dskip_ref, wglu_ref, bglu_ref, poolw_ref, pscale_ref,
     convw_ref, gbr_ref, wout_ref, gffn_ref, rw_ref, rb_ref,
     x1_ref, hp_ref, slab_ref, cnt_ref,
     u_s, y_s, h_s, pbuf_ref, zbuf_ref, cntacc_ref) = refs

    b = pl.program_id(0)
    l = pl.program_id(1)
    seq_start = l == 0

    @pl.when(jnp.logical_and(b == 0, l == 0))
    def _():
        cntacc_ref[...] = jnp.zeros_like(cntacc_ref)

    if first_layer:
        x = x_ref[...]
    else:
        x = _moe_combine(x_ref[...], yprev_ref, slabprev_ref[...])

    hn = _rms(x, gmix_ref[...]).astype(BF16)
    proj = jnp.dot(hn, win_ref[...], preferred_element_type=F32)
    u = proj[:, :SSM_WIDTH]
    p = proj[:, SSM_WIDTH:SSM_WIDTH + POOL_WIDTH]
    c0 = SSM_WIDTH + POOL_WIDTH
    bg = proj[:, c0:c0 + CONV_WIDTH]
    cg = proj[:, c0 + CONV_WIDTH:c0 + 2 * CONV_WIDTH]
    hv = proj[:, c0 + 2 * CONV_WIDTH:]

    y = _s5_mixer(u, seq_start, minw_ref, mcat_ref, coef_ref, u_s, y_s, h_s) + dskip_ref[...] * u
    z = jax.nn.gelu(y)
    glu = jnp.dot(z.astype(BF16), wglu_ref[...], preferred_element_type=F32) + bglu_ref[...]
    y_ssm = z * jax.nn.sigmoid(glu)

    tail = 16
    pbuf_ref[0:tail, :] = jnp.where(seq_start, 0.0, pbuf_ref[TILE_T:TILE_T + tail, :])
    pbuf_ref[tail:, :] = p
    lane_p = lax.broadcasted_iota(I32, (TILE_T, POOL_WIDTH), 1)
    win = jnp.where(lane_p < 64, 2, jnp.where(lane_p < 128, 4, jnp.where(lane_p < 192, 8, 16)))
    total = p
    for d in range(1, 16):
        total = total + jnp.where(win > d, pbuf_ref[pl.ds(tail - d, TILE_T), :], 0.0)
    row_p = lax.broadcasted_iota(I32, (TILE_T, POOL_WIDTH), 0) + l * TILE_T
    count = jnp.minimum(row_p + 1, win).astype(F32)
    pooled = total / count - p
    mixed = jnp.dot(pooled.astype(BF16), poolw_ref[...], preferred_element_type=F32)
    y_pool = mixed * pscale_ref[...]

    zc = cg * hv
    zbuf_ref[0:8, :] = jnp.where(seq_start, 0.0, zbuf_ref[TILE_T:TILE_T + 8, :])
    zbuf_ref[8:, :] = zc
    yc = (convw_ref[0:1, :] * zbuf_ref[pl.ds(6, TILE_T), :]
          + convw_ref[1:2, :] * zbuf_ref[pl.ds(7, TILE_T), :]
          + convw_ref[2:3, :] * zc)
    y_conv = bg * yc

    gbr = gbr_ref[...]
    s1 = SSM_WIDTH + POOL_WIDTH
    mixed_all = jnp.concatenate([_rms(y_ssm, gbr[:, :SSM_WIDTH]).astype(BF16),
                                 _rms(y_pool, gbr[:, SSM_WIDTH:s1]).astype(BF16),
                                 _rms(y_conv, gbr[:, s1:]).astype(BF16)], axis=1)
    x1 = x + jnp.dot(mixed_all, wout_ref[...], preferred_element_type=F32)
    x1_ref[...] = x1

    hn2 = _rms(x1, gffn_ref[...])
    packed = _pack_rows(hn2)
    hp_ref[0] = packed[:, :SC_ROW]
    hp_ref[1] = packed[:, SC_ROW:]
    logits = jnp.dot(hn2.astype(BF16), rw_ref[...], preferred_element_type=F32) + rb_ref[...]
    lane = lax.broadcasted_iota(I32, (TILE_T, LANES), 1)
    neg = jnp.float32(-jnp.inf)
    work = jnp.where(lane < N_EXPERTS, logits, neg)
    vals, idxs = [], []
    for _ in range(TOP_K):
        m = jnp.max(work, axis=1, keepdims=True)
        idx = jnp.min(jnp.where(work == m, lane, LANES), axis=1, keepdims=True)
        vals.append(m)
        idxs.append(idx)
        work = jnp.where(lane == idx, neg, work)
    exps = [jnp.exp(v - vals[0]) for v in vals]
    denom = exps[0] + exps[1] + exps[2] + exps[3]
    gates = [e / denom for e in exps]

    onehot = jnp.zeros((TILE_T, LANES), F32)
    for idx in idxs:
        onehot = jnp.where(lane == idx, 1.0, onehot)
    r_i = lax.broadcasted_iota(I32, (TILE_T, TILE_T), 0)
    c_i = lax.broadcasted_iota(I32, (TILE_T, TILE_T), 1)
    ltri = jnp.where(c_i < r_i, 1.0, 0.0).astype(BF16)
    before = jnp.dot(ltri, onehot.astype(BF16), preferred_element_type=F32) + cntacc_ref[0:1, :]
    slab = jnp.zeros((TILE_T, LANES), I32)
    for k in range(TOP_K):
        rank = jnp.sum(jnp.where(lane == idxs[k], before, 0.0), axis=1, keepdims=True).astype(I32)
        slab = jnp.where(lane == k, idxs[k], slab)
        slab = jnp.where(lane == TOP_K + k, lax.bitcast_convert_type(gates[k], I32), slab)
        slab = jnp.where(lane == 2 * TOP_K + k, rank, slab)
    slab_ref[...] = slab
    newcnt = cntacc_ref[0:1, :] + jnp.sum(onehot, axis=0, keepdims=True)
    cntacc_ref[...] = jnp.broadcast_to(newcnt, cntacc_ref.shape)
    cnt_ref[...] = jnp.broadcast_to(newcnt, cnt_ref.shape).astype(I32)


def _const_spec(shape):
    nd = len(shape)
    return pl.BlockSpec(shape, lambda b, l, nd=nd: (0,) * nd)


def _mixer_call(x, yprev, slabprev, lp, batch, seq):
    first_layer = yprev is None
    n_tok = batch * seq
    steps = seq // TILE_T
    tok_spec = lambda w: pl.BlockSpec((TILE_T, w), lambda b, l: (b * steps + l, 0))
    in_specs = [tok_spec(D_MODEL)]
    args = [x]
    if not first_layer:
        in_specs += [pl.BlockSpec((TOP_K, 2, TILE_T, SC_ROW), lambda b, l: (0, 0, b * steps + l, 0)), tok_spec(LANES)]
        args += [yprev, slabprev]
    weights = [lp['gmix'], lp['win'], lp['minw'], lp['mcat'], lp['coef'], lp['dskip'], lp['wglu'], lp['bglu'],
               lp['poolw'], lp['pscale'], lp['convw'], lp['gbr'], lp['wout'], lp['gffn'], lp['rw'], lp['rb']]
    in_specs += [_const_spec(w.shape) for w in weights]
    args += weights
    out_shape = [jax.ShapeDtypeStruct((n_tok, D_MODEL), F32),
                 jax.ShapeDtypeStruct((2, n_tok, SC_ROW), I32),
                 jax.ShapeDtypeStruct((n_tok, LANES), I32),
                 jax.ShapeDtypeStruct((SUBLANES, LANES), I32)]
    out_specs = [tok_spec(D_MODEL), pl.BlockSpec((2, TILE_T, SC_ROW), lambda b, l: (0, b * steps + l, 0)), tok_spec(LANES),
                 pl.BlockSpec((SUBLANES, LANES), lambda b, l: (0, 0))]
    scratch = [pltpu.VMEM((SSM_WIDTH // LANES, TILE_T, LANES), F32),
               pltpu.VMEM((SSM_WIDTH // LANES, TILE_T, LANES), F32),
               pltpu.VMEM((SUBLANES + N_CHUNK, SSM_GROUPS * LANES), F32),
               pltpu.VMEM((16 + TILE_T, POOL_WIDTH), F32),
               pltpu.VMEM((8 + TILE_T, CONV_WIDTH), F32),
               pltpu.VMEM((SUBLANES, LANES), F32)]
    return pl.pallas_call(
        functools.partial(_mixer_kernel, first_layer=first_layer),
        grid=(batch, steps),
        in_specs=in_specs, out_specs=out_specs, out_shape=out_shape, scratch_shapes=scratch,
        compiler_params=pltpu.CompilerParams(dimension_semantics=("arbitrary", "arbitrary"),
                                             vmem_limit_bytes=VMEM_LIMIT),
        name="mixer_first" if first_layer else "mixer_next",
    )(*args)


def _route_kernel(slab_ref, cnt_ref, dest_ref, meta_ref, *, n_blocks):
    lane1 = lax.broadcasted_iota(I32, (1, LANES), 1)
    counts = jnp.where(lane1 < N_EXPERTS, cnt_ref[0:1, :], 0)
    padded = ((counts + (EXPERT_BLOCK - 1)) // EXPERT_BLOCK) * EXPERT_BLOCK
    pend = padded
    for sh in (1, 2, 4, 8, 16):
        pend = pend + jnp.where(lane1 >= sh, pltpu.roll(pend, sh, axis=1), 0)
    pstart = pend - padded

    @pl.when(pl.program_id(0) == 0)
    def _():
        rows = meta_ref.shape[0]
        bstart = lax.broadcasted_iota(I32, (rows, LANES), 0) * EXPERT_BLOCK
        lane = lax.broadcasted_iota(I32, (rows, LANES), 1)
        done = jnp.where(jnp.logical_and(lane < N_EXPERTS, pend <= bstart), 1, 0)
        be = jnp.minimum(jnp.sum(done, axis=1, keepdims=True), N_EXPERTS - 1)
        total = jnp.sum(jnp.where(lane1 == N_EXPERTS - 1, pend, 0), axis=1, keepdims=True)
        row = lax.broadcasted_iota(I32, (rows, LANES), 0)
        meta_ref[...] = jnp.where(row == n_blocks, total // EXPERT_BLOCK, jnp.broadcast_to(be, (rows, LANES)))

    slab = slab_ref[...]
    lane = lax.broadcasted_iota(I32, slab.shape, 1)
    out = jnp.zeros(slab.shape, F32)
    for k in range(TOP_K):
        idx = slab[:, k:k + 1]
        rank = slab[:, 2 * TOP_K + k:2 * TOP_K + k + 1]
        base = jnp.sum(jnp.where(lane == idx, pstart, 0), axis=1, keepdims=True)
        for h in range(2):
            out = jnp.where(lane == 2 * k + h, (base + rank + h * n_blocks * EXPERT_BLOCK).astype(F32), out)
    dest_ref[...] = out.T[:SUBLANES, :].astype(I32)


def _route_call(slab, counts, n_tok, n_blocks):
    meta_rows = ((n_blocks + 1 + SUBLANES - 1) // SUBLANES) * SUBLANES
    return pl.pallas_call(
        functools.partial(_route_kernel, n_blocks=n_blocks),
        grid=(n_tok // ROUTE_T,),
        in_specs=[pl.BlockSpec((ROUTE_T, LANES), lambda i: (i, 0)),
                  pl.BlockSpec((SUBLANES, LANES), lambda i: (0, 0))],
        out_specs=[pl.BlockSpec((SUBLANES, ROUTE_T), lambda i: (0, i)),
                   pl.BlockSpec((meta_rows, LANES), lambda i: (0, 0))],
        out_shape=[jax.ShapeDtypeStruct((SUBLANES, n_tok), I32),
                   jax.ShapeDtypeStruct((meta_rows, LANES), I32)],
        compiler_params=pltpu.CompilerParams(dimension_semantics=("arbitrary",)),
        name="route",
    )(slab, counts)


def _sc_mesh():
    return plsc.VectorSubcoreMesh(core_axis_name="core", subcore_axis_name="subcore")


def _dispatch(src, idx, n_rows):
    windows = src.shape[0] // SC_WINDOW

    @functools.partial(pl.kernel, out_type=jax.ShapeDtypeStruct((n_rows, SC_ROW), src.dtype),
                       mesh=_sc_mesh(), scratch_types=[])
    def scatter_rows(x_hbm, i_hbm, o_hbm):
        def body(x_vmem, i_vmem):
            pltpu.sync_copy(x_vmem, o_hbm.at[i_vmem.at[0]])

        pltpu.emit_pipeline(
            body,
            grid=(idx.shape[1] // SC_WINDOW,),
            in_specs=[pl.BlockSpec((SC_WINDOW, SC_ROW), lambda i: (lax.rem(i, windows), 0)),
                      pl.BlockSpec((1, SC_WINDOW), lambda i: (0, i))],
            out_specs=[],
            core_axis_name=("core", "subcore"),
            dimension_semantics=(pltpu.PARALLEL,),
        )(x_hbm, i_hbm)

    return scatter_rows(src, idx)


def _combine(src, idx):
    n_idx = idx.shape[1]

    @functools.partial(pl.kernel, out_type=jax.ShapeDtypeStruct((n_idx, SC_ROW), src.dtype),
                       mesh=_sc_mesh(), scratch_types=[])
    def gather_rows(y_hbm, i_hbm, o_hbm):
        def body(i_vmem, o_vmem):
            pltpu.sync_copy(y_hbm.at[i_vmem.at[0]], o_vmem)

        pltpu.emit_pipeline(
            body,
            grid=(n_idx // SC_WINDOW,),
            in_specs=[pl.BlockSpec((1, SC_WINDOW), lambda i: (0, i))],
            out_specs=[pl.BlockSpec((SC_WINDOW, SC_ROW), lambda i: (i, 0))],
            core_axis_name=("core", "subcore"),
            dimension_semantics=(pltpu.PARALLEL,),
        )(i_hbm, o_hbm)

    return gather_rows(src, idx)


def _expert_kernel(be_ref, nv_ref, xs_ref, wgu_ref, bgu_ref, wdn_ref, bdn_ref, ys_ref, wgu_s, wdn_s, act_s):
    blk = pl.program_id(0)
    prev = be_ref[jnp.maximum(blk - 1, 0)]
    changed = jnp.logical_or(blk == 0, be_ref[blk] != prev)
    valid = blk < nv_ref[0]

    @pl.when(jnp.logical_and(valid, changed))
    def _():
        chunk = 128

        def cast_gu(i, _):
            rows = pl.ds(pl.multiple_of(i * chunk, chunk), chunk)
            wgu_s[rows, :] = wgu_ref[rows, :].astype(BF16)
            return 0

        def cast_dn(i, _):
            rows = pl.ds(pl.multiple_of(i * chunk, chunk), chunk)
            wdn_s[rows, :] = wdn_ref[rows, :].astype(BF16)
            return 0

        lax.fori_loop(0, D_MODEL // chunk, cast_gu, 0)
        lax.fori_loop(0, D_EXPERT // chunk, cast_dn, 0)

    @pl.when(valid)
    def _():
        w0 = xs_ref[0]
        w1 = xs_ref[1]
        xb = jnp.concatenate([_unpack_lo(w0).astype(BF16), _unpack_lo(w1).astype(BF16),
                              _unpack_hi(w0).astype(BF16), _unpack_hi(w1).astype(BF16)], axis=1)
        half = D_EXPERT // 2
        for j in range(2):
            cols = slice(j * half, (j + 1) * half)
            ucols = slice(D_EXPERT + j * half, D_EXPERT + (j + 1) * half)
            g = jnp.dot(xb, wgu_s[:, cols], preferred_element_type=F32) + bgu_ref[:, cols]
            up = jnp.dot(xb, wgu_s[:, ucols], preferred_element_type=F32) + bgu_ref[:, ucols]
            g = jnp.minimum(g, SWIGLU_LIMIT)
            up = jnp.clip(up, -SWIGLU_LIMIT, SWIGLU_LIMIT)
            act_s[:, cols] = (g * jax.nn.sigmoid(SWIGLU_ALPHA * g) * (up + 1.0)).astype(BF16)
        y = jnp.dot(act_s[...], wdn_s[...], preferred_element_type=F32) + bdn_ref[...]
        packed = _pack_rows(y)
        ys_ref[0] = packed[:, :SC_ROW]
        ys_ref[1] = packed[:, SC_ROW:]


def _expert_call(xs, block_e, n_valid, w_gu, b_gu, w_dn, b_dn, layer, n_blocks):
    def row_map(i, be, nv):
        return (0, jnp.minimum(i, jnp.maximum(nv[0] - 1, 0)), 0)

    def w_map(i, be, nv):
        return (layer, be[i], 0, 0)

    grid_spec = pltpu.PrefetchScalarGridSpec(
        num_scalar_prefetch=2,
        grid=(n_blocks,),
        in_specs=[pl.BlockSpec((2, EXPERT_BLOCK, SC_ROW), row_map),
                  pl.BlockSpec((None, None, D_MODEL, 2 * D_EXPERT), w_map),
                  pl.BlockSpec((None, None, 1, 2 * D_EXPERT), w_map),
                  pl.BlockSpec((None, None, D_EXPERT, D_MODEL), w_map),
                  pl.BlockSpec((None, None, 1, D_MODEL), w_map)],
        out_specs=pl.BlockSpec((2, EXPERT_BLOCK, SC_ROW), row_map),
        scratch_shapes=[pltpu.VMEM((D_MODEL, 2 * D_EXPERT), BF16),
                        pltpu.VMEM((D_EXPERT, D_MODEL), BF16),
                        pltpu.VMEM((EXPERT_BLOCK, D_EXPERT), BF16)])
    return pl.pallas_call(
        _expert_kernel,
        grid_spec=grid_spec,
        out_shape=jax.ShapeDtypeStruct((2, n_blocks * EXPERT_BLOCK, SC_ROW), I32),
        compiler_params=pltpu.CompilerParams(dimension_semantics=("arbitrary",),
                                             vmem_limit_bytes=VMEM_LIMIT),
        name="experts",
    )(block_e, n_valid, xs, w_gu, b_gu, w_dn, b_dn)


def _final_kernel(x_ref, y_ref, slab_ref, g_ref, o_ref):
    x = _moe_combine(x_ref[...], y_ref, slab_ref[...])
    o_ref[...] = _rms(x, g_ref[...])


def _final_call(x1, ysg, slab, g, n_tok):
    t = TILE_T
    return pl.pallas_call(
        _final_kernel,
        grid=(n_tok // t,),
        in_specs=[pl.BlockSpec((t, D_MODEL), lambda i: (i, 0)),
                  pl.BlockSpec((TOP_K, 2, t, SC_ROW), lambda i: (0, 0, i, 0)),
                  pl.BlockSpec((t, LANES), lambda i: (i, 0)),
                  pl.BlockSpec((1, D_MODEL), lambda i: (0, 0))],
        out_specs=pl.BlockSpec((t, D_MODEL), lambda i: (i, 0)),
        out_shape=jax.ShapeDtypeStruct((n_tok, D_MODEL), F32),
        compiler_params=pltpu.CompilerParams(dimension_semantics=("arbitrary",)),
        name="final_norm",
    )(x1, ysg, slab, g)


def _s5_tables(lam_re, lam_im, b_re, b_im, c_re, c_im, log_step):
    lr = jnp.minimum(lam_re.astype(F32), LAMBDA_RE_MAX)
    li = lam_im.astype(F32)
    step = jnp.exp(log_step.astype(F32))[:, None]
    mag = jnp.exp(lr * step)
    ar = mag * jnp.cos(li * step)
    ai = mag * jnp.sin(li * step)
    nr = ar - 1.0
    den = lr * lr + li * li
    kr = (nr * lr + ai * li) / den
    ki = (ai * lr - nr * li) / den
    bre = b_re.astype(F32)
    bim = b_im.astype(F32)
    bbar_r = kr[..., None] * bre - ki[..., None] * bim
    bbar_i = kr[..., None] * bim + ki[..., None] * bre
    cre = c_re.astype(F32)
    cim = c_im.astype(F32)

    def power(m):
        m = jnp.asarray(m, F32)[..., None, None]
        mg = jnp.exp(m * lr * step)
        return mg * jnp.cos(m * li * step), mg * jnp.sin(m * li * step)

    pw_r, pw_i = power(jnp.arange(CHUNK + 1))
    cp_r = cre[None] * pw_r[:, :, None, :] - cim[None] * pw_i[:, :, None, :]
    cp_i = cre[None] * pw_i[:, :, None, :] + cim[None] * pw_r[:, :, None, :]
    kmat = (jnp.einsum('tgap,gph->tgah', cp_r[:CHUNK], bbar_r) - jnp.einsum('tgap,gph->tgah', cp_i[:CHUNK], bbar_i))
    lag = jnp.arange(CHUNK)[None, :] - jnp.arange(CHUNK)[:, None]
    ktoe = jnp.where((lag >= 0)[:, :, None, None, None], kmat[jnp.clip(lag, 0, CHUNK - 1)], 0.0)
    m_intra = ktoe.transpose(2, 0, 4, 1, 3).reshape(SSM_GROUPS, CHUNK * SSM_GROUP, CHUNK * SSM_GROUP)
    mo_r = cp_r[1:].transpose(1, 3, 0, 2).reshape(SSM_GROUPS, SSM_STATE, CHUNK * SSM_GROUP)
    mo_i = -cp_i[1:].transpose(1, 3, 0, 2).reshape(SSM_GROUPS, SSM_STATE, CHUNK * SSM_GROUP)
    mcat = jnp.concatenate([m_intra, mo_r, mo_i], axis=1).astype(BF16)
    q_r = pw_r[CHUNK - 1::-1][:CHUNK][:, :, :, None]
    q_i = pw_i[CHUNK - 1::-1][:CHUNK][:, :, :, None]
    mn_r = q_r * bbar_r[None] - q_i * bbar_i[None]
    mn_i = q_r * bbar_i[None] + q_i * bbar_r[None]
    minw = jnp.concatenate([mn_r, mn_i], axis=2).transpose(1, 0, 3, 2).reshape(
        SSM_GROUPS, CHUNK * SSM_GROUP, 2 * SSM_STATE).astype(BF16)

    def packed(j):
        pr, pi = power(float(CHUNK * j))
        return (jnp.concatenate([pr, pr], axis=-1).reshape(1, SSM_GROUPS * LANES),
                jnp.concatenate([-pi, pi], axis=-1).reshape(1, SSM_GROUPS * LANES))

    rows = jnp.arange(SUBLANES)[:, None]
    kinds = []
    for d in (1, 2, 4):
        c1, c2 = packed(d)
        mask = (rows >= d).astype(F32)
        kinds += [mask * c1, mask * c2]
    pk = [packed(r + 1) for r in range(SUBLANES)]
    kinds += [jnp.concatenate([p[0] for p in pk], axis=0), jnp.concatenate([p[1] for p in pk], axis=0)]
    coef = jnp.stack(kinds, axis=0)
    return minw, mcat, coef


def _layer_params(l, norm_mix, w_in, ssm_lam_re, ssm_lam_im, ssm_b_re, ssm_b_im, ssm_c_re, ssm_c_im, ssm_d,
                  ssm_log_step, ssm_w_glu, ssm_b_glu, pool_w, pool_scale, conv_w, branch_norm, w_out, norm_ffn,
                  router_w, router_b):
    minw, mcat, coef = _s5_tables(ssm_lam_re[l], ssm_lam_im[l], ssm_b_re[l], ssm_b_im[l], ssm_c_re[l], ssm_c_im[l],
                                  ssm_log_step[l])
    eye4 = jnp.eye(len(POOL_WINDOWS), dtype=F32)
    poolw = jnp.einsum('gcd,gk->gckd', pool_w[l].astype(F32), eye4).reshape(POOL_WIDTH, POOL_WIDTH).astype(BF16)
    rw = jnp.zeros((D_MODEL, LANES), F32).at[:, :N_EXPERTS].set(router_w[l]).astype(BF16)
    rb = jnp.zeros((1, LANES), F32).at[0, :N_EXPERTS].set(router_b[l])
    row = lambda v: v.reshape(1, -1).astype(F32)
    return dict(gmix=row(norm_mix[l]), win=w_in[l].astype(BF16), minw=minw, mcat=mcat, coef=coef, dskip=row(ssm_d[l]),
                wglu=ssm_w_glu[l].astype(BF16), bglu=row(ssm_b_glu[l]), poolw=poolw, pscale=row(pool_scale[l]),
                convw=conv_w[l].astype(F32), gbr=row(branch_norm[l]), wout=w_out[l].astype(BF16),
                gffn=row(norm_ffn[l]), rw=rw, rb=rb)


def kernel(x, norm_mix, w_in, ssm_lam_re, ssm_lam_im, ssm_b_re, ssm_b_im, ssm_c_re, ssm_c_im, ssm_d, ssm_log_step, ssm_w_glu, ssm_b_glu, pool_w, pool_scale, conv_w, branch_norm, w_out, norm_ffn, router_w, router_b, w_gate_up, b_gate_up, w_down, b_down, final_norm):
    batch, seq, d_model = x.shape
    assert d_model == D_MODEL and seq % TILE_T == 0
    depth = w_in.shape[0]
    n_tok = batch * seq
    assert n_tok % ROUTE_T == 0 and n_tok % SC_WINDOW == 0
    n_blocks = -(-(n_tok * TOP_K + N_EXPERTS * (EXPERT_BLOCK - 1)) // EXPERT_BLOCK)
    b_gu = b_gate_up.reshape(depth, N_EXPERTS, 1, 2 * D_EXPERT)
    b_dn = b_down.reshape(depth, N_EXPERTS, 1, D_MODEL)

    xf = x.reshape(n_tok, D_MODEL)
    ysg = None
    slab = None
    for l in range(depth):
        lp = _layer_params(l, norm_mix, w_in, ssm_lam_re, ssm_lam_im, ssm_b_re, ssm_b_im, ssm_c_re, ssm_c_im, ssm_d,
                           ssm_log_step, ssm_w_glu, ssm_b_glu, pool_w, pool_scale, conv_w, branch_norm, w_out,
                           norm_ffn, router_w, router_b)
        xf, hp, slab, counts = _mixer_call(xf, ysg, slab, lp, batch, seq)
        dest, meta = _route_call(slab, counts, n_tok, n_blocks)
        idx = dest.reshape(1, 2 * TOP_K * n_tok)
        block_e = meta[:n_blocks, 0]
        n_valid = meta[n_blocks:n_blocks + 1, 0]
        n_rows = n_blocks * EXPERT_BLOCK
        xs = _dispatch(hp.reshape(2 * n_tok, SC_ROW), idx, 2 * n_rows).reshape(2, n_rows, SC_ROW)
        ys = _expert_call(xs, block_e, n_valid, w_gate_up, b_gu, w_down, b_dn, l, n_blocks)
        ysg = _combine(ys.reshape(2 * n_rows, SC_ROW), idx).reshape(TOP_K, 2, n_tok, SC_ROW)
    out = _final_call(xf, ysg, slab, final_norm.reshape(1, D_MODEL).astype(F32), n_tok)
    return out.reshape(batch, seq, D_MODEL)
```

```python
import functools
import math

import jax
import jax.numpy as jnp
from jax import lax
from jax.experimental import pallas as pl
from jax.experimental.pallas import tpu as pltpu
from jax.experimental.pallas import tpu_sc as plsc

F32 = jnp.float32
BF16 = jnp.bfloat16
I32 = jnp.int32

D_MODEL = 1024
SSM_WIDTH = 512
POOL_WIDTH = 256
CONV_WIDTH = 256
SSM_GROUP = 16
SSM_GROUPS = 32
SSM_STATE = 64
LAMBDA_RE_MAX = -1e-4
POOL_WINDOWS = (2, 4, 8, 16)
POOL_GROUP = 64
CONV_K = 3
IN_WIDTH = SSM_WIDTH + POOL_WIDTH + 3 * CONV_WIDTH
N_EXPERTS = 32
TOP_K = 4
D_EXPERT = 1024
SWIGLU_LIMIT = 7.0
SWIGLU_ALPHA = 1.702
EXPERT_BLOCK = 256
NORM_EPS = 1e-5

LANES = 128
SUBLANES = 8
TILE_T = 512
CHUNK = 16
N_CHUNK = TILE_T // CHUNK
PACKED = D_MODEL // 2
ROUTE_T = 2048
SC_WINDOW = 128
SC_ROW = PACKED // 2
HI_MASK = -65536
VMEM_LIMIT = 56 * 1024 * 1024


def _bf16_round(v):
    return v.astype(BF16).astype(F32)


def _pack_rows(v):
    lo = lax.shift_right_logical(lax.bitcast_convert_type(_bf16_round(v[:, :PACKED]), I32), 16)
    hi = lax.bitcast_convert_type(_bf16_round(v[:, PACKED:]), I32) & HI_MASK
    return hi | lo


def _unpack_lo(w):
    return lax.bitcast_convert_type(lax.shift_left(w, 16), F32)


def _unpack_hi(w):
    return lax.bitcast_convert_type(w & HI_MASK, F32)


def _rms(v, g):
    r = lax.rsqrt(jnp.mean(v * v, axis=-1, keepdims=True) + NORM_EPS)
    return (v * r) * g


def _moe_combine(x, y_ref, slab):
    parts = [x[:, i * SC_ROW:(i + 1) * SC_ROW] for i in range(4)]
    for k in range(TOP_K):
        g = lax.bitcast_convert_type(slab[:, TOP_K + k:TOP_K + k + 1], F32)
        for h in range(2):
            w = y_ref[k, h]
            parts[h] = parts[h] + g * _unpack_lo(w)
            parts[2 + h] = parts[2 + h] + g * _unpack_hi(w)
    return jnp.concatenate(parts, axis=1)


def _block_transpose8(vs, lane):
    vs = list(vs)
    for dist in (4, 2, 1):
        width = SSM_GROUP * dist
        low = (lane % (2 * width)) < width
        for j in range(8):
            if j & dist:
                continue
            a, b = vs[j], vs[j + dist]
            vs[j] = jnp.where(low, a, pltpu.roll(b, width, axis=1))
            vs[j + dist] = jnp.where(low, pltpu.roll(a, LANES - width, axis=1), b)
    return vs


def _cmul_packed(c1, c2, v):
    return c1 * v + c2 * pltpu.roll(v, SSM_STATE, axis=1)


def _chunk_scan(s, carry_in, cf):
    outs = []
    carry = jnp.broadcast_to(carry_in, (SUBLANES, LANES))
    for rg in range(N_CHUNK // SUBLANES):
        x = s[rg * SUBLANES:(rg + 1) * SUBLANES]
        for j, d in enumerate((1, 2, 4)):
            x = x + _cmul_packed(cf[2 * j], cf[2 * j + 1], pltpu.roll(x, d, axis=0))
        x = x + _cmul_packed(cf[6], cf[7], carry)
        outs.append(x)
        carry = jnp.broadcast_to(x[SUBLANES - 1:SUBLANES], (SUBLANES, LANES))
    return jnp.concatenate(outs, axis=0)


def _s5_mixer(u, first, minw_ref, mcat_ref, coef_ref, u_s, y_s, h_s):
    blocks = SSM_WIDTH // LANES
    for b4 in range(blocks):
        u_s[b4] = u[:, b4 * LANES:(b4 + 1) * LANES]
    lane = lax.broadcasted_iota(I32, (N_CHUNK, LANES), 1)

    halves = [[None, None] for _ in range(SSM_GROUPS)]
    for b4 in range(blocks):
        for hh in range(2):
            xs = [u_s[b4, pl.ds(8 * hh + j, N_CHUNK, stride=CHUNK), :].astype(BF16) for j in range(8)]
            ws = _block_transpose8(xs, lane)
            for gl in range(8):
                halves[8 * b4 + gl][hh] = ws[gl]
    ug = [jnp.concatenate(h2, axis=1) for h2 in halves]

    carry_in = jnp.where(first, 0.0, h_s[N_CHUNK + 7:N_CHUNK + 8, :])
    h_s[7:8, :] = carry_in
    for g in range(SSM_GROUPS):
        cols = slice(g * LANES, (g + 1) * LANES)
        s_g = jnp.dot(ug[g], minw_ref[g], preferred_element_type=F32)
        cf = [coef_ref[k, :, cols] for k in range(8)]
        h_s[8:8 + N_CHUNK, cols] = _chunk_scan(s_g, carry_in[:, cols], cf)

    yg = []
    for g in range(SSM_GROUPS):
        hprev = h_s[7:7 + N_CHUNK, g * LANES:(g + 1) * LANES].astype(BF16)
        lhs = jnp.concatenate([ug[g], hprev], axis=1)
        yg.append(jnp.dot(lhs, mcat_ref[g], preferred_element_type=F32).astype(BF16))

    for b4 in range(blocks):
        for hh in range(2):
            ws = [yg[8 * b4 + gl][:, hh * LANES:(hh + 1) * LANES] for gl in range(8)]
            xs = _block_transpose8(ws, lane)
            for j in range(8):
                y_s[b4, pl.ds(8 * hh + j, N_CHUNK, stride=CHUNK), :] = xs[j].astype(F32)
    return jnp.concatenate([y_s[b4] for b4 in range(blocks)], axis=1)


def _mixer_kernel(*refs, first_layer):
    if first_layer:
        x_ref = refs[0]
        refs = refs[1:]
    else:
        x_ref, yprev_ref, slabprev_ref = refs[:3]
        refs = refs[3:]
    (gmix_ref, win_ref, minw_ref, mcat_ref, coef_ref, dskip_ref, wglu_ref, bglu_ref, poolw_ref, pscale_ref,
     convw_ref, gbr_ref, wout_ref, gffn_ref, rw_ref, rb_ref,
     x1_ref, hp_ref, slab_ref, cnt_ref,
     u_s, y_s, h_s, pbuf_ref, zbuf_ref, cntacc_ref) = refs

    b = pl.program_id(0)
    l = pl.program_id(1)
    seq_start = l == 0

    @pl.when(jnp.logical_and(b == 0, l == 0))
    def _():
        cntacc_ref[...] = jnp.zeros_like(cntacc_ref)

    if first_layer:
        x = x_ref[...]
    else:
        x = _moe_combine(x_ref[...], yprev_ref, slabprev_ref[...])

    hn = _rms(x, gmix_ref[...]).astype(BF16)
    proj = jnp.dot(hn, win_ref[...], preferred_element_type=F32)
    u = proj[:, :SSM_WIDTH]
    p = proj[:, SSM_WIDTH:SSM_WIDTH + POOL_WIDTH]
    c0 = SSM_WIDTH + POOL_WIDTH
    bg = proj[:, c0:c0 + CONV_WIDTH]
    cg = proj[:, c0 + CONV_WIDTH:c0 + 2 * CONV_WIDTH]
    hv = proj[:, c0 + 2 * CONV_WIDTH:]

    y = _s5_mixer(u, seq_start, minw_ref, mcat_ref, coef_ref, u_s, y_s, h_s) + dskip_ref[...] * u
    z = jax.nn.gelu(y)
    glu = jnp.dot(z.astype(BF16), wglu_ref[...], preferred_element_type=F32) + bglu_ref[...]
    y_ssm = z * jax.nn.sigmoid(glu)

    tail = 16
    pbuf_ref[0:tail, :] = jnp.where(seq_start, 0.0, pbuf_ref[TILE_T:TILE_T + tail, :])
    pbuf_ref[tail:, :] = p
    lane_p = lax.broadcasted_iota(I32, (TILE_T, POOL_WIDTH), 1)
    win = jnp.where(lane_p < 64, 2, jnp.where(lane_p < 128, 4, jnp.where(lane_p < 192, 8, 16)))
    total = p
    for d in range(1, 16):
        total = total + jnp.where(win > d, pbuf_ref[pl.ds(tail - d, TILE_T), :], 0.0)
    row_p = lax.broadcasted_iota(I32, (TILE_T, POOL_WIDTH), 0) + l * TILE_T
    count = jnp.minimum(row_p + 1, win).astype(F32)
    pooled = total / count - p
    mixed = jnp.dot(pooled.astype(BF16), poolw_ref[...], preferred_element_type=F32)
    y_pool = mixed * pscale_ref[...]

    zc = cg * hv
    zbuf_ref[0:8, :] = jnp.where(seq_start, 0.0, zbuf_ref[TILE_T:TILE_T + 8, :])
    zbuf_ref[8:, :] = zc
    yc = (convw_ref[0:1, :] * zbuf_ref[pl.ds(6, TILE_T), :]
          + convw_ref[1:2, :] * zbuf_ref[pl.ds(7, TILE_T), :]
          + convw_ref[2:3, :] * zc)
    y_conv = bg * yc

    gbr = gbr_ref[...]
    s1 = SSM_WIDTH + POOL_WIDTH
    mixed_all = jnp.concatenate([_rms(y_ssm, gbr[:, :SSM_WIDTH]).astype(BF16),
                                 _rms(y_pool, gbr[:, SSM_WIDTH:s1]).astype(BF16),
                                 _rms(y_conv, gbr[:, s1:]).astype(BF16)], axis=1)
    x1 = x + jnp.dot(mixed_all, wout_ref[...], preferred_element_type=F32)
    x1_ref[...] = x1

    hn2 = _rms(x1, gffn_ref[...])
    packed = _pack_rows(hn2)
    hp_ref[0] = packed[:, :SC_ROW]
    hp_ref[1] = packed[:, SC_ROW:]
    logits = jnp.dot(hn2.astype(BF16), rw_ref[...], preferred_element_type=F32) + rb_ref[...]
    lane = lax.broadcasted_iota(I32, (TILE_T, LANES), 1)
    neg = jnp.float32(-jnp.inf)
    work = jnp.where(lane < N_EXPERTS, logits, neg)
    vals, idxs = [], []
    for _ in range(TOP_K):
        m = jnp.max(work, axis=1, keepdims=True)
        idx = jnp.min(jnp.where(work == m, lane, LANES), axis=1, keepdims=True)
        vals.append(m)
        idxs.append(idx)
        work = jnp.where(lane == idx, neg, work)
    exps = [jnp.exp(v - vals[0]) for v in vals]
    denom = exps[0] + exps[1] + exps[2] + exps[3]
    gates = [e / denom for e in exps]

    onehot = jnp.zeros((TILE_T, LANES), F32)
    for idx in idxs:
        onehot = jnp.where(lane == idx, 1.0, onehot)
    r_i = lax.broadcasted_iota(I32, (TILE_T, TILE_T), 0)
    c_i = lax.broadcasted_iota(I32, (TILE_T, TILE_T), 1)
    ltri = jnp.where(c_i < r_i, 1.0, 0.0).astype(BF16)
    before = jnp.dot(ltri, onehot.astype(BF16), preferred_element_type=F32) + cntacc_ref[0:1, :]
    slab = jnp.zeros((TILE_T, LANES), I32)
    for k in range(TOP_K):
        rank = jnp.sum(jnp.where(lane == idxs[k], before, 0.0), axis=1, keepdims=True).astype(I32)
        slab = jnp.where(lane == k, idxs[k], slab)
        slab = jnp.where(lane == TOP_K + k, lax.bitcast_convert_type(gates[k], I32), slab)
        slab = jnp.where(lane == 2 * TOP_K + k, rank, slab)
    slab_ref[...] = slab
    newcnt = cntacc_ref[0:1, :] + jnp.sum(onehot, axis=0, keepdims=True)
    cntacc_ref[...] = jnp.broadcast_to(newcnt, cntacc_ref.shape)
    cnt_ref[...] = jnp.broadcast_to(newcnt, cnt_ref.shape).astype(I32)


def _const_spec(shape):
    nd = len(shape)
    return pl.BlockSpec(shape, lambda b, l, nd=nd: (0,) * nd)


def _mixer_call(x, x_tile0, yprev, slabprev, lp, batch, seq):
    first_layer = yprev is None
    n_tok = batch * seq
    steps = seq // TILE_T
    tok_spec = lambda w: pl.BlockSpec((TILE_T, w), lambda b, l: (b * steps + l, 0))
    in_specs = [pl.BlockSpec((TILE_T, D_MODEL), lambda b, l: (x_tile0 + b * steps + l, 0))]
    args = [x]
    if not first_layer:
        in_specs += [pl.BlockSpec((TOP_K, 2, TILE_T, SC_ROW), lambda b, l: (0, 0, b * steps + l, 0)), tok_spec(LANES)]
        args += [yprev, slabprev]
    weights = [lp['gmix'], lp['win'], lp['minw'], lp['mcat'], lp['coef'], lp['dskip'], lp['wglu'], lp['bglu'],
               lp['poolw'], lp['pscale'], lp['convw'], lp['gbr'], lp['wout'], lp['gffn'], lp['rw'], lp['rb']]
    in_specs += [_const_spec(w.shape) for w in weights]
    args += weights
    out_shape = [jax.ShapeDtypeStruct((n_tok, D_MODEL), F32),
                 jax.ShapeDtypeStruct((2, n_tok, SC_ROW), I32),
                 jax.ShapeDtypeStruct((n_tok, LANES), I32),
                 jax.ShapeDtypeStruct((SUBLANES, LANES), I32)]
    out_specs = [tok_spec(D_MODEL), pl.BlockSpec((2, TILE_T, SC_ROW), lambda b, l: (0, b * steps + l, 0)), tok_spec(LANES),
                 pl.BlockSpec((SUBLANES, LANES), lambda b, l: (0, 0))]
    scratch = [pltpu.VMEM((SSM_WIDTH // LANES, TILE_T, LANES), F32),
               pltpu.VMEM((SSM_WIDTH // LANES, TILE_T, LANES), F32),
               pltpu.VMEM((SUBLANES + N_CHUNK, SSM_GROUPS * LANES), F32),
               pltpu.VMEM((16 + TILE_T, POOL_WIDTH), F32),
               pltpu.VMEM((8 + TILE_T, CONV_WIDTH), F32),
               pltpu.VMEM((SUBLANES, LANES), F32)]
    return pl.pallas_call(
        functools.partial(_mixer_kernel, first_layer=first_layer),
        grid=(batch, steps),
        in_specs=in_specs, out_specs=out_specs, out_shape=out_shape, scratch_shapes=scratch,
        compiler_params=pltpu.CompilerParams(dimension_semantics=("arbitrary", "arbitrary"),
                                             vmem_limit_bytes=VMEM_LIMIT),
        name="mixer_first" if first_layer else "mixer_next",
    )(*args)


def _route_kernel(slab_ref, cnt_ref, dest_ref, meta_ref, *, n_blocks):
    lane1 = lax.broadcasted_iota(I32, (1, LANES), 1)
    counts = jnp.where(lane1 < N_EXPERTS, cnt_ref[0:1, :], 0)
    padded = ((counts + (EXPERT_BLOCK - 1)) // EXPERT_BLOCK) * EXPERT_BLOCK
    pend = padded
    for sh in (1, 2, 4, 8, 16):
        pend = pend + jnp.where(lane1 >= sh, pltpu.roll(pend, sh, axis=1), 0)
    pstart = pend - padded

    @pl.when(pl.program_id(0) == 0)
    def _():
        rows = meta_ref.shape[0]
        bstart = lax.broadcasted_iota(I32, (rows, LANES), 0) * EXPERT_BLOCK
        lane = lax.broadcasted_iota(I32, (rows, LANES), 1)
        done = jnp.where(jnp.logical_and(lane < N_EXPERTS, pend <= bstart), 1, 0)
        be = jnp.minimum(jnp.sum(done, axis=1, keepdims=True), N_EXPERTS - 1)
        total = jnp.sum(jnp.where(lane1 == N_EXPERTS - 1, pend, 0), axis=1, keepdims=True)
        row = lax.broadcasted_iota(I32, (rows, LANES), 0)
        meta_ref[...] = jnp.where(row == n_blocks, total // EXPERT_BLOCK, jnp.broadcast_to(be, (rows, LANES)))

    slab = slab_ref[...]
    lane = lax.broadcasted_iota(I32, slab.shape, 1)
    out = jnp.zeros(slab.shape, F32)
    for k in range(TOP_K):
        idx = slab[:, k:k + 1]
        rank = slab[:, 2 * TOP_K + k:2 * TOP_K + k + 1]
        base = jnp.sum(jnp.where(lane == idx, pstart, 0), axis=1, keepdims=True)
        for h in range(2):
            out = jnp.where(lane == 2 * k + h, (base + rank + h * n_blocks * EXPERT_BLOCK).astype(F32), out)
    dest_ref[...] = out.T[:SUBLANES, :].astype(I32)


def _route_call(slab, counts, n_tok, n_blocks):
    meta_rows = ((n_blocks + 1 + SUBLANES - 1) // SUBLANES) * SUBLANES
    return pl.pallas_call(
        functools.partial(_route_kernel, n_blocks=n_blocks),
        grid=(n_tok // ROUTE_T,),
        in_specs=[pl.BlockSpec((ROUTE_T, LANES), lambda i: (i, 0)),
                  pl.BlockSpec((SUBLANES, LANES), lambda i: (0, 0))],
        out_specs=[pl.BlockSpec((SUBLANES, ROUTE_T), lambda i: (0, i)),
                   pl.BlockSpec((meta_rows, LANES), lambda i: (0, 0))],
        out_shape=[jax.ShapeDtypeStruct((SUBLANES, n_tok), I32),
                   jax.ShapeDtypeStruct((meta_rows, LANES), I32)],
        compiler_params=pltpu.CompilerParams(dimension_semantics=("arbitrary",)),
        name="route",
    )(slab, counts)


def _sc_mesh():
    return plsc.VectorSubcoreMesh(core_axis_name="core", subcore_axis_name="subcore")


def _dispatch(src, idx, n_rows):
    windows = src.shape[0] // SC_WINDOW

    @functools.partial(pl.kernel, out_type=jax.ShapeDtypeStruct((n_rows, SC_ROW), src.dtype),
                       mesh=_sc_mesh(), scratch_types=[])
    def scatter_rows(x_hbm, i_hbm, o_hbm):
        def body(x_vmem, i_vmem):
            pltpu.sync_copy(x_vmem, o_hbm.at[i_vmem.at[0]])

        pltpu.emit_pipeline(
            body,
            grid=(idx.shape[1] // SC_WINDOW,),
            in_specs=[pl.BlockSpec((SC_WINDOW, SC_ROW), lambda i: (lax.rem(i, windows), 0)),
                      pl.BlockSpec((1, SC_WINDOW), lambda i: (0, i))],
            out_specs=[],
            core_axis_name=("core", "subcore"),
            dimension_semantics=(pltpu.PARALLEL,),
        )(x_hbm, i_hbm)

    return scatter_rows(src, idx)


def _combine(src, idx):
    n_idx = idx.shape[1]

    @functools.partial(pl.kernel, out_type=jax.ShapeDtypeStruct((n_idx, SC_ROW), src.dtype),
                       mesh=_sc_mesh(), scratch_types=[])
    def gather_rows(y_hbm, i_hbm, o_hbm):
        def body(i_vmem, o_vmem):
            pltpu.sync_copy(y_hbm.at[i_vmem.at[0]], o_vmem)

        pltpu.emit_pipeline(
            body,
            grid=(n_idx // SC_WINDOW,),
            in_specs=[pl.BlockSpec((1, SC_WINDOW), lambda i: (0, i))],
            out_specs=[pl.BlockSpec((SC_WINDOW, SC_ROW), lambda i: (i, 0))],
            core_axis_name=("core", "subcore"),
            dimension_semantics=(pltpu.PARALLEL,),
        )(i_hbm, o_hbm)

    return gather_rows(src, idx)


def _expert_kernel(be_ref, nv_ref, xs_ref, wgu_ref, bgu_ref, wdn_ref, bdn_ref, ys_ref, wgu_s, wdn_s, act_s):
    blk = pl.program_id(0)
    prev = be_ref[jnp.maximum(blk - 1, 0)]
    changed = jnp.logical_or(blk == 0, be_ref[blk] != prev)
    valid = blk < nv_ref[0]

    @pl.when(jnp.logical_and(valid, changed))
    def _():
        chunk = 128

        def cast_gu(i, _):
            rows = pl.ds(pl.multiple_of(i * chunk, chunk), chunk)
            wgu_s[rows, :] = wgu_ref[rows, :].astype(BF16)
            return 0

        def cast_dn(i, _):
            rows = pl.ds(pl.multiple_of(i * chunk, chunk), chunk)
            wdn_s[rows, :] = wdn_ref[rows, :].astype(BF16)
            return 0

        lax.fori_loop(0, D_MODEL // chunk, cast_gu, 0)
        lax.fori_loop(0, D_EXPERT // chunk, cast_dn, 0)

    @pl.when(valid)
    def _():
        w0 = xs_ref[0]
        w1 = xs_ref[1]
        xb = jnp.concatenate([_unpack_lo(w0).astype(BF16), _unpack_lo(w1).astype(BF16),
                              _unpack_hi(w0).astype(BF16), _unpack_hi(w1).astype(BF16)], axis=1)
        half = D_EXPERT // 2
        for j in range(2):
            cols = slice(j * half, (j + 1) * half)
            ucols = slice(D_EXPERT + j * half, D_EXPERT + (j + 1) * half)
            g = jnp.dot(xb, wgu_s[:, cols], preferred_element_type=F32) + bgu_ref[:, cols]
            up = jnp.dot(xb, wgu_s[:, ucols], preferred_element_type=F32) + bgu_ref[:, ucols]
            g = jnp.minimum(g, SWIGLU_LIMIT)
            up = jnp.clip(up, -SWIGLU_LIMIT, SWIGLU_LIMIT)
            act_s[:, cols] = (g * jax.nn.sigmoid(SWIGLU_ALPHA * g) * (up + 1.0)).astype(BF16)
        y = jnp.dot(act_s[...], wdn_s[...], preferred_element_type=F32) + bdn_ref[...]
        packed = _pack_rows(y)
        ys_ref[0] = packed[:, :SC_ROW]
        ys_ref[1] = packed[:, SC_ROW:]


def _expert_call(xs, block_e, n_valid, w_gu, b_gu, w_dn, b_dn, layer, n_blocks):
    def row_map(i, be, nv):
        return (0, jnp.minimum(i, jnp.maximum(nv[0] - 1, 0)), 0)

    def w_map(i, be, nv):
        return (layer, be[i], 0, 0)

    grid_spec = pltpu.PrefetchScalarGridSpec(
        num_scalar_prefetch=2,
        grid=(n_blocks,),
        in_specs=[pl.BlockSpec((2, EXPERT_BLOCK, SC_ROW), row_map),
                  pl.BlockSpec((None, None, D_MODEL, 2 * D_EXPERT), w_map),
                  pl.BlockSpec((None, None, 1, 2 * D_EXPERT), w_map),
                  pl.BlockSpec((None, None, D_EXPERT, D_MODEL), w_map),
                  pl.BlockSpec((None, None, 1, D_MODEL), w_map)],
        out_specs=pl.BlockSpec((2, EXPERT_BLOCK, SC_ROW), row_map),
        scratch_shapes=[pltpu.VMEM((D_MODEL, 2 * D_EXPERT), BF16),
                        pltpu.VMEM((D_EXPERT, D_MODEL), BF16),
                        pltpu.VMEM((EXPERT_BLOCK, D_EXPERT), BF16)])
    return pl.pallas_call(
        _expert_kernel,
        grid_spec=grid_spec,
        out_shape=jax.ShapeDtypeStruct((2, n_blocks * EXPERT_BLOCK, SC_ROW), I32),
        compiler_params=pltpu.CompilerParams(dimension_semantics=("arbitrary",),
                                             vmem_limit_bytes=VMEM_LIMIT),
        name="experts",
    )(block_e, n_valid, xs, w_gu, b_gu, w_dn, b_dn)


def _final_kernel(x_ref, y_ref, slab_ref, g_ref, *rest):
    o_ref = rest[-1]
    x = _moe_combine(x_ref[...], y_ref, slab_ref[...])
    o_ref[...] = _rms(x, g_ref[...])


def _final_call(x1, ysg, slab, g, out_tile0, n_out, out_prev):
    t = TILE_T
    in_specs = [pl.BlockSpec((t, D_MODEL), lambda i: (i, 0)),
                pl.BlockSpec((TOP_K, 2, t, SC_ROW), lambda i: (0, 0, i, 0)),
                pl.BlockSpec((t, LANES), lambda i: (i, 0)),
                pl.BlockSpec((1, D_MODEL), lambda i: (0, 0))]
    args = [x1, ysg, slab, g]
    aliases = {}
    if out_prev is not None:
        in_specs.append(pl.BlockSpec(memory_space=pl.ANY))
        args.append(out_prev)
        aliases = {4: 0}
    return pl.pallas_call(
        _final_kernel,
        grid=(x1.shape[0] // t,),
        in_specs=in_specs,
        out_specs=pl.BlockSpec((t, D_MODEL), lambda i: (out_tile0 + i, 0)),
        out_shape=jax.ShapeDtypeStruct((n_out, D_MODEL), F32),
        input_output_aliases=aliases,
        compiler_params=pltpu.CompilerParams(dimension_semantics=("arbitrary",)),
        name="final_norm",
    )(*args)


def _s5_tables(lam_re, lam_im, b_re, b_im, c_re, c_im, log_step):
    lr = jnp.minimum(lam_re.astype(F32), LAMBDA_RE_MAX)
    li = lam_im.astype(F32)
    step = jnp.exp(log_step.astype(F32))[:, None]
    mag = jnp.exp(lr * step)
    ar = mag * jnp.cos(li * step)
    ai = mag * jnp.sin(li * step)
    nr = ar - 1.0
    den = lr * lr + li * li
    kr = (nr * lr + ai * li) / den
    ki = (ai * lr - nr * li) / den
    bre = b_re.astype(F32)
    bim = b_im.astype(F32)
    bbar_r = kr[..., None] * bre - ki[..., None] * bim
    bbar_i = kr[..., None] * bim + ki[..., None] * bre
    cre = c_re.astype(F32)
    cim = c_im.astype(F32)

    def power(m):
        m = jnp.asarray(m, F32)[..., None, None]
        mg = jnp.exp(m * lr * step)
        return mg * jnp.cos(m * li * step), mg * jnp.sin(m * li * step)

    pw_r, pw_i = power(jnp.arange(CHUNK + 1))
    cp_r = cre[None] * pw_r[:, :, None, :] - cim[None] * pw_i[:, :, None, :]
    cp_i = cre[None] * pw_i[:, :, None, :] + cim[None] * pw_r[:, :, None, :]
    kmat = (jnp.einsum('tgap,gph->tgah', cp_r[:CHUNK], bbar_r) - jnp.einsum('tgap,gph->tgah', cp_i[:CHUNK], bbar_i))
    lag = jnp.arange(CHUNK)[None, :] - jnp.arange(CHUNK)[:, None]
    ktoe = jnp.where((lag >= 0)[:, :, None, None, None], kmat[jnp.clip(lag, 0, CHUNK - 1)], 0.0)
    m_intra = ktoe.transpose(2, 0, 4, 1, 3).reshape(SSM_GROUPS, CHUNK * SSM_GROUP, CHUNK * SSM_GROUP)
    mo_r = cp_r[1:].transpose(1, 3, 0, 2).reshape(SSM_GROUPS, SSM_STATE, CHUNK * SSM_GROUP)
    mo_i = -cp_i[1:].transpose(1, 3, 0, 2).reshape(SSM_GROUPS, SSM_STATE, CHUNK * SSM_GROUP)
    mcat = jnp.concatenate([m_intra, mo_r, mo_i], axis=1).astype(BF16)
    q_r = pw_r[CHUNK - 1::-1][:CHUNK][:, :, :, None]
    q_i = pw_i[CHUNK - 1::-1][:CHUNK][:, :, :, None]
    mn_r = q_r * bbar_r[None] - q_i * bbar_i[None]
    mn_i = q_r * bbar_i[None] + q_i * bbar_r[None]
    minw = jnp.concatenate([mn_r, mn_i], axis=2).transpose(1, 0, 3, 2).reshape(
        SSM_GROUPS, CHUNK * SSM_GROUP, 2 * SSM_STATE).astype(BF16)

    def packed(j):
        pr, pi = power(float(CHUNK * j))
        return (jnp.concatenate([pr, pr], axis=-1).reshape(1, SSM_GROUPS * LANES),
                jnp.concatenate([-pi, pi], axis=-1).reshape(1, SSM_GROUPS * LANES))

    rows = jnp.arange(SUBLANES)[:, None]
    kinds = []
    for d in (1, 2, 4):
        c1, c2 = packed(d)
        mask = (rows >= d).astype(F32)
        kinds += [mask * c1, mask * c2]
    pk = [packed(r + 1) for r in range(SUBLANES)]
    kinds += [jnp.concatenate([p[0] for p in pk], axis=0), jnp.concatenate([p[1] for p in pk], axis=0)]
    coef = jnp.stack(kinds, axis=0)
    return minw, mcat, coef


def _layer_params(l, norm_mix, w_in, ssm_lam_re, ssm_lam_im, ssm_b_re, ssm_b_im, ssm_c_re, ssm_c_im, ssm_d,
                  ssm_log_step, ssm_w_glu, ssm_b_glu, pool_w, pool_scale, conv_w, branch_norm, w_out, norm_ffn,
                  router_w, router_b):
    minw, mcat, coef = _s5_tables(ssm_lam_re[l], ssm_lam_im[l], ssm_b_re[l], ssm_b_im[l], ssm_c_re[l], ssm_c_im[l],
                                  ssm_log_step[l])
    eye4 = jnp.eye(len(POOL_WINDOWS), dtype=F32)
    poolw = jnp.einsum('gcd,gk->gckd', pool_w[l].astype(F32), eye4).reshape(POOL_WIDTH, POOL_WIDTH).astype(BF16)
    rw = jnp.zeros((D_MODEL, LANES), F32).at[:, :N_EXPERTS].set(router_w[l]).astype(BF16)
    rb = jnp.zeros((1, LANES), F32).at[0, :N_EXPERTS].set(router_b[l])
    row = lambda v: v.reshape(1, -1).astype(F32)
    return dict(gmix=row(norm_mix[l]), win=w_in[l].astype(BF16), minw=minw, mcat=mcat, coef=coef, dskip=row(ssm_d[l]),
                wglu=ssm_w_glu[l].astype(BF16), bglu=row(ssm_b_glu[l]), poolw=poolw, pscale=row(pool_scale[l]),
                convw=conv_w[l].astype(F32), gbr=row(branch_norm[l]), wout=w_out[l].astype(BF16),
                gffn=row(norm_ffn[l]), rw=rw, rb=rb)


def kernel(x, norm_mix, w_in, ssm_lam_re, ssm_lam_im, ssm_b_re, ssm_b_im, ssm_c_re, ssm_c_im, ssm_d, ssm_log_step, ssm_w_glu, ssm_b_glu, pool_w, pool_scale, conv_w, branch_norm, w_out, norm_ffn, router_w, router_b, w_gate_up, b_gate_up, w_down, b_down, final_norm):
    batch, seq, d_model = x.shape
    assert d_model == D_MODEL and seq % TILE_T == 0
    depth = w_in.shape[0]
    n_streams = 2 if batch % 2 == 0 else 1
    sb = batch // n_streams
    n_tok = sb * seq
    assert n_tok % ROUTE_T == 0 and n_tok % SC_WINDOW == 0
    n_blocks = -(-(n_tok * TOP_K + N_EXPERTS * (EXPERT_BLOCK - 1)) // EXPERT_BLOCK)
    n_rows = n_blocks * EXPERT_BLOCK
    tiles = n_tok // TILE_T
    b_gu = b_gate_up.reshape(depth, N_EXPERTS, 1, 2 * D_EXPERT)
    b_dn = b_down.reshape(depth, N_EXPERTS, 1, D_MODEL)

    xin = [x.reshape(batch * seq, D_MODEL)] * n_streams
    tile0 = [s * tiles for s in range(n_streams)]
    ysg = [None] * n_streams
    slab = [None] * n_streams
    for l in range(depth):
        lp = _layer_params(l, norm_mix, w_in, ssm_lam_re, ssm_lam_im, ssm_b_re, ssm_b_im, ssm_c_re, ssm_c_im, ssm_d,
                           ssm_log_step, ssm_w_glu, ssm_b_glu, pool_w, pool_scale, conv_w, branch_norm, w_out,
                           norm_ffn, router_w, router_b)
        routed = []
        for s in range(n_streams):
            xin[s], hp, slab[s], counts = _mixer_call(xin[s], tile0[s], ysg[s], slab[s], lp, sb, seq)
            tile0[s] = 0
            dest, meta = _route_call(slab[s], counts, n_tok, n_blocks)
            idx = dest.reshape(1, 2 * TOP_K * n_tok)
            xs = _dispatch(hp.reshape(2 * n_tok, SC_ROW), idx, 2 * n_rows).reshape(2, n_rows, SC_ROW)
            routed.append((xs, idx, meta[:n_blocks, 0], meta[n_blocks:n_blocks + 1, 0]))
        for s, (xs, idx, block_e, n_valid) in enumerate(routed):
            ys = _expert_call(xs, block_e, n_valid, w_gate_up, b_gu, w_down, b_dn, l, n_blocks)
            ysg[s] = _combine(ys.reshape(2 * n_rows, SC_ROW), idx).reshape(TOP_K, 2, n_tok, SC_ROW)
    g = final_norm.reshape(1, D_MODEL).astype(F32)
    out = None
    for s in range(n_streams):
        out = _final_call(xin[s], ysg[s], slab[s], g, s * tiles, batch * seq, out)
    return out.reshape(batch, seq, D_MODEL)
```

```python
import functools
import math

import jax
import jax.numpy as jnp
from jax import lax
from jax.experimental import pallas as pl
from jax.experimental.pallas import tpu as pltpu
from jax.experimental.pallas import tpu_sc as plsc

F32 = jnp.float32
BF16 = jnp.bfloat16
I32 = jnp.int32

D_MODEL = 1024
SSM_WIDTH = 512
POOL_WIDTH = 256
CONV_WIDTH = 256
SSM_GROUP = 16
SSM_GROUPS = 32
SSM_STATE = 64
LAMBDA_RE_MAX = -1e-4
POOL_WINDOWS = (2, 4, 8, 16)
POOL_GROUP = 64
CONV_K = 3
IN_WIDTH = SSM_WIDTH + POOL_WIDTH + 3 * CONV_WIDTH
N_EXPERTS = 32
TOP_K = 4
D_EXPERT = 1024
SWIGLU_LIMIT = 7.0
SWIGLU_ALPHA = 1.702
EXPERT_BLOCK = 256
NORM_EPS = 1e-5

LANES = 128
SUBLANES = 8
TILE_T = 512
CHUNK = 16
N_CHUNK = TILE_T // CHUNK
PACKED = D_MODEL // 2
ROUTE_T = 2048
SC_WINDOW = 128
SC_ROW = PACKED // 2
HI_MASK = -65536
VMEM_LIMIT = 56 * 1024 * 1024


def _bf16_round(v):
    return v.astype(BF16).astype(F32)


def _pack_rows(v):
    lo = lax.shift_right_logical(lax.bitcast_convert_type(_bf16_round(v[:, :PACKED]), I32), 16)
    hi = lax.bitcast_convert_type(_bf16_round(v[:, PACKED:]), I32) & HI_MASK
    return hi | lo


def _unpack_lo(w):
    return lax.bitcast_convert_type(lax.shift_left(w, 16), F32)


def _unpack_hi(w):
    return lax.bitcast_convert_type(w & HI_MASK, F32)


def _rms(v, g):
    r = lax.rsqrt(jnp.mean(v * v, axis=-1, keepdims=True) + NORM_EPS)
    return (v * r) * g


def _moe_combine(x, y_ref, slab):
    parts = [x[:, i * SC_ROW:(i + 1) * SC_ROW] for i in range(4)]
    for k in range(TOP_K):
        g = lax.bitcast_convert_type(slab[:, TOP_K + k:TOP_K + k + 1], F32)
        for h in range(2):
            w = y_ref[k, h]
            parts[h] = parts[h] + g * _unpack_lo(w)
            parts[2 + h] = parts[2 + h] + g * _unpack_hi(w)
    return jnp.concatenate(parts, axis=1)


def _block_transpose8(vs, lane):
    vs = list(vs)
    for dist in (4, 2, 1):
        width = SSM_GROUP * dist
        low = (lane % (2 * width)) < width
        for j in range(8):
            if j & dist:
                continue
            a, b = vs[j], vs[j + dist]
            vs[j] = jnp.where(low, a, pltpu.roll(b, width, axis=1))
            vs[j + dist] = jnp.where(low, pltpu.roll(a, LANES - width, axis=1), b)
    return vs


def _cmul_packed(c1, c2, v):
    return c1 * v + c2 * pltpu.roll(v, SSM_STATE, axis=1)


def _chunk_scan(s, carry_in, cf):
    outs = []
    carry = jnp.broadcast_to(carry_in, (SUBLANES, LANES))
    for rg in range(N_CHUNK // SUBLANES):
        x = s[rg * SUBLANES:(rg + 1) * SUBLANES]
        for j, d in enumerate((1, 2, 4)):
            x = x + _cmul_packed(cf[2 * j], cf[2 * j + 1], pltpu.roll(x, d, axis=0))
        x = x + _cmul_packed(cf[6], cf[7], carry)
        outs.append(x)
        carry = jnp.broadcast_to(x[SUBLANES - 1:SUBLANES], (SUBLANES, LANES))
    return jnp.concatenate(outs, axis=0)


def _s5_mixer(u, first, minw_ref, mcat_ref, coef_ref, u_s, y_s, h_s):
    blocks = SSM_WIDTH // LANES
    for b4 in range(blocks):
        u_s[b4] = u[:, b4 * LANES:(b4 + 1) * LANES]
    lane = lax.broadcasted_iota(I32, (N_CHUNK, LANES), 1)

    halves = [[None, None] for _ in range(SSM_GROUPS)]
    for b4 in range(blocks):
        for hh in range(2):
            xs = [u_s[b4, pl.ds(8 * hh + j, N_CHUNK, stride=CHUNK), :].astype(BF16) for j in range(8)]
            ws = _block_transpose8(xs, lane)
            for gl in range(8):
                halves[8 * b4 + gl][hh] = ws[gl]
    ug = [jnp.concatenate(h2, axis=1) for h2 in halves]

    carry_in = jnp.where(first, 0.0, h_s[N_CHUNK + 7:N_CHUNK + 8, :])
    h_s[7:8, :] = carry_in
    for g in range(SSM_GROUPS):
        cols = slice(g * LANES, (g + 1) * LANES)
        s_g = jnp.dot(ug[g], minw_ref[g], preferred_element_type=F32)
        cf = [coef_ref[k, :, cols] for k in range(8)]
        h_s[8:8 + N_CHUNK, cols] = _chunk_scan(s_g, carry_in[:, cols], cf)

    yg = []
    for g in range(SSM_GROUPS):
        hprev = h_s[7:7 + N_CHUNK, g * LANES:(g + 1) * LANES].astype(BF16)
        lhs = jnp.concatenate([ug[g], hprev], axis=1)
        yg.append(jnp.dot(lhs, mcat_ref[g], preferred_element_type=F32).astype(BF16))

    for b4 in range(blocks):
        for hh in range(2):
            ws = [yg[8 * b4 + gl][:, hh * LANES:(hh + 1) * LANES] for gl in range(8)]
            xs = _block_transpose8(ws, lane)
            for j in range(8):
                y_s[b4, pl.ds(8 * hh + j, N_CHUNK, stride=CHUNK), :] = xs[j].astype(F32)
    return jnp.concatenate([y_s[b4] for b4 in range(blocks)], axis=1)


def _mixer_kernel(*refs, first_layer):
    if first_layer:
        x_ref = refs[0]
        refs = refs[1:]
    else:
        x_ref, yprev_ref, slabprev_ref = refs[:3]
        refs = refs[3:]
    (gmix_ref, win_ref, minw_ref, mcat_ref, coef_ref, dskip_ref, wglu_ref, bglu_ref, poolw_ref, pscale_ref,
     convw_ref, gbr_ref, wout_ref, gffn_ref, rw_ref, rb_ref,
     x1_ref, hp_ref, slab_ref, cnt_ref,
     u_s, y_s, h_s, pbuf_ref, zbuf_ref, cntacc_ref) = refs

    b = pl.program_id(0)
    l = pl.program_id(1)
    seq_start = l == 0

    @pl.when(jnp.logical_and(b == 0, l == 0))
    def _():
        cntacc_ref[...] = jnp.zeros_like(cntacc_ref)

    if first_layer:
        x = x_ref[...]
    else:
        x = _moe_combine(x_ref[...], yprev_ref, slabprev_ref[...])

    hn = _rms(x, gmix_ref[...]).astype(BF16)
    proj = jnp.dot(hn, win_ref[...], preferred_element_type=F32)
    u = proj[:, :SSM_WIDTH]
    p = proj[:, SSM_WIDTH:SSM_WIDTH + POOL_WIDTH]
    c0 = SSM_WIDTH + POOL_WIDTH
    bg = proj[:, c0:c0 + CONV_WIDTH]
    cg = proj[:, c0 + CONV_WIDTH:c0 + 2 * CONV_WIDTH]
    hv = proj[:, c0 + 2 * CONV_WIDTH:]

    y = _s5_mixer(u, seq_start, minw_ref, mcat_ref, coef_ref, u_s, y_s, h_s) + dskip_ref[...] * u
    z = jax.nn.gelu(y)
    glu = jnp.dot(z.astype(BF16), wglu_ref[...], preferred_element_type=F32) + bglu_ref[...]
    y_ssm = z * jax.nn.sigmoid(glu)

    tail = 16
    pbuf_ref[0:tail, :] = jnp.where(seq_start, 0.0, pbuf_ref[TILE_T:TILE_T + tail, :])
    pbuf_ref[tail:, :] = p
    lane_p = lax.broadcasted_iota(I32, (TILE_T, POOL_WIDTH), 1)
    win = jnp.where(lane_p < 64, 2, jnp.where(lane_p < 128, 4, jnp.where(lane_p < 192, 8, 16)))
    total = p
    for d in range(1, 16):
        total = total + jnp.where(win > d, pbuf_ref[pl.ds(tail - d, TILE_T), :], 0.0)
    row_p = lax.broadcasted_iota(I32, (TILE_T, POOL_WIDTH), 0) + l * TILE_T
    count = jnp.minimum(row_p + 1, win).astype(F32)
    pooled = total / count - p
    mixed = jnp.dot(pooled.astype(BF16), poolw_ref[...], preferred_element_type=F32)
    y_pool = mixed * pscale_ref[...]

    zc = cg * hv
    zbuf_ref[0:8, :] = jnp.where(seq_start, 0.0, zbuf_ref[TILE_T:TILE_T + 8, :])
    zbuf_ref[8:, :] = zc
    yc = (convw_ref[0:1, :] * zbuf_ref[pl.ds(6, TILE_T), :]
          + convw_ref[1:2, :] * zbuf_ref[pl.ds(7, TILE_T), :]
          + convw_ref[2:3, :] * zc)
    y_conv = bg * yc

    gbr = gbr_ref[...]
    s1 = SSM_WIDTH + POOL_WIDTH
    mixed_all = jnp.concatenate([_rms(y_ssm, gbr[:, :SSM_WIDTH]).astype(BF16),
                                 _rms(y_pool, gbr[:, SSM_WIDTH:s1]).astype(BF16),
                                 _rms(y_conv, gbr[:, s1:]).astype(BF16)], axis=1)
    x1 = x + jnp.dot(mixed_all, wout_ref[...], preferred_element_type=F32)
    x1_ref[...] = x1

    hn2 = _rms(x1, gffn_ref[...])
    packed = _pack_rows(hn2)
    hp_ref[0] = packed[:, :SC_ROW]
    hp_ref[1] = packed[:, SC_ROW:]
    logits = jnp.dot(hn2.astype(BF16), rw_ref[...], preferred_element_type=F32) + rb_ref[...]
    lane = lax.broadcasted_iota(I32, (TILE_T, LANES), 1)
    neg = jnp.float32(-jnp.inf)
    work = jnp.where(lane < N_EXPERTS, logits, neg)
    vals, idxs = [], []
    for _ in range(TOP_K):
        m = jnp.max(work, axis=1, keepdims=True)
        idx = jnp.min(jnp.where(work == m, lane, LANES), axis=1, keepdims=True)
        vals.append(m)
        idxs.append(idx)
        work = jnp.where(lane == idx, neg, work)
    exps = [jnp.exp(v - vals[0]) for v in vals]
    denom = exps[0] + exps[1] + exps[2] + exps[3]
    gates = [e / denom for e in exps]

    onehot = jnp.zeros((TILE_T, LANES), F32)
    for idx in idxs:
        onehot = jnp.where(lane == idx, 1.0, onehot)
    r_i = lax.broadcasted_iota(I32, (TILE_T, TILE_T), 0)
    c_i = lax.broadcasted_iota(I32, (TILE_T, TILE_T), 1)
    ltri = jnp.where(c_i < r_i, 1.0, 0.0).astype(BF16)
    before = jnp.dot(ltri, onehot.astype(BF16), preferred_element_type=F32) + cntacc_ref[0:1, :]
    slab = jnp.zeros((TILE_T, LANES), I32)
    for k in range(TOP_K):
        rank = jnp.sum(jnp.where(lane == idxs[k], before, 0.0), axis=1, keepdims=True).astype(I32)
        slab = jnp.where(lane == k, idxs[k], slab)
        slab = jnp.where(lane == TOP_K + k, lax.bitcast_convert_type(gates[k], I32), slab)
        slab = jnp.where(lane == 2 * TOP_K + k, rank, slab)
    slab_ref[...] = slab
    newcnt = cntacc_ref[0:1, :] + jnp.sum(onehot, axis=0, keepdims=True)
    cntacc_ref[...] = jnp.broadcast_to(newcnt, cntacc_ref.shape)
    cnt_ref[...] = jnp.broadcast_to(newcnt, cnt_ref.shape).astype(I32)


def _const_spec(shape):
    nd = len(shape)
    return pl.BlockSpec(shape, lambda b, l, nd=nd: (0,) * nd)


def _mixer_call(x, x_tile0, yprev, slabprev, lp, batch, seq):
    first_layer = yprev is None
    n_tok = batch * seq
    steps = seq // TILE_T
    tok_spec = lambda w: pl.BlockSpec((TILE_T, w), lambda b, l: (b * steps + l, 0))
    in_specs = [pl.BlockSpec((TILE_T, D_MODEL), lambda b, l: (x_tile0 + b * steps + l, 0))]
    args = [x]
    if not first_layer:
        in_specs += [pl.BlockSpec((TOP_K, 2, TILE_T, SC_ROW), lambda b, l: (0, 0, b * steps + l, 0)), tok_spec(LANES)]
        args += [yprev, slabprev]
    weights = [lp['gmix'], lp['win'], lp['minw'], lp['mcat'], lp['coef'], lp['dskip'], lp['wglu'], lp['bglu'],
               lp['poolw'], lp['pscale'], lp['convw'], lp['gbr'], lp['wout'], lp['gffn'], lp['rw'], lp['rb']]
    in_specs += [_const_spec(w.shape) for w in weights]
    args += weights
    out_shape = [jax.ShapeDtypeStruct((n_tok, D_MODEL), F32),
                 jax.ShapeDtypeStruct((2, n_tok, SC_ROW), I32),
                 jax.ShapeDtypeStruct((n_tok, LANES), I32),
                 jax.ShapeDtypeStruct((SUBLANES, LANES), I32)]
    out_specs = [tok_spec(D_MODEL), pl.BlockSpec((2, TILE_T, SC_ROW), lambda b, l: (0, b * steps + l, 0)), tok_spec(LANES),
                 pl.BlockSpec((SUBLANES, LANES), lambda b, l: (0, 0))]
    scratch = [pltpu.VMEM((SSM_WIDTH // LANES, TILE_T, LANES), F32),
               pltpu.VMEM((SSM_WIDTH // LANES, TILE_T, LANES), F32),
               pltpu.VMEM((SUBLANES + N_CHUNK, SSM_GROUPS * LANES), F32),
               pltpu.VMEM((16 + TILE_T, POOL_WIDTH), F32),
               pltpu.VMEM((8 + TILE_T, CONV_WIDTH), F32),
               pltpu.VMEM((SUBLANES, LANES), F32)]
    return pl.pallas_call(
        functools.partial(_mixer_kernel, first_layer=first_layer),
        grid=(batch, steps),
        in_specs=in_specs, out_specs=out_specs, out_shape=out_shape, scratch_shapes=scratch,
        compiler_params=pltpu.CompilerParams(dimension_semantics=("arbitrary", "arbitrary"),
                                             vmem_limit_bytes=VMEM_LIMIT),
        name="mixer_first" if first_layer else "mixer_next",
    )(*args)


def _route_kernel(slab_ref, cnt_ref, dest_ref, meta_ref, *, n_blocks):
    lane1 = lax.broadcasted_iota(I32, (1, LANES), 1)
    counts = jnp.where(lane1 < N_EXPERTS, cnt_ref[0:1, :], 0)
    padded = ((counts + (EXPERT_BLOCK - 1)) // EXPERT_BLOCK) * EXPERT_BLOCK
    pend = padded
    for sh in (1, 2, 4, 8, 16):
        pend = pend + jnp.where(lane1 >= sh, pltpu.roll(pend, sh, axis=1), 0)
    pstart = pend - padded

    row = lax.broadcasted_iota(I32, meta_ref.shape, 0)
    meta_ref[...] = jnp.where(row == 0, pstart // EXPERT_BLOCK, jnp.where(row == 1, padded // EXPERT_BLOCK, 0))

    slab = slab_ref[...]
    lane = lax.broadcasted_iota(I32, slab.shape, 1)
    out = jnp.zeros(slab.shape, F32)
    for k in range(TOP_K):
        idx = slab[:, k:k + 1]
        rank = slab[:, 2 * TOP_K + k:2 * TOP_K + k + 1]
        base = jnp.sum(jnp.where(lane == idx, pstart, 0), axis=1, keepdims=True)
        for h in range(2):
            out = jnp.where(lane == 2 * k + h, (base + rank + h * n_blocks * EXPERT_BLOCK).astype(F32), out)
    dest_ref[...] = out.T[:SUBLANES, :].astype(I32)


def _route_call(slab, counts, n_tok, n_blocks):
    meta_rows = SUBLANES
    return pl.pallas_call(
        functools.partial(_route_kernel, n_blocks=n_blocks),
        grid=(n_tok // ROUTE_T,),
        in_specs=[pl.BlockSpec((ROUTE_T, LANES), lambda i: (i, 0)),
                  pl.BlockSpec((SUBLANES, LANES), lambda i: (0, 0))],
        out_specs=[pl.BlockSpec((SUBLANES, ROUTE_T), lambda i: (0, i)),
                   pl.BlockSpec((meta_rows, LANES), lambda i: (0, 0))],
        out_shape=[jax.ShapeDtypeStruct((SUBLANES, n_tok), I32),
                   jax.ShapeDtypeStruct((meta_rows, LANES), I32)],
        compiler_params=pltpu.CompilerParams(dimension_semantics=("arbitrary",)),
        name="route",
    )(slab, counts)


def _sc_mesh():
    return plsc.VectorSubcoreMesh(core_axis_name="core", subcore_axis_name="subcore")


def _dispatch(src, idx, n_rows):
    windows = src.shape[0] // SC_WINDOW

    @functools.partial(pl.kernel, out_type=jax.ShapeDtypeStruct((n_rows, SC_ROW), src.dtype),
                       mesh=_sc_mesh(), scratch_types=[])
    def scatter_rows(x_hbm, i_hbm, o_hbm):
        def body(x_vmem, i_vmem):
            pltpu.sync_copy(x_vmem, o_hbm.at[i_vmem.at[0]])

        pltpu.emit_pipeline(
            body,
            grid=(idx.shape[1] // SC_WINDOW,),
            in_specs=[pl.BlockSpec((SC_WINDOW, SC_ROW), lambda i: (lax.rem(i, windows), 0)),
                      pl.BlockSpec((1, SC_WINDOW), lambda i: (0, i))],
            out_specs=[],
            core_axis_name=("core", "subcore"),
            dimension_semantics=(pltpu.PARALLEL,),
        )(x_hbm, i_hbm)

    return scatter_rows(src, idx)


def _combine(src, idx):
    n_idx = idx.shape[1]

    @functools.partial(pl.kernel, out_type=jax.ShapeDtypeStruct((n_idx, SC_ROW), src.dtype),
                       mesh=_sc_mesh(), scratch_types=[])
    def gather_rows(y_hbm, i_hbm, o_hbm):
        def body(i_vmem, o_vmem):
            pltpu.sync_copy(y_hbm.at[i_vmem.at[0]], o_vmem)

        pltpu.emit_pipeline(
            body,
            grid=(n_idx // SC_WINDOW,),
            in_specs=[pl.BlockSpec((1, SC_WINDOW), lambda i: (0, i))],
            out_specs=[pl.BlockSpec((SC_WINDOW, SC_ROW), lambda i: (i, 0))],
            core_axis_name=("core", "subcore"),
            dimension_semantics=(pltpu.PARALLEL,),
        )(i_hbm, o_hbm)

    return gather_rows(src, idx)


def _expert_block(xw, wgu_s, bgu_ref, wdn_s, bdn_ref, act_s):
    w0 = xw[0]
    w1 = xw[1]
    xb = jnp.concatenate([_unpack_lo(w0).astype(BF16), _unpack_lo(w1).astype(BF16),
                          _unpack_hi(w0).astype(BF16), _unpack_hi(w1).astype(BF16)], axis=1)
    half = D_EXPERT // 2
    for j in range(2):
        cols = slice(j * half, (j + 1) * half)
        ucols = slice(D_EXPERT + j * half, D_EXPERT + (j + 1) * half)
        g = jnp.dot(xb, wgu_s[:, cols], preferred_element_type=F32) + bgu_ref[:, cols]
        up = jnp.dot(xb, wgu_s[:, ucols], preferred_element_type=F32) + bgu_ref[:, ucols]
        g = jnp.minimum(g, SWIGLU_LIMIT)
        up = jnp.clip(up, -SWIGLU_LIMIT, SWIGLU_LIMIT)
        act_s[:, cols] = (g * jax.nn.sigmoid(SWIGLU_ALPHA * g) * (up + 1.0)).astype(BF16)
    y = jnp.dot(act_s[...], wdn_s[...], preferred_element_type=F32) + bdn_ref[...]
    return _pack_rows(y)


def _expert_kernel(first_ref, nblk_ref, xs_hbm, wgu_ref, bgu_ref, wdn_ref, bdn_ref, ys_hbm,
                   wgu_s, wdn_s, act_s, xbuf, ybuf, sem_in, sem_out):
    e = pl.program_id(0)
    nb = nblk_ref[e]
    b0 = first_ref[e]

    def rows_of(i):
        return pl.ds(pl.multiple_of((b0 + i) * EXPERT_BLOCK, EXPERT_BLOCK), EXPERT_BLOCK)

    def x_copy(i, slot):
        return pltpu.make_async_copy(xs_hbm.at[:, rows_of(i), :], xbuf.at[slot], sem_in.at[slot])

    def y_copy(i, slot):
        return pltpu.make_async_copy(ybuf.at[slot], ys_hbm.at[:, rows_of(i), :], sem_out.at[slot])

    @pl.when(nb > 0)
    def _():
        x_copy(0, 0).start()
        chunk = 128

        def cast_gu(i, _):
            rows = pl.ds(pl.multiple_of(i * chunk, chunk), chunk)
            wgu_s[rows, :] = wgu_ref[rows, :].astype(BF16)
            return 0

        def cast_dn(i, _):
            rows = pl.ds(pl.multiple_of(i * chunk, chunk), chunk)
            wdn_s[rows, :] = wdn_ref[rows, :].astype(BF16)
            return 0

        lax.fori_loop(0, D_MODEL // chunk, cast_gu, 0)
        lax.fori_loop(0, D_EXPERT // chunk, cast_dn, 0)

        def pair(j, _):
            for slot in range(2):
                i = 2 * j + slot

                @pl.when(i < nb)
                def _(i=i, slot=slot):
                    x_copy(i, slot).wait()

                    @pl.when(i + 1 < nb)
                    def _():
                        x_copy(i + 1, 1 - slot).start()

                    @pl.when(i >= 2)
                    def _():
                        y_copy(i - 2, slot).wait()

                    packed = _expert_block(xbuf[slot], wgu_s, bgu_ref, wdn_s, bdn_ref, act_s)
                    ybuf[slot, 0] = packed[:, :SC_ROW]
                    ybuf[slot, 1] = packed[:, SC_ROW:]
                    y_copy(i, slot).start()
            return 0

        lax.fori_loop(0, (nb + 1) // 2, pair, 0)

        for slot in range(2):
            @pl.when(nb > slot)
            def _(slot=slot):
                last = nb - 1 - lax.rem(nb - 1 - slot, 2)
                y_copy(last, slot).wait()


def _expert_call(xs, first_blk, n_blk, w_gu, b_gu, w_dn, b_dn, layer):
    def w_map(e, first, nblk):
        return (layer, e, 0, 0)

    grid_spec = pltpu.PrefetchScalarGridSpec(
        num_scalar_prefetch=2,
        grid=(N_EXPERTS,),
        in_specs=[pl.BlockSpec(memory_space=pl.ANY),
                  pl.BlockSpec((None, None, D_MODEL, 2 * D_EXPERT), w_map),
                  pl.BlockSpec((None, None, 1, 2 * D_EXPERT), w_map),
                  pl.BlockSpec((None, None, D_EXPERT, D_MODEL), w_map),
                  pl.BlockSpec((None, None, 1, D_MODEL), w_map)],
        out_specs=pl.BlockSpec(memory_space=pl.ANY),
        scratch_shapes=[pltpu.VMEM((D_MODEL, 2 * D_EXPERT), BF16),
                        pltpu.VMEM((D_EXPERT, D_MODEL), BF16),
                        pltpu.VMEM((EXPERT_BLOCK, D_EXPERT), BF16),
                        pltpu.VMEM((2, 2, EXPERT_BLOCK, SC_ROW), I32),
                        pltpu.VMEM((2, 2, EXPERT_BLOCK, SC_ROW), I32),
                        pltpu.SemaphoreType.DMA((2,)),
                        pltpu.SemaphoreType.DMA((2,))])
    return pl.pallas_call(
        _expert_kernel,
        grid_spec=grid_spec,
        out_shape=jax.ShapeDtypeStruct(xs.shape, I32),
        compiler_params=pltpu.CompilerParams(dimension_semantics=("arbitrary",),
                                             vmem_limit_bytes=VMEM_LIMIT),
        name="experts",
    )(first_blk, n_blk, xs, w_gu, b_gu, w_dn, b_dn)


def _final_kernel(x_ref, y_ref, slab_ref, g_ref, *rest):
    o_ref = rest[-1]
    x = _moe_combine(x_ref[...], y_ref, slab_ref[...])
    o_ref[...] = _rms(x, g_ref[...])


def _final_call(x1, ysg, slab, g, out_tile0, n_out, out_prev):
    t = TILE_T
    in_specs = [pl.BlockSpec((t, D_MODEL), lambda i: (i, 0)),
                pl.BlockSpec((TOP_K, 2, t, SC_ROW), lambda i: (0, 0, i, 0)),
                pl.BlockSpec((t, LANES), lambda i: (i, 0)),
                pl.BlockSpec((1, D_MODEL), lambda i: (0, 0))]
    args = [x1, ysg, slab, g]
    aliases = {}
    if out_prev is not None:
        in_specs.append(pl.BlockSpec(memory_space=pl.ANY))
        args.append(out_prev)
        aliases = {4: 0}
    return pl.pallas_call(
        _final_kernel,
        grid=(x1.shape[0] // t,),
        in_specs=in_specs,
        out_specs=pl.BlockSpec((t, D_MODEL), lambda i: (out_tile0 + i, 0)),
        out_shape=jax.ShapeDtypeStruct((n_out, D_MODEL), F32),
        input_output_aliases=aliases,
        compiler_params=pltpu.CompilerParams(dimension_semantics=("arbitrary",)),
        name="final_norm",
    )(*args)


def _s5_tables(lam_re, lam_im, b_re, b_im, c_re, c_im, log_step):
    lr = jnp.minimum(lam_re.astype(F32), LAMBDA_RE_MAX)
    li = lam_im.astype(F32)
    step = jnp.exp(log_step.astype(F32))[:, None]
    mag = jnp.exp(lr * step)
    ar = mag * jnp.cos(li * step)
    ai = mag * jnp.sin(li * step)
    nr = ar - 1.0
    den = lr * lr + li * li
    kr = (nr * lr + ai * li) / den
    ki = (ai * lr - nr * li) / den
    bre = b_re.astype(F32)
    bim = b_im.astype(F32)
    bbar_r = kr[..., None] * bre - ki[..., None] * bim
    bbar_i = kr[..., None] * bim + ki[..., None] * bre
    cre = c_re.astype(F32)
    cim = c_im.astype(F32)

    def power(m):
        m = jnp.asarray(m, F32)[..., None, None]
        mg = jnp.exp(m * lr * step)
        return mg * jnp.cos(m * li * step), mg * jnp.sin(m * li * step)

    pw_r, pw_i = power(jnp.arange(CHUNK + 1))
    cp_r = cre[None] * pw_r[:, :, None, :] - cim[None] * pw_i[:, :, None, :]
    cp_i = cre[None] * pw_i[:, :, None, :] + cim[None] * pw_r[:, :, None, :]
    kmat = (jnp.einsum('tgap,gph->tgah', cp_r[:CHUNK], bbar_r) - jnp.einsum('tgap,gph->tgah', cp_i[:CHUNK], bbar_i))
    lag = jnp.arange(CHUNK)[None, :] - jnp.arange(CHUNK)[:, None]
    ktoe = jnp.where((lag >= 0)[:, :, None, None, None], kmat[jnp.clip(lag, 0, CHUNK - 1)], 0.0)
    m_intra = ktoe.transpose(2, 0, 4, 1, 3).reshape(SSM_GROUPS, CHUNK * SSM_GROUP, CHUNK * SSM_GROUP)
    mo_r = cp_r[1:].transpose(1, 3, 0, 2).reshape(SSM_GROUPS, SSM_STATE, CHUNK * SSM_GROUP)
    mo_i = -cp_i[1:].transpose(1, 3, 0, 2).reshape(SSM_GROUPS, SSM_STATE, CHUNK * SSM_GROUP)
    mcat = jnp.concatenate([m_intra, mo_r, mo_i], axis=1).astype(BF16)
    q_r = pw_r[CHUNK - 1::-1][:CHUNK][:, :, :, None]
    q_i = pw_i[CHUNK - 1::-1][:CHUNK][:, :, :, None]
    mn_r = q_r * bbar_r[None] - q_i * bbar_i[None]
    mn_i = q_r * bbar_i[None] + q_i * bbar_r[None]
    minw = jnp.concatenate([mn_r, mn_i], axis=2).transpose(1, 0, 3, 2).reshape(
        SSM_GROUPS, CHUNK * SSM_GROUP, 2 * SSM_STATE).astype(BF16)

    def packed(j):
        pr, pi = power(float(CHUNK * j))
        return (jnp.concatenate([pr, pr], axis=-1).reshape(1, SSM_GROUPS * LANES),
                jnp.concatenate([-pi, pi], axis=-1).reshape(1, SSM_GROUPS * LANES))

    rows = jnp.arange(SUBLANES)[:, None]
    kinds = []
    for d in (1, 2, 4):
        c1, c2 = packed(d)
        mask = (rows >= d).astype(F32)
        kinds += [mask * c1, mask * c2]
    pk = [packed(r + 1) for r in range(SUBLANES)]
    kinds += [jnp.concatenate([p[0] for p in pk], axis=0), jnp.concatenate([p[1] for p in pk], axis=0)]
    coef = jnp.stack(kinds, axis=0)
    return minw, mcat, coef


def _layer_params(l, norm_mix, w_in, ssm_lam_re, ssm_lam_im, ssm_b_re, ssm_b_im, ssm_c_re, ssm_c_im, ssm_d,
                  ssm_log_step, ssm_w_glu, ssm_b_glu, pool_w, pool_scale, conv_w, branch_norm, w_out, norm_ffn,
                  router_w, router_b):
    minw, mcat, coef = _s5_tables(ssm_lam_re[l], ssm_lam_im[l], ssm_b_re[l], ssm_b_im[l], ssm_c_re[l], ssm_c_im[l],
                                  ssm_log_step[l])
    eye4 = jnp.eye(len(POOL_WINDOWS), dtype=F32)
    poolw = jnp.einsum('gcd,gk->gckd', pool_w[l].astype(F32), eye4).reshape(POOL_WIDTH, POOL_WIDTH).astype(BF16)
    rw = jnp.zeros((D_MODEL, LANES), F32).at[:, :N_EXPERTS].set(router_w[l]).astype(BF16)
    rb = jnp.zeros((1, LANES), F32).at[0, :N_EXPERTS].set(router_b[l])
    row = lambda v: v.reshape(1, -1).astype(F32)
    return dict(gmix=row(norm_mix[l]), win=w_in[l].astype(BF16), minw=minw, mcat=mcat, coef=coef, dskip=row(ssm_d[l]),
                wglu=ssm_w_glu[l].astype(BF16), bglu=row(ssm_b_glu[l]), poolw=poolw, pscale=row(pool_scale[l]),
                convw=conv_w[l].astype(F32), gbr=row(branch_norm[l]), wout=w_out[l].astype(BF16),
                gffn=row(norm_ffn[l]), rw=rw, rb=rb)


def kernel(x, norm_mix, w_in, ssm_lam_re, ssm_lam_im, ssm_b_re, ssm_b_im, ssm_c_re, ssm_c_im, ssm_d, ssm_log_step, ssm_w_glu, ssm_b_glu, pool_w, pool_scale, conv_w, branch_norm, w_out, norm_ffn, router_w, router_b, w_gate_up, b_gate_up, w_down, b_down, final_norm):
    batch, seq, d_model = x.shape
    assert d_model == D_MODEL and seq % TILE_T == 0
    depth = w_in.shape[0]
    n_streams = 2 if batch % 2 == 0 else 1
    sb = batch // n_streams
    n_tok = sb * seq
    assert n_tok % ROUTE_T == 0 and n_tok % SC_WINDOW == 0
    n_blocks = -(-(n_tok * TOP_K + N_EXPERTS * (EXPERT_BLOCK - 1)) // EXPERT_BLOCK)
    n_rows = n_blocks * EXPERT_BLOCK
    tiles = n_tok // TILE_T
    b_gu = b_gate_up.reshape(depth, N_EXPERTS, 1, 2 * D_EXPERT)
    b_dn = b_down.reshape(depth, N_EXPERTS, 1, D_MODEL)

    xin = [x.reshape(batch * seq, D_MODEL)] * n_streams
    tile0 = [s * tiles for s in range(n_streams)]
    ysg = [None] * n_streams
    slab = [None] * n_streams
    for l in range(depth):
        lp = _layer_params(l, norm_mix, w_in, ssm_lam_re, ssm_lam_im, ssm_b_re, ssm_b_im, ssm_c_re, ssm_c_im, ssm_d,
                           ssm_log_step, ssm_w_glu, ssm_b_glu, pool_w, pool_scale, conv_w, branch_norm, w_out,
                           norm_ffn, router_w, router_b)
        routed = []
        for s in range(n_streams):
            xin[s], hp, slab[s], counts = _mixer_call(xin[s], tile0[s], ysg[s], slab[s], lp, sb, seq)
            tile0[s] = 0
            dest, meta = _route_call(slab[s], counts, n_tok, n_blocks)
            idx = dest.reshape(1, 2 * TOP_K * n_tok)
            xs = _dispatch(hp.reshape(2 * n_tok, SC_ROW), idx, 2 * n_rows).reshape(2, n_rows, SC_ROW)
            routed.append((xs, idx, meta[0, :N_EXPERTS], meta[1, :N_EXPERTS]))
        for s, (xs, idx, first_blk, n_blk) in enumerate(routed):
            ys = _expert_call(xs, first_blk, n_blk, w_gate_up, b_gu, w_down, b_dn, l)
            ysg[s] = _combine(ys.reshape(2 * n_rows, SC_ROW), idx).reshape(TOP_K, 2, n_tok, SC_ROW)
    g = final_norm.reshape(1, D_MODEL).astype(F32)
    out = None
    for s in range(n_streams):
        out = _final_call(xin[s], ysg[s], slab[s], g, s * tiles, batch * seq, out)
    return out.reshape(batch, seq, D_MODEL)
```

```python
import functools
import math

import jax
import jax.numpy as jnp
from jax import lax
from jax.experimental import pallas as pl
from jax.experimental.pallas import tpu as pltpu
from jax.experimental.pallas import tpu_sc as plsc

F32 = jnp.float32
BF16 = jnp.bfloat16
I32 = jnp.int32

D_MODEL = 1024
SSM_WIDTH = 512
POOL_WIDTH = 256
CONV_WIDTH = 256
SSM_GROUP = 16
SSM_GROUPS = 32
SSM_STATE = 64
LAMBDA_RE_MAX = -1e-4
POOL_WINDOWS = (2, 4, 8, 16)
POOL_GROUP = 64
CONV_K = 3
IN_WIDTH = SSM_WIDTH + POOL_WIDTH + 3 * CONV_WIDTH
N_EXPERTS = 32
TOP_K = 4
D_EXPERT = 1024
SWIGLU_LIMIT = 7.0
SWIGLU_ALPHA = 1.702
EXPERT_BLOCK = 256
NORM_EPS = 1e-5

LANES = 128
SUBLANES = 8
TILE_T = 512
CHUNK = 16
N_CHUNK = TILE_T // CHUNK
PACKED = D_MODEL // 2
ROUTE_T = 2048
SC_WINDOW = 128
SC_ROW = PACKED // 2
HI_MASK = -65536
VMEM_LIMIT = 56 * 1024 * 1024


def _bf16_round(v):
    return v.astype(BF16).astype(F32)


def _pack_rows(v):
    lo = lax.shift_right_logical(lax.bitcast_convert_type(_bf16_round(v[:, :PACKED]), I32), 16)
    hi = lax.bitcast_convert_type(_bf16_round(v[:, PACKED:]), I32) & HI_MASK
    return hi | lo


def _unpack_lo(w):
    return lax.bitcast_convert_type(lax.shift_left(w, 16), F32)


def _unpack_hi(w):
    return lax.bitcast_convert_type(w & HI_MASK, F32)


def _rms(v, g):
    r = lax.rsqrt(jnp.mean(v * v, axis=-1, keepdims=True) + NORM_EPS)
    return (v * r) * g


def _moe_combine(x, y_ref, slab):
    parts = [x[:, i * SC_ROW:(i + 1) * SC_ROW] for i in range(4)]
    for k in range(TOP_K):
        g = lax.bitcast_convert_type(slab[:, TOP_K + k:TOP_K + k + 1], F32)
        for h in range(2):
            w = y_ref[k, h]
            parts[h] = parts[h] + g * _unpack_lo(w)
            parts[2 + h] = parts[2 + h] + g * _unpack_hi(w)
    return jnp.concatenate(parts, axis=1)


def _block_transpose8(vs, lane):
    vs = list(vs)
    for dist in (4, 2, 1):
        width = SSM_GROUP * dist
        low = (lane % (2 * width)) < width
        for j in range(8):
            if j & dist:
                continue
            a, b = vs[j], vs[j + dist]
            vs[j] = jnp.where(low, a, pltpu.roll(b, width, axis=1))
            vs[j + dist] = jnp.where(low, pltpu.roll(a, LANES - width, axis=1), b)
    return vs


def _cmul_packed(c1, c2, v):
    return c1 * v + c2 * pltpu.roll(v, SSM_STATE, axis=1)


def _chunk_scan(s, carry_in, cf):
    outs = []
    carry = jnp.broadcast_to(carry_in, (SUBLANES, LANES))
    for rg in range(N_CHUNK // SUBLANES):
        x = s[rg * SUBLANES:(rg + 1) * SUBLANES]
        for j, d in enumerate((1, 2, 4)):
            x = x + _cmul_packed(cf[2 * j], cf[2 * j + 1], pltpu.roll(x, d, axis=0))
        x = x + _cmul_packed(cf[6], cf[7], carry)
        outs.append(x)
        carry = jnp.broadcast_to(x[SUBLANES - 1:SUBLANES], (SUBLANES, LANES))
    return jnp.concatenate(outs, axis=0)


def _s5_mixer(u, first, minw_ref, mcat_ref, coef_ref, u_s, y_s, h_s):
    blocks = SSM_WIDTH // LANES
    for b4 in range(blocks):
        u_s[b4] = u[:, b4 * LANES:(b4 + 1) * LANES]
    lane = lax.broadcasted_iota(I32, (N_CHUNK, LANES), 1)

    halves = [[None, None] for _ in range(SSM_GROUPS)]
    for b4 in range(blocks):
        for hh in range(2):
            xs = [u_s[b4, pl.ds(8 * hh + j, N_CHUNK, stride=CHUNK), :].astype(BF16) for j in range(8)]
            ws = _block_transpose8(xs, lane)
            for gl in range(8):
                halves[8 * b4 + gl][hh] = ws[gl]
    ug = [jnp.concatenate(h2, axis=1) for h2 in halves]

    carry_in = jnp.where(first, 0.0, h_s[N_CHUNK + 7:N_CHUNK + 8, :])
    h_s[7:8, :] = carry_in
    for g in range(SSM_GROUPS):
        cols = slice(g * LANES, (g + 1) * LANES)
        s_g = jnp.dot(ug[g], minw_ref[g], preferred_element_type=F32)
        cf = [coef_ref[k, :, cols] for k in range(8)]
        h_s[8:8 + N_CHUNK, cols] = _chunk_scan(s_g, carry_in[:, cols], cf)

    yg = []
    for g in range(SSM_GROUPS):
        hprev = h_s[7:7 + N_CHUNK, g * LANES:(g + 1) * LANES].astype(BF16)
        lhs = jnp.concatenate([ug[g], hprev], axis=1)
        yg.append(jnp.dot(lhs, mcat_ref[g], preferred_element_type=F32).astype(BF16))

    for b4 in range(blocks):
        for hh in range(2):
            ws = [yg[8 * b4 + gl][:, hh * LANES:(hh + 1) * LANES] for gl in range(8)]
            xs = _block_transpose8(ws, lane)
            for j in range(8):
                y_s[b4, pl.ds(8 * hh + j, N_CHUNK, stride=CHUNK), :] = xs[j].astype(F32)
    return jnp.concatenate([y_s[b4] for b4 in range(blocks)], axis=1)


def _mixer_kernel(*refs, first_layer):
    if first_layer:
        x_ref = refs[0]
        refs = refs[1:]
    else:
        x_ref, yprev_ref, slabprev_ref = refs[:3]
        refs = refs[3:]
    (gmix_ref, win_ref, minw_ref, mcat_ref, coef_ref, dskip_ref, wglu_ref, bglu_ref, poolw_ref, pscale_ref,
     convw_ref, gbr_ref, wout_ref, gffn_ref, rw_ref, rb_ref,
     x1_ref, hp_ref, slab_ref, cnt_ref,
     u_s, y_s, h_s, pbuf_ref, zbuf_ref, cntacc_ref) = refs

    b = pl.program_id(0)
    l = pl.program_id(1)
    seq_start = l == 0

    @pl.when(jnp.logical_and(b == 0, l == 0))
    def _():
        cntacc_ref[...] = jnp.zeros_like(cntacc_ref)

    if first_layer:
        x = x_ref[...]
    else:
        x = _moe_combine(x_ref[...], yprev_ref, slabprev_ref[...])

    hn = _rms(x, gmix_ref[...]).astype(BF16)
    proj = jnp.dot(hn, win_ref[...], preferred_element_type=F32)
    u = proj[:, :SSM_WIDTH]
    p = proj[:, SSM_WIDTH:SSM_WIDTH + POOL_WIDTH]
    c0 = SSM_WIDTH + POOL_WIDTH
    bg = proj[:, c0:c0 + CONV_WIDTH]
    cg = proj[:, c0 + CONV_WIDTH:c0 + 2 * CONV_WIDTH]
    hv = proj[:, c0 + 2 * CONV_WIDTH:]

    y = _s5_mixer(u, seq_start, minw_ref, mcat_ref, coef_ref, u_s, y_s, h_s) + dskip_ref[...] * u
    z = jax.nn.gelu(y)
    glu = jnp.dot(z.astype(BF16), wglu_ref[...], preferred_element_type=F32) + bglu_ref[...]
    y_ssm = z * jax.nn.sigmoid(glu)

    tail = 16
    pbuf_ref[0:tail, :] = jnp.where(seq_start, 0.0, pbuf_ref[TILE_T:TILE_T + tail, :])
    pbuf_ref[tail:, :] = p
    lane_p = lax.broadcasted_iota(I32, (TILE_T, POOL_WIDTH), 1)
    win = jnp.where(lane_p < 64, 2, jnp.where(lane_p < 128, 4, jnp.where(lane_p < 192, 8, 16)))
    total = p
    for d in range(1, 16):
        total = total + jnp.where(win > d, pbuf_ref[pl.ds(tail - d, TILE_T), :], 0.0)
    row_p = lax.broadcasted_iota(I32, (TILE_T, POOL_WIDTH), 0) + l * TILE_T
    count = jnp.minimum(row_p + 1, win).astype(F32)
    pooled = total / count - p
    mixed = jnp.dot(pooled.astype(BF16), poolw_ref[...], preferred_element_type=F32)
    y_pool = mixed * pscale_ref[...]

    zc = cg * hv
    zbuf_ref[0:8, :] = jnp.where(seq_start, 0.0, zbuf_ref[TILE_T:TILE_T + 8, :])
    zbuf_ref[8:, :] = zc
    yc = (convw_ref[0:1, :] * zbuf_ref[pl.ds(6, TILE_T), :]
          + convw_ref[1:2, :] * zbuf_ref[pl.ds(7, TILE_T), :]
          + convw_ref[2:3, :] * zc)
    y_conv = bg * yc

    gbr = gbr_ref[...]
    s1 = SSM_WIDTH + POOL_WIDTH
    mixed_all = jnp.concatenate([_rms(y_ssm, gbr[:, :SSM_WIDTH]).astype(BF16),
                                 _rms(y_pool, gbr[:, SSM_WIDTH:s1]).astype(BF16),
                                 _rms(y_conv, gbr[:, s1:]).astype(BF16)], axis=1)
    x1 = x + jnp.dot(mixed_all, wout_ref[...], preferred_element_type=F32)
    x1_ref[...] = x1

    hn2 = _rms(x1, gffn_ref[...])
    packed = _pack_rows(hn2)
    hp_ref[0] = packed[:, :SC_ROW]
    hp_ref[1] = packed[:, SC_ROW:]
    logits = jnp.dot(hn2.astype(BF16), rw_ref[...], preferred_element_type=F32) + rb_ref[...]
    lane = lax.broadcasted_iota(I32, (TILE_T, LANES), 1)
    neg = jnp.float32(-jnp.inf)
    work = jnp.where(lane < N_EXPERTS, logits, neg)
    vals, idxs = [], []
    for _ in range(TOP_K):
        m = jnp.max(work, axis=1, keepdims=True)
        idx = jnp.min(jnp.where(work == m, lane, LANES), axis=1, keepdims=True)
        vals.append(m)
        idxs.append(idx)
        work = jnp.where(lane == idx, neg, work)
    exps = [jnp.exp(v - vals[0]) for v in vals]
    denom = exps[0] + exps[1] + exps[2] + exps[3]
    gates = [e / denom for e in exps]

    onehot = jnp.zeros((TILE_T, LANES), F32)
    for idx in idxs:
        onehot = jnp.where(lane == idx, 1.0, onehot)
    r_i = lax.broadcasted_iota(I32, (TILE_T, TILE_T), 0)
    c_i = lax.broadcasted_iota(I32, (TILE_T, TILE_T), 1)
    ltri = jnp.where(c_i < r_i, 1.0, 0.0).astype(BF16)
    before = jnp.dot(ltri, onehot.astype(BF16), preferred_element_type=F32) + cntacc_ref[0:1, :]
    slab = jnp.zeros((TILE_T, LANES), I32)
    for k in range(TOP_K):
        rank = jnp.sum(jnp.where(lane == idxs[k], before, 0.0), axis=1, keepdims=True).astype(I32)
        slab = jnp.where(lane == k, idxs[k], slab)
        slab = jnp.where(lane == TOP_K + k, lax.bitcast_convert_type(gates[k], I32), slab)
        slab = jnp.where(lane == 2 * TOP_K + k, rank, slab)
    slab_ref[...] = slab
    newcnt = cntacc_ref[0:1, :] + jnp.sum(onehot, axis=0, keepdims=True)
    cntacc_ref[...] = jnp.broadcast_to(newcnt, cntacc_ref.shape)
    cnt_ref[...] = jnp.broadcast_to(newcnt, cnt_ref.shape).astype(I32)


def _const_spec(shape):
    nd = len(shape)
    return pl.BlockSpec(shape, lambda b, l, nd=nd: (0,) * nd)


def _mixer_call(x, x_tile0, yprev, slabprev, lp, batch, seq):
    first_layer = yprev is None
    n_tok = batch * seq
    steps = seq // TILE_T
    tok_spec = lambda w: pl.BlockSpec((TILE_T, w), lambda b, l: (b * steps + l, 0))
    in_specs = [pl.BlockSpec((TILE_T, D_MODEL), lambda b, l: (x_tile0 + b * steps + l, 0))]
    args = [x]
    if not first_layer:
        in_specs += [pl.BlockSpec((TOP_K, 2, TILE_T, SC_ROW), lambda b, l: (0, 0, b * steps + l, 0)), tok_spec(LANES)]
        args += [yprev, slabprev]
    weights = [lp['gmix'], lp['win'], lp['minw'], lp['mcat'], lp['coef'], lp['dskip'], lp['wglu'], lp['bglu'],
               lp['poolw'], lp['pscale'], lp['convw'], lp['gbr'], lp['wout'], lp['gffn'], lp['rw'], lp['rb']]
    in_specs += [_const_spec(w.shape) for w in weights]
    args += weights
    out_shape = [jax.ShapeDtypeStruct((n_tok, D_MODEL), F32),
                 jax.ShapeDtypeStruct((2, n_tok, SC_ROW), I32),
                 jax.ShapeDtypeStruct((n_tok, LANES), I32),
                 jax.ShapeDtypeStruct((SUBLANES, LANES), I32)]
    out_specs = [tok_spec(D_MODEL), pl.BlockSpec((2, TILE_T, SC_ROW), lambda b, l: (0, b * steps + l, 0)), tok_spec(LANES),
                 pl.BlockSpec((SUBLANES, LANES), lambda b, l: (0, 0))]
    scratch = [pltpu.VMEM((SSM_WIDTH // LANES, TILE_T, LANES), F32),
               pltpu.VMEM((SSM_WIDTH // LANES, TILE_T, LANES), F32),
               pltpu.VMEM((SUBLANES + N_CHUNK, SSM_GROUPS * LANES), F32),
               pltpu.VMEM((16 + TILE_T, POOL_WIDTH), F32),
               pltpu.VMEM((8 + TILE_T, CONV_WIDTH), F32),
               pltpu.VMEM((SUBLANES, LANES), F32)]
    return pl.pallas_call(
        functools.partial(_mixer_kernel, first_layer=first_layer),
        grid=(batch, steps),
        in_specs=in_specs, out_specs=out_specs, out_shape=out_shape, scratch_shapes=scratch,
        compiler_params=pltpu.CompilerParams(dimension_semantics=("arbitrary", "arbitrary"),
                                             vmem_limit_bytes=VMEM_LIMIT),
        name="mixer_first" if first_layer else "mixer_next",
    )(*args)


def _route_kernel(slab_ref, cnt_ref, dest_ref, meta_ref, *, n_blocks):
    lane1 = lax.broadcasted_iota(I32, (1, LANES), 1)
    counts = jnp.where(lane1 < N_EXPERTS, cnt_ref[0:1, :], 0)
    padded = ((counts + (EXPERT_BLOCK - 1)) // EXPERT_BLOCK) * EXPERT_BLOCK
    pend = padded
    for sh in (1, 2, 4, 8, 16):
        pend = pend + jnp.where(lane1 >= sh, pltpu.roll(pend, sh, axis=1), 0)
    pstart = pend - padded

    row = lax.broadcasted_iota(I32, meta_ref.shape, 0)
    meta_ref[...] = jnp.where(row == 0, pstart // EXPERT_BLOCK, jnp.where(row == 1, padded // EXPERT_BLOCK, 0))

    slab = slab_ref[...]
    lane = lax.broadcasted_iota(I32, slab.shape, 1)
    out = jnp.zeros(slab.shape, F32)
    for k in range(TOP_K):
        idx = slab[:, k:k + 1]
        rank = slab[:, 2 * TOP_K + k:2 * TOP_K + k + 1]
        base = jnp.sum(jnp.where(lane == idx, pstart, 0), axis=1, keepdims=True)
        for h in range(2):
            out = jnp.where(lane == 2 * k + h, (base + rank + h * n_blocks * EXPERT_BLOCK).astype(F32), out)
    dest_ref[...] = out.T[:SUBLANES, :].astype(I32)


def _route_call(slab, counts, n_tok, n_blocks):
    meta_rows = SUBLANES
    return pl.pallas_call(
        functools.partial(_route_kernel, n_blocks=n_blocks),
        grid=(n_tok // ROUTE_T,),
        in_specs=[pl.BlockSpec((ROUTE_T, LANES), lambda i: (i, 0)),
                  pl.BlockSpec((SUBLANES, LANES), lambda i: (0, 0))],
        out_specs=[pl.BlockSpec((SUBLANES, ROUTE_T), lambda i: (0, i)),
                   pl.BlockSpec((meta_rows, LANES), lambda i: (0, 0))],
        out_shape=[jax.ShapeDtypeStruct((SUBLANES, n_tok), I32),
                   jax.ShapeDtypeStruct((meta_rows, LANES), I32)],
        compiler_params=pltpu.CompilerParams(dimension_semantics=("arbitrary",)),
        name="route",
    )(slab, counts)


def _sc_mesh():
    return plsc.VectorSubcoreMesh(core_axis_name="core", subcore_axis_name="subcore")


def _dispatch(src, idx, n_rows):
    windows = src.shape[0] // SC_WINDOW

    @functools.partial(pl.kernel, out_type=jax.ShapeDtypeStruct((n_rows, SC_ROW), src.dtype),
                       mesh=_sc_mesh(), scratch_types=[])
    def scatter_rows(x_hbm, i_hbm, o_hbm):
        def body(x_vmem, i_vmem):
            pltpu.sync_copy(x_vmem, o_hbm.at[i_vmem.at[0]])

        pltpu.emit_pipeline(
            body,
            grid=(idx.shape[1] // SC_WINDOW,),
            in_specs=[pl.BlockSpec((SC_WINDOW, SC_ROW), lambda i: (lax.rem(i, windows), 0)),
                      pl.BlockSpec((1, SC_WINDOW), lambda i: (0, i))],
            out_specs=[],
            core_axis_name=("core", "subcore"),
            dimension_semantics=(pltpu.PARALLEL,),
        )(x_hbm, i_hbm)

    return scatter_rows(src, idx)


def _combine(src, idx):
    n_idx = idx.shape[1]

    @functools.partial(pl.kernel, out_type=jax.ShapeDtypeStruct((n_idx, SC_ROW), src.dtype),
                       mesh=_sc_mesh(), scratch_types=[])
    def gather_rows(y_hbm, i_hbm, o_hbm):
        def body(i_vmem, o_vmem):
            pltpu.sync_copy(y_hbm.at[i_vmem.at[0]], o_vmem)

        pltpu.emit_pipeline(
            body,
            grid=(n_idx // SC_WINDOW,),
            in_specs=[pl.BlockSpec((1, SC_WINDOW), lambda i: (0, i))],
            out_specs=[pl.BlockSpec((SC_WINDOW, SC_ROW), lambda i: (i, 0))],
            core_axis_name=("core", "subcore"),
            dimension_semantics=(pltpu.PARALLEL,),
        )(i_hbm, o_hbm)

    return gather_rows(src, idx)


def _expert_block(xw, wgu_s, bgu_ref, wdn_s, bdn_ref, act_s):
    w0 = xw[0]
    w1 = xw[1]
    xb = jnp.concatenate([_unpack_lo(w0).astype(BF16), _unpack_lo(w1).astype(BF16),
                          _unpack_hi(w0).astype(BF16), _unpack_hi(w1).astype(BF16)], axis=1)
    half = D_EXPERT // 2
    for j in range(2):
        cols = slice(j * half, (j + 1) * half)
        ucols = slice(D_EXPERT + j * half, D_EXPERT + (j + 1) * half)
        g = jnp.dot(xb, wgu_s[:, cols], preferred_element_type=F32) + bgu_ref[:, cols]
        up = jnp.dot(xb, wgu_s[:, ucols], preferred_element_type=F32) + bgu_ref[:, ucols]
        g = jnp.minimum(g, SWIGLU_LIMIT)
        up = jnp.clip(up, -SWIGLU_LIMIT, SWIGLU_LIMIT)
        act_s[:, cols] = (g * jax.nn.sigmoid(SWIGLU_ALPHA * g) * (up + 1.0)).astype(BF16)
    y = jnp.dot(act_s[...], wdn_s[...], preferred_element_type=F32) + bdn_ref[...]
    return _pack_rows(y)


def _expert_kernel(first_ref, nblk_ref, xs_hbm, wgu_ref, bgu_ref, wdn_ref, bdn_ref, ys_hbm,
                   wgu_s, wdn_s, act_s, xbuf, ybuf, sem_in, sem_out):
    e = pl.program_id(0)
    nb = nblk_ref[e]
    b0 = first_ref[e]

    def rows_of(i):
        return pl.ds(pl.multiple_of((b0 + i) * EXPERT_BLOCK, EXPERT_BLOCK), EXPERT_BLOCK)

    def x_copy(i, slot):
        return pltpu.make_async_copy(xs_hbm.at[:, rows_of(i), :], xbuf.at[slot], sem_in.at[slot])

    def y_copy(i, slot):
        return pltpu.make_async_copy(ybuf.at[slot], ys_hbm.at[:, rows_of(i), :], sem_out.at[slot])

    @pl.when(nb > 0)
    def _():
        x_copy(0, 0).start(priority=1)
        chunk = 128

        def cast_gu(i, _):
            rows = pl.ds(pl.multiple_of(i * chunk, chunk), chunk)
            wgu_s[rows, :] = wgu_ref[rows, :].astype(BF16)
            return 0

        def cast_dn(i, _):
            rows = pl.ds(pl.multiple_of(i * chunk, chunk), chunk)
            wdn_s[rows, :] = wdn_ref[rows, :].astype(BF16)
            return 0

        lax.fori_loop(0, D_MODEL // chunk, cast_gu, 0)
        lax.fori_loop(0, D_EXPERT // chunk, cast_dn, 0)

        def pair(j, _):
            for slot in range(2):
                i = 2 * j + slot

                @pl.when(i < nb)
                def _(i=i, slot=slot):
                    x_copy(i, slot).wait()

                    @pl.when(i + 1 < nb)
                    def _():
                        x_copy(i + 1, 1 - slot).start(priority=1)

                    @pl.when(i >= 2)
                    def _():
                        y_copy(i - 2, slot).wait()

                    packed = _expert_block(xbuf[slot], wgu_s, bgu_ref, wdn_s, bdn_ref, act_s)
                    ybuf[slot, 0] = packed[:, :SC_ROW]
                    ybuf[slot, 1] = packed[:, SC_ROW:]
                    y_copy(i, slot).start()
            return 0

        lax.fori_loop(0, (nb + 1) // 2, pair, 0)

        for slot in range(2):
            @pl.when(nb > slot)
            def _(slot=slot):
                last = nb - 1 - lax.rem(nb - 1 - slot, 2)
                y_copy(last, slot).wait()


def _expert_call(xs, first_blk, n_blk, w_gu, b_gu, w_dn, b_dn, layer):
    def w_map(e, first, nblk):
        return (layer, e, 0, 0)

    grid_spec = pltpu.PrefetchScalarGridSpec(
        num_scalar_prefetch=2,
        grid=(N_EXPERTS,),
        in_specs=[pl.BlockSpec(memory_space=pl.ANY),
                  pl.BlockSpec((None, None, D_MODEL, 2 * D_EXPERT), w_map),
                  pl.BlockSpec((None, None, 1, 2 * D_EXPERT), w_map),
                  pl.BlockSpec((None, None, D_EXPERT, D_MODEL), w_map),
                  pl.BlockSpec((None, None, 1, D_MODEL), w_map)],
        out_specs=pl.BlockSpec(memory_space=pl.ANY),
        scratch_shapes=[pltpu.VMEM((D_MODEL, 2 * D_EXPERT), BF16),
                        pltpu.VMEM((D_EXPERT, D_MODEL), BF16),
                        pltpu.VMEM((EXPERT_BLOCK, D_EXPERT), BF16),
                        pltpu.VMEM((2, 2, EXPERT_BLOCK, SC_ROW), I32),
                        pltpu.VMEM((2, 2, EXPERT_BLOCK, SC_ROW), I32),
                        pltpu.SemaphoreType.DMA((2,)),
                        pltpu.SemaphoreType.DMA((2,))])
    return pl.pallas_call(
        _expert_kernel,
        grid_spec=grid_spec,
        out_shape=jax.ShapeDtypeStruct(xs.shape, I32),
        compiler_params=pltpu.CompilerParams(dimension_semantics=("arbitrary",),
                                             vmem_limit_bytes=VMEM_LIMIT),
        name="experts",
    )(first_blk, n_blk, xs, w_gu, b_gu, w_dn, b_dn)


def _final_kernel(x_ref, y_ref, slab_ref, g_ref, *rest):
    o_ref = rest[-1]
    x = _moe_combine(x_ref[...], y_ref, slab_ref[...])
    o_ref[...] = _rms(x, g_ref[...])


def _final_call(x1, ysg, slab, g, out_tile0, n_out, out_prev):
    t = TILE_T
    in_specs = [pl.BlockSpec((t, D_MODEL), lambda i: (i, 0)),
                pl.BlockSpec((TOP_K, 2, t, SC_ROW), lambda i: (0, 0, i, 0)),
                pl.BlockSpec((t, LANES), lambda i: (i, 0)),
                pl.BlockSpec((1, D_MODEL), lambda i: (0, 0))]
    args = [x1, ysg, slab, g]
    aliases = {}
    if out_prev is not None:
        in_specs.append(pl.BlockSpec(memory_space=pl.ANY))
        args.append(out_prev)
        aliases = {4: 0}
    return pl.pallas_call(
        _final_kernel,
        grid=(x1.shape[0] // t,),
        in_specs=in_specs,
        out_specs=pl.BlockSpec((t, D_MODEL), lambda i: (out_tile0 + i, 0)),
        out_shape=jax.ShapeDtypeStruct((n_out, D_MODEL), F32),
        input_output_aliases=aliases,
        compiler_params=pltpu.CompilerParams(dimension_semantics=("arbitrary",)),
        name="final_norm",
    )(*args)


def _s5_tables(lam_re, lam_im, b_re, b_im, c_re, c_im, log_step):
    lr = jnp.minimum(lam_re.astype(F32), LAMBDA_RE_MAX)
    li = lam_im.astype(F32)
    step = jnp.exp(log_step.astype(F32))[:, None]
    mag = jnp.exp(lr * step)
    ar = mag * jnp.cos(li * step)
    ai = mag * jnp.sin(li * step)
    nr = ar - 1.0
    den = lr * lr + li * li
    kr = (nr * lr + ai * li) / den
    ki = (ai * lr - nr * li) / den
    bre = b_re.astype(F32)
    bim = b_im.astype(F32)
    bbar_r = kr[..., None] * bre - ki[..., None] * bim
    bbar_i = kr[..., None] * bim + ki[..., None] * bre
    cre = c_re.astype(F32)
    cim = c_im.astype(F32)

    def power(m):
        m = jnp.asarray(m, F32)[..., None, None]
        mg = jnp.exp(m * lr * step)
        return mg * jnp.cos(m * li * step), mg * jnp.sin(m * li * step)

    pw_r, pw_i = power(jnp.arange(CHUNK + 1))
    cp_r = cre[None] * pw_r[:, :, None, :] - cim[None] * pw_i[:, :, None, :]
    cp_i = cre[None] * pw_i[:, :, None, :] + cim[None] * pw_r[:, :, None, :]
    kmat = (jnp.einsum('tgap,gph->tgah', cp_r[:CHUNK], bbar_r) - jnp.einsum('tgap,gph->tgah', cp_i[:CHUNK], bbar_i))
    lag = jnp.arange(CHUNK)[None, :] - jnp.arange(CHUNK)[:, None]
    ktoe = jnp.where((lag >= 0)[:, :, None, None, None], kmat[jnp.clip(lag, 0, CHUNK - 1)], 0.0)
    m_intra = ktoe.transpose(2, 0, 4, 1, 3).reshape(SSM_GROUPS, CHUNK * SSM_GROUP, CHUNK * SSM_GROUP)
    mo_r = cp_r[1:].transpose(1, 3, 0, 2).reshape(SSM_GROUPS, SSM_STATE, CHUNK * SSM_GROUP)
    mo_i = -cp_i[1:].transpose(1, 3, 0, 2).reshape(SSM_GROUPS, SSM_STATE, CHUNK * SSM_GROUP)
    mcat = jnp.concatenate([m_intra, mo_r, mo_i], axis=1).astype(BF16)
    q_r = pw_r[CHUNK - 1::-1][:CHUNK][:, :, :, None]
    q_i = pw_i[CHUNK - 1::-1][:CHUNK][:, :, :, None]
    mn_r = q_r * bbar_r[None] - q_i * bbar_i[None]
    mn_i = q_r * bbar_i[None] + q_i * bbar_r[None]
    minw = jnp.concatenate([mn_r, mn_i], axis=2).transpose(1, 0, 3, 2).reshape(
        SSM_GROUPS, CHUNK * SSM_GROUP, 2 * SSM_STATE).astype(BF16)

    def packed(j):
        pr, pi = power(float(CHUNK * j))
        return (jnp.concatenate([pr, pr], axis=-1).reshape(1, SSM_GROUPS * LANES),
                jnp.concatenate([-pi, pi], axis=-1).reshape(1, SSM_GROUPS * LANES))

    rows = jnp.arange(SUBLANES)[:, None]
    kinds = []
    for d in (1, 2, 4):
        c1, c2 = packed(d)
        mask = (rows >= d).astype(F32)
        kinds += [mask * c1, mask * c2]
    pk = [packed(r + 1) for r in range(SUBLANES)]
    kinds += [jnp.concatenate([p[0] for p in pk], axis=0), jnp.concatenate([p[1] for p in pk], axis=0)]
    coef = jnp.stack(kinds, axis=0)
    return minw, mcat, coef


def _layer_params(l, norm_mix, w_in, ssm_lam_re, ssm_lam_im, ssm_b_re, ssm_b_im, ssm_c_re, ssm_c_im, ssm_d,
                  ssm_log_step, ssm_w_glu, ssm_b_glu, pool_w, pool_scale, conv_w, branch_norm, w_out, norm_ffn,
                  router_w, router_b):
    minw, mcat, coef = _s5_tables(ssm_lam_re[l], ssm_lam_im[l], ssm_b_re[l], ssm_b_im[l], ssm_c_re[l], ssm_c_im[l],
                                  ssm_log_step[l])
    eye4 = jnp.eye(len(POOL_WINDOWS), dtype=F32)
    poolw = jnp.einsum('gcd,gk->gckd', pool_w[l].astype(F32), eye4).reshape(POOL_WIDTH, POOL_WIDTH).astype(BF16)
    rw = jnp.zeros((D_MODEL, LANES), F32).at[:, :N_EXPERTS].set(router_w[l]).astype(BF16)
    rb = jnp.zeros((1, LANES), F32).at[0, :N_EXPERTS].set(router_b[l])
    row = lambda v: v.reshape(1, -1).astype(F32)
    return dict(gmix=row(norm_mix[l]), win=w_in[l].astype(BF16), minw=minw, mcat=mcat, coef=coef, dskip=row(ssm_d[l]),
                wglu=ssm_w_glu[l].astype(BF16), bglu=row(ssm_b_glu[l]), poolw=poolw, pscale=row(pool_scale[l]),
                convw=conv_w[l].astype(F32), gbr=row(branch_norm[l]), wout=w_out[l].astype(BF16),
                gffn=row(norm_ffn[l]), rw=rw, rb=rb)


def kernel(x, norm_mix, w_in, ssm_lam_re, ssm_lam_im, ssm_b_re, ssm_b_im, ssm_c_re, ssm_c_im, ssm_d, ssm_log_step, ssm_w_glu, ssm_b_glu, pool_w, pool_scale, conv_w, branch_norm, w_out, norm_ffn, router_w, router_b, w_gate_up, b_gate_up, w_down, b_down, final_norm):
    batch, seq, d_model = x.shape
    assert d_model == D_MODEL and seq % TILE_T == 0
    depth = w_in.shape[0]
    n_streams = 2 if batch % 2 == 0 else 1
    sb = batch // n_streams
    n_tok = sb * seq
    assert n_tok % ROUTE_T == 0 and n_tok % SC_WINDOW == 0
    n_blocks = -(-(n_tok * TOP_K + N_EXPERTS * (EXPERT_BLOCK - 1)) // EXPERT_BLOCK)
    n_rows = n_blocks * EXPERT_BLOCK
    tiles = n_tok // TILE_T
    b_gu = b_gate_up.reshape(depth, N_EXPERTS, 1, 2 * D_EXPERT)
    b_dn = b_down.reshape(depth, N_EXPERTS, 1, D_MODEL)

    xin = [x.reshape(batch * seq, D_MODEL)] * n_streams
    tile0 = [s * tiles for s in range(n_streams)]
    ysg = [None] * n_streams
    slab = [None] * n_streams
    for l in range(depth):
        lp = _layer_params(l, norm_mix, w_in, ssm_lam_re, ssm_lam_im, ssm_b_re, ssm_b_im, ssm_c_re, ssm_c_im, ssm_d,
                           ssm_log_step, ssm_w_glu, ssm_b_glu, pool_w, pool_scale, conv_w, branch_norm, w_out,
                           norm_ffn, router_w, router_b)
        routed = []
        for s in range(n_streams):
            xin[s], hp, slab[s], counts = _mixer_call(xin[s], tile0[s], ysg[s], slab[s], lp, sb, seq)
            tile0[s] = 0
            dest, meta = _route_call(slab[s], counts, n_tok, n_blocks)
            idx = dest.reshape(1, 2 * TOP_K * n_tok)
            xs = _dispatch(hp.reshape(2 * n_tok, SC_ROW), idx, 2 * n_rows).reshape(2, n_rows, SC_ROW)
            routed.append((xs, idx, meta[0, :N_EXPERTS], meta[1, :N_EXPERTS]))
        for s, (xs, idx, first_blk, n_blk) in enumerate(routed):
            ys = _expert_call(xs, first_blk, n_blk, w_gate_up, b_gu, w_down, b_dn, l)
            ysg[s] = _combine(ys.reshape(2 * n_rows, SC_ROW), idx).reshape(TOP_K, 2, n_tok, SC_ROW)
    g = final_norm.reshape(1, D_MODEL).astype(F32)
    out = None
    for s in range(n_streams):
        out = _final_call(xin[s], ysg[s], slab[s], g, s * tiles, batch * seq, out)
    return out.reshape(batch, seq, D_MODEL)
```

```python
import functools
import math

import jax
import jax.numpy as jnp
from jax import lax
from jax.experimental import pallas as pl
from jax.experimental.pallas import tpu as pltpu
from jax.experimental.pallas import tpu_sc as plsc

F32 = jnp.float32
BF16 = jnp.bfloat16
I32 = jnp.int32

D_MODEL = 1024
SSM_WIDTH = 512
POOL_WIDTH = 256
CONV_WIDTH = 256
SSM_GROUP = 16
SSM_GROUPS = 32
SSM_STATE = 64
LAMBDA_RE_MAX = -1e-4
POOL_WINDOWS = (2, 4, 8, 16)
POOL_GROUP = 64
CONV_K = 3
IN_WIDTH = SSM_WIDTH + POOL_WIDTH + 3 * CONV_WIDTH
N_EXPERTS = 32
TOP_K = 4
D_EXPERT = 1024
SWIGLU_LIMIT = 7.0
SWIGLU_ALPHA = 1.702
EXPERT_BLOCK = 256
NORM_EPS = 1e-5

LANES = 128
SUBLANES = 8
TILE_T = 512
CHUNK = 16
N_CHUNK = TILE_T // CHUNK
POOL_TAIL = SUBLANES * len(POOL_WINDOWS)
assert POOL_WINDOWS == tuple(2 ** (k + 1) for k in range(len(POOL_WINDOWS))) and POOL_TAIL >= POOL_WINDOWS[-1]
PACKED = D_MODEL // 2
ROUTE_T = 2048
SC_WINDOW = 128
SC_ROW = PACKED // 2
HI_MASK = -65536
VMEM_LIMIT = 56 * 1024 * 1024


def _bf16_round(v):
    return v.astype(BF16).astype(F32)


def _pack_rows(v):
    lo = lax.shift_right_logical(lax.bitcast_convert_type(_bf16_round(v[:, :PACKED]), I32), 16)
    hi = lax.bitcast_convert_type(_bf16_round(v[:, PACKED:]), I32) & HI_MASK
    return hi | lo


def _unpack_lo(w):
    return lax.bitcast_convert_type(lax.shift_left(w, 16), F32)


def _unpack_hi(w):
    return lax.bitcast_convert_type(w & HI_MASK, F32)


def _rms(v, g):
    r = lax.rsqrt(jnp.mean(v * v, axis=-1, keepdims=True) + NORM_EPS)
    return (v * r) * g


def _moe_combine(x, y_ref, slab):
    parts = [x[:, i * SC_ROW:(i + 1) * SC_ROW] for i in range(4)]
    for k in range(TOP_K):
        g = lax.bitcast_convert_type(slab[:, TOP_K + k:TOP_K + k + 1], F32)
        for h in range(2):
            w = y_ref[k, h]
            parts[h] = parts[h] + g * _unpack_lo(w)
            parts[2 + h] = parts[2 + h] + g * _unpack_hi(w)
    return jnp.concatenate(parts, axis=1)


def _block_transpose8(vs, lane):
    vs = list(vs)
    for dist in (4, 2, 1):
        width = SSM_GROUP * dist
        low = (lane % (2 * width)) < width
        for j in range(8):
            if j & dist:
                continue
            a, b = vs[j], vs[j + dist]
            vs[j] = jnp.where(low, a, pltpu.roll(b, width, axis=1))
            vs[j + dist] = jnp.where(low, pltpu.roll(a, LANES - width, axis=1), b)
    return vs


def _cmul_packed(c1, c2, v):
    return c1 * v + c2 * pltpu.roll(v, SSM_STATE, axis=1)


def _chunk_scan(s, carry_in, cf):
    outs = []
    carry = jnp.broadcast_to(carry_in, (SUBLANES, LANES))
    for rg in range(N_CHUNK // SUBLANES):
        x = s[rg * SUBLANES:(rg + 1) * SUBLANES]
        for j, d in enumerate((1, 2, 4)):
            x = x + _cmul_packed(cf[2 * j], cf[2 * j + 1], pltpu.roll(x, d, axis=0))
        x = x + _cmul_packed(cf[6], cf[7], carry)
        outs.append(x)
        carry = jnp.broadcast_to(x[SUBLANES - 1:SUBLANES], (SUBLANES, LANES))
    return jnp.concatenate(outs, axis=0)


def _s5_mixer(u, first, minw_ref, mcat_ref, coef_ref, u_s, y_s, h_s):
    blocks = SSM_WIDTH // LANES
    for b4 in range(blocks):
        u_s[b4] = u[:, b4 * LANES:(b4 + 1) * LANES]
    lane = lax.broadcasted_iota(I32, (N_CHUNK, LANES), 1)

    halves = [[None, None] for _ in range(SSM_GROUPS)]
    for b4 in range(blocks):
        for hh in range(2):
            xs = [u_s[b4, pl.ds(8 * hh + j, N_CHUNK, stride=CHUNK), :].astype(BF16) for j in range(8)]
            ws = _block_transpose8(xs, lane)
            for gl in range(8):
                halves[8 * b4 + gl][hh] = ws[gl]
    ug = [jnp.concatenate(h2, axis=1) for h2 in halves]

    carry_in = jnp.where(first, 0.0, h_s[N_CHUNK + 7:N_CHUNK + 8, :])
    h_s[7:8, :] = carry_in
    for g in range(SSM_GROUPS):
        cols = slice(g * LANES, (g + 1) * LANES)
        s_g = jnp.dot(ug[g], minw_ref[g], preferred_element_type=F32)
        cf = [coef_ref[k, :, cols] for k in range(8)]
        h_s[8:8 + N_CHUNK, cols] = _chunk_scan(s_g, carry_in[:, cols], cf)

    yg = []
    for g in range(SSM_GROUPS):
        hprev = h_s[7:7 + N_CHUNK, g * LANES:(g + 1) * LANES].astype(BF16)
        lhs = jnp.concatenate([ug[g], hprev], axis=1)
        yg.append(jnp.dot(lhs, mcat_ref[g], preferred_element_type=F32).astype(BF16))

    for b4 in range(blocks):
        for hh in range(2):
            ws = [yg[8 * b4 + gl][:, hh * LANES:(hh + 1) * LANES] for gl in range(8)]
            xs = _block_transpose8(ws, lane)
            for j in range(8):
                y_s[b4, pl.ds(8 * hh + j, N_CHUNK, stride=CHUNK), :] = xs[j].astype(F32)
    return jnp.concatenate([y_s[b4] for b4 in range(blocks)], axis=1)


def _mixer_kernel(*refs, first_layer):
    if first_layer:
        x_ref = refs[0]
        refs = refs[1:]
    else:
        x_ref, yprev_ref, slabprev_ref = refs[:3]
        refs = refs[3:]
    (gmix_ref, win_ref, minw_ref, mcat_ref, coef_ref, dskip_ref, wglu_ref, bglu_ref, poolw_ref, pscale_ref,
     convw_ref, gbr_ref, wout_ref, gffn_ref, rw_ref, rb_ref, ltri_ref,
     x1_ref, hp_ref, slab_ref, cnt_ref,
     u_s, y_s, h_s, pbuf_ref, zbuf_ref, cntacc_ref) = refs

    b = pl.program_id(0)
    l = pl.program_id(1)
    seq_start = l == 0

    @pl.when(jnp.logical_and(b == 0, l == 0))
    def _():
        cntacc_ref[...] = jnp.zeros_like(cntacc_ref)

    if first_layer:
        x = x_ref[...]
    else:
        x = _moe_combine(x_ref[...], yprev_ref, slabprev_ref[...])

    hn = _rms(x, gmix_ref[...]).astype(BF16)
    proj = jnp.dot(hn, win_ref[...], preferred_element_type=F32)
    u = proj[:, :SSM_WIDTH]
    p = proj[:, SSM_WIDTH:SSM_WIDTH + POOL_WIDTH]
    c0 = SSM_WIDTH + POOL_WIDTH
    bg = proj[:, c0:c0 + CONV_WIDTH]
    cg = proj[:, c0 + CONV_WIDTH:c0 + 2 * CONV_WIDTH]
    hv = proj[:, c0 + 2 * CONV_WIDTH:]

    y = _s5_mixer(u, seq_start, minw_ref, mcat_ref, coef_ref, u_s, y_s, h_s) + dskip_ref[...] * u
    z = jax.nn.gelu(y)
    glu = jnp.dot(z.astype(BF16), wglu_ref[...], preferred_element_type=F32) + bglu_ref[...]
    y_ssm = z * jax.nn.sigmoid(glu)

    tail = POOL_TAIL
    pbuf_ref[0, 0:tail, :] = jnp.where(seq_start, 0.0, pbuf_ref[0, TILE_T:TILE_T + tail, :])
    pbuf_ref[0, tail:, :] = p
    sums = []
    for stage, w in enumerate(POOL_WINDOWS):
        src = 0 if stage == 0 else 1 + (stage - 1) % 2
        dst = 1 + stage % 2
        d = w // 2
        lo = SUBLANES * (stage + 1)
        n = tail + TILE_T - lo
        e = pbuf_ref[src, pl.ds(lo, n), :] + pbuf_ref[src, pl.ds(lo - d, n), :]
        if stage + 1 < len(POOL_WINDOWS):
            pbuf_ref[dst, pl.ds(lo, n), :] = e
        sums.append(e[tail - lo:, :])
    lane_p = lax.broadcasted_iota(I32, (TILE_T, POOL_WIDTH), 1)
    total = sums[-1]
    win = jnp.full((TILE_T, POOL_WIDTH), POOL_WINDOWS[-1], I32)
    for i in range(len(POOL_WINDOWS) - 2, -1, -1):
        sel = lane_p < (i + 1) * POOL_GROUP
        total = jnp.where(sel, sums[i], total)
        win = jnp.where(sel, POOL_WINDOWS[i], win)
    row_p = lax.broadcasted_iota(I32, (TILE_T, POOL_WIDTH), 0) + l * TILE_T
    count = jnp.minimum(row_p + 1, win).astype(F32)
    pooled = total / count - p
    mixed = jnp.dot(pooled.astype(BF16), poolw_ref[...], preferred_element_type=F32)
    y_pool = mixed * pscale_ref[...]

    zc = cg * hv
    zbuf_ref[0:8, :] = jnp.where(seq_start, 0.0, zbuf_ref[TILE_T:TILE_T + 8, :])
    zbuf_ref[8:, :] = zc
    yc = (convw_ref[0:1, :] * zbuf_ref[pl.ds(6, TILE_T), :]
          + convw_ref[1:2, :] * zbuf_ref[pl.ds(7, TILE_T), :]
          + convw_ref[2:3, :] * zc)
    y_conv = bg * yc

    gbr = gbr_ref[...]
    s1 = SSM_WIDTH + POOL_WIDTH
    mixed_all = jnp.concatenate([_rms(y_ssm, gbr[:, :SSM_WIDTH]).astype(BF16),
                                 _rms(y_pool, gbr[:, SSM_WIDTH:s1]).astype(BF16),
                                 _rms(y_conv, gbr[:, s1:]).astype(BF16)], axis=1)
    x1 = x + jnp.dot(mixed_all, wout_ref[...], preferred_element_type=F32)
    x1_ref[...] = x1

    hn2 = _rms(x1, gffn_ref[...])
    packed = _pack_rows(hn2)
    hp_ref[0] = packed[:, :SC_ROW]
    hp_ref[1] = packed[:, SC_ROW:]
    logits = jnp.dot(hn2.astype(BF16), rw_ref[...], preferred_element_type=F32) + rb_ref[...]
    lane = lax.broadcasted_iota(I32, (TILE_T, LANES), 1)
    lane_f = lane.astype(F32)
    neg = jnp.float32(-jnp.inf)
    work = jnp.where(lane < N_EXPERTS, logits, neg)
    vals, idxs = [], []
    for _ in range(TOP_K):
        m = jnp.max(work, axis=1, keepdims=True)
        idx = jnp.min(jnp.where(work == m, lane_f, float(LANES)), axis=1, keepdims=True)
        vals.append(m)
        idxs.append(idx)
        work = jnp.where(lane_f == idx, neg, work)
    exps = [jnp.exp(v - vals[0]) for v in vals]
    denom = exps[0] + exps[1] + exps[2] + exps[3]
    gates = [e / denom for e in exps]

    onehot = jnp.zeros((TILE_T, LANES), F32)
    for idx in idxs:
        onehot = jnp.where(lane_f == idx, 1.0, onehot)
    before = jnp.dot(ltri_ref[...], onehot.astype(BF16), preferred_element_type=F32) + cntacc_ref[0:1, :]
    slab = jnp.zeros((TILE_T, LANES), I32)
    for k in range(TOP_K):
        rank = jnp.sum(jnp.where(lane_f == idxs[k], before, 0.0), axis=1, keepdims=True).astype(I32)
        slab = jnp.where(lane == k, idxs[k].astype(I32), slab)
        slab = jnp.where(lane == TOP_K + k, lax.bitcast_convert_type(gates[k], I32), slab)
        slab = jnp.where(lane == 2 * TOP_K + k, rank, slab)
    slab_ref[...] = slab
    newcnt = cntacc_ref[0:1, :] + jnp.sum(onehot, axis=0, keepdims=True)
    cntacc_ref[...] = jnp.broadcast_to(newcnt, cntacc_ref.shape)
    cnt_ref[...] = jnp.broadcast_to(newcnt, cnt_ref.shape).astype(I32)


def _layer_spec(shape, layer):
    nd = len(shape) - 1
    sel = layer if shape[0] > 1 else 0
    return pl.BlockSpec((None,) + tuple(shape[1:]), lambda b, l, nd=nd, sel=sel: (sel,) + (0,) * nd)


def _mixer_call(x, x_tile0, yprev, slabprev, lp, layer, batch, seq):
    first_layer = yprev is None
    n_tok = batch * seq
    steps = seq // TILE_T
    tok_spec = lambda w: pl.BlockSpec((TILE_T, w), lambda b, l: (b * steps + l, 0))
    in_specs = [pl.BlockSpec((TILE_T, D_MODEL), lambda b, l: (x_tile0 + b * steps + l, 0))]
    args = [x]
    if not first_layer:
        in_specs += [pl.BlockSpec((TOP_K, 2, TILE_T, SC_ROW), lambda b, l: (0, 0, b * steps + l, 0)), tok_spec(LANES)]
        args += [yprev, slabprev]
    weights = [lp['gmix'], lp['win'], lp['minw'], lp['mcat'], lp['coef'], lp['dskip'], lp['wglu'], lp['bglu'],
               lp['poolw'], lp['pscale'], lp['convw'], lp['gbr'], lp['wout'], lp['gffn'], lp['rw'], lp['rb'],
               lp['ltri']]
    in_specs += [_layer_spec(w.shape, layer) for w in weights]
    args += weights
    out_shape = [jax.ShapeDtypeStruct((n_tok, D_MODEL), F32),
                 jax.ShapeDtypeStruct((2, n_tok, SC_ROW), I32),
                 jax.ShapeDtypeStruct((n_tok, LANES), I32),
                 jax.ShapeDtypeStruct((SUBLANES, LANES), I32)]
    out_specs = [tok_spec(D_MODEL), pl.BlockSpec((2, TILE_T, SC_ROW), lambda b, l: (0, b * steps + l, 0)), tok_spec(LANES),
                 pl.BlockSpec((SUBLANES, LANES), lambda b, l: (0, 0))]
    scratch = [pltpu.VMEM((SSM_WIDTH // LANES, TILE_T, LANES), F32),
               pltpu.VMEM((SSM_WIDTH // LANES, TILE_T, LANES), F32),
               pltpu.VMEM((SUBLANES + N_CHUNK, SSM_GROUPS * LANES), F32),
               pltpu.VMEM((3, POOL_TAIL + TILE_T, POOL_WIDTH), F32),
               pltpu.VMEM((8 + TILE_T, CONV_WIDTH), F32),
               pltpu.VMEM((SUBLANES, LANES), F32)]
    return pl.pallas_call(
        functools.partial(_mixer_kernel, first_layer=first_layer),
        grid=(batch, steps),
        in_specs=in_specs, out_specs=out_specs, out_shape=out_shape, scratch_shapes=scratch,
        compiler_params=pltpu.CompilerParams(dimension_semantics=("arbitrary", "arbitrary"),
                                             vmem_limit_bytes=VMEM_LIMIT),
        name="mixer_first" if first_layer else "mixer_next",
    )(*args)


def _route_kernel(slab_ref, cnt_ref, dest_ref, meta_ref, *, n_blocks):
    lane1 = lax.broadcasted_iota(I32, (1, LANES), 1)
    counts = jnp.where(lane1 < N_EXPERTS, cnt_ref[0:1, :], 0)
    padded = ((counts + (EXPERT_BLOCK - 1)) // EXPERT_BLOCK) * EXPERT_BLOCK
    pend = padded
    for sh in (1, 2, 4, 8, 16):
        pend = pend + jnp.where(lane1 >= sh, pltpu.roll(pend, sh, axis=1), 0)
    pstart = pend - padded

    @pl.when(pl.program_id(0) == 0)
    def _():
        rows = meta_ref.shape[0]
        bstart = lax.broadcasted_iota(I32, (rows, LANES), 0) * EXPERT_BLOCK
        lane = lax.broadcasted_iota(I32, (rows, LANES), 1)
        done = jnp.where(jnp.logical_and(lane < N_EXPERTS, pend <= bstart), 1, 0)
        be = jnp.minimum(jnp.sum(done, axis=1, keepdims=True), N_EXPERTS - 1)
        total = jnp.sum(jnp.where(lane1 == N_EXPERTS - 1, pend, 0), axis=1, keepdims=True)
        row = lax.broadcasted_iota(I32, (rows, LANES), 0)
        meta_ref[...] = jnp.where(row == n_blocks, total // EXPERT_BLOCK, jnp.broadcast_to(be, (rows, LANES)))

    slab = slab_ref[...]
    lane = lax.broadcasted_iota(I32, slab.shape, 1)
    out = jnp.zeros(slab.shape, F32)
    for k in range(TOP_K):
        idx = slab[:, k:k + 1]
        rank = slab[:, 2 * TOP_K + k:2 * TOP_K + k + 1]
        base = jnp.sum(jnp.where(lane == idx, pstart, 0), axis=1, keepdims=True)
        for h in range(2):
            out = jnp.where(lane == 2 * k + h, (base + rank + h * n_blocks * EXPERT_BLOCK).astype(F32), out)
    dest_ref[...] = out.T[:SUBLANES, :].astype(I32)


def _route_call(slab, counts, n_tok, n_blocks):
    meta_rows = ((n_blocks + 1 + SUBLANES - 1) // SUBLANES) * SUBLANES
    return pl.pallas_call(
        functools.partial(_route_kernel, n_blocks=n_blocks),
        grid=(n_tok // ROUTE_T,),
        in_specs=[pl.BlockSpec((ROUTE_T, LANES), lambda i: (i, 0)),
                  pl.BlockSpec((SUBLANES, LANES), lambda i: (0, 0))],
        out_specs=[pl.BlockSpec((SUBLANES, ROUTE_T), lambda i: (0, i)),
                   pl.BlockSpec((meta_rows, LANES), lambda i: (0, 0))],
        out_shape=[jax.ShapeDtypeStruct((SUBLANES, n_tok), I32),
                   jax.ShapeDtypeStruct((meta_rows, LANES), I32)],
        compiler_params=pltpu.CompilerParams(dimension_semantics=("arbitrary",)),
        name="route",
    )(slab, counts)


def _sc_mesh():
    return plsc.VectorSubcoreMesh(core_axis_name="core", subcore_axis_name="subcore")


def _dispatch(src, idx, n_rows):
    windows = src.shape[0] // SC_WINDOW

    @functools.partial(pl.kernel, out_type=jax.ShapeDtypeStruct((n_rows, SC_ROW), src.dtype),
                       mesh=_sc_mesh(), scratch_types=[])
    def scatter_rows(x_hbm, i_hbm, o_hbm):
        def body(x_vmem, i_vmem):
            pltpu.sync_copy(x_vmem, o_hbm.at[i_vmem.at[0]])

        pltpu.emit_pipeline(
            body,
            grid=(idx.shape[1] // SC_WINDOW,),
            in_specs=[pl.BlockSpec((SC_WINDOW, SC_ROW), lambda i: (lax.rem(i, windows), 0)),
                      pl.BlockSpec((1, SC_WINDOW), lambda i: (0, i))],
            out_specs=[],
            core_axis_name=("core", "subcore"),
            dimension_semantics=(pltpu.PARALLEL,),
        )(x_hbm, i_hbm)

    return scatter_rows(src, idx)


def _combine(src, idx):
    n_idx = idx.shape[1]

    @functools.partial(pl.kernel, out_type=jax.ShapeDtypeStruct((n_idx, SC_ROW), src.dtype),
                       mesh=_sc_mesh(), scratch_types=[])
    def gather_rows(y_hbm, i_hbm, o_hbm):
        def body(i_vmem, o_vmem):
            pltpu.sync_copy(y_hbm.at[i_vmem.at[0]], o_vmem)

        pltpu.emit_pipeline(
            body,
            grid=(n_idx // SC_WINDOW,),
            in_specs=[pl.BlockSpec((1, SC_WINDOW), lambda i: (0, i))],
            out_specs=[pl.BlockSpec((SC_WINDOW, SC_ROW), lambda i: (i, 0))],
            core_axis_name=("core", "subcore"),
            dimension_semantics=(pltpu.PARALLEL,),
        )(i_hbm, o_hbm)

    return gather_rows(src, idx)


def _expert_block(xw, wgu_s, bgu_ref, wdn_s, bdn_ref, act_s):
    w0 = xw[0]
    w1 = xw[1]
    xb = jnp.concatenate([_unpack_lo(w0).astype(BF16), _unpack_lo(w1).astype(BF16),
                          _unpack_hi(w0).astype(BF16), _unpack_hi(w1).astype(BF16)], axis=1)
    half = D_EXPERT // 2
    for j in range(2):
        cols = slice(j * half, (j + 1) * half)
        ucols = slice(D_EXPERT + j * half, D_EXPERT + (j + 1) * half)
        g = jnp.dot(xb, wgu_s[:, cols], preferred_element_type=F32) + bgu_ref[:, cols]
        up = jnp.dot(xb, wgu_s[:, ucols], preferred_element_type=F32) + bgu_ref[:, ucols]
        g = jnp.minimum(g, SWIGLU_LIMIT)
        up = jnp.clip(up, -SWIGLU_LIMIT, SWIGLU_LIMIT)
        act_s[:, cols] = (g * jax.nn.sigmoid(SWIGLU_ALPHA * g) * (up + 1.0)).astype(BF16)
    y = jnp.dot(act_s[...], wdn_s[...], preferred_element_type=F32) + bdn_ref[...]
    return _pack_rows(y)


def _expert_kernel(be_ref, nv_ref, xs_ref, wgu_ref, bgu_ref, wdn_ref, bdn_ref, ys_ref, wgu_s, wdn_s, act_s):
    blk = pl.program_id(0)
    prev = be_ref[jnp.maximum(blk - 1, 0)]
    changed = jnp.logical_or(blk == 0, be_ref[blk] != prev)
    valid = blk < nv_ref[0]

    @pl.when(jnp.logical_and(valid, changed))
    def _():
        chunk = 128

        def cast_gu(i, _):
            rows = pl.ds(pl.multiple_of(i * chunk, chunk), chunk)
            wgu_s[rows, :] = wgu_ref[rows, :].astype(BF16)
            return 0

        def cast_dn(i, _):
            rows = pl.ds(pl.multiple_of(i * chunk, chunk), chunk)
            wdn_s[rows, :] = wdn_ref[rows, :].astype(BF16)
            return 0

        lax.fori_loop(0, D_MODEL // chunk, cast_gu, 0)
        lax.fori_loop(0, D_EXPERT // chunk, cast_dn, 0)

    @pl.when(valid)
    def _():
        packed = _expert_block(xs_ref, wgu_s, bgu_ref, wdn_s, bdn_ref, act_s)
        ys_ref[0] = packed[:, :SC_ROW]
        ys_ref[1] = packed[:, SC_ROW:]


def _expert_call(xs, block_e, n_valid, w_gu, b_gu, w_dn, b_dn, layer, n_blocks):
    def row_map(i, be, nv):
        return (0, jnp.minimum(i, jnp.maximum(nv[0] - 1, 0)), 0)

    def w_map(i, be, nv):
        return (layer, be[i], 0, 0)

    grid_spec = pltpu.PrefetchScalarGridSpec(
        num_scalar_prefetch=2,
        grid=(n_blocks,),
        in_specs=[pl.BlockSpec((2, EXPERT_BLOCK, SC_ROW), row_map),
                  pl.BlockSpec((None, None, D_MODEL, 2 * D_EXPERT), w_map),
                  pl.BlockSpec((None, None, 1, 2 * D_EXPERT), w_map),
                  pl.BlockSpec((None, None, D_EXPERT, D_MODEL), w_map),
                  pl.BlockSpec((None, None, 1, D_MODEL), w_map)],
        out_specs=pl.BlockSpec((2, EXPERT_BLOCK, SC_ROW), row_map),
        scratch_shapes=[pltpu.VMEM((D_MODEL, 2 * D_EXPERT), BF16),
                        pltpu.VMEM((D_EXPERT, D_MODEL), BF16),
                        pltpu.VMEM((EXPERT_BLOCK, D_EXPERT), BF16)])
    return pl.pallas_call(
        _expert_kernel,
        grid_spec=grid_spec,
        out_shape=jax.ShapeDtypeStruct((2, n_blocks * EXPERT_BLOCK, SC_ROW), I32),
        compiler_params=pltpu.CompilerParams(dimension_semantics=("arbitrary",),
                                             vmem_limit_bytes=VMEM_LIMIT),
        name="experts",
    )(block_e, n_valid, xs, w_gu, b_gu, w_dn, b_dn)


def _final_kernel(x_ref, y_ref, slab_ref, g_ref, *rest):
    o_ref = rest[-1]
    x = _moe_combine(x_ref[...], y_ref, slab_ref[...])
    o_ref[...] = _rms(x, g_ref[...])


def _final_call(x1, ysg, slab, g, out_tile0, n_out, out_prev):
    t = TILE_T
    in_specs = [pl.BlockSpec((t, D_MODEL), lambda i: (i, 0)),
                pl.BlockSpec((TOP_K, 2, t, SC_ROW), lambda i: (0, 0, i, 0)),
                pl.BlockSpec((t, LANES), lambda i: (i, 0)),
                pl.BlockSpec((1, D_MODEL), lambda i: (0, 0))]
    args = [x1, ysg, slab, g]
    aliases = {}
    if out_prev is not None:
        in_specs.append(pl.BlockSpec(memory_space=pl.ANY))
        args.append(out_prev)
        aliases = {4: 0}
    return pl.pallas_call(
        _final_kernel,
        grid=(x1.shape[0] // t,),
        in_specs=in_specs,
        out_specs=pl.BlockSpec((t, D_MODEL), lambda i: (out_tile0 + i, 0)),
        out_shape=jax.ShapeDtypeStruct((n_out, D_MODEL), F32),
        input_output_aliases=aliases,
        compiler_params=pltpu.CompilerParams(dimension_semantics=("arbitrary",)),
        name="final_norm",
    )(*args)


def _s5_tables(lam_re, lam_im, b_re, b_im, c_re, c_im, log_step):
    lr = jnp.minimum(lam_re.astype(F32), LAMBDA_RE_MAX)
    li = lam_im.astype(F32)
    step = jnp.exp(log_step.astype(F32))[:, None]
    mag = jnp.exp(lr * step)
    ar = mag * jnp.cos(li * step)
    ai = mag * jnp.sin(li * step)
    nr = ar - 1.0
    den = lr * lr + li * li
    kr = (nr * lr + ai * li) / den
    ki = (ai * lr - nr * li) / den
    bre = b_re.astype(F32)
    bim = b_im.astype(F32)
    bbar_r = kr[..., None] * bre - ki[..., None] * bim
    bbar_i = kr[..., None] * bim + ki[..., None] * bre
    cre = c_re.astype(F32)
    cim = c_im.astype(F32)

    def power(m):
        m = jnp.asarray(m, F32)[..., None, None]
        mg = jnp.exp(m * lr * step)
        return mg * jnp.cos(m * li * step), mg * jnp.sin(m * li * step)

    pw_r, pw_i = power(jnp.arange(CHUNK + 1))
    cp_r = cre[None] * pw_r[:, :, None, :] - cim[None] * pw_i[:, :, None, :]
    cp_i = cre[None] * pw_i[:, :, None, :] + cim[None] * pw_r[:, :, None, :]
    kmat = (jnp.einsum('tgap,gph->tgah', cp_r[:CHUNK], bbar_r) - jnp.einsum('tgap,gph->tgah', cp_i[:CHUNK], bbar_i))
    width = CHUNK * SSM_GROUP
    kcat = kmat.transpose(1, 3, 0, 2).reshape(SSM_GROUPS, SSM_GROUP, width)
    m_intra = jnp.stack([jnp.pad(kcat[:, :, :width - s * SSM_GROUP], ((0, 0), (0, 0), (s * SSM_GROUP, 0)))
                         for s in range(CHUNK)], axis=1).reshape(SSM_GROUPS, width, width)
    mo_r = cp_r[1:].transpose(1, 3, 0, 2).reshape(SSM_GROUPS, SSM_STATE, CHUNK * SSM_GROUP)
    mo_i = -cp_i[1:].transpose(1, 3, 0, 2).reshape(SSM_GROUPS, SSM_STATE, CHUNK * SSM_GROUP)
    mcat = jnp.concatenate([m_intra, mo_r, mo_i], axis=1).astype(BF16)
    q_r = pw_r[CHUNK - 1::-1][:CHUNK][:, :, :, None]
    q_i = pw_i[CHUNK - 1::-1][:CHUNK][:, :, :, None]
    mn_r = q_r * bbar_r[None] - q_i * bbar_i[None]
    mn_i = q_r * bbar_i[None] + q_i * bbar_r[None]
    minw = jnp.concatenate([mn_r, mn_i], axis=2).transpose(1, 0, 3, 2).reshape(
        SSM_GROUPS, CHUNK * SSM_GROUP, 2 * SSM_STATE).astype(BF16)

    def packed(j):
        pr, pi = power(float(CHUNK * j))
        return (jnp.concatenate([pr, pr], axis=-1).reshape(1, SSM_GROUPS * LANES),
                jnp.concatenate([-pi, pi], axis=-1).reshape(1, SSM_GROUPS * LANES))

    rows = jnp.arange(SUBLANES)[:, None]
    kinds = []
    for d in (1, 2, 4):
        c1, c2 = packed(d)
        mask = (rows >= d).astype(F32)
        kinds += [mask * c1, mask * c2]
    pk = [packed(r + 1) for r in range(SUBLANES)]
    kinds += [jnp.concatenate([p[0] for p in pk], axis=0), jnp.concatenate([p[1] for p in pk], axis=0)]
    coef = jnp.stack(kinds, axis=0)
    return minw, mcat, coef


def _mixer_params(norm_mix, w_in, ssm_lam_re, ssm_lam_im, ssm_b_re, ssm_b_im, ssm_c_re, ssm_c_im, ssm_d,
                  ssm_log_step, ssm_w_glu, ssm_b_glu, pool_w, pool_scale, conv_w, branch_norm, w_out, norm_ffn,
                  router_w, router_b):
    depth = w_in.shape[0]
    minw, mcat, coef = jax.vmap(_s5_tables)(ssm_lam_re, ssm_lam_im, ssm_b_re, ssm_b_im, ssm_c_re, ssm_c_im,
                                            ssm_log_step)
    eye4 = jnp.eye(len(POOL_WINDOWS), dtype=F32)
    poolw = jnp.einsum('lgcd,gk->lgckd', pool_w.astype(F32), eye4).reshape(depth, POOL_WIDTH, POOL_WIDTH).astype(BF16)
    rw = jnp.pad(router_w.astype(F32), ((0, 0), (0, 0), (0, LANES - N_EXPERTS))).astype(BF16)
    rb = jnp.pad(router_b.astype(F32), ((0, 0), (0, LANES - N_EXPERTS))).reshape(depth, 1, LANES)
    row = lambda v: v.reshape(depth, 1, -1).astype(F32)
    ltri = jnp.tril(jnp.ones((TILE_T, TILE_T), F32), -1).astype(BF16)
    return dict(gmix=row(norm_mix), win=w_in.astype(BF16), minw=minw, mcat=mcat, coef=coef, dskip=row(ssm_d),
                wglu=ssm_w_glu.astype(BF16), bglu=row(ssm_b_glu), poolw=poolw, pscale=row(pool_scale),
                convw=conv_w.astype(F32), gbr=row(branch_norm), wout=w_out.astype(BF16),
                gffn=row(norm_ffn), rw=rw, rb=rb, ltri=ltri[None])


def kernel(x, norm_mix, w_in, ssm_lam_re, ssm_lam_im, ssm_b_re, ssm_b_im, ssm_c_re, ssm_c_im, ssm_d, ssm_log_step, ssm_w_glu, ssm_b_glu, pool_w, pool_scale, conv_w, branch_norm, w_out, norm_ffn, router_w, router_b, w_gate_up, b_gate_up, w_down, b_down, final_norm):
    batch, seq, d_model = x.shape
    assert d_model == D_MODEL and seq % TILE_T == 0
    depth = w_in.shape[0]
    n_streams = 2 if batch % 2 == 0 else 1
    sb = batch // n_streams
    n_tok = sb * seq
    assert n_tok % ROUTE_T == 0 and n_tok % SC_WINDOW == 0
    n_blocks = -(-(n_tok * TOP_K + N_EXPERTS * (EXPERT_BLOCK - 1)) // EXPERT_BLOCK)
    n_rows = n_blocks * EXPERT_BLOCK
    tiles = n_tok // TILE_T
    b_gu = b_gate_up.reshape(depth, N_EXPERTS, 1, 2 * D_EXPERT)
    b_dn = b_down.reshape(depth, N_EXPERTS, 1, D_MODEL)

    xin = [x.reshape(batch * seq, D_MODEL)] * n_streams
    tile0 = [s * tiles for s in range(n_streams)]
    ysg = [None] * n_streams
    slab = [None] * n_streams
    lp = _mixer_params(norm_mix, w_in, ssm_lam_re, ssm_lam_im, ssm_b_re, ssm_b_im, ssm_c_re, ssm_c_im, ssm_d,
                       ssm_log_step, ssm_w_glu, ssm_b_glu, pool_w, pool_scale, conv_w, branch_norm, w_out,
                       norm_ffn, router_w, router_b)
    for l in range(depth):
        routed = []
        for s in range(n_streams):
            xin[s], hp, slab[s], counts = _mixer_call(xin[s], tile0[s], ysg[s], slab[s], lp, l, sb, seq)
            tile0[s] = 0
            dest, meta = _route_call(slab[s], counts, n_tok, n_blocks)
            idx = dest.reshape(1, 2 * TOP_K * n_tok)
            xs = _dispatch(hp.reshape(2 * n_tok, SC_ROW), idx, 2 * n_rows).reshape(2, n_rows, SC_ROW)
            routed.append((xs, idx, meta[:n_blocks, 0], meta[n_blocks:n_blocks + 1, 0]))
        for s, (xs, idx, block_e, n_valid) in enumerate(routed):
            ys = _expert_call(xs, block_e, n_valid, w_gate_up, b_gu, w_down, b_dn, l, n_blocks)
            ysg[s] = _combine(ys.reshape(2 * n_rows, SC_ROW), idx).reshape(TOP_K, 2, n_tok, SC_ROW)
    g = final_norm.reshape(1, D_MODEL).astype(F32)
    out = None
    for s in range(n_streams):
        out = _final_call(xin[s], ysg[s], slab[s], g, s * tiles, batch * seq, out)
    return out.reshape(batch, seq, D_MODEL)
```

```python
import functools
import math

import jax
import jax.numpy as jnp
from jax import lax
from jax.experimental import pallas as pl
from jax.experimental.pallas import tpu as pltpu
from jax.experimental.pallas import tpu_sc as plsc

F32 = jnp.float32
BF16 = jnp.bfloat16
I32 = jnp.int32

D_MODEL = 1024
SSM_WIDTH = 512
POOL_WIDTH = 256
CONV_WIDTH = 256
SSM_GROUP = 16
SSM_GROUPS = 32
SSM_STATE = 64
LAMBDA_RE_MAX = -1e-4
POOL_WINDOWS = (2, 4, 8, 16)
POOL_GROUP = 64
CONV_K = 3
IN_WIDTH = SSM_WIDTH + POOL_WIDTH + 3 * CONV_WIDTH
N_EXPERTS = 32
TOP_K = 4
D_EXPERT = 1024
SWIGLU_LIMIT = 7.0
SWIGLU_ALPHA = 1.702
EXPERT_BLOCK = 256
NORM_EPS = 1e-5

LANES = 128
SUBLANES = 8
TILE_T = 512
CHUNK = 16
N_CHUNK = TILE_T // CHUNK
POOL_TAIL = SUBLANES * len(POOL_WINDOWS)
assert POOL_WINDOWS == tuple(2 ** (k + 1) for k in range(len(POOL_WINDOWS))) and POOL_TAIL >= POOL_WINDOWS[-1]
PACKED = D_MODEL // 2
ROUTE_T = 2048
SC_WINDOW = 128
SC_ROW = PACKED // 2
HI_MASK = -65536
VMEM_LIMIT = 56 * 1024 * 1024


def _bf16_round(v):
    return v.astype(BF16).astype(F32)


def _pack_rows(v):
    lo = lax.shift_right_logical(lax.bitcast_convert_type(_bf16_round(v[:, :PACKED]), I32), 16)
    hi = lax.bitcast_convert_type(_bf16_round(v[:, PACKED:]), I32) & HI_MASK
    return hi | lo


def _unpack_lo(w):
    return lax.bitcast_convert_type(lax.shift_left(w, 16), F32)


def _unpack_hi(w):
    return lax.bitcast_convert_type(w & HI_MASK, F32)


def _rms(v, g):
    r = lax.rsqrt(jnp.mean(v * v, axis=-1, keepdims=True) + NORM_EPS)
    return (v * r) * g


def _moe_combine(x, y_ref, slab):
    parts = [x[:, i * SC_ROW:(i + 1) * SC_ROW] for i in range(4)]
    for k in range(TOP_K):
        g = slab[:, TOP_K + k:TOP_K + k + 1]
        for h in range(2):
            w = y_ref[k, h]
            parts[h] = parts[h] + g * _unpack_lo(w)
            parts[2 + h] = parts[2 + h] + g * _unpack_hi(w)
    return jnp.concatenate(parts, axis=1)


def _block_transpose8(vs, lane):
    vs = list(vs)
    for dist in (4, 2, 1):
        width = SSM_GROUP * dist
        low = (lane % (2 * width)) < width
        for j in range(8):
            if j & dist:
                continue
            a, b = vs[j], vs[j + dist]
            vs[j] = jnp.where(low, a, pltpu.roll(b, width, axis=1))
            vs[j + dist] = jnp.where(low, pltpu.roll(a, LANES - width, axis=1), b)
    return vs


def _cmul_packed(c1, c2, v):
    return c1 * v + c2 * pltpu.roll(v, SSM_STATE, axis=1)


def _chunk_scan(s, carry_in, cf):
    outs = []
    carry = jnp.broadcast_to(carry_in, (SUBLANES, LANES))
    for rg in range(N_CHUNK // SUBLANES):
        x = s[rg * SUBLANES:(rg + 1) * SUBLANES]
        for j, d in enumerate((1, 2, 4)):
            x = x + _cmul_packed(cf[2 * j], cf[2 * j + 1], pltpu.roll(x, d, axis=0))
        x = x + _cmul_packed(cf[6], cf[7], carry)
        outs.append(x)
        carry = jnp.broadcast_to(x[SUBLANES - 1:SUBLANES], (SUBLANES, LANES))
    return jnp.concatenate(outs, axis=0)


def _s5_mixer(u, first, minw_ref, mcat_ref, coef_ref, u_s, y_s, h_s):
    blocks = SSM_WIDTH // LANES
    for b4 in range(blocks):
        u_s[b4] = u[:, b4 * LANES:(b4 + 1) * LANES]
    lane = lax.broadcasted_iota(I32, (N_CHUNK, LANES), 1)

    halves = [[None, None] for _ in range(SSM_GROUPS)]
    for b4 in range(blocks):
        for hh in range(2):
            xs = [u_s[b4, pl.ds(8 * hh + j, N_CHUNK, stride=CHUNK), :].astype(BF16) for j in range(8)]
            ws = _block_transpose8(xs, lane)
            for gl in range(8):
                halves[8 * b4 + gl][hh] = ws[gl]
    ug = [jnp.concatenate(h2, axis=1) for h2 in halves]

    carry_in = jnp.where(first, 0.0, h_s[N_CHUNK + 7:N_CHUNK + 8, :])
    h_s[7:8, :] = carry_in
    for g in range(SSM_GROUPS):
        cols = slice(g * LANES, (g + 1) * LANES)
        s_g = jnp.dot(ug[g], minw_ref[g], preferred_element_type=F32)
        cf = [coef_ref[k, :, cols] for k in range(8)]
        h_s[8:8 + N_CHUNK, cols] = _chunk_scan(s_g, carry_in[:, cols], cf)

    yg = []
    for g in range(SSM_GROUPS):
        hprev = h_s[7:7 + N_CHUNK, g * LANES:(g + 1) * LANES].astype(BF16)
        lhs = jnp.concatenate([ug[g], hprev], axis=1)
        yg.append(jnp.dot(lhs, mcat_ref[g], preferred_element_type=F32).astype(BF16))

    for b4 in range(blocks):
        for hh in range(2):
            ws = [yg[8 * b4 + gl][:, hh * LANES:(hh + 1) * LANES] for gl in range(8)]
            xs = _block_transpose8(ws, lane)
            for j in range(8):
                y_s[b4, pl.ds(8 * hh + j, N_CHUNK, stride=CHUNK), :] = xs[j].astype(F32)
    return jnp.concatenate([y_s[b4] for b4 in range(blocks)], axis=1)


def _mixer_kernel(*refs, first_layer):
    if first_layer:
        x_ref = refs[0]
        refs = refs[1:]
    else:
        x_ref, yprev_ref, slabprev_ref = refs[:3]
        refs = refs[3:]
    (gmix_ref, win_ref, minw_ref, mcat_ref, coef_ref, dskip_ref, wglu_ref, bglu_ref, poolw_ref, pscale_ref,
     convw_ref, gbr_ref, wout_ref, gffn_ref, rw_ref, rb_ref, utri_ref,
     x1_ref, hp_ref, slab_ref, slabt_ref, cnt_ref,
     u_s, y_s, h_s, pbuf_ref, zbuf_ref, cntacc_ref) = refs

    b = pl.program_id(0)
    l = pl.program_id(1)
    seq_start = l == 0

    @pl.when(jnp.logical_and(b == 0, l == 0))
    def _():
        cntacc_ref[...] = jnp.zeros_like(cntacc_ref)

    if first_layer:
        x = x_ref[...]
    else:
        x = _moe_combine(x_ref[...], yprev_ref, slabprev_ref[...])

    hn = _rms(x, gmix_ref[...]).astype(BF16)
    proj = jnp.dot(hn, win_ref[...], preferred_element_type=F32)
    u = proj[:, :SSM_WIDTH]
    p = proj[:, SSM_WIDTH:SSM_WIDTH + POOL_WIDTH]
    c0 = SSM_WIDTH + POOL_WIDTH
    bg = proj[:, c0:c0 + CONV_WIDTH]
    cg = proj[:, c0 + CONV_WIDTH:c0 + 2 * CONV_WIDTH]
    hv = proj[:, c0 + 2 * CONV_WIDTH:]

    y = _s5_mixer(u, seq_start, minw_ref, mcat_ref, coef_ref, u_s, y_s, h_s) + dskip_ref[...] * u
    z = jax.nn.gelu(y)
    glu = jnp.dot(z.astype(BF16), wglu_ref[...], preferred_element_type=F32) + bglu_ref[...]
    y_ssm = z * jax.nn.sigmoid(glu)

    tail = POOL_TAIL
    pbuf_ref[0, 0:tail, :] = jnp.where(seq_start, 0.0, pbuf_ref[0, TILE_T:TILE_T + tail, :])
    pbuf_ref[0, tail:, :] = p
    sums = []
    for stage, w in enumerate(POOL_WINDOWS):
        src = 0 if stage == 0 else 1 + (stage - 1) % 2
        dst = 1 + stage % 2
        d = w // 2
        lo = SUBLANES * (stage + 1)
        n = tail + TILE_T - lo
        e = pbuf_ref[src, pl.ds(lo, n), :] + pbuf_ref[src, pl.ds(lo - d, n), :]
        if stage + 1 < len(POOL_WINDOWS):
            pbuf_ref[dst, pl.ds(lo, n), :] = e
        sums.append(e[tail - lo:, :])
    lane_p = lax.broadcasted_iota(I32, (TILE_T, POOL_WIDTH), 1)
    total = sums[-1]
    win = jnp.full((TILE_T, POOL_WIDTH), POOL_WINDOWS[-1], I32)
    for i in range(len(POOL_WINDOWS) - 2, -1, -1):
        sel = lane_p < (i + 1) * POOL_GROUP
        total = jnp.where(sel, sums[i], total)
        win = jnp.where(sel, POOL_WINDOWS[i], win)
    row_p = lax.broadcasted_iota(I32, (TILE_T, POOL_WIDTH), 0) + l * TILE_T
    count = jnp.minimum(row_p + 1, win).astype(F32)
    pooled = total / count - p
    mixed = jnp.dot(pooled.astype(BF16), poolw_ref[...], preferred_element_type=F32)
    y_pool = mixed * pscale_ref[...]

    zc = cg * hv
    zbuf_ref[0:8, :] = jnp.where(seq_start, 0.0, zbuf_ref[TILE_T:TILE_T + 8, :])
    zbuf_ref[8:, :] = zc
    yc = (convw_ref[0:1, :] * zbuf_ref[pl.ds(6, TILE_T), :]
          + convw_ref[1:2, :] * zbuf_ref[pl.ds(7, TILE_T), :]
          + convw_ref[2:3, :] * zc)
    y_conv = bg * yc

    gbr = gbr_ref[...]
    s1 = SSM_WIDTH + POOL_WIDTH
    mixed_all = jnp.concatenate([_rms(y_ssm, gbr[:, :SSM_WIDTH]).astype(BF16),
                                 _rms(y_pool, gbr[:, SSM_WIDTH:s1]).astype(BF16),
                                 _rms(y_conv, gbr[:, s1:]).astype(BF16)], axis=1)
    x1 = x + jnp.dot(mixed_all, wout_ref[...], preferred_element_type=F32)
    x1_ref[...] = x1

    hn2 = _rms(x1, gffn_ref[...])
    packed = _pack_rows(hn2)
    hp_ref[0] = packed[:, :SC_ROW]
    hp_ref[1] = packed[:, SC_ROW:]
    logits = jnp.dot(hn2.astype(BF16), rw_ref[...], preferred_element_type=F32) + rb_ref[...]
    work = logits.T[:N_EXPERTS, :]
    eio = lax.broadcasted_iota(I32, (N_EXPERTS, TILE_T), 0).astype(F32)
    neg = jnp.float32(-jnp.inf)
    vals, idxs = [], []
    for _ in range(TOP_K):
        m = jnp.max(work, axis=0, keepdims=True)
        idx = jnp.min(jnp.where(work == m, eio, float(N_EXPERTS)), axis=0, keepdims=True)
        vals.append(m)
        idxs.append(idx)
        work = jnp.where(eio == idx, neg, work)
    exps = [jnp.exp(v - vals[0]) for v in vals]
    denom = exps[0] + exps[1] + exps[2] + exps[3]
    gates = [e / denom for e in exps]

    onehot = jnp.zeros((N_EXPERTS, TILE_T), F32)
    for idx in idxs:
        onehot = jnp.where(eio == idx, 1.0, onehot)
    before = jnp.dot(onehot.astype(BF16), utri_ref[...], preferred_element_type=F32) + cntacc_ref[:, 0:1]
    ranks = [jnp.sum(jnp.where(eio == idx, before, 0.0), axis=0, keepdims=True) for idx in idxs]
    row = lax.broadcasted_iota(I32, (LANES, TILE_T), 0)
    slab_t = jnp.zeros((LANES, TILE_T), F32)
    for k in range(TOP_K):
        slab_t = jnp.where(row == k, idxs[k], slab_t)
        slab_t = jnp.where(row == TOP_K + k, gates[k], slab_t)
        slab_t = jnp.where(row == 2 * TOP_K + k, ranks[k], slab_t)
    slabt_ref[...] = slab_t[:2 * SUBLANES, :]
    slab_ref[...] = slab_t.T
    newcnt = cntacc_ref[:, 0:1] + jnp.sum(onehot, axis=1, keepdims=True)
    cntacc_ref[...] = jnp.broadcast_to(newcnt, cntacc_ref.shape)
    cnt_ref[...] = jnp.broadcast_to(newcnt, cnt_ref.shape)


def _layer_spec(shape, layer):
    nd = len(shape) - 1
    sel = layer if shape[0] > 1 else 0
    return pl.BlockSpec((None,) + tuple(shape[1:]), lambda b, l, nd=nd, sel=sel: (sel,) + (0,) * nd)


def _mixer_call(x, x_tile0, yprev, slabprev, lp, layer, batch, seq):
    first_layer = yprev is None
    n_tok = batch * seq
    steps = seq // TILE_T
    tok_spec = lambda w: pl.BlockSpec((TILE_T, w), lambda b, l: (b * steps + l, 0))
    in_specs = [pl.BlockSpec((TILE_T, D_MODEL), lambda b, l: (x_tile0 + b * steps + l, 0))]
    args = [x]
    if not first_layer:
        in_specs += [pl.BlockSpec((TOP_K, 2, TILE_T, SC_ROW), lambda b, l: (0, 0, b * steps + l, 0)), tok_spec(LANES)]
        args += [yprev, slabprev]
    weights = [lp['gmix'], lp['win'], lp['minw'], lp['mcat'], lp['coef'], lp['dskip'], lp['wglu'], lp['bglu'],
               lp['poolw'], lp['pscale'], lp['convw'], lp['gbr'], lp['wout'], lp['gffn'], lp['rw'], lp['rb'],
               lp['utri']]
    in_specs += [_layer_spec(w.shape, layer) for w in weights]
    args += weights
    out_shape = [jax.ShapeDtypeStruct((n_tok, D_MODEL), F32),
                 jax.ShapeDtypeStruct((2, n_tok, SC_ROW), I32),
                 jax.ShapeDtypeStruct((n_tok, LANES), F32),
                 jax.ShapeDtypeStruct((2 * SUBLANES, n_tok), F32),
                 jax.ShapeDtypeStruct((N_EXPERTS, LANES), F32)]
    out_specs = [tok_spec(D_MODEL), pl.BlockSpec((2, TILE_T, SC_ROW), lambda b, l: (0, b * steps + l, 0)), tok_spec(LANES),
                 pl.BlockSpec((2 * SUBLANES, TILE_T), lambda b, l: (0, b * steps + l)),
                 pl.BlockSpec((N_EXPERTS, LANES), lambda b, l: (0, 0))]
    scratch = [pltpu.VMEM((SSM_WIDTH // LANES, TILE_T, LANES), F32),
               pltpu.VMEM((SSM_WIDTH // LANES, TILE_T, LANES), F32),
               pltpu.VMEM((SUBLANES + N_CHUNK, SSM_GROUPS * LANES), F32),
               pltpu.VMEM((3, POOL_TAIL + TILE_T, POOL_WIDTH), F32),
               pltpu.VMEM((8 + TILE_T, CONV_WIDTH), F32),
               pltpu.VMEM((N_EXPERTS, LANES), F32)]
    return pl.pallas_call(
        functools.partial(_mixer_kernel, first_layer=first_layer),
        grid=(batch, steps),
        in_specs=in_specs, out_specs=out_specs, out_shape=out_shape, scratch_shapes=scratch,
        compiler_params=pltpu.CompilerParams(dimension_semantics=("arbitrary", "arbitrary"),
                                             vmem_limit_bytes=VMEM_LIMIT),
        name="mixer_first" if first_layer else "mixer_next",
    )(*args)


def _route_kernel(slabt_ref, cnt_ref, dest_ref, meta_ref, *, n_blocks):
    lane1 = lax.broadcasted_iota(I32, (1, LANES), 1)
    e_row = lax.broadcasted_iota(I32, (N_EXPERTS, LANES), 0)
    e_lane = lax.broadcasted_iota(I32, (N_EXPERTS, LANES), 1)
    cnt = cnt_ref[...].astype(I32)
    counts = jnp.sum(jnp.where(e_row == e_lane, cnt, 0), axis=0, keepdims=True)
    padded = ((counts + (EXPERT_BLOCK - 1)) // EXPERT_BLOCK) * EXPERT_BLOCK
    pend = padded
    for sh in (1, 2, 4, 8, 16):
        pend = pend + jnp.where(lane1 >= sh, pltpu.roll(pend, sh, axis=1), 0)
    pstart = pend - padded

    @pl.when(pl.program_id(0) == 0)
    def _():
        rows = meta_ref.shape[0]
        bstart = lax.broadcasted_iota(I32, (rows, LANES), 0) * EXPERT_BLOCK
        lane = lax.broadcasted_iota(I32, (rows, LANES), 1)
        done = jnp.where(jnp.logical_and(lane < N_EXPERTS, pend <= bstart), 1, 0)
        be = jnp.minimum(jnp.sum(done, axis=1, keepdims=True), N_EXPERTS - 1)
        total = jnp.sum(jnp.where(lane1 == N_EXPERTS - 1, pend, 0), axis=1, keepdims=True)
        row = lax.broadcasted_iota(I32, (rows, LANES), 0)
        meta_ref[...] = jnp.where(row == n_blocks, total // EXPERT_BLOCK, jnp.broadcast_to(be, (rows, LANES)))

    pstart_col = jnp.sum(jnp.where(e_row == e_lane, pstart, 0), axis=1, keepdims=True)
    tokens = slabt_ref.shape[1]
    eio = lax.broadcasted_iota(I32, (N_EXPERTS, tokens), 0)
    row = lax.broadcasted_iota(I32, (SUBLANES, tokens), 0)
    out = jnp.zeros((SUBLANES, tokens), I32)
    for k in range(TOP_K):
        idx = slabt_ref[k:k + 1, :].astype(I32)
        rank = slabt_ref[2 * TOP_K + k:2 * TOP_K + k + 1, :].astype(I32)
        base = jnp.sum(jnp.where(eio == idx, pstart_col, 0), axis=0, keepdims=True)
        for h in range(2):
            out = jnp.where(row == 2 * k + h, base + rank + h * n_blocks * EXPERT_BLOCK, out)
    dest_ref[...] = out


def _route_call(slabt, counts, n_tok, n_blocks):
    meta_rows = ((n_blocks + 1 + SUBLANES - 1) // SUBLANES) * SUBLANES
    return pl.pallas_call(
        functools.partial(_route_kernel, n_blocks=n_blocks),
        grid=(n_tok // ROUTE_T,),
        in_specs=[pl.BlockSpec((2 * SUBLANES, ROUTE_T), lambda i: (0, i)),
                  pl.BlockSpec((N_EXPERTS, LANES), lambda i: (0, 0))],
        out_specs=[pl.BlockSpec((SUBLANES, ROUTE_T), lambda i: (0, i)),
                   pl.BlockSpec((meta_rows, LANES), lambda i: (0, 0))],
        out_shape=[jax.ShapeDtypeStruct((SUBLANES, n_tok), I32),
                   jax.ShapeDtypeStruct((meta_rows, LANES), I32)],
        compiler_params=pltpu.CompilerParams(dimension_semantics=("arbitrary",)),
        name="route",
    )(slabt, counts)


def _sc_mesh():
    return plsc.VectorSubcoreMesh(core_axis_name="core", subcore_axis_name="subcore")


def _dispatch(src, idx, n_rows):
    windows = src.shape[0] // SC_WINDOW

    @functools.partial(pl.kernel, out_type=jax.ShapeDtypeStruct((n_rows, SC_ROW), src.dtype),
                       mesh=_sc_mesh(), scratch_types=[])
    def scatter_rows(x_hbm, i_hbm, o_hbm):
        def body(x_vmem, i_vmem):
            pltpu.sync_copy(x_vmem, o_hbm.at[i_vmem.at[0]])

        pltpu.emit_pipeline(
            body,
            grid=(idx.shape[1] // SC_WINDOW,),
            in_specs=[pl.BlockSpec((SC_WINDOW, SC_ROW), lambda i: (lax.rem(i, windows), 0)),
                      pl.BlockSpec((1, SC_WINDOW), lambda i: (0, i))],
            out_specs=[],
            core_axis_name=("core", "subcore"),
            dimension_semantics=(pltpu.PARALLEL,),
        )(x_hbm, i_hbm)

    return scatter_rows(src, idx)


def _combine(src, idx):
    n_idx = idx.shape[1]

    @functools.partial(pl.kernel, out_type=jax.ShapeDtypeStruct((n_idx, SC_ROW), src.dtype),
                       mesh=_sc_mesh(), scratch_types=[])
    def gather_rows(y_hbm, i_hbm, o_hbm):
        def body(i_vmem, o_vmem):
            pltpu.sync_copy(y_hbm.at[i_vmem.at[0]], o_vmem)

        pltpu.emit_pipeline(
            body,
            grid=(n_idx // SC_WINDOW,),
            in_specs=[pl.BlockSpec((1, SC_WINDOW), lambda i: (0, i))],
            out_specs=[pl.BlockSpec((SC_WINDOW, SC_ROW), lambda i: (i, 0))],
            core_axis_name=("core", "subcore"),
            dimension_semantics=(pltpu.PARALLEL,),
        )(i_hbm, o_hbm)

    return gather_rows(src, idx)


def _expert_block(xw, wgu_s, bgu_ref, wdn_s, bdn_ref, act_s):
    w0 = xw[0]
    w1 = xw[1]
    xb = jnp.concatenate([_unpack_lo(w0).astype(BF16), _unpack_lo(w1).astype(BF16),
                          _unpack_hi(w0).astype(BF16), _unpack_hi(w1).astype(BF16)], axis=1)
    half = D_EXPERT // 2
    for j in range(2):
        cols = slice(j * half, (j + 1) * half)
        ucols = slice(D_EXPERT + j * half, D_EXPERT + (j + 1) * half)
        g = jnp.dot(xb, wgu_s[:, cols], preferred_element_type=F32) + bgu_ref[:, cols]
        up = jnp.dot(xb, wgu_s[:, ucols], preferred_element_type=F32) + bgu_ref[:, ucols]
        g = jnp.minimum(g, SWIGLU_LIMIT)
        up = jnp.clip(up, -SWIGLU_LIMIT, SWIGLU_LIMIT)
        act_s[:, cols] = (g * jax.nn.sigmoid(SWIGLU_ALPHA * g) * (up + 1.0)).astype(BF16)
    y = jnp.dot(act_s[...], wdn_s[...], preferred_element_type=F32) + bdn_ref[...]
    return _pack_rows(y)


def _expert_kernel(be_ref, nv_ref, xs_ref, wgu_ref, bgu_ref, wdn_ref, bdn_ref, ys_ref, wgu_s, wdn_s, act_s):
    blk = pl.program_id(0)
    prev = be_ref[jnp.maximum(blk - 1, 0)]
    changed = jnp.logical_or(blk == 0, be_ref[blk] != prev)
    valid = blk < nv_ref[0]

    @pl.when(jnp.logical_and(valid, changed))
    def _():
        chunk = 128

        def cast_gu(i, _):
            rows = pl.ds(pl.multiple_of(i * chunk, chunk), chunk)
            wgu_s[rows, :] = wgu_ref[rows, :].astype(BF16)
            return 0

        def cast_dn(i, _):
            rows = pl.ds(pl.multiple_of(i * chunk, chunk), chunk)
            wdn_s[rows, :] = wdn_ref[rows, :].astype(BF16)
            return 0

        lax.fori_loop(0, D_MODEL // chunk, cast_gu, 0)
        lax.fori_loop(0, D_EXPERT // chunk, cast_dn, 0)

    @pl.when(valid)
    def _():
        packed = _expert_block(xs_ref, wgu_s, bgu_ref, wdn_s, bdn_ref, act_s)
        ys_ref[0] = packed[:, :SC_ROW]
        ys_ref[1] = packed[:, SC_ROW:]


def _expert_call(xs, block_e, n_valid, w_gu, b_gu, w_dn, b_dn, layer, n_blocks):
    def row_map(i, be, nv):
        return (0, jnp.minimum(i, jnp.maximum(nv[0] - 1, 0)), 0)

    def w_map(i, be, nv):
        return (layer, be[i], 0, 0)

    grid_spec = pltpu.PrefetchScalarGridSpec(
        num_scalar_prefetch=2,
        grid=(n_blocks,),
        in_specs=[pl.BlockSpec((2, EXPERT_BLOCK, SC_ROW), row_map),
                  pl.BlockSpec((None, None, D_MODEL, 2 * D_EXPERT), w_map),
                  pl.BlockSpec((None, None, 1, 2 * D_EXPERT), w_map),
                  pl.BlockSpec((None, None, D_EXPERT, D_MODEL), w_map),
                  pl.BlockSpec((None, None, 1, D_MODEL), w_map)],
        out_specs=pl.BlockSpec((2, EXPERT_BLOCK, SC_ROW), row_map),
        scratch_shapes=[pltpu.VMEM((D_MODEL, 2 * D_EXPERT), BF16),
                        pltpu.VMEM((D_EXPERT, D_MODEL), BF16),
                        pltpu.VMEM((EXPERT_BLOCK, D_EXPERT), BF16)])
    return pl.pallas_call(
        _expert_kernel,
        grid_spec=grid_spec,
        out_shape=jax.ShapeDtypeStruct((2, n_blocks * EXPERT_BLOCK, SC_ROW), I32),
        compiler_params=pltpu.CompilerParams(dimension_semantics=("arbitrary",),
                                             vmem_limit_bytes=VMEM_LIMIT),
        name="experts",
    )(block_e, n_valid, xs, w_gu, b_gu, w_dn, b_dn)


def _final_kernel(x_ref, y_ref, slab_ref, g_ref, *rest):
    o_ref = rest[-1]
    x = _moe_combine(x_ref[...], y_ref, slab_ref[...])
    o_ref[...] = _rms(x, g_ref[...])


def _final_call(x1, ysg, slab, g, out_tile0, n_out, out_prev):
    t = TILE_T
    in_specs = [pl.BlockSpec((t, D_MODEL), lambda i: (i, 0)),
                pl.BlockSpec((TOP_K, 2, t, SC_ROW), lambda i: (0, 0, i, 0)),
                pl.BlockSpec((t, LANES), lambda i: (i, 0)),
                pl.BlockSpec((1, D_MODEL), lambda i: (0, 0))]
    args = [x1, ysg, slab, g]
    aliases = {}
    if out_prev is not None:
        in_specs.append(pl.BlockSpec(memory_space=pl.ANY))
        args.append(out_prev)
        aliases = {4: 0}
    return pl.pallas_call(
        _final_kernel,
        grid=(x1.shape[0] // t,),
        in_specs=in_specs,
        out_specs=pl.BlockSpec((t, D_MODEL), lambda i: (out_tile0 + i, 0)),
        out_shape=jax.ShapeDtypeStruct((n_out, D_MODEL), F32),
        input_output_aliases=aliases,
        compiler_params=pltpu.CompilerParams(dimension_semantics=("arbitrary",)),
        name="final_norm",
    )(*args)


def _s5_tables(lam_re, lam_im, b_re, b_im, c_re, c_im, log_step):
    lr = jnp.minimum(lam_re.astype(F32), LAMBDA_RE_MAX)
    li = lam_im.astype(F32)
    step = jnp.exp(log_step.astype(F32))[:, None]
    mag = jnp.exp(lr * step)
    ar = mag * jnp.cos(li * step)
    ai = mag * jnp.sin(li * step)
    nr = ar - 1.0
    den = lr * lr + li * li
    kr = (nr * lr + ai * li) / den
    ki = (ai * lr - nr * li) / den
    bre = b_re.astype(F32)
    bim = b_im.astype(F32)
    bbar_r = kr[..., None] * bre - ki[..., None] * bim
    bbar_i = kr[..., None] * bim + ki[..., None] * bre
    cre = c_re.astype(F32)
    cim = c_im.astype(F32)

    def power(m):
        m = jnp.asarray(m, F32)[..., None, None]
        mg = jnp.exp(m * lr * step)
        return mg * jnp.cos(m * li * step), mg * jnp.sin(m * li * step)

    pw_r, pw_i = power(jnp.arange(CHUNK + 1))
    cp_r = cre[None] * pw_r[:, :, None, :] - cim[None] * pw_i[:, :, None, :]
    cp_i = cre[None] * pw_i[:, :, None, :] + cim[None] * pw_r[:, :, None, :]
    kmat = (jnp.einsum('tgap,gph->tgah', cp_r[:CHUNK], bbar_r) - jnp.einsum('tgap,gph->tgah', cp_i[:CHUNK], bbar_i))
    width = CHUNK * SSM_GROUP
    kcat = kmat.transpose(1, 3, 0, 2).reshape(SSM_GROUPS, SSM_GROUP, width)
    m_intra = jnp.stack([jnp.pad(kcat[:, :, :width - s * SSM_GROUP], ((0, 0), (0, 0), (s * SSM_GROUP, 0)))
                         for s in range(CHUNK)], axis=1).reshape(SSM_GROUPS, width, width)
    mo_r = cp_r[1:].transpose(1, 3, 0, 2).reshape(SSM_GROUPS, SSM_STATE, CHUNK * SSM_GROUP)
    mo_i = -cp_i[1:].transpose(1, 3, 0, 2).reshape(SSM_GROUPS, SSM_STATE, CHUNK * SSM_GROUP)
    mcat = jnp.concatenate([m_intra, mo_r, mo_i], axis=1).astype(BF16)
    q_r = pw_r[CHUNK - 1::-1][:CHUNK][:, :, :, None]
    q_i = pw_i[CHUNK - 1::-1][:CHUNK][:, :, :, None]
    mn_r = q_r * bbar_r[None] - q_i * bbar_i[None]
    mn_i = q_r * bbar_i[None] + q_i * bbar_r[None]
    minw = jnp.concatenate([mn_r, mn_i], axis=2).transpose(1, 0, 3, 2).reshape(
        SSM_GROUPS, CHUNK * SSM_GROUP, 2 * SSM_STATE).astype(BF16)

    def packed(j):
        pr, pi = power(float(CHUNK * j))
        return (jnp.concatenate([pr, pr], axis=-1).reshape(1, SSM_GROUPS * LANES),
                jnp.concatenate([-pi, pi], axis=-1).reshape(1, SSM_GROUPS * LANES))

    rows = jnp.arange(SUBLANES)[:, None]
    kinds = []
    for d in (1, 2, 4):
        c1, c2 = packed(d)
        mask = (rows >= d).astype(F32)
        kinds += [mask * c1, mask * c2]
    pk = [packed(r + 1) for r in range(SUBLANES)]
    kinds += [jnp.concatenate([p[0] for p in pk], axis=0), jnp.concatenate([p[1] for p in pk], axis=0)]
    coef = jnp.stack(kinds, axis=0)
    return minw, mcat, coef


def _mixer_params(norm_mix, w_in, ssm_lam_re, ssm_lam_im, ssm_b_re, ssm_b_im, ssm_c_re, ssm_c_im, ssm_d,
                  ssm_log_step, ssm_w_glu, ssm_b_glu, pool_w, pool_scale, conv_w, branch_norm, w_out, norm_ffn,
                  router_w, router_b):
    depth = w_in.shape[0]
    minw, mcat, coef = jax.vmap(_s5_tables)(ssm_lam_re, ssm_lam_im, ssm_b_re, ssm_b_im, ssm_c_re, ssm_c_im,
                                            ssm_log_step)
    eye4 = jnp.eye(len(POOL_WINDOWS), dtype=F32)
    poolw = jnp.einsum('lgcd,gk->lgckd', pool_w.astype(F32), eye4).reshape(depth, POOL_WIDTH, POOL_WIDTH).astype(BF16)
    rw = jnp.pad(router_w.astype(F32), ((0, 0), (0, 0), (0, LANES - N_EXPERTS))).astype(BF16)
    rb = jnp.pad(router_b.astype(F32), ((0, 0), (0, LANES - N_EXPERTS))).reshape(depth, 1, LANES)
    row = lambda v: v.reshape(depth, 1, -1).astype(F32)
    utri = jnp.triu(jnp.ones((TILE_T, TILE_T), F32), 1).astype(BF16)
    return dict(gmix=row(norm_mix), win=w_in.astype(BF16), minw=minw, mcat=mcat, coef=coef, dskip=row(ssm_d),
                wglu=ssm_w_glu.astype(BF16), bglu=row(ssm_b_glu), poolw=poolw, pscale=row(pool_scale),
                convw=conv_w.astype(F32), gbr=row(branch_norm), wout=w_out.astype(BF16),
                gffn=row(norm_ffn), rw=rw, rb=rb, utri=utri[None])


def kernel(x, norm_mix, w_in, ssm_lam_re, ssm_lam_im, ssm_b_re, ssm_b_im, ssm_c_re, ssm_c_im, ssm_d, ssm_log_step, ssm_w_glu, ssm_b_glu, pool_w, pool_scale, conv_w, branch_norm, w_out, norm_ffn, router_w, router_b, w_gate_up, b_gate_up, w_down, b_down, final_norm):
    batch, seq, d_model = x.shape
    assert d_model == D_MODEL and seq % TILE_T == 0
    depth = w_in.shape[0]
    n_streams = 2 if batch % 2 == 0 else 1
    sb = batch // n_streams
    n_tok = sb * seq
    assert n_tok % ROUTE_T == 0 and n_tok % SC_WINDOW == 0
    n_blocks = -(-(n_tok * TOP_K + N_EXPERTS * (EXPERT_BLOCK - 1)) // EXPERT_BLOCK)
    n_rows = n_blocks * EXPERT_BLOCK
    tiles = n_tok // TILE_T
    b_gu = b_gate_up.reshape(depth, N_EXPERTS, 1, 2 * D_EXPERT)
    b_dn = b_down.reshape(depth, N_EXPERTS, 1, D_MODEL)

    xin = [x.reshape(batch * seq, D_MODEL)] * n_streams
    tile0 = [s * tiles for s in range(n_streams)]
    ysg = [None] * n_streams
    slab = [None] * n_streams
    lp = _mixer_params(norm_mix, w_in, ssm_lam_re, ssm_lam_im, ssm_b_re, ssm_b_im, ssm_c_re, ssm_c_im, ssm_d,
                       ssm_log_step, ssm_w_glu, ssm_b_glu, pool_w, pool_scale, conv_w, branch_norm, w_out,
                       norm_ffn, router_w, router_b)
    for l in range(depth):
        routed = []
        for s in range(n_streams):
            xin[s], hp, slab[s], slabt, counts = _mixer_call(xin[s], tile0[s], ysg[s], slab[s], lp, l, sb, seq)
            tile0[s] = 0
            dest, meta = _route_call(slabt, counts, n_tok, n_blocks)
            idx = dest.reshape(1, 2 * TOP_K * n_tok)
            xs = _dispatch(hp.reshape(2 * n_tok, SC_ROW), idx, 2 * n_rows).reshape(2, n_rows, SC_ROW)
            routed.append((xs, idx, meta[:n_blocks, 0], meta[n_blocks:n_blocks + 1, 0]))
        for s, (xs, idx, block_e, n_valid) in enumerate(routed):
            ys = _expert_call(xs, block_e, n_valid, w_gate_up, b_gu, w_down, b_dn, l, n_blocks)
            ysg[s] = _combine(ys.reshape(2 * n_rows, SC_ROW), idx).reshape(TOP_K, 2, n_tok, SC_ROW)
    g = final_norm.reshape(1, D_MODEL).astype(F32)
    out = None
    for s in range(n_streams):
        out = _final_call(xin[s], ysg[s], slab[s], g, s * tiles, batch * seq, out)
    return out.reshape(batch, seq, D_MODEL)
```

```python
import functools
import math

import jax
import jax.numpy as jnp
from jax import lax
from jax.experimental import pallas as pl
from jax.experimental.pallas import tpu as pltpu
from jax.experimental.pallas import tpu_sc as plsc

F32 = jnp.float32
BF16 = jnp.bfloat16
I32 = jnp.int32

D_MODEL = 1024
SSM_WIDTH = 512
POOL_WIDTH = 256
CONV_WIDTH = 256
SSM_GROUP = 16
SSM_GROUPS = 32
SSM_STATE = 64
LAMBDA_RE_MAX = -1e-4
POOL_WINDOWS = (2, 4, 8, 16)
POOL_GROUP = 64
CONV_K = 3
IN_WIDTH = SSM_WIDTH + POOL_WIDTH + 3 * CONV_WIDTH
N_EXPERTS = 32
TOP_K = 4
D_EXPERT = 1024
SWIGLU_LIMIT = 7.0
SWIGLU_ALPHA = 1.702
EXPERT_BLOCK = 256
STEP_ROWS = 512
NORM_EPS = 1e-5

LANES = 128
SUBLANES = 8
TILE_T = 512
FINAL_T = 512
CHUNK = 16
N_CHUNK = TILE_T // CHUNK
POOL_TAIL = SUBLANES * len(POOL_WINDOWS)
assert POOL_WINDOWS == tuple(2 ** (k + 1) for k in range(len(POOL_WINDOWS))) and POOL_TAIL >= POOL_WINDOWS[-1]
PACKED = D_MODEL // 2
ROUTE_T = 2048
SC_WINDOW = 128
SC_ROW = PACKED // 2
HI_MASK = -65536
VMEM_LIMIT = 56 * 1024 * 1024


def _bf16_round(v):
    return v.astype(BF16).astype(F32)


def _pack_rows(v):
    lo = lax.shift_right_logical(lax.bitcast_convert_type(_bf16_round(v[:, :PACKED]), I32), 16)
    hi = lax.bitcast_convert_type(_bf16_round(v[:, PACKED:]), I32) & HI_MASK
    return hi | lo


def _unpack_lo(w):
    return lax.bitcast_convert_type(lax.shift_left(w, 16), F32)


def _unpack_hi(w):
    return lax.bitcast_convert_type(w & HI_MASK, F32)


def _rms(v, g):
    r = lax.rsqrt(jnp.mean(v * v, axis=-1, keepdims=True) + NORM_EPS)
    return (v * r) * g


def _moe_combine(x, y_ref, slab):
    parts = [x[:, i * SC_ROW:(i + 1) * SC_ROW] for i in range(4)]
    for k in range(TOP_K):
        g = slab[:, TOP_K + k:TOP_K + k + 1]
        for h in range(2):
            w = y_ref[k, h]
            parts[h] = parts[h] + g * _unpack_lo(w)
            parts[2 + h] = parts[2 + h] + g * _unpack_hi(w)
    return jnp.concatenate(parts, axis=1)


def _block_transpose8(vs, lane):
    vs = list(vs)
    for dist in (4, 2, 1):
        width = SSM_GROUP * dist
        low = (lane % (2 * width)) < width
        for j in range(8):
            if j & dist:
                continue
            a, b = vs[j], vs[j + dist]
            vs[j] = jnp.where(low, a, pltpu.roll(b, width, axis=1))
            vs[j + dist] = jnp.where(low, pltpu.roll(a, LANES - width, axis=1), b)
    return vs


def _cmul_packed(c1, c2, v):
    return c1 * v + c2 * pltpu.roll(v, SSM_STATE, axis=1)


def _chunk_scan(s, carry_in, cf):
    outs = []
    carry = jnp.broadcast_to(carry_in, (SUBLANES, LANES))
    for rg in range(N_CHUNK // SUBLANES):
        x = s[rg * SUBLANES:(rg + 1) * SUBLANES]
        for j, d in enumerate((1, 2, 4)):
            x = x + _cmul_packed(cf[2 * j], cf[2 * j + 1], pltpu.roll(x, d, axis=0))
        x = x + _cmul_packed(cf[6], cf[7], carry)
        outs.append(x)
        carry = jnp.broadcast_to(x[SUBLANES - 1:SUBLANES], (SUBLANES, LANES))
    return jnp.concatenate(outs, axis=0)


def _s5_mixer(u, first, minw_ref, mcat_ref, coef_ref, u_s, y_s, h_s):
    blocks = SSM_WIDTH // LANES
    for b4 in range(blocks):
        u_s[b4] = u[:, b4 * LANES:(b4 + 1) * LANES]
    lane = lax.broadcasted_iota(I32, (N_CHUNK, LANES), 1)

    halves = [[None, None] for _ in range(SSM_GROUPS)]
    for b4 in range(blocks):
        for hh in range(2):
            xs = [u_s[b4, pl.ds(8 * hh + j, N_CHUNK, stride=CHUNK), :].astype(BF16) for j in range(8)]
            ws = _block_transpose8(xs, lane)
            for gl in range(8):
                halves[8 * b4 + gl][hh] = ws[gl]
    ug = [jnp.concatenate(h2, axis=1) for h2 in halves]

    carry_in = jnp.where(first, 0.0, h_s[N_CHUNK + 7:N_CHUNK + 8, :])
    h_s[7:8, :] = carry_in
    for g in range(SSM_GROUPS):
        cols = slice(g * LANES, (g + 1) * LANES)
        s_g = jnp.dot(ug[g], minw_ref[g], preferred_element_type=F32)
        cf = [coef_ref[k, :, cols] for k in range(8)]
        h_s[8:8 + N_CHUNK, cols] = _chunk_scan(s_g, carry_in[:, cols], cf)

    yg = []
    for g in range(SSM_GROUPS):
        hprev = h_s[7:7 + N_CHUNK, g * LANES:(g + 1) * LANES].astype(BF16)
        lhs = jnp.concatenate([ug[g], hprev], axis=1)
        yg.append(jnp.dot(lhs, mcat_ref[g], preferred_element_type=F32).astype(BF16))

    for b4 in range(blocks):
        for hh in range(2):
            ws = [yg[8 * b4 + gl][:, hh * LANES:(hh + 1) * LANES] for gl in range(8)]
            xs = _block_transpose8(ws, lane)
            for j in range(8):
                y_s[b4, pl.ds(8 * hh + j, N_CHUNK, stride=CHUNK), :] = xs[j].astype(F32)
    return jnp.concatenate([y_s[b4] for b4 in range(blocks)], axis=1)


def _mixer_kernel(*refs, first_layer):
    if first_layer:
        x_ref = refs[0]
        refs = refs[1:]
    else:
        x_ref, yprev_ref, slabprev_ref = refs[:3]
        refs = refs[3:]
    (gmix_ref, win_ref, minw_ref, mcat_ref, coef_ref, dskip_ref, wglu_ref, bglu_ref, poolw_ref, pscale_ref,
     convw_ref, gbr_ref, wout_ref, gffn_ref, rw_ref, rb_ref, utri_ref,
     x1_ref, hp_ref, slab_ref, slabt_ref, cnt_ref,
     u_s, y_s, h_s, pbuf_ref, zbuf_ref, cntacc_ref) = refs

    b = pl.program_id(0)
    l = pl.program_id(1)
    seq_start = l == 0

    @pl.when(jnp.logical_and(b == 0, l == 0))
    def _():
        cntacc_ref[...] = jnp.zeros_like(cntacc_ref)

    if first_layer:
        x = x_ref[...]
    else:
        x = _moe_combine(x_ref[...], yprev_ref, slabprev_ref[...])

    hn = _rms(x, gmix_ref[...]).astype(BF16)
    proj = jnp.dot(hn, win_ref[...], preferred_element_type=F32)
    u = proj[:, :SSM_WIDTH]
    p = proj[:, SSM_WIDTH:SSM_WIDTH + POOL_WIDTH]
    c0 = SSM_WIDTH + POOL_WIDTH
    bg = proj[:, c0:c0 + CONV_WIDTH]
    cg = proj[:, c0 + CONV_WIDTH:c0 + 2 * CONV_WIDTH]
    hv = proj[:, c0 + 2 * CONV_WIDTH:]

    y = _s5_mixer(u, seq_start, minw_ref, mcat_ref, coef_ref, u_s, y_s, h_s) + dskip_ref[...] * u
    z = jax.nn.gelu(y)
    glu = jnp.dot(z.astype(BF16), wglu_ref[...], preferred_element_type=F32) + bglu_ref[...]
    y_ssm = z * jax.nn.sigmoid(glu)

    tail = POOL_TAIL
    pbuf_ref[0, 0:tail, :] = jnp.where(seq_start, 0.0, pbuf_ref[0, TILE_T:TILE_T + tail, :])
    pbuf_ref[0, tail:, :] = p
    sums = []
    for stage, w in enumerate(POOL_WINDOWS):
        src = 0 if stage == 0 else 1 + (stage - 1) % 2
        dst = 1 + stage % 2
        d = w // 2
        lo = SUBLANES * (stage + 1)
        n = tail + TILE_T - lo
        e = pbuf_ref[src, pl.ds(lo, n), :] + pbuf_ref[src, pl.ds(lo - d, n), :]
        if stage + 1 < len(POOL_WINDOWS):
            pbuf_ref[dst, pl.ds(lo, n), :] = e
        sums.append(e[tail - lo:, :])
    lane_p = lax.broadcasted_iota(I32, (TILE_T, POOL_WIDTH), 1)
    total = sums[-1]
    win = jnp.full((TILE_T, POOL_WIDTH), POOL_WINDOWS[-1], I32)
    for i in range(len(POOL_WINDOWS) - 2, -1, -1):
        sel = lane_p < (i + 1) * POOL_GROUP
        total = jnp.where(sel, sums[i], total)
        win = jnp.where(sel, POOL_WINDOWS[i], win)
    row_p = lax.broadcasted_iota(I32, (TILE_T, POOL_WIDTH), 0) + l * TILE_T
    count = jnp.minimum(row_p + 1, win).astype(F32)
    pooled = total / count - p
    mixed = jnp.dot(pooled.astype(BF16), poolw_ref[...], preferred_element_type=F32)
    y_pool = mixed * pscale_ref[...]

    zc = cg * hv
    zbuf_ref[0:8, :] = jnp.where(seq_start, 0.0, zbuf_ref[TILE_T:TILE_T + 8, :])
    zbuf_ref[8:, :] = zc
    yc = (convw_ref[0:1, :] * zbuf_ref[pl.ds(6, TILE_T), :]
          + convw_ref[1:2, :] * zbuf_ref[pl.ds(7, TILE_T), :]
          + convw_ref[2:3, :] * zc)
    y_conv = bg * yc

    gbr = gbr_ref[...]
    s1 = SSM_WIDTH + POOL_WIDTH
    mixed_all = jnp.concatenate([_rms(y_ssm, gbr[:, :SSM_WIDTH]).astype(BF16),
                                 _rms(y_pool, gbr[:, SSM_WIDTH:s1]).astype(BF16),
                                 _rms(y_conv, gbr[:, s1:]).astype(BF16)], axis=1)
    x1 = x + jnp.dot(mixed_all, wout_ref[...], preferred_element_type=F32)
    x1_ref[...] = x1

    hn2 = _rms(x1, gffn_ref[...])
    packed = _pack_rows(hn2)
    hp_ref[0] = packed[:, :SC_ROW]
    hp_ref[1] = packed[:, SC_ROW:]
    logits = jnp.dot(hn2.astype(BF16), rw_ref[...], preferred_element_type=F32) + rb_ref[...]
    work = logits.T[:N_EXPERTS, :]
    eio = lax.broadcasted_iota(I32, (N_EXPERTS, TILE_T), 0).astype(F32)
    neg = jnp.float32(-jnp.inf)
    vals, idxs = [], []
    for _ in range(TOP_K):
        m = jnp.max(work, axis=0, keepdims=True)
        idx = jnp.min(jnp.where(work == m, eio, float(N_EXPERTS)), axis=0, keepdims=True)
        vals.append(m)
        idxs.append(idx)
        work = jnp.where(eio == idx, neg, work)
    exps = [jnp.exp(v - vals[0]) for v in vals]
    denom = exps[0] + exps[1] + exps[2] + exps[3]
    gates = [e / denom for e in exps]

    onehot = jnp.zeros((N_EXPERTS, TILE_T), F32)
    for idx in idxs:
        onehot = jnp.where(eio == idx, 1.0, onehot)
    before = jnp.dot(onehot.astype(BF16), utri_ref[...], preferred_element_type=F32) + cntacc_ref[:, 0:1]
    ranks = [jnp.sum(jnp.where(eio == idx, before, 0.0), axis=0, keepdims=True) for idx in idxs]
    row = lax.broadcasted_iota(I32, (LANES, TILE_T), 0)
    slab_t = jnp.zeros((LANES, TILE_T), F32)
    for k in range(TOP_K):
        slab_t = jnp.where(row == k, idxs[k], slab_t)
        slab_t = jnp.where(row == TOP_K + k, gates[k], slab_t)
        slab_t = jnp.where(row == 2 * TOP_K + k, ranks[k], slab_t)
    slabt_ref[...] = slab_t[:2 * SUBLANES, :]
    slab_ref[...] = slab_t.T
    newcnt = cntacc_ref[:, 0:1] + jnp.sum(onehot, axis=1, keepdims=True)
    cntacc_ref[...] = jnp.broadcast_to(newcnt, cntacc_ref.shape)
    cnt_ref[...] = jnp.broadcast_to(newcnt, cnt_ref.shape)


def _layer_spec(shape, layer):
    nd = len(shape) - 1
    sel = layer if shape[0] > 1 else 0
    return pl.BlockSpec((None,) + tuple(shape[1:]), lambda b, l, nd=nd, sel=sel: (sel,) + (0,) * nd,
                        pipeline_mode=pl.Buffered(1))


def _mixer_call(x, x_tile0, yprev, slabprev, lp, layer, batch, seq):
    first_layer = yprev is None
    n_tok = batch * seq
    steps = seq // TILE_T
    tok_spec = lambda w: pl.BlockSpec((TILE_T, w), lambda b, l: (b * steps + l, 0))
    in_specs = [pl.BlockSpec((TILE_T, D_MODEL), lambda b, l: (x_tile0 + b * steps + l, 0))]
    args = [x]
    if not first_layer:
        in_specs += [pl.BlockSpec((TOP_K, 2, TILE_T, SC_ROW), lambda b, l: (0, 0, b * steps + l, 0)), tok_spec(LANES)]
        args += [yprev, slabprev]
    weights = [lp['gmix'], lp['win'], lp['minw'], lp['mcat'], lp['coef'], lp['dskip'], lp['wglu'], lp['bglu'],
               lp['poolw'], lp['pscale'], lp['convw'], lp['gbr'], lp['wout'], lp['gffn'], lp['rw'], lp['rb'],
               lp['utri']]
    in_specs += [_layer_spec(w.shape, layer) for w in weights]
    args += weights
    out_shape = [jax.ShapeDtypeStruct((n_tok, D_MODEL), F32),
                 jax.ShapeDtypeStruct((2, n_tok, SC_ROW), I32),
                 jax.ShapeDtypeStruct((n_tok, LANES), F32),
                 jax.ShapeDtypeStruct((2 * SUBLANES, n_tok), F32),
                 jax.ShapeDtypeStruct((N_EXPERTS, LANES), F32)]
    out_specs = [tok_spec(D_MODEL), pl.BlockSpec((2, TILE_T, SC_ROW), lambda b, l: (0, b * steps + l, 0)), tok_spec(LANES),
                 pl.BlockSpec((2 * SUBLANES, TILE_T), lambda b, l: (0, b * steps + l)),
                 pl.BlockSpec((N_EXPERTS, LANES), lambda b, l: (0, 0))]
    scratch = [pltpu.VMEM((SSM_WIDTH // LANES, TILE_T, LANES), F32),
               pltpu.VMEM((SSM_WIDTH // LANES, TILE_T, LANES), F32),
               pltpu.VMEM((SUBLANES + N_CHUNK, SSM_GROUPS * LANES), F32),
               pltpu.VMEM((3, POOL_TAIL + TILE_T, POOL_WIDTH), F32),
               pltpu.VMEM((8 + TILE_T, CONV_WIDTH), F32),
               pltpu.VMEM((N_EXPERTS, LANES), F32)]
    return pl.pallas_call(
        functools.partial(_mixer_kernel, first_layer=first_layer),
        grid=(batch, steps),
        in_specs=in_specs, out_specs=out_specs, out_shape=out_shape, scratch_shapes=scratch,
        compiler_params=pltpu.CompilerParams(dimension_semantics=("arbitrary", "arbitrary"),
                                             vmem_limit_bytes=VMEM_LIMIT),
        name="mixer_first" if first_layer else "mixer_next",
    )(*args)


def _route_kernel(slabt_ref, cnt_ref, dest_ref, meta_ref, *, n_steps):
    lane1 = lax.broadcasted_iota(I32, (1, LANES), 1)
    e_row = lax.broadcasted_iota(I32, (N_EXPERTS, LANES), 0)
    e_lane = lax.broadcasted_iota(I32, (N_EXPERTS, LANES), 1)
    cnt = cnt_ref[...].astype(I32)
    counts = jnp.sum(jnp.where(e_row == e_lane, cnt, 0), axis=0, keepdims=True)
    padded = ((counts + (STEP_ROWS - 1)) // STEP_ROWS) * STEP_ROWS
    pend = padded
    for sh in (1, 2, 4, 8, 16):
        pend = pend + jnp.where(lane1 >= sh, pltpu.roll(pend, sh, axis=1), 0)
    pstart = pend - padded

    @pl.when(pl.program_id(0) == 0)
    def _():
        rows = meta_ref.shape[0]
        bstart = lax.broadcasted_iota(I32, (rows, LANES), 0) * STEP_ROWS
        lane = lax.broadcasted_iota(I32, (rows, LANES), 1)
        done = jnp.where(jnp.logical_and(lane < N_EXPERTS, pend <= bstart), 1, 0)
        be = jnp.minimum(jnp.sum(done, axis=1, keepdims=True), N_EXPERTS - 1)
        mine = lane == be
        left = jnp.sum(jnp.where(mine, counts + pstart, 0), axis=1, keepdims=True) - bstart[:, 0:1]
        nblk = jnp.clip((left + (EXPERT_BLOCK - 1)) // EXPERT_BLOCK, 0, STEP_ROWS // EXPERT_BLOCK)
        total = jnp.sum(jnp.where(lane1 == N_EXPERTS - 1, pend, 0), axis=1, keepdims=True)
        row = lax.broadcasted_iota(I32, (rows, LANES), 0)
        table = jnp.where(lane == 0, be, jnp.where(lane == 1, nblk, 0))
        meta_ref[...] = jnp.where(row == n_steps, total // STEP_ROWS, table)

    pstart_col = jnp.sum(jnp.where(e_row == e_lane, pstart, 0), axis=1, keepdims=True)
    tokens = slabt_ref.shape[1]
    eio = lax.broadcasted_iota(I32, (N_EXPERTS, tokens), 0)
    row = lax.broadcasted_iota(I32, (SUBLANES, tokens), 0)
    out = jnp.zeros((SUBLANES, tokens), I32)
    for k in range(TOP_K):
        idx = slabt_ref[k:k + 1, :].astype(I32)
        rank = slabt_ref[2 * TOP_K + k:2 * TOP_K + k + 1, :].astype(I32)
        base = jnp.sum(jnp.where(eio == idx, pstart_col, 0), axis=0, keepdims=True)
        for h in range(2):
            out = jnp.where(row == 2 * k + h, base + rank + h * n_steps * STEP_ROWS, out)
    dest_ref[...] = out


def _route_call(slabt, counts, n_tok, n_steps):
    meta_rows = ((n_steps + 1 + SUBLANES - 1) // SUBLANES) * SUBLANES
    return pl.pallas_call(
        functools.partial(_route_kernel, n_steps=n_steps),
        grid=(n_tok // ROUTE_T,),
        in_specs=[pl.BlockSpec((2 * SUBLANES, ROUTE_T), lambda i: (0, i)),
                  pl.BlockSpec((N_EXPERTS, LANES), lambda i: (0, 0))],
        out_specs=[pl.BlockSpec((SUBLANES, ROUTE_T), lambda i: (0, i)),
                   pl.BlockSpec((meta_rows, LANES), lambda i: (0, 0))],
        out_shape=[jax.ShapeDtypeStruct((SUBLANES, n_tok), I32),
                   jax.ShapeDtypeStruct((meta_rows, LANES), I32)],
        compiler_params=pltpu.CompilerParams(dimension_semantics=("arbitrary",)),
        name="route",
    )(slabt, counts)


def _sc_mesh():
    return plsc.VectorSubcoreMesh(core_axis_name="core", subcore_axis_name="subcore")


def _dispatch(src, idx, n_rows):
    windows = src.shape[0] // SC_WINDOW

    @functools.partial(pl.kernel, out_type=jax.ShapeDtypeStruct((n_rows, SC_ROW), src.dtype),
                       mesh=_sc_mesh(), scratch_types=[])
    def scatter_rows(x_hbm, i_hbm, o_hbm):
        def body(x_vmem, i_vmem):
            pltpu.sync_copy(x_vmem, o_hbm.at[i_vmem.at[0]])

        pltpu.emit_pipeline(
            body,
            grid=(idx.shape[1] // SC_WINDOW,),
            in_specs=[pl.BlockSpec((SC_WINDOW, SC_ROW), lambda i: (lax.rem(i, windows), 0)),
                      pl.BlockSpec((1, SC_WINDOW), lambda i: (0, i))],
            out_specs=[],
            core_axis_name=("core", "subcore"),
            dimension_semantics=(pltpu.PARALLEL,),
        )(x_hbm, i_hbm)

    return scatter_rows(src, idx)


def _combine(src, idx):
    n_idx = idx.shape[1]

    @functools.partial(pl.kernel, out_type=jax.ShapeDtypeStruct((n_idx, SC_ROW), src.dtype),
                       mesh=_sc_mesh(), scratch_types=[])
    def gather_rows(y_hbm, i_hbm, o_hbm):
        def body(i_vmem, o_vmem):
            pltpu.sync_copy(y_hbm.at[i_vmem.at[0]], o_vmem)

        pltpu.emit_pipeline(
            body,
            grid=(n_idx // SC_WINDOW,),
            in_specs=[pl.BlockSpec((1, SC_WINDOW), lambda i: (0, i))],
            out_specs=[pl.BlockSpec((SC_WINDOW, SC_ROW), lambda i: (i, 0))],
            core_axis_name=("core", "subcore"),
            dimension_semantics=(pltpu.PARALLEL,),
        )(i_hbm, o_hbm)

    return gather_rows(src, idx)


def _expert_rows(xs_ref, ys_ref, rows, wgu_s, bgu_ref, wdn_s, bdn_ref, act_s):
    w0 = xs_ref[0, 0:rows, :]
    w1 = xs_ref[1, 0:rows, :]
    xb = jnp.concatenate([_unpack_lo(w0).astype(BF16), _unpack_lo(w1).astype(BF16),
                          _unpack_hi(w0).astype(BF16), _unpack_hi(w1).astype(BF16)], axis=1)
    half = D_EXPERT // 2
    for j in range(2):
        cols = slice(j * half, (j + 1) * half)
        ucols = slice(D_EXPERT + j * half, D_EXPERT + (j + 1) * half)
        g = jnp.dot(xb, wgu_s[:, cols], preferred_element_type=F32) + bgu_ref[:, cols]
        up = jnp.dot(xb, wgu_s[:, ucols], preferred_element_type=F32) + bgu_ref[:, ucols]
        g = jnp.minimum(g, SWIGLU_LIMIT)
        up = jnp.clip(up, -SWIGLU_LIMIT, SWIGLU_LIMIT)
        act_s[0:rows, cols] = (g * jax.nn.sigmoid(SWIGLU_ALPHA * g) * (up + 1.0)).astype(BF16)
    y = jnp.dot(act_s[0:rows, :], wdn_s[...], preferred_element_type=F32) + bdn_ref[...]
    packed = _pack_rows(y)
    ys_ref[0, 0:rows, :] = packed[:, :SC_ROW]
    ys_ref[1, 0:rows, :] = packed[:, SC_ROW:]


def _expert_kernel(be_ref, nb_ref, nv_ref, xs_ref, wgu_ref, bgu_ref, wdn_ref, bdn_ref, ys_ref, wgu_s, wdn_s, act_s):
    blk = pl.program_id(0)
    prev = be_ref[jnp.maximum(blk - 1, 0)]
    changed = jnp.logical_or(blk == 0, be_ref[blk] != prev)
    valid = blk < nv_ref[0]

    @pl.when(jnp.logical_and(valid, changed))
    def _():
        chunk = 128

        def cast_gu(i, _):
            rows = pl.ds(pl.multiple_of(i * chunk, chunk), chunk)
            wgu_s[rows, :] = wgu_ref[rows, :].astype(BF16)
            return 0

        def cast_dn(i, _):
            rows = pl.ds(pl.multiple_of(i * chunk, chunk), chunk)
            wdn_s[rows, :] = wdn_ref[rows, :].astype(BF16)
            return 0

        lax.fori_loop(0, D_MODEL // chunk, cast_gu, 0)
        lax.fori_loop(0, D_EXPERT // chunk, cast_dn, 0)

    for nblk in range(1, STEP_ROWS // EXPERT_BLOCK + 1):
        @pl.when(jnp.logical_and(valid, nb_ref[blk] == nblk))
        def _(nblk=nblk):
            _expert_rows(xs_ref, ys_ref, nblk * EXPERT_BLOCK, wgu_s, bgu_ref, wdn_s, bdn_ref, act_s)


def _expert_call(xs, step_e, step_nblk, n_valid, w_gu, b_gu, w_dn, b_dn, layer, n_steps):
    def row_map(i, be, nb, nv):
        return (0, jnp.minimum(i, jnp.maximum(nv[0] - 1, 0)), 0)

    def w_map(i, be, nb, nv):
        return (layer, be[i], 0, 0)

    grid_spec = pltpu.PrefetchScalarGridSpec(
        num_scalar_prefetch=3,
        grid=(n_steps,),
        in_specs=[pl.BlockSpec((2, STEP_ROWS, SC_ROW), row_map),
                  pl.BlockSpec((None, None, D_MODEL, 2 * D_EXPERT), w_map),
                  pl.BlockSpec((None, None, 1, 2 * D_EXPERT), w_map),
                  pl.BlockSpec((None, None, D_EXPERT, D_MODEL), w_map),
                  pl.BlockSpec((None, None, 1, D_MODEL), w_map)],
        out_specs=pl.BlockSpec((2, STEP_ROWS, SC_ROW), row_map),
        scratch_shapes=[pltpu.VMEM((D_MODEL, 2 * D_EXPERT), BF16),
                        pltpu.VMEM((D_EXPERT, D_MODEL), BF16),
                        pltpu.VMEM((STEP_ROWS, D_EXPERT), BF16)])
    return pl.pallas_call(
        _expert_kernel,
        grid_spec=grid_spec,
        out_shape=jax.ShapeDtypeStruct((2, n_steps * STEP_ROWS, SC_ROW), I32),
        compiler_params=pltpu.CompilerParams(dimension_semantics=("arbitrary",),
                                             vmem_limit_bytes=VMEM_LIMIT),
        name="experts",
    )(step_e, step_nblk, n_valid, xs, w_gu, b_gu, w_dn, b_dn)


def _final_kernel(x_ref, y_ref, slab_ref, g_ref, *rest):
    o_ref = rest[-1]
    x = _moe_combine(x_ref[...], y_ref, slab_ref[...])
    o_ref[...] = _rms(x, g_ref[...])


def _final_call(x1, ysg, slab, g, out_row0, n_out, out_prev):
    t = FINAL_T
    out_tile0 = out_row0 // t
    in_specs = [pl.BlockSpec((t, D_MODEL), lambda i: (i, 0)),
                pl.BlockSpec((TOP_K, 2, t, SC_ROW), lambda i: (0, 0, i, 0)),
                pl.BlockSpec((t, LANES), lambda i: (i, 0)),
                pl.BlockSpec((1, D_MODEL), lambda i: (0, 0))]
    args = [x1, ysg, slab, g]
    aliases = {}
    if out_prev is not None:
        in_specs.append(pl.BlockSpec(memory_space=pl.ANY))
        args.append(out_prev)
        aliases = {4: 0}
    return pl.pallas_call(
        _final_kernel,
        grid=(x1.shape[0] // t,),
        in_specs=in_specs,
        out_specs=pl.BlockSpec((t, D_MODEL), lambda i: (out_tile0 + i, 0)),
        out_shape=jax.ShapeDtypeStruct((n_out, D_MODEL), F32),
        input_output_aliases=aliases,
        compiler_params=pltpu.CompilerParams(dimension_semantics=("arbitrary",)),
        name="final_norm",
    )(*args)


def _s5_tables(lam_re, lam_im, b_re, b_im, c_re, c_im, log_step):
    lr = jnp.minimum(lam_re.astype(F32), LAMBDA_RE_MAX)
    li = lam_im.astype(F32)
    step = jnp.exp(log_step.astype(F32))[:, None]
    mag = jnp.exp(lr * step)
    ar = mag * jnp.cos(li * step)
    ai = mag * jnp.sin(li * step)
    nr = ar - 1.0
    den = lr * lr + li * li
    kr = (nr * lr + ai * li) / den
    ki = (ai * lr - nr * li) / den
    bre = b_re.astype(F32)
    bim = b_im.astype(F32)
    bbar_r = kr[..., None] * bre - ki[..., None] * bim
    bbar_i = kr[..., None] * bim + ki[..., None] * bre
    cre = c_re.astype(F32)
    cim = c_im.astype(F32)

    def power(m):
        m = jnp.asarray(m, F32)[..., None, None]
        mg = jnp.exp(m * lr * step)
        return mg * jnp.cos(m * li * step), mg * jnp.sin(m * li * step)

    pw_r, pw_i = power(jnp.arange(CHUNK + 1))
    cp_r = cre[None] * pw_r[:, :, None, :] - cim[None] * pw_i[:, :, None, :]
    cp_i = cre[None] * pw_i[:, :, None, :] + cim[None] * pw_r[:, :, None, :]
    width = CHUNK * SSM_GROUP
    kcat = (jnp.einsum('tgap,gph->ghta', cp_r[:CHUNK], bbar_r)
            - jnp.einsum('tgap,gph->ghta', cp_i[:CHUNK], bbar_i)).reshape(SSM_GROUPS, SSM_GROUP, width).astype(BF16)
    m_intra = jnp.stack([jnp.pad(kcat[:, :, :width - s * SSM_GROUP], ((0, 0), (0, 0), (s * SSM_GROUP, 0)))
                         for s in range(CHUNK)], axis=1).reshape(SSM_GROUPS, width, width)
    mo_r = cp_r[1:].transpose(1, 3, 0, 2).reshape(SSM_GROUPS, SSM_STATE, width)
    mo_i = -cp_i[1:].transpose(1, 3, 0, 2).reshape(SSM_GROUPS, SSM_STATE, width)
    mcat = jnp.concatenate([m_intra, mo_r.astype(BF16), mo_i.astype(BF16)], axis=1)
    q_r = pw_r[CHUNK - 1::-1][:CHUNK][:, :, :, None]
    q_i = pw_i[CHUNK - 1::-1][:CHUNK][:, :, :, None]
    mn_r = q_r * bbar_r[None] - q_i * bbar_i[None]
    mn_i = q_r * bbar_i[None] + q_i * bbar_r[None]
    minw = jnp.concatenate([mn_r, mn_i], axis=2).transpose(1, 0, 3, 2).reshape(
        SSM_GROUPS, CHUNK * SSM_GROUP, 2 * SSM_STATE).astype(BF16)

    def packed(j):
        pr, pi = power(float(CHUNK * j))
        return (jnp.concatenate([pr, pr], axis=-1).reshape(1, SSM_GROUPS * LANES),
                jnp.concatenate([-pi, pi], axis=-1).reshape(1, SSM_GROUPS * LANES))

    rows = jnp.arange(SUBLANES)[:, None]
    kinds = []
    for d in (1, 2, 4):
        c1, c2 = packed(d)
        mask = (rows >= d).astype(F32)
        kinds += [mask * c1, mask * c2]
    pk = [packed(r + 1) for r in range(SUBLANES)]
    kinds += [jnp.concatenate([p[0] for p in pk], axis=0), jnp.concatenate([p[1] for p in pk], axis=0)]
    coef = jnp.stack(kinds, axis=0)
    return minw, mcat, coef


def _mixer_params(norm_mix, w_in, ssm_lam_re, ssm_lam_im, ssm_b_re, ssm_b_im, ssm_c_re, ssm_c_im, ssm_d,
                  ssm_log_step, ssm_w_glu, ssm_b_glu, pool_w, pool_scale, conv_w, branch_norm, w_out, norm_ffn,
                  router_w, router_b):
    depth = w_in.shape[0]
    minw, mcat, coef = jax.vmap(_s5_tables)(ssm_lam_re, ssm_lam_im, ssm_b_re, ssm_b_im, ssm_c_re, ssm_c_im,
                                            ssm_log_step)
    eye4 = jnp.eye(len(POOL_WINDOWS), dtype=F32)
    poolw = jnp.einsum('lgcd,gk->lgckd', pool_w.astype(F32), eye4).reshape(depth, POOL_WIDTH, POOL_WIDTH).astype(BF16)
    rw = jnp.pad(router_w.astype(F32), ((0, 0), (0, 0), (0, LANES - N_EXPERTS))).astype(BF16)
    rb = jnp.pad(router_b.astype(F32), ((0, 0), (0, LANES - N_EXPERTS))).reshape(depth, 1, LANES)
    row = lambda v: v.reshape(depth, 1, -1).astype(F32)
    utri = jnp.triu(jnp.ones((TILE_T, TILE_T), F32), 1).astype(BF16)
    return dict(gmix=row(norm_mix), win=w_in.astype(BF16), minw=minw, mcat=mcat, coef=coef, dskip=row(ssm_d),
                wglu=ssm_w_glu.astype(BF16), bglu=row(ssm_b_glu), poolw=poolw, pscale=row(pool_scale),
                convw=conv_w.astype(F32), gbr=row(branch_norm), wout=w_out.astype(BF16),
                gffn=row(norm_ffn), rw=rw, rb=rb, utri=utri[None])


def kernel(x, norm_mix, w_in, ssm_lam_re, ssm_lam_im, ssm_b_re, ssm_b_im, ssm_c_re, ssm_c_im, ssm_d, ssm_log_step, ssm_w_glu, ssm_b_glu, pool_w, pool_scale, conv_w, branch_norm, w_out, norm_ffn, router_w, router_b, w_gate_up, b_gate_up, w_down, b_down, final_norm):
    batch, seq, d_model = x.shape
    assert d_model == D_MODEL and seq % TILE_T == 0
    depth = w_in.shape[0]
    n_streams = 2 if batch % 2 == 0 else 1
    sb = batch // n_streams
    n_tok = sb * seq
    assert n_tok % ROUTE_T == 0 and n_tok % SC_WINDOW == 0
    n_steps = -(-(n_tok * TOP_K + N_EXPERTS * (STEP_ROWS - 1)) // STEP_ROWS)
    n_rows = n_steps * STEP_ROWS
    tiles = n_tok // TILE_T
    b_gu = b_gate_up.reshape(depth, N_EXPERTS, 1, 2 * D_EXPERT)
    b_dn = b_down.reshape(depth, N_EXPERTS, 1, D_MODEL)

    xin = [x.reshape(batch * seq, D_MODEL)] * n_streams
    tile0 = [s * tiles for s in range(n_streams)]
    ysg = [None] * n_streams
    slab = [None] * n_streams
    lp = _mixer_params(norm_mix, w_in, ssm_lam_re, ssm_lam_im, ssm_b_re, ssm_b_im, ssm_c_re, ssm_c_im, ssm_d,
                       ssm_log_step, ssm_w_glu, ssm_b_glu, pool_w, pool_scale, conv_w, branch_norm, w_out,
                       norm_ffn, router_w, router_b)
    for l in range(depth):
        routed = []
        for s in range(n_streams):
            xin[s], hp, slab[s], slabt, counts = _mixer_call(xin[s], tile0[s], ysg[s], slab[s], lp, l, sb, seq)
            tile0[s] = 0
            dest, meta = _route_call(slabt, counts, n_tok, n_steps)
            idx = dest.reshape(1, 2 * TOP_K * n_tok)
            xs = _dispatch(hp.reshape(2 * n_tok, SC_ROW), idx, 2 * n_rows).reshape(2, n_rows, SC_ROW)
            routed.append((xs, idx, meta[:n_steps, 0], meta[:n_steps, 1], meta[n_steps:n_steps + 1, 0]))
        for s, (xs, idx, step_e, step_nblk, n_valid) in enumerate(routed):
            ys = _expert_call(xs, step_e, step_nblk, n_valid, w_gate_up, b_gu, w_down, b_dn, l, n_steps)
            ysg[s] = _combine(ys.reshape(2 * n_rows, SC_ROW), idx).reshape(TOP_K, 2, n_tok, SC_ROW)
    g = final_norm.reshape(1, D_MODEL).astype(F32)
    out = None
    for s in range(n_streams):
        out = _final_call(xin[s], ysg[s], slab[s], g, s * n_tok, batch * seq, out)
    return out.reshape(batch, seq, D_MODEL)
```

```python
import functools
import math

import jax
import jax.numpy as jnp
from jax import lax
from jax.experimental import pallas as pl
from jax.experimental.pallas import tpu as pltpu
from jax.experimental.pallas import tpu_sc as plsc

F32 = jnp.float32
BF16 = jnp.bfloat16
I32 = jnp.int32

D_MODEL = 1024
SSM_WIDTH = 512
POOL_WIDTH = 256
CONV_WIDTH = 256
SSM_GROUP = 16
SSM_GROUPS = 32
SSM_STATE = 64
LAMBDA_RE_MAX = -1e-4
POOL_WINDOWS = (2, 4, 8, 16)
POOL_GROUP = 64
CONV_K = 3
IN_WIDTH = SSM_WIDTH + POOL_WIDTH + 3 * CONV_WIDTH
N_EXPERTS = 32
TOP_K = 4
D_EXPERT = 1024
SWIGLU_LIMIT = 7.0
SWIGLU_ALPHA = 1.702
EXPERT_BLOCK = 256
STEP_ROWS = 1024
NORM_EPS = 1e-5

LANES = 128
SUBLANES = 8
TILE_T = 512
FINAL_T = 512
CHUNK = 16
N_CHUNK = TILE_T // CHUNK
POOL_TAIL = SUBLANES * len(POOL_WINDOWS)
assert POOL_WINDOWS == tuple(2 ** (k + 1) for k in range(len(POOL_WINDOWS))) and POOL_TAIL >= POOL_WINDOWS[-1]
PACKED = D_MODEL // 2
ROUTE_T = 2048
SC_WINDOW = 128
SC_ROW = PACKED // 2
HI_MASK = -65536
VMEM_LIMIT = 56 * 1024 * 1024


def _bf16_round(v):
    return v.astype(BF16).astype(F32)


def _pack_rows(v):
    lo = lax.shift_right_logical(lax.bitcast_convert_type(_bf16_round(v[:, :PACKED]), I32), 16)
    hi = lax.bitcast_convert_type(_bf16_round(v[:, PACKED:]), I32) & HI_MASK
    return hi | lo


def _unpack_lo(w):
    return lax.bitcast_convert_type(lax.shift_left(w, 16), F32)


def _unpack_hi(w):
    return lax.bitcast_convert_type(w & HI_MASK, F32)


def _rms(v, g):
    r = lax.rsqrt(jnp.mean(v * v, axis=-1, keepdims=True) + NORM_EPS)
    return (v * r) * g


def _moe_combine(x, y_ref, slab):
    parts = [x[:, i * SC_ROW:(i + 1) * SC_ROW] for i in range(4)]
    for k in range(TOP_K):
        g = slab[:, TOP_K + k:TOP_K + k + 1]
        for h in range(2):
            w = y_ref[k, h]
            parts[h] = parts[h] + g * _unpack_lo(w)
            parts[2 + h] = parts[2 + h] + g * _unpack_hi(w)
    return jnp.concatenate(parts, axis=1)


def _block_transpose8(vs, lane):
    vs = list(vs)
    for dist in (4, 2, 1):
        width = SSM_GROUP * dist
        low = (lane % (2 * width)) < width
        for j in range(8):
            if j & dist:
                continue
            a, b = vs[j], vs[j + dist]
            vs[j] = jnp.where(low, a, pltpu.roll(b, width, axis=1))
            vs[j + dist] = jnp.where(low, pltpu.roll(a, LANES - width, axis=1), b)
    return vs


def _cmul_packed(c1, c2, v):
    return c1 * v + c2 * pltpu.roll(v, SSM_STATE, axis=1)


def _chunk_scan(s, carry_in, cf):
    outs = []
    carry = jnp.broadcast_to(carry_in, (SUBLANES, LANES))
    for rg in range(N_CHUNK // SUBLANES):
        x = s[rg * SUBLANES:(rg + 1) * SUBLANES]
        for j, d in enumerate((1, 2, 4)):
            x = x + _cmul_packed(cf[2 * j], cf[2 * j + 1], pltpu.roll(x, d, axis=0))
        x = x + _cmul_packed(cf[6], cf[7], carry)
        outs.append(x)
        carry = jnp.broadcast_to(x[SUBLANES - 1:SUBLANES], (SUBLANES, LANES))
    return jnp.concatenate(outs, axis=0)


def _s5_mixer(u, first, minw_ref, mcat_ref, coef_ref, u_s, y_s, h_s):
    blocks = SSM_WIDTH // LANES
    for b4 in range(blocks):
        u_s[b4] = u[:, b4 * LANES:(b4 + 1) * LANES]
    lane = lax.broadcasted_iota(I32, (N_CHUNK, LANES), 1)

    halves = [[None, None] for _ in range(SSM_GROUPS)]
    for b4 in range(blocks):
        for hh in range(2):
            xs = [u_s[b4, pl.ds(8 * hh + j, N_CHUNK, stride=CHUNK), :].astype(BF16) for j in range(8)]
            ws = _block_transpose8(xs, lane)
            for gl in range(8):
                halves[8 * b4 + gl][hh] = ws[gl]
    ug = [jnp.concatenate(h2, axis=1) for h2 in halves]

    carry_in = jnp.where(first, 0.0, h_s[N_CHUNK + 7:N_CHUNK + 8, :])
    h_s[7:8, :] = carry_in
    for g in range(SSM_GROUPS):
        cols = slice(g * LANES, (g + 1) * LANES)
        s_g = jnp.dot(ug[g], minw_ref[g], preferred_element_type=F32)
        cf = [coef_ref[k, :, cols] for k in range(8)]
        h_s[8:8 + N_CHUNK, cols] = _chunk_scan(s_g, carry_in[:, cols], cf)

    yg = []
    for g in range(SSM_GROUPS):
        hprev = h_s[7:7 + N_CHUNK, g * LANES:(g + 1) * LANES].astype(BF16)
        lhs = jnp.concatenate([ug[g], hprev], axis=1)
        yg.append(jnp.dot(lhs, mcat_ref[g], preferred_element_type=F32).astype(BF16))

    for b4 in range(blocks):
        for hh in range(2):
            ws = [yg[8 * b4 + gl][:, hh * LANES:(hh + 1) * LANES] for gl in range(8)]
            xs = _block_transpose8(ws, lane)
            for j in range(8):
                y_s[b4, pl.ds(8 * hh + j, N_CHUNK, stride=CHUNK), :] = xs[j].astype(F32)
    return jnp.concatenate([y_s[b4] for b4 in range(blocks)], axis=1)


def _mixer_kernel(*refs, first_layer):
    if first_layer:
        x_ref = refs[0]
        refs = refs[1:]
    else:
        x_ref, yprev_ref, slabprev_ref = refs[:3]
        refs = refs[3:]
    (gmix_ref, win_ref, minw_ref, mcat_ref, coef_ref, dskip_ref, wglu_ref, bglu_ref, poolw_ref, pscale_ref,
     convw_ref, gbr_ref, wout_ref, gffn_ref, rw_ref, rb_ref, utri_ref,
     x1_ref, hp_ref, slab_ref, slabt_ref, cnt_ref,
     u_s, y_s, h_s, pbuf_ref, zbuf_ref, cntacc_ref) = refs

    b = pl.program_id(0)
    l = pl.program_id(1)
    seq_start = l == 0

    @pl.when(jnp.logical_and(b == 0, l == 0))
    def _():
        cntacc_ref[...] = jnp.zeros_like(cntacc_ref)

    if first_layer:
        x = x_ref[...]
    else:
        x = _moe_combine(x_ref[...], yprev_ref, slabprev_ref[...])

    hn = _rms(x, gmix_ref[...]).astype(BF16)
    proj = jnp.dot(hn, win_ref[...], preferred_element_type=F32)
    u = proj[:, :SSM_WIDTH]
    p = proj[:, SSM_WIDTH:SSM_WIDTH + POOL_WIDTH]
    c0 = SSM_WIDTH + POOL_WIDTH
    bg = proj[:, c0:c0 + CONV_WIDTH]
    cg = proj[:, c0 + CONV_WIDTH:c0 + 2 * CONV_WIDTH]
    hv = proj[:, c0 + 2 * CONV_WIDTH:]

    y = _s5_mixer(u, seq_start, minw_ref, mcat_ref, coef_ref, u_s, y_s, h_s) + dskip_ref[...] * u
    z = jax.nn.gelu(y)
    glu = jnp.dot(z.astype(BF16), wglu_ref[...], preferred_element_type=F32) + bglu_ref[...]
    y_ssm = z * jax.nn.sigmoid(glu)

    tail = POOL_TAIL
    pbuf_ref[0, 0:tail, :] = jnp.where(seq_start, 0.0, pbuf_ref[0, TILE_T:TILE_T + tail, :])
    pbuf_ref[0, tail:, :] = p
    sums = []
    for stage, w in enumerate(POOL_WINDOWS):
        src = 0 if stage == 0 else 1 + (stage - 1) % 2
        dst = 1 + stage % 2
        d = w // 2
        lo = SUBLANES * (stage + 1)
        n = tail + TILE_T - lo
        e = pbuf_ref[src, pl.ds(lo, n), :] + pbuf_ref[src, pl.ds(lo - d, n), :]
        if stage + 1 < len(POOL_WINDOWS):
            pbuf_ref[dst, pl.ds(lo, n), :] = e
        sums.append(e[tail - lo:, :])
    lane_p = lax.broadcasted_iota(I32, (TILE_T, POOL_WIDTH), 1)
    total = sums[-1]
    win = jnp.full((TILE_T, POOL_WIDTH), POOL_WINDOWS[-1], I32)
    for i in range(len(POOL_WINDOWS) - 2, -1, -1):
        sel = lane_p < (i + 1) * POOL_GROUP
        total = jnp.where(sel, sums[i], total)
        win = jnp.where(sel, POOL_WINDOWS[i], win)
    row_p = lax.broadcasted_iota(I32, (TILE_T, POOL_WIDTH), 0) + l * TILE_T
    count = jnp.minimum(row_p + 1, win).astype(F32)
    pooled = total / count - p
    mixed = jnp.dot(pooled.astype(BF16), poolw_ref[...], preferred_element_type=F32)
    y_pool = mixed * pscale_ref[...]

    zc = cg * hv
    zbuf_ref[0:8, :] = jnp.where(seq_start, 0.0, zbuf_ref[TILE_T:TILE_T + 8, :])
    zbuf_ref[8:, :] = zc
    yc = (convw_ref[0:1, :] * zbuf_ref[pl.ds(6, TILE_T), :]
          + convw_ref[1:2, :] * zbuf_ref[pl.ds(7, TILE_T), :]
          + convw_ref[2:3, :] * zc)
    y_conv = bg * yc

    gbr = gbr_ref[...]
    s1 = SSM_WIDTH + POOL_WIDTH
    mixed_all = jnp.concatenate([_rms(y_ssm, gbr[:, :SSM_WIDTH]).astype(BF16),
                                 _rms(y_pool, gbr[:, SSM_WIDTH:s1]).astype(BF16),
                                 _rms(y_conv, gbr[:, s1:]).astype(BF16)], axis=1)
    x1 = x + jnp.dot(mixed_all, wout_ref[...], preferred_element_type=F32)
    x1_ref[...] = x1

    hn2 = _rms(x1, gffn_ref[...])
    packed = _pack_rows(hn2)
    hp_ref[0] = packed[:, :SC_ROW]
    hp_ref[1] = packed[:, SC_ROW:]
    logits = jnp.dot(hn2.astype(BF16), rw_ref[...], preferred_element_type=F32) + rb_ref[...]
    work = logits.T[:N_EXPERTS, :]
    eio = lax.broadcasted_iota(I32, (N_EXPERTS, TILE_T), 0).astype(F32)
    neg = jnp.float32(-jnp.inf)
    vals, idxs = [], []
    for _ in range(TOP_K):
        m = jnp.max(work, axis=0, keepdims=True)
        idx = jnp.min(jnp.where(work == m, eio, float(N_EXPERTS)), axis=0, keepdims=True)
        vals.append(m)
        idxs.append(idx)
        work = jnp.where(eio == idx, neg, work)
    exps = [jnp.exp(v - vals[0]) for v in vals]
    denom = exps[0] + exps[1] + exps[2] + exps[3]
    gates = [e / denom for e in exps]

    onehot = jnp.zeros((N_EXPERTS, TILE_T), F32)
    for idx in idxs:
        onehot = jnp.where(eio == idx, 1.0, onehot)
    before = jnp.dot(onehot.astype(BF16), utri_ref[...], preferred_element_type=F32) + cntacc_ref[:, 0:1]
    ranks = [jnp.sum(jnp.where(eio == idx, before, 0.0), axis=0, keepdims=True) for idx in idxs]
    row = lax.broadcasted_iota(I32, (LANES, TILE_T), 0)
    slab_t = jnp.zeros((LANES, TILE_T), F32)
    for k in range(TOP_K):
        slab_t = jnp.where(row == k, idxs[k], slab_t)
        slab_t = jnp.where(row == TOP_K + k, gates[k], slab_t)
        slab_t = jnp.where(row == 2 * TOP_K + k, ranks[k], slab_t)
    slabt_ref[...] = slab_t[:2 * SUBLANES, :]
    slab_ref[...] = slab_t.T
    newcnt = cntacc_ref[:, 0:1] + jnp.sum(onehot, axis=1, keepdims=True)
    cntacc_ref[...] = jnp.broadcast_to(newcnt, cntacc_ref.shape)
    cnt_ref[...] = jnp.broadcast_to(newcnt, cnt_ref.shape)


def _layer_spec(shape, layer):
    nd = len(shape) - 1
    sel = layer if shape[0] > 1 else 0
    return pl.BlockSpec((None,) + tuple(shape[1:]), lambda b, l, nd=nd, sel=sel: (sel,) + (0,) * nd,
                        pipeline_mode=pl.Buffered(1))


def _mixer_call(x, x_tile0, yprev, slabprev, lp, layer, batch, seq):
    first_layer = yprev is None
    n_tok = batch * seq
    steps = seq // TILE_T
    tok_spec = lambda w: pl.BlockSpec((TILE_T, w), lambda b, l: (b * steps + l, 0))
    in_specs = [pl.BlockSpec((TILE_T, D_MODEL), lambda b, l: (x_tile0 + b * steps + l, 0))]
    args = [x]
    if not first_layer:
        in_specs += [pl.BlockSpec((TOP_K, 2, TILE_T, SC_ROW), lambda b, l: (0, 0, b * steps + l, 0)), tok_spec(LANES)]
        args += [yprev, slabprev]
    weights = [lp['gmix'], lp['win'], lp['minw'], lp['mcat'], lp['coef'], lp['dskip'], lp['wglu'], lp['bglu'],
               lp['poolw'], lp['pscale'], lp['convw'], lp['gbr'], lp['wout'], lp['gffn'], lp['rw'], lp['rb'],
               lp['utri']]
    in_specs += [_layer_spec(w.shape, layer) for w in weights]
    args += weights
    out_shape = [jax.ShapeDtypeStruct((n_tok, D_MODEL), F32),
                 jax.ShapeDtypeStruct((2, n_tok, SC_ROW), I32),
                 jax.ShapeDtypeStruct((n_tok, LANES), F32),
                 jax.ShapeDtypeStruct((2 * SUBLANES, n_tok), F32),
                 jax.ShapeDtypeStruct((N_EXPERTS, LANES), F32)]
    out_specs = [tok_spec(D_MODEL), pl.BlockSpec((2, TILE_T, SC_ROW), lambda b, l: (0, b * steps + l, 0)), tok_spec(LANES),
                 pl.BlockSpec((2 * SUBLANES, TILE_T), lambda b, l: (0, b * steps + l)),
                 pl.BlockSpec((N_EXPERTS, LANES), lambda b, l: (0, 0))]
    scratch = [pltpu.VMEM((SSM_WIDTH // LANES, TILE_T, LANES), F32),
               pltpu.VMEM((SSM_WIDTH // LANES, TILE_T, LANES), F32),
               pltpu.VMEM((SUBLANES + N_CHUNK, SSM_GROUPS * LANES), F32),
               pltpu.VMEM((3, POOL_TAIL + TILE_T, POOL_WIDTH), F32),
               pltpu.VMEM((8 + TILE_T, CONV_WIDTH), F32),
               pltpu.VMEM((N_EXPERTS, LANES), F32)]
    return pl.pallas_call(
        functools.partial(_mixer_kernel, first_layer=first_layer),
        grid=(batch, steps),
        in_specs=in_specs, out_specs=out_specs, out_shape=out_shape, scratch_shapes=scratch,
        compiler_params=pltpu.CompilerParams(dimension_semantics=("arbitrary", "arbitrary"),
                                             vmem_limit_bytes=VMEM_LIMIT),
        name="mixer_first" if first_layer else "mixer_next",
    )(*args)


def _route_kernel(slabt_ref, cnt_ref, dest_ref, meta_ref, *, n_steps):
    lane1 = lax.broadcasted_iota(I32, (1, LANES), 1)
    e_row = lax.broadcasted_iota(I32, (N_EXPERTS, LANES), 0)
    e_lane = lax.broadcasted_iota(I32, (N_EXPERTS, LANES), 1)
    cnt = cnt_ref[...].astype(I32)
    counts = jnp.sum(jnp.where(e_row == e_lane, cnt, 0), axis=0, keepdims=True)
    padded = ((counts + (STEP_ROWS - 1)) // STEP_ROWS) * STEP_ROWS
    pend = padded
    for sh in (1, 2, 4, 8, 16):
        pend = pend + jnp.where(lane1 >= sh, pltpu.roll(pend, sh, axis=1), 0)
    pstart = pend - padded

    @pl.when(pl.program_id(0) == 0)
    def _():
        rows = meta_ref.shape[0]
        bstart = lax.broadcasted_iota(I32, (rows, LANES), 0) * STEP_ROWS
        lane = lax.broadcasted_iota(I32, (rows, LANES), 1)
        done = jnp.where(jnp.logical_and(lane < N_EXPERTS, pend <= bstart), 1, 0)
        be = jnp.minimum(jnp.sum(done, axis=1, keepdims=True), N_EXPERTS - 1)
        mine = lane == be
        left = jnp.sum(jnp.where(mine, counts + pstart, 0), axis=1, keepdims=True) - bstart[:, 0:1]
        nblk = jnp.clip((left + (EXPERT_BLOCK - 1)) // EXPERT_BLOCK, 0, STEP_ROWS // EXPERT_BLOCK)
        total = jnp.sum(jnp.where(lane1 == N_EXPERTS - 1, pend, 0), axis=1, keepdims=True)
        row = lax.broadcasted_iota(I32, (rows, LANES), 0)
        table = jnp.where(lane == 0, be, jnp.where(lane == 1, nblk, 0))
        meta_ref[...] = jnp.where(row == n_steps, total // STEP_ROWS, table)

    pstart_col = jnp.sum(jnp.where(e_row == e_lane, pstart, 0), axis=1, keepdims=True)
    tokens = slabt_ref.shape[1]
    eio = lax.broadcasted_iota(I32, (N_EXPERTS, tokens), 0)
    row = lax.broadcasted_iota(I32, (SUBLANES, tokens), 0)
    out = jnp.zeros((SUBLANES, tokens), I32)
    for k in range(TOP_K):
        idx = slabt_ref[k:k + 1, :].astype(I32)
        rank = slabt_ref[2 * TOP_K + k:2 * TOP_K + k + 1, :].astype(I32)
        base = jnp.sum(jnp.where(eio == idx, pstart_col, 0), axis=0, keepdims=True)
        for h in range(2):
            out = jnp.where(row == 2 * k + h, base + rank + h * n_steps * STEP_ROWS, out)
    dest_ref[...] = out


def _route_call(slabt, counts, n_tok, n_steps):
    meta_rows = ((n_steps + 1 + SUBLANES - 1) // SUBLANES) * SUBLANES
    return pl.pallas_call(
        functools.partial(_route_kernel, n_steps=n_steps),
        grid=(n_tok // ROUTE_T,),
        in_specs=[pl.BlockSpec((2 * SUBLANES, ROUTE_T), lambda i: (0, i)),
                  pl.BlockSpec((N_EXPERTS, LANES), lambda i: (0, 0))],
        out_specs=[pl.BlockSpec((SUBLANES, ROUTE_T), lambda i: (0, i)),
                   pl.BlockSpec((meta_rows, LANES), lambda i: (0, 0))],
        out_shape=[jax.ShapeDtypeStruct((SUBLANES, n_tok), I32),
                   jax.ShapeDtypeStruct((meta_rows, LANES), I32)],
        compiler_params=pltpu.CompilerParams(dimension_semantics=("arbitrary",)),
        name="route",
    )(slabt, counts)


def _sc_mesh():
    return plsc.VectorSubcoreMesh(core_axis_name="core", subcore_axis_name="subcore")


def _dispatch(src, idx, n_rows):
    windows = src.shape[0] // SC_WINDOW

    @functools.partial(pl.kernel, out_type=jax.ShapeDtypeStruct((n_rows, SC_ROW), src.dtype),
                       mesh=_sc_mesh(), scratch_types=[])
    def scatter_rows(x_hbm, i_hbm, o_hbm):
        def body(x_vmem, i_vmem):
            pltpu.sync_copy(x_vmem, o_hbm.at[i_vmem.at[0]])

        pltpu.emit_pipeline(
            body,
            grid=(idx.shape[1] // SC_WINDOW,),
            in_specs=[pl.BlockSpec((SC_WINDOW, SC_ROW), lambda i: (lax.rem(i, windows), 0)),
                      pl.BlockSpec((1, SC_WINDOW), lambda i: (0, i))],
            out_specs=[],
            core_axis_name=("core", "subcore"),
            dimension_semantics=(pltpu.PARALLEL,),
        )(x_hbm, i_hbm)

    return scatter_rows(src, idx)


def _combine(src, idx):
    n_idx = idx.shape[1]

    @functools.partial(pl.kernel, out_type=jax.ShapeDtypeStruct((n_idx, SC_ROW), src.dtype),
                       mesh=_sc_mesh(), scratch_types=[])
    def gather_rows(y_hbm, i_hbm, o_hbm):
        def body(i_vmem, o_vmem):
            pltpu.sync_copy(y_hbm.at[i_vmem.at[0]], o_vmem)

        pltpu.emit_pipeline(
            body,
            grid=(n_idx // SC_WINDOW,),
            in_specs=[pl.BlockSpec((1, SC_WINDOW), lambda i: (0, i))],
            out_specs=[pl.BlockSpec((SC_WINDOW, SC_ROW), lambda i: (i, 0))],
            core_axis_name=("core", "subcore"),
            dimension_semantics=(pltpu.PARALLEL,),
        )(i_hbm, o_hbm)

    return gather_rows(src, idx)


def _expert_rows(xs_ref, ys_ref, blk, wgu_s, bgu_ref, wdn_s, bdn_ref, act_s):
    rows = pl.ds(pl.multiple_of(blk * EXPERT_BLOCK, EXPERT_BLOCK), EXPERT_BLOCK)
    w0 = xs_ref[0, rows, :]
    w1 = xs_ref[1, rows, :]
    xb = jnp.concatenate([_unpack_lo(w0).astype(BF16), _unpack_lo(w1).astype(BF16),
                          _unpack_hi(w0).astype(BF16), _unpack_hi(w1).astype(BF16)], axis=1)
    half = D_EXPERT // 2
    for j in range(2):
        cols = slice(j * half, (j + 1) * half)
        ucols = slice(D_EXPERT + j * half, D_EXPERT + (j + 1) * half)
        g = jnp.dot(xb, wgu_s[:, cols], preferred_element_type=F32) + bgu_ref[:, cols]
        up = jnp.dot(xb, wgu_s[:, ucols], preferred_element_type=F32) + bgu_ref[:, ucols]
        g = jnp.minimum(g, SWIGLU_LIMIT)
        up = jnp.clip(up, -SWIGLU_LIMIT, SWIGLU_LIMIT)
        act_s[:, cols] = (g * jax.nn.sigmoid(SWIGLU_ALPHA * g) * (up + 1.0)).astype(BF16)
    y = jnp.dot(act_s[...], wdn_s[...], preferred_element_type=F32) + bdn_ref[...]
    packed = _pack_rows(y)
    ys_ref[0, rows, :] = packed[:, :SC_ROW]
    ys_ref[1, rows, :] = packed[:, SC_ROW:]


def _expert_kernel(be_ref, nb_ref, nv_ref, xs_ref, wgu_ref, bgu_ref, wdn_ref, bdn_ref, ys_ref, wgu_s, wdn_s, act_s):
    blk = pl.program_id(0)
    prev = be_ref[jnp.maximum(blk - 1, 0)]
    changed = jnp.logical_or(blk == 0, be_ref[blk] != prev)
    valid = blk < nv_ref[0]

    @pl.when(jnp.logical_and(valid, changed))
    def _():
        chunk = 128

        def cast_gu(i, _):
            rows = pl.ds(pl.multiple_of(i * chunk, chunk), chunk)
            wgu_s[rows, :] = wgu_ref[rows, :].astype(BF16)
            return 0

        def cast_dn(i, _):
            rows = pl.ds(pl.multiple_of(i * chunk, chunk), chunk)
            wdn_s[rows, :] = wdn_ref[rows, :].astype(BF16)
            return 0

        lax.fori_loop(0, D_MODEL // chunk, cast_gu, 0)
        lax.fori_loop(0, D_EXPERT // chunk, cast_dn, 0)

    @pl.when(valid)
    def _():
        def one_block(i, _):
            _expert_rows(xs_ref, ys_ref, i, wgu_s, bgu_ref, wdn_s, bdn_ref, act_s)
            return 0

        lax.fori_loop(0, nb_ref[blk], one_block, 0)


def _expert_call(xs, step_e, step_nblk, n_valid, w_gu, b_gu, w_dn, b_dn, layer, n_steps):
    def row_map(i, be, nb, nv):
        return (0, jnp.minimum(i, jnp.maximum(nv[0] - 1, 0)), 0)

    def w_map(i, be, nb, nv):
        return (layer, be[i], 0, 0)

    grid_spec = pltpu.PrefetchScalarGridSpec(
        num_scalar_prefetch=3,
        grid=(n_steps,),
        in_specs=[pl.BlockSpec((2, STEP_ROWS, SC_ROW), row_map),
                  pl.BlockSpec((None, None, D_MODEL, 2 * D_EXPERT), w_map),
                  pl.BlockSpec((None, None, 1, 2 * D_EXPERT), w_map),
                  pl.BlockSpec((None, None, D_EXPERT, D_MODEL), w_map),
                  pl.BlockSpec((None, None, 1, D_MODEL), w_map)],
        out_specs=pl.BlockSpec((2, STEP_ROWS, SC_ROW), row_map),
        scratch_shapes=[pltpu.VMEM((D_MODEL, 2 * D_EXPERT), BF16),
                        pltpu.VMEM((D_EXPERT, D_MODEL), BF16),
                        pltpu.VMEM((EXPERT_BLOCK, D_EXPERT), BF16)])
    return pl.pallas_call(
        _expert_kernel,
        grid_spec=grid_spec,
        out_shape=jax.ShapeDtypeStruct((2, n_steps * STEP_ROWS, SC_ROW), I32),
        compiler_params=pltpu.CompilerParams(dimension_semantics=("arbitrary",),
                                             vmem_limit_bytes=VMEM_LIMIT),
        name="experts",
    )(step_e, step_nblk, n_valid, xs, w_gu, b_gu, w_dn, b_dn)


def _final_kernel(x_ref, y_ref, slab_ref, g_ref, *rest):
    o_ref = rest[-1]
    x = _moe_combine(x_ref[...], y_ref, slab_ref[...])
    o_ref[...] = _rms(x, g_ref[...])


def _final_call(x1, ysg, slab, g, out_row0, n_out, out_prev):
    t = FINAL_T
    out_tile0 = out_row0 // t
    in_specs = [pl.BlockSpec((t, D_MODEL), lambda i: (i, 0)),
                pl.BlockSpec((TOP_K, 2, t, SC_ROW), lambda i: (0, 0, i, 0)),
                pl.BlockSpec((t, LANES), lambda i: (i, 0)),
                pl.BlockSpec((1, D_MODEL), lambda i: (0, 0))]
    args = [x1, ysg, slab, g]
    aliases = {}
    if out_prev is not None:
        in_specs.append(pl.BlockSpec(memory_space=pl.ANY))
        args.append(out_prev)
        aliases = {4: 0}
    return pl.pallas_call(
        _final_kernel,
        grid=(x1.shape[0] // t,),
        in_specs=in_specs,
        out_specs=pl.BlockSpec((t, D_MODEL), lambda i: (out_tile0 + i, 0)),
        out_shape=jax.ShapeDtypeStruct((n_out, D_MODEL), F32),
        input_output_aliases=aliases,
        compiler_params=pltpu.CompilerParams(dimension_semantics=("arbitrary",)),
        name="final_norm",
    )(*args)


def _s5_tables(lam_re, lam_im, b_re, b_im, c_re, c_im, log_step):
    lr = jnp.minimum(lam_re.astype(F32), LAMBDA_RE_MAX)
    li = lam_im.astype(F32)
    step = jnp.exp(log_step.astype(F32))[:, None]
    mag = jnp.exp(lr * step)
    ar = mag * jnp.cos(li * step)
    ai = mag * jnp.sin(li * step)
    nr = ar - 1.0
    den = lr * lr + li * li
    kr = (nr * lr + ai * li) / den
    ki = (ai * lr - nr * li) / den
    bre = b_re.astype(F32)
    bim = b_im.astype(F32)
    bbar_r = kr[..., None] * bre - ki[..., None] * bim
    bbar_i = kr[..., None] * bim + ki[..., None] * bre
    cre = c_re.astype(F32)
    cim = c_im.astype(F32)

    def power(m):
        m = jnp.asarray(m, F32)[..., None, None]
        mg = jnp.exp(m * lr * step)
        return mg * jnp.cos(m * li * step), mg * jnp.sin(m * li * step)

    pw_r, pw_i = power(jnp.arange(CHUNK + 1))
    cp_r = cre[None] * pw_r[:, :, None, :] - cim[None] * pw_i[:, :, None, :]
    cp_i = cre[None] * pw_i[:, :, None, :] + cim[None] * pw_r[:, :, None, :]
    width = CHUNK * SSM_GROUP
    kcat = (jnp.einsum('tgap,gph->ghta', cp_r[:CHUNK], bbar_r)
            - jnp.einsum('tgap,gph->ghta', cp_i[:CHUNK], bbar_i)).reshape(SSM_GROUPS, SSM_GROUP, width).astype(BF16)
    m_intra = jnp.stack([jnp.pad(kcat[:, :, :width - s * SSM_GROUP], ((0, 0), (0, 0), (s * SSM_GROUP, 0)))
                         for s in range(CHUNK)], axis=1).reshape(SSM_GROUPS, width, width)
    mo_r = cp_r[1:].transpose(1, 3, 0, 2).reshape(SSM_GROUPS, SSM_STATE, width)
    mo_i = -cp_i[1:].transpose(1, 3, 0, 2).reshape(SSM_GROUPS, SSM_STATE, width)
    mcat = jnp.concatenate([m_intra, mo_r.astype(BF16), mo_i.astype(BF16)], axis=1)
    q_r = pw_r[CHUNK - 1::-1][:CHUNK][:, :, :, None]
    q_i = pw_i[CHUNK - 1::-1][:CHUNK][:, :, :, None]
    mn_r = q_r * bbar_r[None] - q_i * bbar_i[None]
    mn_i = q_r * bbar_i[None] + q_i * bbar_r[None]
    minw = jnp.concatenate([mn_r, mn_i], axis=2).transpose(1, 0, 3, 2).reshape(
        SSM_GROUPS, CHUNK * SSM_GROUP, 2 * SSM_STATE).astype(BF16)

    def packed(j):
        pr, pi = power(float(CHUNK * j))
        return (jnp.concatenate([pr, pr], axis=-1).reshape(1, SSM_GROUPS * LANES),
                jnp.concatenate([-pi, pi], axis=-1).reshape(1, SSM_GROUPS * LANES))

    rows = jnp.arange(SUBLANES)[:, None]
    kinds = []
    for d in (1, 2, 4):
        c1, c2 = packed(d)
        mask = (rows >= d).astype(F32)
        kinds += [mask * c1, mask * c2]
    pk = [packed(r + 1) for r in range(SUBLANES)]
    kinds += [jnp.concatenate([p[0] for p in pk], axis=0), jnp.concatenate([p[1] for p in pk], axis=0)]
    coef = jnp.stack(kinds, axis=0)
    return minw, mcat, coef


def _mixer_params(norm_mix, w_in, ssm_lam_re, ssm_lam_im, ssm_b_re, ssm_b_im, ssm_c_re, ssm_c_im, ssm_d,
                  ssm_log_step, ssm_w_glu, ssm_b_glu, pool_w, pool_scale, conv_w, branch_norm, w_out, norm_ffn,
                  router_w, router_b):
    depth = w_in.shape[0]
    minw, mcat, coef = jax.vmap(_s5_tables)(ssm_lam_re, ssm_lam_im, ssm_b_re, ssm_b_im, ssm_c_re, ssm_c_im,
                                            ssm_log_step)
    eye4 = jnp.eye(len(POOL_WINDOWS), dtype=F32)
    poolw = jnp.einsum('lgcd,gk->lgckd', pool_w.astype(F32), eye4).reshape(depth, POOL_WIDTH, POOL_WIDTH).astype(BF16)
    rw = jnp.pad(router_w.astype(F32), ((0, 0), (0, 0), (0, LANES - N_EXPERTS))).astype(BF16)
    rb = jnp.pad(router_b.astype(F32), ((0, 0), (0, LANES - N_EXPERTS))).reshape(depth, 1, LANES)
    row = lambda v: v.reshape(depth, 1, -1).astype(F32)
    utri = jnp.triu(jnp.ones((TILE_T, TILE_T), F32), 1).astype(BF16)
    return dict(gmix=row(norm_mix), win=w_in.astype(BF16), minw=minw, mcat=mcat, coef=coef, dskip=row(ssm_d),
                wglu=ssm_w_glu.astype(BF16), bglu=row(ssm_b_glu), poolw=poolw, pscale=row(pool_scale),
                convw=conv_w.astype(F32), gbr=row(branch_norm), wout=w_out.astype(BF16),
                gffn=row(norm_ffn), rw=rw, rb=rb, utri=utri[None])


def kernel(x, norm_mix, w_in, ssm_lam_re, ssm_lam_im, ssm_b_re, ssm_b_im, ssm_c_re, ssm_c_im, ssm_d, ssm_log_step, ssm_w_glu, ssm_b_glu, pool_w, pool_scale, conv_w, branch_norm, w_out, norm_ffn, router_w, router_b, w_gate_up, b_gate_up, w_down, b_down, final_norm):
    batch, seq, d_model = x.shape
    assert d_model == D_MODEL and seq % TILE_T == 0
    depth = w_in.shape[0]
    n_streams = 2 if batch % 2 == 0 else 1
    sb = batch // n_streams
    n_tok = sb * seq
    assert n_tok % ROUTE_T == 0 and n_tok % SC_WINDOW == 0
    n_steps = -(-(n_tok * TOP_K + N_EXPERTS * (STEP_ROWS - 1)) // STEP_ROWS)
    n_rows = n_steps * STEP_ROWS
    tiles = n_tok // TILE_T
    b_gu = b_gate_up.reshape(depth, N_EXPERTS, 1, 2 * D_EXPERT)
    b_dn = b_down.reshape(depth, N_EXPERTS, 1, D_MODEL)

    xin = [x.reshape(batch * seq, D_MODEL)] * n_streams
    tile0 = [s * tiles for s in range(n_streams)]
    ysg = [None] * n_streams
    slab = [None] * n_streams
    lp = _mixer_params(norm_mix, w_in, ssm_lam_re, ssm_lam_im, ssm_b_re, ssm_b_im, ssm_c_re, ssm_c_im, ssm_d,
                       ssm_log_step, ssm_w_glu, ssm_b_glu, pool_w, pool_scale, conv_w, branch_norm, w_out,
                       norm_ffn, router_w, router_b)
    for l in range(depth):
        routed = []
        for s in range(n_streams):
            xin[s], hp, slab[s], slabt, counts = _mixer_call(xin[s], tile0[s], ysg[s], slab[s], lp, l, sb, seq)
            tile0[s] = 0
            dest, meta = _route_call(slabt, counts, n_tok, n_steps)
            idx = dest.reshape(1, 2 * TOP_K * n_tok)
            xs = _dispatch(hp.reshape(2 * n_tok, SC_ROW), idx, 2 * n_rows).reshape(2, n_rows, SC_ROW)
            routed.append((xs, idx, meta[:n_steps, 0], meta[:n_steps, 1], meta[n_steps:n_steps + 1, 0]))
        for s, (xs, idx, step_e, step_nblk, n_valid) in enumerate(routed):
            ys = _expert_call(xs, step_e, step_nblk, n_valid, w_gate_up, b_gu, w_down, b_dn, l, n_steps)
            ysg[s] = _combine(ys.reshape(2 * n_rows, SC_ROW), idx).reshape(TOP_K, 2, n_tok, SC_ROW)
    g = final_norm.reshape(1, D_MODEL).astype(F32)
    out = None
    for s in range(n_streams):
        out = _final_call(xin[s], ysg[s], slab[s], g, s * n_tok, batch * seq, out)
    return out.reshape(batch, seq, D_MODEL)
```

```python
import functools
import math

import jax
import jax.numpy as jnp
from jax import lax
from jax.experimental import pallas as pl
from jax.experimental.pallas import tpu as pltpu
from jax.experimental.pallas import tpu_sc as plsc

F32 = jnp.float32
BF16 = jnp.bfloat16
I32 = jnp.int32

D_MODEL = 1024
SSM_WIDTH = 512
POOL_WIDTH = 256
CONV_WIDTH = 256
SSM_GROUP = 16
SSM_GROUPS = 32
SSM_STATE = 64
LAMBDA_RE_MAX = -1e-4
POOL_WINDOWS = (2, 4, 8, 16)
POOL_GROUP = 64
CONV_K = 3
IN_WIDTH = SSM_WIDTH + POOL_WIDTH + 3 * CONV_WIDTH
N_EXPERTS = 32
TOP_K = 4
D_EXPERT = 1024
SWIGLU_LIMIT = 7.0
SWIGLU_ALPHA = 1.702
EXPERT_BLOCK = 256
STEP_ROWS = 1024
NORM_EPS = 1e-5

LANES = 128
SUBLANES = 8
TILE_T = 512
FINAL_T = 512
CHUNK = 16
N_CHUNK = TILE_T // CHUNK
POOL_TAIL = SUBLANES * len(POOL_WINDOWS)
assert POOL_WINDOWS == tuple(2 ** (k + 1) for k in range(len(POOL_WINDOWS))) and POOL_TAIL >= POOL_WINDOWS[-1]
PACKED = D_MODEL // 2
ROUTE_T = 2048
SC_WINDOW = 128
SC_ROW = PACKED // 2
HI_MASK = -65536
VMEM_LIMIT = 56 * 1024 * 1024


def _bf16_round(v):
    return v.astype(BF16).astype(F32)


def _pack_rows(v):
    lo = lax.shift_right_logical(lax.bitcast_convert_type(_bf16_round(v[:, :PACKED]), I32), 16)
    hi = lax.bitcast_convert_type(_bf16_round(v[:, PACKED:]), I32) & HI_MASK
    return hi | lo


def _unpack_lo(w):
    return lax.bitcast_convert_type(lax.shift_left(w, 16), F32)


def _unpack_hi(w):
    return lax.bitcast_convert_type(w & HI_MASK, F32)


def _rms(v, g):
    r = lax.rsqrt(jnp.mean(v * v, axis=-1, keepdims=True) + NORM_EPS)
    return (v * r) * g


def _moe_combine(x, y_ref, slab):
    parts = [x[:, i * SC_ROW:(i + 1) * SC_ROW] for i in range(4)]
    for k in range(TOP_K):
        g = slab[:, TOP_K + k:TOP_K + k + 1]
        for h in range(2):
            w = y_ref[k, h]
            parts[h] = parts[h] + g * _unpack_lo(w)
            parts[2 + h] = parts[2 + h] + g * _unpack_hi(w)
    return jnp.concatenate(parts, axis=1)


def _block_transpose8(vs, lane):
    vs = list(vs)
    for dist in (4, 2, 1):
        width = SSM_GROUP * dist
        low = (lane % (2 * width)) < width
        for j in range(8):
            if j & dist:
                continue
            a, b = vs[j], vs[j + dist]
            vs[j] = jnp.where(low, a, pltpu.roll(b, width, axis=1))
            vs[j + dist] = jnp.where(low, pltpu.roll(a, LANES - width, axis=1), b)
    return vs


def _cmul_packed(c1, c2, v):
    return c1 * v + c2 * pltpu.roll(v, SSM_STATE, axis=1)


def _chunk_scan(s, carry_in, cf):
    outs = []
    carry = jnp.broadcast_to(carry_in, (SUBLANES, LANES))
    for rg in range(N_CHUNK // SUBLANES):
        x = s[rg * SUBLANES:(rg + 1) * SUBLANES]
        for j, d in enumerate((1, 2, 4)):
            x = x + _cmul_packed(cf[2 * j], cf[2 * j + 1], pltpu.roll(x, d, axis=0))
        x = x + _cmul_packed(cf[6], cf[7], carry)
        outs.append(x)
        carry = jnp.broadcast_to(x[SUBLANES - 1:SUBLANES], (SUBLANES, LANES))
    return jnp.concatenate(outs, axis=0)


def _s5_mixer(u, first, minw_ref, mcat_ref, coef_ref, u_s, y_s, h_s):
    blocks = SSM_WIDTH // LANES
    for b4 in range(blocks):
        u_s[b4] = u[:, b4 * LANES:(b4 + 1) * LANES]
    lane = lax.broadcasted_iota(I32, (N_CHUNK, LANES), 1)

    halves = [[None, None] for _ in range(SSM_GROUPS)]
    for b4 in range(blocks):
        for hh in range(2):
            xs = [u_s[b4, pl.ds(8 * hh + j, N_CHUNK, stride=CHUNK), :].astype(BF16) for j in range(8)]
            ws = _block_transpose8(xs, lane)
            for gl in range(8):
                halves[8 * b4 + gl][hh] = ws[gl]
    ug = [jnp.concatenate(h2, axis=1) for h2 in halves]

    carry_in = jnp.where(first, 0.0, h_s[N_CHUNK + 7:N_CHUNK + 8, :])
    h_s[7:8, :] = carry_in
    for g in range(SSM_GROUPS):
        cols = slice(g * LANES, (g + 1) * LANES)
        s_g = jnp.dot(ug[g], minw_ref[g], preferred_element_type=F32)
        cf = [coef_ref[k, :, cols] for k in range(8)]
        h_s[8:8 + N_CHUNK, cols] = _chunk_scan(s_g, carry_in[:, cols], cf)

    yg = []
    for g in range(SSM_GROUPS):
        hprev = h_s[7:7 + N_CHUNK, g * LANES:(g + 1) * LANES].astype(BF16)
        lhs = jnp.concatenate([ug[g], hprev], axis=1)
        yg.append(jnp.dot(lhs, mcat_ref[g], preferred_element_type=F32).astype(BF16))

    for b4 in range(blocks):
        for hh in range(2):
            ws = [yg[8 * b4 + gl][:, hh * LANES:(hh + 1) * LANES] for gl in range(8)]
            xs = _block_transpose8(ws, lane)
            for j in range(8):
                y_s[b4, pl.ds(8 * hh + j, N_CHUNK, stride=CHUNK), :] = xs[j].astype(F32)
    return jnp.concatenate([y_s[b4] for b4 in range(blocks)], axis=1)


def _mixer_kernel(*refs, first_layer):
    if first_layer:
        x_ref = refs[0]
        refs = refs[1:]
    else:
        x_ref, yprev_ref, slabprev_ref = refs[:3]
        refs = refs[3:]
    (gmix_ref, win_ref, minw_ref, mcat_ref, coef_ref, dskip_ref, wglu_ref, bglu_ref, poolw_ref, pscale_ref,
     convw_ref, gbr_ref, wout_ref, gffn_ref, rw_ref, rb_ref, utri_ref,
     x1_ref, hp_ref, slab_ref, slabt_ref, cnt_ref,
     u_s, y_s, h_s, pbuf_ref, zbuf_ref, cntacc_ref) = refs

    b = pl.program_id(0)
    l = pl.program_id(1)
    seq_start = l == 0

    @pl.when(jnp.logical_and(b == 0, l == 0))
    def _():
        cntacc_ref[...] = jnp.zeros_like(cntacc_ref)

    if first_layer:
        x = x_ref[...]
    else:
        x = _moe_combine(x_ref[...], yprev_ref, slabprev_ref[...])

    hn = _rms(x, gmix_ref[...]).astype(BF16)
    proj = jnp.dot(hn, win_ref[...], preferred_element_type=F32)
    u = proj[:, :SSM_WIDTH]
    p = proj[:, SSM_WIDTH:SSM_WIDTH + POOL_WIDTH]
    c0 = SSM_WIDTH + POOL_WIDTH
    bg = proj[:, c0:c0 + CONV_WIDTH]
    cg = proj[:, c0 + CONV_WIDTH:c0 + 2 * CONV_WIDTH]
    hv = proj[:, c0 + 2 * CONV_WIDTH:]

    y = _s5_mixer(u, seq_start, minw_ref, mcat_ref, coef_ref, u_s, y_s, h_s) + dskip_ref[...] * u
    z = jax.nn.gelu(y)
    glu = jnp.dot(z.astype(BF16), wglu_ref[...], preferred_element_type=F32) + bglu_ref[...]
    y_ssm = z * jax.nn.sigmoid(glu)

    tail = POOL_TAIL
    pbuf_ref[0, 0:tail, :] = jnp.where(seq_start, 0.0, pbuf_ref[0, TILE_T:TILE_T + tail, :])
    pbuf_ref[0, tail:, :] = p
    sums = []
    for stage, w in enumerate(POOL_WINDOWS):
        src = 0 if stage == 0 else 1 + (stage - 1) % 2
        dst = 1 + stage % 2
        d = w // 2
        lo = SUBLANES * (stage + 1)
        n = tail + TILE_T - lo
        e = pbuf_ref[src, pl.ds(lo, n), :] + pbuf_ref[src, pl.ds(lo - d, n), :]
        if stage + 1 < len(POOL_WINDOWS):
            pbuf_ref[dst, pl.ds(lo, n), :] = e
        sums.append(e[tail - lo:, :])
    lane_p = lax.broadcasted_iota(I32, (TILE_T, POOL_WIDTH), 1)
    total = sums[-1]
    win = jnp.full((TILE_T, POOL_WIDTH), POOL_WINDOWS[-1], I32)
    for i in range(len(POOL_WINDOWS) - 2, -1, -1):
        sel = lane_p < (i + 1) * POOL_GROUP
        total = jnp.where(sel, sums[i], total)
        win = jnp.where(sel, POOL_WINDOWS[i], win)
    row_p = lax.broadcasted_iota(I32, (TILE_T, POOL_WIDTH), 0) + l * TILE_T
    count = jnp.minimum(row_p + 1, win).astype(F32)
    pooled = total / count - p
    mixed = jnp.dot(pooled.astype(BF16), poolw_ref[...], preferred_element_type=F32)
    y_pool = mixed * pscale_ref[...]

    zc = cg * hv
    zbuf_ref[0:8, :] = jnp.where(seq_start, 0.0, zbuf_ref[TILE_T:TILE_T + 8, :])
    zbuf_ref[8:, :] = zc
    yc = (convw_ref[0:1, :] * zbuf_ref[pl.ds(6, TILE_T), :]
          + convw_ref[1:2, :] * zbuf_ref[pl.ds(7, TILE_T), :]
          + convw_ref[2:3, :] * zc)
    y_conv = bg * yc

    gbr = gbr_ref[...]
    s1 = SSM_WIDTH + POOL_WIDTH
    mixed_all = jnp.concatenate([_rms(y_ssm, gbr[:, :SSM_WIDTH]).astype(BF16),
                                 _rms(y_pool, gbr[:, SSM_WIDTH:s1]).astype(BF16),
                                 _rms(y_conv, gbr[:, s1:]).astype(BF16)], axis=1)
    x1 = x + jnp.dot(mixed_all, wout_ref[...], preferred_element_type=F32)
    x1_ref[...] = x1

    hn2 = _rms(x1, gffn_ref[...])
    packed = _pack_rows(hn2)
    hp_ref[0] = packed[:, :SC_ROW]
    hp_ref[1] = packed[:, SC_ROW:]
    logits = jnp.dot(hn2.astype(BF16), rw_ref[...], preferred_element_type=F32) + rb_ref[...]
    work = logits.T[:N_EXPERTS, :]
    eio = lax.broadcasted_iota(I32, (N_EXPERTS, TILE_T), 0).astype(F32)
    neg = jnp.float32(-jnp.inf)
    vals, idxs = [], []
    for _ in range(TOP_K):
        m = jnp.max(work, axis=0, keepdims=True)
        idx = jnp.min(jnp.where(work == m, eio, float(N_EXPERTS)), axis=0, keepdims=True)
        vals.append(m)
        idxs.append(idx)
        work = jnp.where(eio == idx, neg, work)
    exps = [jnp.exp(v - vals[0]) for v in vals]
    denom = exps[0] + exps[1] + exps[2] + exps[3]
    gates = [e / denom for e in exps]

    onehot = jnp.zeros((N_EXPERTS, TILE_T), F32)
    for idx in idxs:
        onehot = jnp.where(eio == idx, 1.0, onehot)
    before = jnp.dot(onehot.astype(BF16), utri_ref[...], preferred_element_type=F32) + cntacc_ref[:, 0:1]
    ranks = [jnp.sum(jnp.where(eio == idx, before, 0.0), axis=0, keepdims=True) for idx in idxs]
    row = lax.broadcasted_iota(I32, (LANES, TILE_T), 0)
    slab_t = jnp.zeros((LANES, TILE_T), F32)
    for k in range(TOP_K):
        slab_t = jnp.where(row == k, idxs[k], slab_t)
        slab_t = jnp.where(row == TOP_K + k, gates[k], slab_t)
        slab_t = jnp.where(row == 2 * TOP_K + k, ranks[k], slab_t)
    slabt_ref[...] = slab_t[:2 * SUBLANES, :]
    slab_ref[...] = slab_t.T
    newcnt = cntacc_ref[:, 0:1] + jnp.sum(onehot, axis=1, keepdims=True)
    cntacc_ref[...] = jnp.broadcast_to(newcnt, cntacc_ref.shape)
    cnt_ref[...] = jnp.broadcast_to(newcnt, cnt_ref.shape)


def _layer_spec(shape, layer):
    nd = len(shape) - 1
    sel = layer if shape[0] > 1 else 0
    return pl.BlockSpec((None,) + tuple(shape[1:]), lambda b, l, nd=nd, sel=sel: (sel,) + (0,) * nd,
                        pipeline_mode=pl.Buffered(1))


def _mixer_call(x, x_tile0, yprev, slabprev, lp, layer, batch, seq):
    first_layer = yprev is None
    n_tok = batch * seq
    steps = seq // TILE_T
    tok_spec = lambda w: pl.BlockSpec((TILE_T, w), lambda b, l: (b * steps + l, 0))
    in_specs = [pl.BlockSpec((TILE_T, D_MODEL), lambda b, l: (x_tile0 + b * steps + l, 0))]
    args = [x]
    if not first_layer:
        in_specs += [pl.BlockSpec((TOP_K, 2, TILE_T, SC_ROW), lambda b, l: (0, 0, b * steps + l, 0)), tok_spec(LANES)]
        args += [yprev, slabprev]
    weights = [lp['gmix'], lp['win'], lp['minw'], lp['mcat'], lp['coef'], lp['dskip'], lp['wglu'], lp['bglu'],
               lp['poolw'], lp['pscale'], lp['convw'], lp['gbr'], lp['wout'], lp['gffn'], lp['rw'], lp['rb'],
               lp['utri']]
    in_specs += [_layer_spec(w.shape, layer) for w in weights]
    args += weights
    out_shape = [jax.ShapeDtypeStruct((n_tok, D_MODEL), F32),
                 jax.ShapeDtypeStruct((2, n_tok, SC_ROW), I32),
                 jax.ShapeDtypeStruct((n_tok, LANES), F32),
                 jax.ShapeDtypeStruct((2 * SUBLANES, n_tok), F32),
                 jax.ShapeDtypeStruct((N_EXPERTS, LANES), F32)]
    out_specs = [tok_spec(D_MODEL), pl.BlockSpec((2, TILE_T, SC_ROW), lambda b, l: (0, b * steps + l, 0)), tok_spec(LANES),
                 pl.BlockSpec((2 * SUBLANES, TILE_T), lambda b, l: (0, b * steps + l)),
                 pl.BlockSpec((N_EXPERTS, LANES), lambda b, l: (0, 0))]
    scratch = [pltpu.VMEM((SSM_WIDTH // LANES, TILE_T, LANES), F32),
               pltpu.VMEM((SSM_WIDTH // LANES, TILE_T, LANES), F32),
               pltpu.VMEM((SUBLANES + N_CHUNK, SSM_GROUPS * LANES), F32),
               pltpu.VMEM((3, POOL_TAIL + TILE_T, POOL_WIDTH), F32),
               pltpu.VMEM((8 + TILE_T, CONV_WIDTH), F32),
               pltpu.VMEM((N_EXPERTS, LANES), F32)]
    return pl.pallas_call(
        functools.partial(_mixer_kernel, first_layer=first_layer),
        grid=(batch, steps),
        in_specs=in_specs, out_specs=out_specs, out_shape=out_shape, scratch_shapes=scratch,
        compiler_params=pltpu.CompilerParams(dimension_semantics=("arbitrary", "arbitrary"),
                                             vmem_limit_bytes=VMEM_LIMIT),
        name="mixer_first" if first_layer else "mixer_next",
    )(*args)


def _route_kernel(slabt_ref, cnt_ref, dest_ref, meta_ref, *, n_steps):
    lane1 = lax.broadcasted_iota(I32, (1, LANES), 1)
    e_row = lax.broadcasted_iota(I32, (N_EXPERTS, LANES), 0)
    e_lane = lax.broadcasted_iota(I32, (N_EXPERTS, LANES), 1)
    cnt = cnt_ref[...].astype(I32)
    counts = jnp.sum(jnp.where(e_row == e_lane, cnt, 0), axis=0, keepdims=True)
    padded = ((counts + (STEP_ROWS - 1)) // STEP_ROWS) * STEP_ROWS
    pend = padded
    for sh in (1, 2, 4, 8, 16):
        pend = pend + jnp.where(lane1 >= sh, pltpu.roll(pend, sh, axis=1), 0)
    pstart = pend - padded

    @pl.when(pl.program_id(0) == 0)
    def _():
        rows = meta_ref.shape[0]
        bstart = lax.broadcasted_iota(I32, (rows, LANES), 0) * STEP_ROWS
        lane = lax.broadcasted_iota(I32, (rows, LANES), 1)
        done = jnp.where(jnp.logical_and(lane < N_EXPERTS, pend <= bstart), 1, 0)
        be = jnp.minimum(jnp.sum(done, axis=1, keepdims=True), N_EXPERTS - 1)
        mine = lane == be
        left = jnp.sum(jnp.where(mine, counts + pstart, 0), axis=1, keepdims=True) - bstart[:, 0:1]
        nblk = jnp.clip((left + (EXPERT_BLOCK - 1)) // EXPERT_BLOCK, 0, STEP_ROWS // EXPERT_BLOCK)
        total = jnp.sum(jnp.where(lane1 == N_EXPERTS - 1, pend, 0), axis=1, keepdims=True)
        row = lax.broadcasted_iota(I32, (rows, LANES), 0)
        table = jnp.where(lane == 0, be, jnp.where(lane == 1, nblk, 0))
        meta_ref[...] = jnp.where(row == n_steps, total // STEP_ROWS, table)

    pstart_col = jnp.sum(jnp.where(e_row == e_lane, pstart, 0), axis=1, keepdims=True)
    tokens = slabt_ref.shape[1]
    eio = lax.broadcasted_iota(I32, (N_EXPERTS, tokens), 0)
    row = lax.broadcasted_iota(I32, (SUBLANES, tokens), 0)
    out = jnp.zeros((SUBLANES, tokens), I32)
    for k in range(TOP_K):
        idx = slabt_ref[k:k + 1, :].astype(I32)
        rank = slabt_ref[2 * TOP_K + k:2 * TOP_K + k + 1, :].astype(I32)
        base = jnp.sum(jnp.where(eio == idx, pstart_col, 0), axis=0, keepdims=True)
        for h in range(2):
            out = jnp.where(row == 2 * k + h, base + rank + h * n_steps * STEP_ROWS, out)
    dest_ref[...] = out


def _route_call(slabt, counts, n_tok, n_steps):
    meta_rows = ((n_steps + 1 + SUBLANES - 1) // SUBLANES) * SUBLANES
    return pl.pallas_call(
        functools.partial(_route_kernel, n_steps=n_steps),
        grid=(n_tok // ROUTE_T,),
        in_specs=[pl.BlockSpec((2 * SUBLANES, ROUTE_T), lambda i: (0, i)),
                  pl.BlockSpec((N_EXPERTS, LANES), lambda i: (0, 0))],
        out_specs=[pl.BlockSpec((SUBLANES, ROUTE_T), lambda i: (0, i)),
                   pl.BlockSpec((meta_rows, LANES), lambda i: (0, 0))],
        out_shape=[jax.ShapeDtypeStruct((SUBLANES, n_tok), I32),
                   jax.ShapeDtypeStruct((meta_rows, LANES), I32)],
        compiler_params=pltpu.CompilerParams(dimension_semantics=("arbitrary",)),
        name="route",
    )(slabt, counts)


def _sc_mesh():
    return plsc.VectorSubcoreMesh(core_axis_name="core", subcore_axis_name="subcore")


def _dispatch(src, idx, n_rows):
    windows = src.shape[0] // SC_WINDOW

    @functools.partial(pl.kernel, out_type=jax.ShapeDtypeStruct((n_rows, SC_ROW), src.dtype),
                       mesh=_sc_mesh(), scratch_types=[])
    def scatter_rows(x_hbm, i_hbm, o_hbm):
        def body(x_vmem, i_vmem):
            pltpu.sync_copy(x_vmem, o_hbm.at[i_vmem.at[0]])

        pltpu.emit_pipeline(
            body,
            grid=(idx.shape[1] // SC_WINDOW,),
            in_specs=[pl.BlockSpec((SC_WINDOW, SC_ROW), lambda i: (lax.rem(i, windows), 0)),
                      pl.BlockSpec((1, SC_WINDOW), lambda i: (0, i))],
            out_specs=[],
            core_axis_name=("core", "subcore"),
            dimension_semantics=(pltpu.PARALLEL,),
        )(x_hbm, i_hbm)

    return scatter_rows(src, idx)


def _combine(src, idx):
    n_idx = idx.shape[1]

    @functools.partial(pl.kernel, out_type=jax.ShapeDtypeStruct((n_idx, SC_ROW), src.dtype),
                       mesh=_sc_mesh(), scratch_types=[])
    def gather_rows(y_hbm, i_hbm, o_hbm):
        def body(i_vmem, o_vmem):
            pltpu.sync_copy(y_hbm.at[i_vmem.at[0]], o_vmem)

        pltpu.emit_pipeline(
            body,
            grid=(n_idx // SC_WINDOW,),
            in_specs=[pl.BlockSpec((1, SC_WINDOW), lambda i: (0, i))],
            out_specs=[pl.BlockSpec((SC_WINDOW, SC_ROW), lambda i: (i, 0))],
            core_axis_name=("core", "subcore"),
            dimension_semantics=(pltpu.PARALLEL,),
        )(i_hbm, o_hbm)

    return gather_rows(src, idx)


def _expert_rows(xs_ref, ys_ref, row0, n, wgu_s, bgu_ref, wdn_s, bdn_ref, act_s):
    rows = slice(row0, row0 + n)
    arow = slice(0, n)
    w0 = xs_ref[0, rows, :]
    w1 = xs_ref[1, rows, :]
    xb = jnp.concatenate([_unpack_lo(w0).astype(BF16), _unpack_lo(w1).astype(BF16),
                          _unpack_hi(w0).astype(BF16), _unpack_hi(w1).astype(BF16)], axis=1)
    half = D_EXPERT // 2
    for j in range(2):
        cols = slice(j * half, (j + 1) * half)
        ucols = slice(D_EXPERT + j * half, D_EXPERT + (j + 1) * half)
        g = jnp.dot(xb, wgu_s[:, cols], preferred_element_type=F32) + bgu_ref[:, cols]
        up = jnp.dot(xb, wgu_s[:, ucols], preferred_element_type=F32) + bgu_ref[:, ucols]
        g = jnp.minimum(g, SWIGLU_LIMIT)
        up = jnp.clip(up, -SWIGLU_LIMIT, SWIGLU_LIMIT)
        act_s[arow, cols] = (g * jax.nn.sigmoid(SWIGLU_ALPHA * g) * (up + 1.0)).astype(BF16)
    y = jnp.dot(act_s[arow, :], wdn_s[...], preferred_element_type=F32) + bdn_ref[...]
    packed = _pack_rows(y)
    ys_ref[0, rows, :] = packed[:, :SC_ROW]
    ys_ref[1, rows, :] = packed[:, SC_ROW:]


def _expert_kernel(be_ref, nb_ref, nv_ref, xs_ref, wgu_ref, bgu_ref, wdn_ref, bdn_ref, ys_ref, wgu_s, wdn_s, act_s):
    blk = pl.program_id(0)
    prev = be_ref[jnp.maximum(blk - 1, 0)]
    changed = jnp.logical_or(blk == 0, be_ref[blk] != prev)
    valid = blk < nv_ref[0]

    @pl.when(jnp.logical_and(valid, changed))
    def _():
        chunk = 128

        def cast_gu(i, _):
            rows = pl.ds(pl.multiple_of(i * chunk, chunk), chunk)
            wgu_s[rows, :] = wgu_ref[rows, :].astype(BF16)
            return 0

        def cast_dn(i, _):
            rows = pl.ds(pl.multiple_of(i * chunk, chunk), chunk)
            wdn_s[rows, :] = wdn_ref[rows, :].astype(BF16)
            return 0

        lax.fori_loop(0, D_MODEL // chunk, cast_gu, 0)
        lax.fori_loop(0, D_EXPERT // chunk, cast_dn, 0)

    for nblk in range(1, STEP_ROWS // EXPERT_BLOCK + 1):
        @pl.when(jnp.logical_and(valid, nb_ref[blk] == nblk))
        def _(nblk=nblk):
            row0, left = 0, nblk
            while left:
                run = 1 << (left.bit_length() - 1)
                _expert_rows(xs_ref, ys_ref, row0, run * EXPERT_BLOCK, wgu_s, bgu_ref, wdn_s, bdn_ref, act_s)
                row0 += run * EXPERT_BLOCK
                left -= run


def _expert_call(xs, step_e, step_nblk, n_valid, w_gu, b_gu, w_dn, b_dn, layer, n_steps):
    def row_map(i, be, nb, nv):
        return (0, jnp.minimum(i, jnp.maximum(nv[0] - 1, 0)), 0)

    def w_map(i, be, nb, nv):
        return (layer, be[i], 0, 0)

    grid_spec = pltpu.PrefetchScalarGridSpec(
        num_scalar_prefetch=3,
        grid=(n_steps,),
        in_specs=[pl.BlockSpec((2, STEP_ROWS, SC_ROW), row_map),
                  pl.BlockSpec((None, None, D_MODEL, 2 * D_EXPERT), w_map),
                  pl.BlockSpec((None, None, 1, 2 * D_EXPERT), w_map),
                  pl.BlockSpec((None, None, D_EXPERT, D_MODEL), w_map),
                  pl.BlockSpec((None, None, 1, D_MODEL), w_map)],
        out_specs=pl.BlockSpec((2, STEP_ROWS, SC_ROW), row_map),
        scratch_shapes=[pltpu.VMEM((D_MODEL, 2 * D_EXPERT), BF16),
                        pltpu.VMEM((D_EXPERT, D_MODEL), BF16),
                        pltpu.VMEM((STEP_ROWS, D_EXPERT), BF16)])
    return pl.pallas_call(
        _expert_kernel,
        grid_spec=grid_spec,
        out_shape=jax.ShapeDtypeStruct((2, n_steps * STEP_ROWS, SC_ROW), I32),
        compiler_params=pltpu.CompilerParams(dimension_semantics=("arbitrary",),
                                             vmem_limit_bytes=VMEM_LIMIT),
        name="experts",
    )(step_e, step_nblk, n_valid, xs, w_gu, b_gu, w_dn, b_dn)


def _final_kernel(x_ref, y_ref, slab_ref, g_ref, *rest):
    o_ref = rest[-1]
    x = _moe_combine(x_ref[...], y_ref, slab_ref[...])
    o_ref[...] = _rms(x, g_ref[...])


def _final_call(x1, ysg, slab, g, out_row0, n_out, out_prev):
    t = FINAL_T
    out_tile0 = out_row0 // t
    in_specs = [pl.BlockSpec((t, D_MODEL), lambda i: (i, 0)),
                pl.BlockSpec((TOP_K, 2, t, SC_ROW), lambda i: (0, 0, i, 0)),
                pl.BlockSpec((t, LANES), lambda i: (i, 0)),
                pl.BlockSpec((1, D_MODEL), lambda i: (0, 0))]
    args = [x1, ysg, slab, g]
    aliases = {}
    if out_prev is not None:
        in_specs.append(pl.BlockSpec(memory_space=pl.ANY))
        args.append(out_prev)
        aliases = {4: 0}
    return pl.pallas_call(
        _final_kernel,
        grid=(x1.shape[0] // t,),
        in_specs=in_specs,
        out_specs=pl.BlockSpec((t, D_MODEL), lambda i: (out_tile0 + i, 0)),
        out_shape=jax.ShapeDtypeStruct((n_out, D_MODEL), F32),
        input_output_aliases=aliases,
        compiler_params=pltpu.CompilerParams(dimension_semantics=("arbitrary",)),
        name="final_norm",
    )(*args)


def _s5_tables(lam_re, lam_im, b_re, b_im, c_re, c_im, log_step):
    lr = jnp.minimum(lam_re.astype(F32), LAMBDA_RE_MAX)
    li = lam_im.astype(F32)
    step = jnp.exp(log_step.astype(F32))[:, None]
    mag = jnp.exp(lr * step)
    ar = mag * jnp.cos(li * step)
    ai = mag * jnp.sin(li * step)
    nr = ar - 1.0
    den = lr * lr + li * li
    kr = (nr * lr + ai * li) / den
    ki = (ai * lr - nr * li) / den
    bre = b_re.astype(F32)
    bim = b_im.astype(F32)
    bbar_r = kr[..., None] * bre - ki[..., None] * bim
    bbar_i = kr[..., None] * bim + ki[..., None] * bre
    cre = c_re.astype(F32)
    cim = c_im.astype(F32)

    def power(m):
        m = jnp.asarray(m, F32)[..., None, None]
        mg = jnp.exp(m * lr * step)
        return mg * jnp.cos(m * li * step), mg * jnp.sin(m * li * step)

    pw_r, pw_i = power(jnp.arange(CHUNK + 1))
    cp_r = cre[None] * pw_r[:, :, None, :] - cim[None] * pw_i[:, :, None, :]
    cp_i = cre[None] * pw_i[:, :, None, :] + cim[None] * pw_r[:, :, None, :]
    width = CHUNK * SSM_GROUP
    kcat = (jnp.einsum('tgap,gph->ghta', cp_r[:CHUNK], bbar_r)
            - jnp.einsum('tgap,gph->ghta', cp_i[:CHUNK], bbar_i)).reshape(SSM_GROUPS, SSM_GROUP, width).astype(BF16)
    m_intra = jnp.stack([jnp.pad(kcat[:, :, :width - s * SSM_GROUP], ((0, 0), (0, 0), (s * SSM_GROUP, 0)))
                         for s in range(CHUNK)], axis=1).reshape(SSM_GROUPS, width, width)
    mo_r = cp_r[1:].transpose(1, 3, 0, 2).reshape(SSM_GROUPS, SSM_STATE, width)
    mo_i = -cp_i[1:].transpose(1, 3, 0, 2).reshape(SSM_GROUPS, SSM_STATE, width)
    mcat = jnp.concatenate([m_intra, mo_r.astype(BF16), mo_i.astype(BF16)], axis=1)
    q_r = pw_r[CHUNK - 1::-1][:CHUNK][:, :, :, None]
    q_i = pw_i[CHUNK - 1::-1][:CHUNK][:, :, :, None]
    mn_r = q_r * bbar_r[None] - q_i * bbar_i[None]
    mn_i = q_r * bbar_i[None] + q_i * bbar_r[None]
    minw = jnp.concatenate([mn_r, mn_i], axis=2).transpose(1, 0, 3, 2).reshape(
        SSM_GROUPS, CHUNK * SSM_GROUP, 2 * SSM_STATE).astype(BF16)

    def packed(j):
        pr, pi = power(float(CHUNK * j))
        return (jnp.concatenate([pr, pr], axis=-1).reshape(1, SSM_GROUPS * LANES),
                jnp.concatenate([-pi, pi], axis=-1).reshape(1, SSM_GROUPS * LANES))

    rows = jnp.arange(SUBLANES)[:, None]
    kinds = []
    for d in (1, 2, 4):
        c1, c2 = packed(d)
        mask = (rows >= d).astype(F32)
        kinds += [mask * c1, mask * c2]
    pk = [packed(r + 1) for r in range(SUBLANES)]
    kinds += [jnp.concatenate([p[0] for p in pk], axis=0), jnp.concatenate([p[1] for p in pk], axis=0)]
    coef = jnp.stack(kinds, axis=0)
    return minw, mcat, coef


def _mixer_params(norm_mix, w_in, ssm_lam_re, ssm_lam_im, ssm_b_re, ssm_b_im, ssm_c_re, ssm_c_im, ssm_d,
                  ssm_log_step, ssm_w_glu, ssm_b_glu, pool_w, pool_scale, conv_w, branch_norm, w_out, norm_ffn,
                  router_w, router_b):
    depth = w_in.shape[0]
    minw, mcat, coef = jax.vmap(_s5_tables)(ssm_lam_re, ssm_lam_im, ssm_b_re, ssm_b_im, ssm_c_re, ssm_c_im,
                                            ssm_log_step)
    eye4 = jnp.eye(len(POOL_WINDOWS), dtype=F32)
    poolw = jnp.einsum('lgcd,gk->lgckd', pool_w.astype(F32), eye4).reshape(depth, POOL_WIDTH, POOL_WIDTH).astype(BF16)
    rw = jnp.pad(router_w.astype(F32), ((0, 0), (0, 0), (0, LANES - N_EXPERTS))).astype(BF16)
    rb = jnp.pad(router_b.astype(F32), ((0, 0), (0, LANES - N_EXPERTS))).reshape(depth, 1, LANES)
    row = lambda v: v.reshape(depth, 1, -1).astype(F32)
    utri = jnp.triu(jnp.ones((TILE_T, TILE_T), F32), 1).astype(BF16)
    return dict(gmix=row(norm_mix), win=w_in.astype(BF16), minw=minw, mcat=mcat, coef=coef, dskip=row(ssm_d),
                wglu=ssm_w_glu.astype(BF16), bglu=row(ssm_b_glu), poolw=poolw, pscale=row(pool_scale),
                convw=conv_w.astype(F32), gbr=row(branch_norm), wout=w_out.astype(BF16),
                gffn=row(norm_ffn), rw=rw, rb=rb, utri=utri[None])


def kernel(x, norm_mix, w_in, ssm_lam_re, ssm_lam_im, ssm_b_re, ssm_b_im, ssm_c_re, ssm_c_im, ssm_d, ssm_log_step, ssm_w_glu, ssm_b_glu, pool_w, pool_scale, conv_w, branch_norm, w_out, norm_ffn, router_w, router_b, w_gate_up, b_gate_up, w_down, b_down, final_norm):
    batch, seq, d_model = x.shape
    assert d_model == D_MODEL and seq % TILE_T == 0
    depth = w_in.shape[0]
    n_streams = 2 if batch % 2 == 0 else 1
    sb = batch // n_streams
    n_tok = sb * seq
    assert n_tok % ROUTE_T == 0 and n_tok % SC_WINDOW == 0
    n_steps = -(-(n_tok * TOP_K + N_EXPERTS * (STEP_ROWS - 1)) // STEP_ROWS)
    n_rows = n_steps * STEP_ROWS
    tiles = n_tok // TILE_T
    b_gu = b_gate_up.reshape(depth, N_EXPERTS, 1, 2 * D_EXPERT)
    b_dn = b_down.reshape(depth, N_EXPERTS, 1, D_MODEL)

    xin = [x.reshape(batch * seq, D_MODEL)] * n_streams
    tile0 = [s * tiles for s in range(n_streams)]
    ysg = [None] * n_streams
    slab = [None] * n_streams
    lp = _mixer_params(norm_mix, w_in, ssm_lam_re, ssm_lam_im, ssm_b_re, ssm_b_im, ssm_c_re, ssm_c_im, ssm_d,
                       ssm_log_step, ssm_w_glu, ssm_b_glu, pool_w, pool_scale, conv_w, branch_norm, w_out,
                       norm_ffn, router_w, router_b)
    for l in range(depth):
        routed = []
        for s in range(n_streams):
            xin[s], hp, slab[s], slabt, counts = _mixer_call(xin[s], tile0[s], ysg[s], slab[s], lp, l, sb, seq)
            tile0[s] = 0
            dest, meta = _route_call(slabt, counts, n_tok, n_steps)
            idx = dest.reshape(1, 2 * TOP_K * n_tok)
            xs = _dispatch(hp.reshape(2 * n_tok, SC_ROW), idx, 2 * n_rows).reshape(2, n_rows, SC_ROW)
            routed.append((xs, idx, meta[:n_steps, 0], meta[:n_steps, 1], meta[n_steps:n_steps + 1, 0]))
        for s, (xs, idx, step_e, step_nblk, n_valid) in enumerate(routed):
            ys = _expert_call(xs, step_e, step_nblk, n_valid, w_gate_up, b_gu, w_down, b_dn, l, n_steps)
            ysg[s] = _combine(ys.reshape(2 * n_rows, SC_ROW), idx).reshape(TOP_K, 2, n_tok, SC_ROW)
    g = final_norm.reshape(1, D_MODEL).astype(F32)
    out = None
    for s in range(n_streams):
        out = _final_call(xin[s], ysg[s], slab[s], g, s * n_tok, batch * seq, out)
    return out.reshape(batch, seq, D_MODEL)
```

```python
import functools
import math

import jax
import jax.numpy as jnp
from jax import lax
from jax.experimental import pallas as pl
from jax.experimental.pallas import tpu as pltpu
from jax.experimental.pallas import tpu_sc as plsc

F32 = jnp.float32
BF16 = jnp.bfloat16
I32 = jnp.int32

D_MODEL = 1024
SSM_WIDTH = 512
POOL_WIDTH = 256
CONV_WIDTH = 256
SSM_GROUP = 16
SSM_GROUPS = 32
SSM_STATE = 64
LAMBDA_RE_MAX = -1e-4
POOL_WINDOWS = (2, 4, 8, 16)
POOL_GROUP = 64
CONV_K = 3
IN_WIDTH = SSM_WIDTH + POOL_WIDTH + 3 * CONV_WIDTH
N_EXPERTS = 32
TOP_K = 4
D_EXPERT = 1024
SWIGLU_LIMIT = 7.0
SWIGLU_ALPHA = 1.702
EXPERT_BLOCK = 128
STEP_ROWS = 1024
NORM_EPS = 1e-5

LANES = 128
SUBLANES = 8
TILE_T = 512
FINAL_T = 512
CHUNK = 16
N_CHUNK = TILE_T // CHUNK
POOL_TAIL = SUBLANES * len(POOL_WINDOWS)
assert POOL_WINDOWS == tuple(2 ** (k + 1) for k in range(len(POOL_WINDOWS))) and POOL_TAIL >= POOL_WINDOWS[-1]
PACKED = D_MODEL // 2
ROUTE_T = 2048
SC_WINDOW = 128
SC_ROW = PACKED // 2
HI_MASK = -65536
VMEM_LIMIT = 56 * 1024 * 1024


def _bf16_round(v):
    return v.astype(BF16).astype(F32)


def _pack_rows(v):
    lo = lax.shift_right_logical(lax.bitcast_convert_type(_bf16_round(v[:, :PACKED]), I32), 16)
    hi = lax.bitcast_convert_type(_bf16_round(v[:, PACKED:]), I32) & HI_MASK
    return hi | lo


def _unpack_lo(w):
    return lax.bitcast_convert_type(lax.shift_left(w, 16), F32)


def _unpack_hi(w):
    return lax.bitcast_convert_type(w & HI_MASK, F32)


def _rms(v, g):
    r = lax.rsqrt(jnp.mean(v * v, axis=-1, keepdims=True) + NORM_EPS)
    return (v * r) * g


def _moe_combine(x, y_ref, slab):
    parts = [x[:, i * SC_ROW:(i + 1) * SC_ROW] for i in range(4)]
    for k in range(TOP_K):
        g = slab[:, TOP_K + k:TOP_K + k + 1]
        for h in range(2):
            w = y_ref[k, h]
            parts[h] = parts[h] + g * _unpack_lo(w)
            parts[2 + h] = parts[2 + h] + g * _unpack_hi(w)
    return jnp.concatenate(parts, axis=1)


def _block_transpose8(vs, lane):
    vs = list(vs)
    for dist in (4, 2, 1):
        width = SSM_GROUP * dist
        low = (lane % (2 * width)) < width
        for j in range(8):
            if j & dist:
                continue
            a, b = vs[j], vs[j + dist]
            vs[j] = jnp.where(low, a, pltpu.roll(b, width, axis=1))
            vs[j + dist] = jnp.where(low, pltpu.roll(a, LANES - width, axis=1), b)
    return vs


def _cmul_packed(c1, c2, v):
    return c1 * v + c2 * pltpu.roll(v, SSM_STATE, axis=1)


def _chunk_scan(s, carry_in, cf):
    outs = []
    carry = jnp.broadcast_to(carry_in, (SUBLANES, LANES))
    for rg in range(N_CHUNK // SUBLANES):
        x = s[rg * SUBLANES:(rg + 1) * SUBLANES]
        for j, d in enumerate((1, 2, 4)):
            x = x + _cmul_packed(cf[2 * j], cf[2 * j + 1], pltpu.roll(x, d, axis=0))
        x = x + _cmul_packed(cf[6], cf[7], carry)
        outs.append(x)
        carry = jnp.broadcast_to(x[SUBLANES - 1:SUBLANES], (SUBLANES, LANES))
    return jnp.concatenate(outs, axis=0)


def _s5_mixer(u, first, minw_ref, mcat_ref, coef_ref, u_s, y_s, h_s):
    blocks = SSM_WIDTH // LANES
    for b4 in range(blocks):
        u_s[b4] = u[:, b4 * LANES:(b4 + 1) * LANES]
    lane = lax.broadcasted_iota(I32, (N_CHUNK, LANES), 1)

    halves = [[None, None] for _ in range(SSM_GROUPS)]
    for b4 in range(blocks):
        for hh in range(2):
            xs = [u_s[b4, pl.ds(8 * hh + j, N_CHUNK, stride=CHUNK), :].astype(BF16) for j in range(8)]
            ws = _block_transpose8(xs, lane)
            for gl in range(8):
                halves[8 * b4 + gl][hh] = ws[gl]
    ug = [jnp.concatenate(h2, axis=1) for h2 in halves]

    carry_in = jnp.where(first, 0.0, h_s[N_CHUNK + 7:N_CHUNK + 8, :])
    h_s[7:8, :] = carry_in
    for g in range(SSM_GROUPS):
        cols = slice(g * LANES, (g + 1) * LANES)
        s_g = jnp.dot(ug[g], minw_ref[g], preferred_element_type=F32)
        cf = [coef_ref[k, :, cols] for k in range(8)]
        h_s[8:8 + N_CHUNK, cols] = _chunk_scan(s_g, carry_in[:, cols], cf)

    yg = []
    for g in range(SSM_GROUPS):
        hprev = h_s[7:7 + N_CHUNK, g * LANES:(g + 1) * LANES].astype(BF16)
        lhs = jnp.concatenate([ug[g], hprev], axis=1)
        yg.append(jnp.dot(lhs, mcat_ref[g], preferred_element_type=F32).astype(BF16))

    for b4 in range(blocks):
        for hh in range(2):
            ws = [yg[8 * b4 + gl][:, hh * LANES:(hh + 1) * LANES] for gl in range(8)]
            xs = _block_transpose8(ws, lane)
            for j in range(8):
                y_s[b4, pl.ds(8 * hh + j, N_CHUNK, stride=CHUNK), :] = xs[j].astype(F32)
    return jnp.concatenate([y_s[b4] for b4 in range(blocks)], axis=1)


def _mixer_kernel(*refs, first_layer):
    if first_layer:
        x_ref = refs[0]
        refs = refs[1:]
    else:
        x_ref, yprev_ref, slabprev_ref = refs[:3]
        refs = refs[3:]
    (gmix_ref, win_ref, minw_ref, mcat_ref, coef_ref, dskip_ref, wglu_ref, bglu_ref, poolw_ref, pscale_ref,
     convw_ref, gbr_ref, wout_ref, gffn_ref, rw_ref, rb_ref, utri_ref,
     x1_ref, hp_ref, slab_ref, slabt_ref, cnt_ref,
     u_s, y_s, h_s, pbuf_ref, zbuf_ref, cntacc_ref) = refs

    b = pl.program_id(0)
    l = pl.program_id(1)
    seq_start = l == 0

    @pl.when(jnp.logical_and(b == 0, l == 0))
    def _():
        cntacc_ref[...] = jnp.zeros_like(cntacc_ref)

    if first_layer:
        x = x_ref[...]
    else:
        x = _moe_combine(x_ref[...], yprev_ref, slabprev_ref[...])

    hn = _rms(x, gmix_ref[...]).astype(BF16)
    proj = jnp.dot(hn, win_ref[...], preferred_element_type=F32)
    u = proj[:, :SSM_WIDTH]
    p = proj[:, SSM_WIDTH:SSM_WIDTH + POOL_WIDTH]
    c0 = SSM_WIDTH + POOL_WIDTH
    bg = proj[:, c0:c0 + CONV_WIDTH]
    cg = proj[:, c0 + CONV_WIDTH:c0 + 2 * CONV_WIDTH]
    hv = proj[:, c0 + 2 * CONV_WIDTH:]

    y = _s5_mixer(u, seq_start, minw_ref, mcat_ref, coef_ref, u_s, y_s, h_s) + dskip_ref[...] * u
    z = jax.nn.gelu(y)
    glu = jnp.dot(z.astype(BF16), wglu_ref[...], preferred_element_type=F32) + bglu_ref[...]
    y_ssm = z * jax.nn.sigmoid(glu)

    tail = POOL_TAIL
    pbuf_ref[0, 0:tail, :] = jnp.where(seq_start, 0.0, pbuf_ref[0, TILE_T:TILE_T + tail, :])
    pbuf_ref[0, tail:, :] = p
    sums = []
    for stage, w in enumerate(POOL_WINDOWS):
        src = 0 if stage == 0 else 1 + (stage - 1) % 2
        dst = 1 + stage % 2
        d = w // 2
        lo = SUBLANES * (stage + 1)
        n = tail + TILE_T - lo
        e = pbuf_ref[src, pl.ds(lo, n), :] + pbuf_ref[src, pl.ds(lo - d, n), :]
        if stage + 1 < len(POOL_WINDOWS):
            pbuf_ref[dst, pl.ds(lo, n), :] = e
        sums.append(e[tail - lo:, :])
    lane_p = lax.broadcasted_iota(I32, (TILE_T, POOL_WIDTH), 1)
    total = sums[-1]
    win = jnp.full((TILE_T, POOL_WIDTH), POOL_WINDOWS[-1], I32)
    for i in range(len(POOL_WINDOWS) - 2, -1, -1):
        sel = lane_p < (i + 1) * POOL_GROUP
        total = jnp.where(sel, sums[i], total)
        win = jnp.where(sel, POOL_WINDOWS[i], win)
    row_p = lax.broadcasted_iota(I32, (TILE_T, POOL_WIDTH), 0) + l * TILE_T
    count = jnp.minimum(row_p + 1, win).astype(F32)
    pooled = total / count - p
    mixed = jnp.dot(pooled.astype(BF16), poolw_ref[...], preferred_element_type=F32)
    y_pool = mixed * pscale_ref[...]

    zc = cg * hv
    zbuf_ref[0:8, :] = jnp.where(seq_start, 0.0, zbuf_ref[TILE_T:TILE_T + 8, :])
    zbuf_ref[8:, :] = zc
    yc = (convw_ref[0:1, :] * zbuf_ref[pl.ds(6, TILE_T), :]
          + convw_ref[1:2, :] * zbuf_ref[pl.ds(7, TILE_T), :]
          + convw_ref[2:3, :] * zc)
    y_conv = bg * yc

    gbr = gbr_ref[...]
    s1 = SSM_WIDTH + POOL_WIDTH
    mixed_all = jnp.concatenate([_rms(y_ssm, gbr[:, :SSM_WIDTH]).astype(BF16),
                                 _rms(y_pool, gbr[:, SSM_WIDTH:s1]).astype(BF16),
                                 _rms(y_conv, gbr[:, s1:]).astype(BF16)], axis=1)
    x1 = x + jnp.dot(mixed_all, wout_ref[...], preferred_element_type=F32)
    x1_ref[...] = x1

    hn2 = _rms(x1, gffn_ref[...])
    packed = _pack_rows(hn2)
    hp_ref[0] = packed[:, :SC_ROW]
    hp_ref[1] = packed[:, SC_ROW:]
    logits = jnp.dot(hn2.astype(BF16), rw_ref[...], preferred_element_type=F32) + rb_ref[...]
    work = logits.T[:N_EXPERTS, :]
    eio = lax.broadcasted_iota(I32, (N_EXPERTS, TILE_T), 0).astype(F32)
    neg = jnp.float32(-jnp.inf)
    vals, idxs = [], []
    for _ in range(TOP_K):
        m = jnp.max(work, axis=0, keepdims=True)
        idx = jnp.min(jnp.where(work == m, eio, float(N_EXPERTS)), axis=0, keepdims=True)
        vals.append(m)
        idxs.append(idx)
        work = jnp.where(eio == idx, neg, work)
    exps = [jnp.exp(v - vals[0]) for v in vals]
    denom = exps[0] + exps[1] + exps[2] + exps[3]
    gates = [e / denom for e in exps]

    onehot = jnp.zeros((N_EXPERTS, TILE_T), F32)
    for idx in idxs:
        onehot = jnp.where(eio == idx, 1.0, onehot)
    before = jnp.dot(onehot.astype(BF16), utri_ref[...], preferred_element_type=F32) + cntacc_ref[:, 0:1]
    ranks = [jnp.sum(jnp.where(eio == idx, before, 0.0), axis=0, keepdims=True) for idx in idxs]
    row = lax.broadcasted_iota(I32, (LANES, TILE_T), 0)
    slab_t = jnp.zeros((LANES, TILE_T), F32)
    for k in range(TOP_K):
        slab_t = jnp.where(row == k, idxs[k], slab_t)
        slab_t = jnp.where(row == TOP_K + k, gates[k], slab_t)
        slab_t = jnp.where(row == 2 * TOP_K + k, ranks[k], slab_t)
    slabt_ref[...] = slab_t[:2 * SUBLANES, :]
    slab_ref[...] = slab_t.T
    newcnt = cntacc_ref[:, 0:1] + jnp.sum(onehot, axis=1, keepdims=True)
    cntacc_ref[...] = jnp.broadcast_to(newcnt, cntacc_ref.shape)
    cnt_ref[...] = jnp.broadcast_to(newcnt, cnt_ref.shape)


def _layer_spec(shape, layer):
    nd = len(shape) - 1
    sel = layer if shape[0] > 1 else 0
    return pl.BlockSpec((None,) + tuple(shape[1:]), lambda b, l, nd=nd, sel=sel: (sel,) + (0,) * nd,
                        pipeline_mode=pl.Buffered(1))


def _mixer_call(x, x_tile0, yprev, slabprev, lp, layer, batch, seq):
    first_layer = yprev is None
    n_tok = batch * seq
    steps = seq // TILE_T
    tok_spec = lambda w: pl.BlockSpec((TILE_T, w), lambda b, l: (b * steps + l, 0))
    in_specs = [pl.BlockSpec((TILE_T, D_MODEL), lambda b, l: (x_tile0 + b * steps + l, 0))]
    args = [x]
    if not first_layer:
        in_specs += [pl.BlockSpec((TOP_K, 2, TILE_T, SC_ROW), lambda b, l: (0, 0, b * steps + l, 0)), tok_spec(LANES)]
        args += [yprev, slabprev]
    weights = [lp['gmix'], lp['win'], lp['minw'], lp['mcat'], lp['coef'], lp['dskip'], lp['wglu'], lp['bglu'],
               lp['poolw'], lp['pscale'], lp['convw'], lp['gbr'], lp['wout'], lp['gffn'], lp['rw'], lp['rb'],
               lp['utri']]
    in_specs += [_layer_spec(w.shape, layer) for w in weights]
    args += weights
    out_shape = [jax.ShapeDtypeStruct((n_tok, D_MODEL), F32),
                 jax.ShapeDtypeStruct((2, n_tok, SC_ROW), I32),
                 jax.ShapeDtypeStruct((n_tok, LANES), F32),
                 jax.ShapeDtypeStruct((2 * SUBLANES, n_tok), F32),
                 jax.ShapeDtypeStruct((N_EXPERTS, LANES), F32)]
    out_specs = [tok_spec(D_MODEL), pl.BlockSpec((2, TILE_T, SC_ROW), lambda b, l: (0, b * steps + l, 0)), tok_spec(LANES),
                 pl.BlockSpec((2 * SUBLANES, TILE_T), lambda b, l: (0, b * steps + l)),
                 pl.BlockSpec((N_EXPERTS, LANES), lambda b, l: (0, 0))]
    scratch = [pltpu.VMEM((SSM_WIDTH // LANES, TILE_T, LANES), F32),
               pltpu.VMEM((SSM_WIDTH // LANES, TILE_T, LANES), F32),
               pltpu.VMEM((SUBLANES + N_CHUNK, SSM_GROUPS * LANES), F32),
               pltpu.VMEM((3, POOL_TAIL + TILE_T, POOL_WIDTH), F32),
               pltpu.VMEM((8 + TILE_T, CONV_WIDTH), F32),
               pltpu.VMEM((N_EXPERTS, LANES), F32)]
    return pl.pallas_call(
        functools.partial(_mixer_kernel, first_layer=first_layer),
        grid=(batch, steps),
        in_specs=in_specs, out_specs=out_specs, out_shape=out_shape, scratch_shapes=scratch,
        compiler_params=pltpu.CompilerParams(dimension_semantics=("arbitrary", "arbitrary"),
                                             vmem_limit_bytes=VMEM_LIMIT),
        name="mixer_first" if first_layer else "mixer_next",
    )(*args)


def _route_kernel(slabt_ref, cnt_ref, dest_ref, meta_ref, *, n_steps):
    lane1 = lax.broadcasted_iota(I32, (1, LANES), 1)
    e_row = lax.broadcasted_iota(I32, (N_EXPERTS, LANES), 0)
    e_lane = lax.broadcasted_iota(I32, (N_EXPERTS, LANES), 1)
    cnt = cnt_ref[...].astype(I32)
    counts = jnp.sum(jnp.where(e_row == e_lane, cnt, 0), axis=0, keepdims=True)
    padded = ((counts + (STEP_ROWS - 1)) // STEP_ROWS) * STEP_ROWS
    pend = padded
    for sh in (1, 2, 4, 8, 16):
        pend = pend + jnp.where(lane1 >= sh, pltpu.roll(pend, sh, axis=1), 0)
    pstart = pend - padded

    @pl.when(pl.program_id(0) == 0)
    def _():
        rows = meta_ref.shape[0]
        bstart = lax.broadcasted_iota(I32, (rows, LANES), 0) * STEP_ROWS
        lane = lax.broadcasted_iota(I32, (rows, LANES), 1)
        done = jnp.where(jnp.logical_and(lane < N_EXPERTS, pend <= bstart), 1, 0)
        be = jnp.minimum(jnp.sum(done, axis=1, keepdims=True), N_EXPERTS - 1)
        mine = lane == be
        left = jnp.sum(jnp.where(mine, counts + pstart, 0), axis=1, keepdims=True) - bstart[:, 0:1]
        nblk = jnp.clip((left + (EXPERT_BLOCK - 1)) // EXPERT_BLOCK, 0, STEP_ROWS // EXPERT_BLOCK)
        total = jnp.sum(jnp.where(lane1 == N_EXPERTS - 1, pend, 0), axis=1, keepdims=True)
        row = lax.broadcasted_iota(I32, (rows, LANES), 0)
        table = jnp.where(lane == 0, be, jnp.where(lane == 1, nblk, 0))
        meta_ref[...] = jnp.where(row == n_steps, total // STEP_ROWS, table)

    pstart_col = jnp.sum(jnp.where(e_row == e_lane, pstart, 0), axis=1, keepdims=True)
    tokens = slabt_ref.shape[1]
    eio = lax.broadcasted_iota(I32, (N_EXPERTS, tokens), 0)
    row = lax.broadcasted_iota(I32, (SUBLANES, tokens), 0)
    out = jnp.zeros((SUBLANES, tokens), I32)
    for k in range(TOP_K):
        idx = slabt_ref[k:k + 1, :].astype(I32)
        rank = slabt_ref[2 * TOP_K + k:2 * TOP_K + k + 1, :].astype(I32)
        base = jnp.sum(jnp.where(eio == idx, pstart_col, 0), axis=0, keepdims=True)
        for h in range(2):
            out = jnp.where(row == 2 * k + h, base + rank + h * n_steps * STEP_ROWS, out)
    dest_ref[...] = out


def _route_call(slabt, counts, n_tok, n_steps):
    meta_rows = ((n_steps + 1 + SUBLANES - 1) // SUBLANES) * SUBLANES
    return pl.pallas_call(
        functools.partial(_route_kernel, n_steps=n_steps),
        grid=(n_tok // ROUTE_T,),
        in_specs=[pl.BlockSpec((2 * SUBLANES, ROUTE_T), lambda i: (0, i)),
                  pl.BlockSpec((N_EXPERTS, LANES), lambda i: (0, 0))],
        out_specs=[pl.BlockSpec((SUBLANES, ROUTE_T), lambda i: (0, i)),
                   pl.BlockSpec((meta_rows, LANES), lambda i: (0, 0))],
        out_shape=[jax.ShapeDtypeStruct((SUBLANES, n_tok), I32),
                   jax.ShapeDtypeStruct((meta_rows, LANES), I32)],
        compiler_params=pltpu.CompilerParams(dimension_semantics=("arbitrary",)),
        name="route",
    )(slabt, counts)


def _sc_mesh():
    return plsc.VectorSubcoreMesh(core_axis_name="core", subcore_axis_name="subcore")


def _dispatch(src, idx, n_rows):
    windows = src.shape[0] // SC_WINDOW

    @functools.partial(pl.kernel, out_type=jax.ShapeDtypeStruct((n_rows, SC_ROW), src.dtype),
                       mesh=_sc_mesh(), scratch_types=[])
    def scatter_rows(x_hbm, i_hbm, o_hbm):
        def body(x_vmem, i_vmem):
            pltpu.sync_copy(x_vmem, o_hbm.at[i_vmem.at[0]])

        pltpu.emit_pipeline(
            body,
            grid=(idx.shape[1] // SC_WINDOW,),
            in_specs=[pl.BlockSpec((SC_WINDOW, SC_ROW), lambda i: (lax.rem(i, windows), 0)),
                      pl.BlockSpec((1, SC_WINDOW), lambda i: (0, i))],
            out_specs=[],
            core_axis_name=("core", "subcore"),
            dimension_semantics=(pltpu.PARALLEL,),
        )(x_hbm, i_hbm)

    return scatter_rows(src, idx)


def _combine(src, idx):
    n_idx = idx.shape[1]

    @functools.partial(pl.kernel, out_type=jax.ShapeDtypeStruct((n_idx, SC_ROW), src.dtype),
                       mesh=_sc_mesh(), scratch_types=[])
    def gather_rows(y_hbm, i_hbm, o_hbm):
        def body(i_vmem, o_vmem):
            pltpu.sync_copy(y_hbm.at[i_vmem.at[0]], o_vmem)

        pltpu.emit_pipeline(
            body,
            grid=(n_idx // SC_WINDOW,),
            in_specs=[pl.BlockSpec((1, SC_WINDOW), lambda i: (0, i))],
            out_specs=[pl.BlockSpec((SC_WINDOW, SC_ROW), lambda i: (i, 0))],
            core_axis_name=("core", "subcore"),
            dimension_semantics=(pltpu.PARALLEL,),
        )(i_hbm, o_hbm)

    return gather_rows(src, idx)


def _expert_rows(xs_ref, ys_ref, row0, n, wgu_s, bgu_ref, wdn_s, bdn_ref, act_s):
    rows = slice(row0, row0 + n)
    arow = slice(0, n)
    w0 = xs_ref[0, rows, :]
    w1 = xs_ref[1, rows, :]
    xb = jnp.concatenate([_unpack_lo(w0).astype(BF16), _unpack_lo(w1).astype(BF16),
                          _unpack_hi(w0).astype(BF16), _unpack_hi(w1).astype(BF16)], axis=1)
    half = D_EXPERT // 2
    for j in range(2):
        cols = slice(j * half, (j + 1) * half)
        ucols = slice(D_EXPERT + j * half, D_EXPERT + (j + 1) * half)
        g = jnp.dot(xb, wgu_s[:, cols], preferred_element_type=F32) + bgu_ref[:, cols]
        up = jnp.dot(xb, wgu_s[:, ucols], preferred_element_type=F32) + bgu_ref[:, ucols]
        g = jnp.minimum(g, SWIGLU_LIMIT)
        up = jnp.clip(up, -SWIGLU_LIMIT, SWIGLU_LIMIT)
        act_s[arow, cols] = (g * jax.nn.sigmoid(SWIGLU_ALPHA * g) * (up + 1.0)).astype(BF16)
    y = jnp.dot(act_s[arow, :], wdn_s[...], preferred_element_type=F32) + bdn_ref[...]
    packed = _pack_rows(y)
    ys_ref[0, rows, :] = packed[:, :SC_ROW]
    ys_ref[1, rows, :] = packed[:, SC_ROW:]


def _expert_kernel(be_ref, nb_ref, nv_ref, xs_ref, wgu_ref, bgu_ref, wdn_ref, bdn_ref, ys_ref, wgu_s, wdn_s, act_s):
    blk = pl.program_id(0)
    prev = be_ref[jnp.maximum(blk - 1, 0)]
    changed = jnp.logical_or(blk == 0, be_ref[blk] != prev)
    valid = blk < nv_ref[0]

    @pl.when(jnp.logical_and(valid, changed))
    def _():
        chunk = 128

        def cast_gu(i, _):
            rows = pl.ds(pl.multiple_of(i * chunk, chunk), chunk)
            wgu_s[rows, :] = wgu_ref[rows, :].astype(BF16)
            return 0

        def cast_dn(i, _):
            rows = pl.ds(pl.multiple_of(i * chunk, chunk), chunk)
            wdn_s[rows, :] = wdn_ref[rows, :].astype(BF16)
            return 0

        lax.fori_loop(0, D_MODEL // chunk, cast_gu, 0)
        lax.fori_loop(0, D_EXPERT // chunk, cast_dn, 0)

    for nblk in range(1, STEP_ROWS // EXPERT_BLOCK + 1):
        @pl.when(jnp.logical_and(valid, nb_ref[blk] == nblk))
        def _(nblk=nblk):
            row0, left = 0, nblk
            while left:
                run = 1 << (left.bit_length() - 1)
                _expert_rows(xs_ref, ys_ref, row0, run * EXPERT_BLOCK, wgu_s, bgu_ref, wdn_s, bdn_ref, act_s)
                row0 += run * EXPERT_BLOCK
                left -= run


def _expert_call(xs, step_e, step_nblk, n_valid, w_gu, b_gu, w_dn, b_dn, layer, n_steps):
    def row_map(i, be, nb, nv):
        return (0, jnp.minimum(i, jnp.maximum(nv[0] - 1, 0)), 0)

    def w_map(i, be, nb, nv):
        return (layer, be[i], 0, 0)

    grid_spec = pltpu.PrefetchScalarGridSpec(
        num_scalar_prefetch=3,
        grid=(n_steps,),
        in_specs=[pl.BlockSpec((2, STEP_ROWS, SC_ROW), row_map),
                  pl.BlockSpec((None, None, D_MODEL, 2 * D_EXPERT), w_map),
                  pl.BlockSpec((None, None, 1, 2 * D_EXPERT), w_map),
                  pl.BlockSpec((None, None, D_EXPERT, D_MODEL), w_map),
                  pl.BlockSpec((None, None, 1, D_MODEL), w_map)],
        out_specs=pl.BlockSpec((2, STEP_ROWS, SC_ROW), row_map),
        scratch_shapes=[pltpu.VMEM((D_MODEL, 2 * D_EXPERT), BF16),
                        pltpu.VMEM((D_EXPERT, D_MODEL), BF16),
                        pltpu.VMEM((STEP_ROWS, D_EXPERT), BF16)])
    return pl.pallas_call(
        _expert_kernel,
        grid_spec=grid_spec,
        out_shape=jax.ShapeDtypeStruct((2, n_steps * STEP_ROWS, SC_ROW), I32),
        compiler_params=pltpu.CompilerParams(dimension_semantics=("arbitrary",),
                                             vmem_limit_bytes=VMEM_LIMIT),
        name="experts",
    )(step_e, step_nblk, n_valid, xs, w_gu, b_gu, w_dn, b_dn)


def _final_kernel(x_ref, y_ref, slab_ref, g_ref, *rest):
    o_ref = rest[-1]
    x = _moe_combine(x_ref[...], y_ref, slab_ref[...])
    o_ref[...] = _rms(x, g_ref[...])


def _final_call(x1, ysg, slab, g, out_row0, n_out, out_prev):
    t = FINAL_T
    out_tile0 = out_row0 // t
    in_specs = [pl.BlockSpec((t, D_MODEL), lambda i: (i, 0)),
                pl.BlockSpec((TOP_K, 2, t, SC_ROW), lambda i: (0, 0, i, 0)),
                pl.BlockSpec((t, LANES), lambda i: (i, 0)),
                pl.BlockSpec((1, D_MODEL), lambda i: (0, 0))]
    args = [x1, ysg, slab, g]
    aliases = {}
    if out_prev is not None:
        in_specs.append(pl.BlockSpec(memory_space=pl.ANY))
        args.append(out_prev)
        aliases = {4: 0}
    return pl.pallas_call(
        _final_kernel,
        grid=(x1.shape[0] // t,),
        in_specs=in_specs,
        out_specs=pl.BlockSpec((t, D_MODEL), lambda i: (out_tile0 + i, 0)),
        out_shape=jax.ShapeDtypeStruct((n_out, D_MODEL), F32),
        input_output_aliases=aliases,
        compiler_params=pltpu.CompilerParams(dimension_semantics=("arbitrary",)),
        name="final_norm",
    )(*args)


def _s5_tables(lam_re, lam_im, b_re, b_im, c_re, c_im, log_step):
    lr = jnp.minimum(lam_re.astype(F32), LAMBDA_RE_MAX)
    li = lam_im.astype(F32)
    step = jnp.exp(log_step.astype(F32))[:, None]
    mag = jnp.exp(lr * step)
    ar = mag * jnp.cos(li * step)
    ai = mag * jnp.sin(li * step)
    nr = ar - 1.0
    den = lr * lr + li * li
    kr = (nr * lr + ai * li) / den
    ki = (ai * lr - nr * li) / den
    bre = b_re.astype(F32)
    bim = b_im.astype(F32)
    bbar_r = kr[..., None] * bre - ki[..., None] * bim
    bbar_i = kr[..., None] * bim + ki[..., None] * bre
    cre = c_re.astype(F32)
    cim = c_im.astype(F32)

    def power(m):
        m = jnp.asarray(m, F32)[..., None, None]
        mg = jnp.exp(m * lr * step)
        return mg * jnp.cos(m * li * step), mg * jnp.sin(m * li * step)

    pw_r, pw_i = power(jnp.arange(CHUNK + 1))
    cp_r = cre[None] * pw_r[:, :, None, :] - cim[None] * pw_i[:, :, None, :]
    cp_i = cre[None] * pw_i[:, :, None, :] + cim[None] * pw_r[:, :, None, :]
    width = CHUNK * SSM_GROUP
    kcat = (jnp.einsum('tgap,gph->ghta', cp_r[:CHUNK], bbar_r)
            - jnp.einsum('tgap,gph->ghta', cp_i[:CHUNK], bbar_i)).astype(BF16)
    lag = jnp.arange(CHUNK)
    place = (lag[None, :, None] - lag[:, None, None] == lag[None, None, :]).astype(BF16)
    m_intra = jnp.einsum('stl,ghla->gshta', place, kcat,
                         preferred_element_type=F32).astype(BF16).reshape(SSM_GROUPS, width, width)
    mo_r = cp_r[1:].transpose(1, 3, 0, 2).reshape(SSM_GROUPS, SSM_STATE, width)
    mo_i = -cp_i[1:].transpose(1, 3, 0, 2).reshape(SSM_GROUPS, SSM_STATE, width)
    mcat = jnp.concatenate([m_intra, mo_r.astype(BF16), mo_i.astype(BF16)], axis=1)
    q_r = pw_r[CHUNK - 1::-1][:CHUNK][:, :, :, None]
    q_i = pw_i[CHUNK - 1::-1][:CHUNK][:, :, :, None]
    mn_r = q_r * bbar_r[None] - q_i * bbar_i[None]
    mn_i = q_r * bbar_i[None] + q_i * bbar_r[None]
    minw = jnp.concatenate([mn_r, mn_i], axis=2).transpose(1, 0, 3, 2).reshape(
        SSM_GROUPS, CHUNK * SSM_GROUP, 2 * SSM_STATE).astype(BF16)

    def packed(j):
        pr, pi = power(float(CHUNK * j))
        return (jnp.concatenate([pr, pr], axis=-1).reshape(1, SSM_GROUPS * LANES),
                jnp.concatenate([-pi, pi], axis=-1).reshape(1, SSM_GROUPS * LANES))

    rows = jnp.arange(SUBLANES)[:, None]
    kinds = []
    for d in (1, 2, 4):
        c1, c2 = packed(d)
        mask = (rows >= d).astype(F32)
        kinds += [mask * c1, mask * c2]
    pk = [packed(r + 1) for r in range(SUBLANES)]
    kinds += [jnp.concatenate([p[0] for p in pk], axis=0), jnp.concatenate([p[1] for p in pk], axis=0)]
    coef = jnp.stack(kinds, axis=0)
    return minw, mcat, coef


def _mixer_params(norm_mix, w_in, ssm_lam_re, ssm_lam_im, ssm_b_re, ssm_b_im, ssm_c_re, ssm_c_im, ssm_d,
                  ssm_log_step, ssm_w_glu, ssm_b_glu, pool_w, pool_scale, conv_w, branch_norm, w_out, norm_ffn,
                  router_w, router_b):
    depth = w_in.shape[0]
    minw, mcat, coef = jax.vmap(_s5_tables)(ssm_lam_re, ssm_lam_im, ssm_b_re, ssm_b_im, ssm_c_re, ssm_c_im,
                                            ssm_log_step)
    eye4 = jnp.eye(len(POOL_WINDOWS), dtype=F32)
    poolw = jnp.einsum('lgcd,gk->lgckd', pool_w.astype(F32), eye4).reshape(depth, POOL_WIDTH, POOL_WIDTH).astype(BF16)
    rw = jnp.pad(router_w.astype(F32), ((0, 0), (0, 0), (0, LANES - N_EXPERTS))).astype(BF16)
    rb = jnp.pad(router_b.astype(F32), ((0, 0), (0, LANES - N_EXPERTS))).reshape(depth, 1, LANES)
    row = lambda v: v.reshape(depth, 1, -1).astype(F32)
    utri = jnp.triu(jnp.ones((TILE_T, TILE_T), F32), 1).astype(BF16)
    return dict(gmix=row(norm_mix), win=w_in.astype(BF16), minw=minw, mcat=mcat, coef=coef, dskip=row(ssm_d),
                wglu=ssm_w_glu.astype(BF16), bglu=row(ssm_b_glu), poolw=poolw, pscale=row(pool_scale),
                convw=conv_w.astype(F32), gbr=row(branch_norm), wout=w_out.astype(BF16),
                gffn=row(norm_ffn), rw=rw, rb=rb, utri=utri[None])


def kernel(x, norm_mix, w_in, ssm_lam_re, ssm_lam_im, ssm_b_re, ssm_b_im, ssm_c_re, ssm_c_im, ssm_d, ssm_log_step, ssm_w_glu, ssm_b_glu, pool_w, pool_scale, conv_w, branch_norm, w_out, norm_ffn, router_w, router_b, w_gate_up, b_gate_up, w_down, b_down, final_norm):
    batch, seq, d_model = x.shape
    assert d_model == D_MODEL and seq % TILE_T == 0
    depth = w_in.shape[0]
    n_streams = 2 if batch % 2 == 0 else 1
    sb = batch // n_streams
    n_tok = sb * seq
    assert n_tok % ROUTE_T == 0 and n_tok % SC_WINDOW == 0
    n_steps = -(-(n_tok * TOP_K + N_EXPERTS * (STEP_ROWS - 1)) // STEP_ROWS)
    n_rows = n_steps * STEP_ROWS
    tiles = n_tok // TILE_T
    b_gu = b_gate_up.reshape(depth, N_EXPERTS, 1, 2 * D_EXPERT)
    b_dn = b_down.reshape(depth, N_EXPERTS, 1, D_MODEL)

    xin = [x.reshape(batch * seq, D_MODEL)] * n_streams
    tile0 = [s * tiles for s in range(n_streams)]
    ysg = [None] * n_streams
    slab = [None] * n_streams
    lp = _mixer_params(norm_mix, w_in, ssm_lam_re, ssm_lam_im, ssm_b_re, ssm_b_im, ssm_c_re, ssm_c_im, ssm_d,
                       ssm_log_step, ssm_w_glu, ssm_b_glu, pool_w, pool_scale, conv_w, branch_norm, w_out,
                       norm_ffn, router_w, router_b)
    for l in range(depth):
        routed = []
        for s in range(n_streams):
            xin[s], hp, slab[s], slabt, counts = _mixer_call(xin[s], tile0[s], ysg[s], slab[s], lp, l, sb, seq)
            tile0[s] = 0
            dest, meta = _route_call(slabt, counts, n_tok, n_steps)
            idx = dest.reshape(1, 2 * TOP_K * n_tok)
            xs = _dispatch(hp.reshape(2 * n_tok, SC_ROW), idx, 2 * n_rows).reshape(2, n_rows, SC_ROW)
            routed.append((xs, idx, meta[:n_steps, 0], meta[:n_steps, 1], meta[n_steps:n_steps + 1, 0]))
        for s, (xs, idx, step_e, step_nblk, n_valid) in enumerate(routed):
            ys = _expert_call(xs, step_e, step_nblk, n_valid, w_gate_up, b_gu, w_down, b_dn, l, n_steps)
            ysg[s] = _combine(ys.reshape(2 * n_rows, SC_ROW), idx).reshape(TOP_K, 2, n_tok, SC_ROW)
    g = final_norm.reshape(1, D_MODEL).astype(F32)
    out = None
    for s in range(n_streams):
        out = _final_call(xin[s], ysg[s], slab[s], g, s * n_tok, batch * seq, out)
    return out.reshape(batch, seq, D_MODEL)
```

```python
import functools

import jax
import jax.numpy as jnp
from jax import lax
from jax.experimental import pallas as pl
from jax.experimental.pallas import tpu as pltpu
from jax.experimental.pallas import tpu_sc as plsc

F32 = jnp.float32
BF16 = jnp.bfloat16
I32 = jnp.int32

D_MODEL = 1024
SSM_WIDTH = 512
POOL_WIDTH = 256
CONV_WIDTH = 256
SSM_GROUP = 16
SSM_GROUPS = 32
SSM_STATE = 64
LAMBDA_RE_MAX = -1e-4
POOL_WINDOWS = (2, 4, 8, 16)
POOL_GROUP = 64
N_EXPERTS = 32
TOP_K = 4
D_EXPERT = 1024
SWIGLU_LIMIT = 7.0
SWIGLU_ALPHA = 1.702
EXPERT_BLOCK = 256
STEP_ROWS = 1024
NORM_EPS = 1e-5

LANES = 128
SUBLANES = 8
TILE_T = 512
FINAL_T = 512
CHUNK = 16
N_CHUNK = TILE_T // CHUNK
POOL_TAIL = SUBLANES * len(POOL_WINDOWS)
assert POOL_WINDOWS == tuple(2 ** (k + 1) for k in range(len(POOL_WINDOWS))) and POOL_TAIL >= POOL_WINDOWS[-1]
PACKED = D_MODEL // 2
ROUTE_T = 2048
SC_WINDOW = 128
SC_ROW = PACKED // 2
HI_MASK = -65536
VMEM_LIMIT = 56 * 1024 * 1024


def _bf16_round(v):
    return v.astype(BF16).astype(F32)


def _pack_rows(v):
    lo = lax.shift_right_logical(lax.bitcast_convert_type(_bf16_round(v[:, :PACKED]), I32), 16)
    hi = lax.bitcast_convert_type(_bf16_round(v[:, PACKED:]), I32) & HI_MASK
    return hi | lo


def _unpack_lo(w):
    return lax.bitcast_convert_type(lax.shift_left(w, 16), F32)


def _unpack_hi(w):
    return lax.bitcast_convert_type(w & HI_MASK, F32)


def _rms(v, g):
    r = lax.rsqrt(jnp.mean(v * v, axis=-1, keepdims=True) + NORM_EPS)
    return (v * r) * g


def _moe_combine(x, y_ref, slab):
    parts = [x[:, i * SC_ROW:(i + 1) * SC_ROW] for i in range(4)]
    for k in range(TOP_K):
        g = slab[:, TOP_K + k:TOP_K + k + 1]
        for h in range(2):
            w = y_ref[k, h]
            parts[h] = parts[h] + g * _unpack_lo(w)
            parts[2 + h] = parts[2 + h] + g * _unpack_hi(w)
    return jnp.concatenate(parts, axis=1)


def _block_transpose8(vs, lane):
    vs = list(vs)
    for dist in (4, 2, 1):
        width = SSM_GROUP * dist
        low = (lane % (2 * width)) < width
        for j in range(8):
            if j & dist:
                continue
            a, b = vs[j], vs[j + dist]
            vs[j] = jnp.where(low, a, pltpu.roll(b, width, axis=1))
            vs[j + dist] = jnp.where(low, pltpu.roll(a, LANES - width, axis=1), b)
    return vs


def _cmul_packed(c1, c2, v):
    return c1 * v + c2 * pltpu.roll(v, SSM_STATE, axis=1)


def _chunk_scan(s, carry_in, cf):
    outs = []
    carry = jnp.broadcast_to(carry_in, (SUBLANES, LANES))
    for rg in range(N_CHUNK // SUBLANES):
        x = s[rg * SUBLANES:(rg + 1) * SUBLANES]
        for j, d in enumerate((1, 2, 4)):
            x = x + _cmul_packed(cf[2 * j], cf[2 * j + 1], pltpu.roll(x, d, axis=0))
        x = x + _cmul_packed(cf[6], cf[7], carry)
        outs.append(x)
        carry = jnp.broadcast_to(x[SUBLANES - 1:SUBLANES], (SUBLANES, LANES))
    return jnp.concatenate(outs, axis=0)


def _s5_mixer(u, first, minw_ref, mcat_ref, coef_ref, u_s, y_s, h_s):
    blocks = SSM_WIDTH // LANES
    for b4 in range(blocks):
        u_s[b4] = u[:, b4 * LANES:(b4 + 1) * LANES]
    lane = lax.broadcasted_iota(I32, (N_CHUNK, LANES), 1)

    halves = [[None, None] for _ in range(SSM_GROUPS)]
    for b4 in range(blocks):
        for hh in range(2):
            xs = [u_s[b4, pl.ds(8 * hh + j, N_CHUNK, stride=CHUNK), :].astype(BF16) for j in range(8)]
            ws = _block_transpose8(xs, lane)
            for gl in range(8):
                halves[8 * b4 + gl][hh] = ws[gl]
    ug = [jnp.concatenate(h2, axis=1) for h2 in halves]

    carry_in = jnp.where(first, 0.0, h_s[N_CHUNK + 7:N_CHUNK + 8, :])
    h_s[7:8, :] = carry_in
    for g in range(SSM_GROUPS):
        cols = slice(g * LANES, (g + 1) * LANES)
        s_g = jnp.dot(ug[g], minw_ref[g], preferred_element_type=F32)
        cf = [coef_ref[k, :, cols] for k in range(8)]
        h_s[8:8 + N_CHUNK, cols] = _chunk_scan(s_g, carry_in[:, cols], cf)

    yg = []
    for g in range(SSM_GROUPS):
        hprev = h_s[7:7 + N_CHUNK, g * LANES:(g + 1) * LANES].astype(BF16)
        lhs = jnp.concatenate([ug[g], hprev], axis=1)
        yg.append(jnp.dot(lhs, mcat_ref[g], preferred_element_type=F32).astype(BF16))

    for b4 in range(blocks):
        for hh in range(2):
            ws = [yg[8 * b4 + gl][:, hh * LANES:(hh + 1) * LANES] for gl in range(8)]
            xs = _block_transpose8(ws, lane)
            for j in range(8):
                y_s[b4, pl.ds(8 * hh + j, N_CHUNK, stride=CHUNK), :] = xs[j].astype(F32)
    return jnp.concatenate([y_s[b4] for b4 in range(blocks)], axis=1)


def _mixer_kernel(*refs, first_layer):
    if first_layer:
        x_ref = refs[0]
        refs = refs[1:]
    else:
        x_ref, yprev_ref, slabprev_ref = refs[:3]
        refs = refs[3:]
    (gmix_ref, win_ref, minw_ref, mcat_ref, coef_ref, dskip_ref, wglu_ref, bglu_ref, poolw_ref, pscale_ref,
     convw_ref, gbr_ref, wout_ref, gffn_ref, rw_ref, rb_ref, utri_ref,
     x1_ref, hp_ref, slab_ref, slabt_ref, cnt_ref,
     u_s, y_s, h_s, pbuf_ref, zbuf_ref, cntacc_ref) = refs

    b = pl.program_id(0)
    l = pl.program_id(1)
    seq_start = l == 0

    @pl.when(jnp.logical_and(b == 0, l == 0))
    def _():
        cntacc_ref[...] = jnp.zeros_like(cntacc_ref)

    if first_layer:
        x = x_ref[...]
    else:
        x = _moe_combine(x_ref[...], yprev_ref, slabprev_ref[...])

    hn = _rms(x, gmix_ref[...]).astype(BF16)
    proj = jnp.dot(hn, win_ref[...], preferred_element_type=F32)
    u = proj[:, :SSM_WIDTH]
    p = proj[:, SSM_WIDTH:SSM_WIDTH + POOL_WIDTH]
    c0 = SSM_WIDTH + POOL_WIDTH
    bg = proj[:, c0:c0 + CONV_WIDTH]
    cg = proj[:, c0 + CONV_WIDTH:c0 + 2 * CONV_WIDTH]
    hv = proj[:, c0 + 2 * CONV_WIDTH:]

    y = _s5_mixer(u, seq_start, minw_ref, mcat_ref, coef_ref, u_s, y_s, h_s) + dskip_ref[...] * u
    z = jax.nn.gelu(y)
    glu = jnp.dot(z.astype(BF16), wglu_ref[...], preferred_element_type=F32) + bglu_ref[...]
    y_ssm = z * jax.nn.sigmoid(glu)

    tail = POOL_TAIL
    pbuf_ref[0, 0:tail, :] = jnp.where(seq_start, 0.0, pbuf_ref[0, TILE_T:TILE_T + tail, :])
    pbuf_ref[0, tail:, :] = p
    sums = []
    for stage, w in enumerate(POOL_WINDOWS):
        src = 0 if stage == 0 else 1 + (stage - 1) % 2
        dst = 1 + stage % 2
        d = w // 2
        lo = SUBLANES * (stage + 1)
        n = tail + TILE_T - lo
        e = pbuf_ref[src, pl.ds(lo, n), :] + pbuf_ref[src, pl.ds(lo - d, n), :]
        if stage + 1 < len(POOL_WINDOWS):
            pbuf_ref[dst, pl.ds(lo, n), :] = e
        sums.append(e[tail - lo:, :])
    lane_p = lax.broadcasted_iota(I32, (TILE_T, POOL_WIDTH), 1)
    total = sums[-1]
    win = jnp.full((TILE_T, POOL_WIDTH), POOL_WINDOWS[-1], I32)
    for i in range(len(POOL_WINDOWS) - 2, -1, -1):
        sel = lane_p < (i + 1) * POOL_GROUP
        total = jnp.where(sel, sums[i], total)
        win = jnp.where(sel, POOL_WINDOWS[i], win)
    row_p = lax.broadcasted_iota(I32, (TILE_T, POOL_WIDTH), 0) + l * TILE_T
    count = jnp.minimum(row_p + 1, win).astype(F32)
    pooled = total / count - p
    mixed = jnp.dot(pooled.astype(BF16), poolw_ref[...], preferred_element_type=F32)
    y_pool = mixed * pscale_ref[...]

    zc = cg * hv
    zbuf_ref[0:8, :] = jnp.where(seq_start, 0.0, zbuf_ref[TILE_T:TILE_T + 8, :])
    zbuf_ref[8:, :] = zc
    yc = (convw_ref[0:1, :] * zbuf_ref[pl.ds(6, TILE_T), :]
          + convw_ref[1:2, :] * zbuf_ref[pl.ds(7, TILE_T), :]
          + convw_ref[2:3, :] * zc)
    y_conv = bg * yc

    gbr = gbr_ref[...]
    s1 = SSM_WIDTH + POOL_WIDTH
    mixed_all = jnp.concatenate([_rms(y_ssm, gbr[:, :SSM_WIDTH]).astype(BF16),
                                 _rms(y_pool, gbr[:, SSM_WIDTH:s1]).astype(BF16),
                                 _rms(y_conv, gbr[:, s1:]).astype(BF16)], axis=1)
    x1 = x + jnp.dot(mixed_all, wout_ref[...], preferred_element_type=F32)
    x1_ref[...] = x1

    hn2 = _rms(x1, gffn_ref[...])
    packed = _pack_rows(hn2)
    hp_ref[0] = packed[:, :SC_ROW]
    hp_ref[1] = packed[:, SC_ROW:]
    logits = jnp.dot(hn2.astype(BF16), rw_ref[...], preferred_element_type=F32) + rb_ref[...]
    work = logits.T[:N_EXPERTS, :]
    eio = lax.broadcasted_iota(I32, (N_EXPERTS, TILE_T), 0).astype(F32)
    neg = jnp.float32(-jnp.inf)
    vals, idxs = [], []
    for _ in range(TOP_K):
        m = jnp.max(work, axis=0, keepdims=True)
        idx = jnp.min(jnp.where(work == m, eio, float(N_EXPERTS)), axis=0, keepdims=True)
        vals.append(m)
        idxs.append(idx)
        work = jnp.where(eio == idx, neg, work)
    exps = [jnp.exp(v - vals[0]) for v in vals]
    denom = exps[0] + exps[1] + exps[2] + exps[3]
    gates = [e / denom for e in exps]

    onehot = jnp.zeros((N_EXPERTS, TILE_T), F32)
    for idx in idxs:
        onehot = jnp.where(eio == idx, 1.0, onehot)
    before = jnp.dot(onehot.astype(BF16), utri_ref[...], preferred_element_type=F32) + cntacc_ref[:, 0:1]
    ranks = [jnp.sum(jnp.where(eio == idx, before, 0.0), axis=0, keepdims=True) for idx in idxs]
    row = lax.broadcasted_iota(I32, (LANES, TILE_T), 0)
    slab_t = jnp.zeros((LANES, TILE_T), F32)
    for k in range(TOP_K):
        slab_t = jnp.where(row == k, idxs[k], slab_t)
        slab_t = jnp.where(row == TOP_K + k, gates[k], slab_t)
        slab_t = jnp.where(row == 2 * TOP_K + k, ranks[k], slab_t)
    slabt_ref[...] = slab_t[:2 * SUBLANES, :]
    slab_ref[...] = slab_t.T
    newcnt = cntacc_ref[:, 0:1] + jnp.sum(onehot, axis=1, keepdims=True)
    cntacc_ref[...] = jnp.broadcast_to(newcnt, cntacc_ref.shape)
    cnt_ref[...] = jnp.broadcast_to(newcnt, cnt_ref.shape)


def _layer_spec(shape, layer):
    nd = len(shape) - 1
    sel = layer if shape[0] > 1 else 0
    return pl.BlockSpec((None,) + tuple(shape[1:]), lambda b, l, nd=nd, sel=sel: (sel,) + (0,) * nd,
                        pipeline_mode=pl.Buffered(1))


def _mixer_call(x, x_tile0, yprev, slabprev, lp, layer, batch, seq):
    first_layer = yprev is None
    n_tok = batch * seq
    steps = seq // TILE_T
    tok_spec = lambda w: pl.BlockSpec((TILE_T, w), lambda b, l: (b * steps + l, 0))
    in_specs = [pl.BlockSpec((TILE_T, D_MODEL), lambda b, l: (x_tile0 + b * steps + l, 0))]
    args = [x]
    if not first_layer:
        in_specs += [pl.BlockSpec((TOP_K, 2, TILE_T, SC_ROW), lambda b, l: (0, 0, b * steps + l, 0)), tok_spec(LANES)]
        args += [yprev, slabprev]
    weights = [lp['gmix'], lp['win'], lp['minw'], lp['mcat'], lp['coef'], lp['dskip'], lp['wglu'], lp['bglu'],
               lp['poolw'], lp['pscale'], lp['convw'], lp['gbr'], lp['wout'], lp['gffn'], lp['rw'], lp['rb'],
               lp['utri']]
    in_specs += [_layer_spec(w.shape, layer) for w in weights]
    args += weights
    out_shape = [jax.ShapeDtypeStruct((n_tok, D_MODEL), F32),
                 jax.ShapeDtypeStruct((2, n_tok, SC_ROW), I32),
                 jax.ShapeDtypeStruct((n_tok, LANES), F32),
                 jax.ShapeDtypeStruct((2 * SUBLANES, n_tok), F32),
                 jax.ShapeDtypeStruct((N_EXPERTS, LANES), F32)]
    out_specs = [tok_spec(D_MODEL), pl.BlockSpec((2, TILE_T, SC_ROW), lambda b, l: (0, b * steps + l, 0)), tok_spec(LANES),
                 pl.BlockSpec((2 * SUBLANES, TILE_T), lambda b, l: (0, b * steps + l)),
                 pl.BlockSpec((N_EXPERTS, LANES), lambda b, l: (0, 0))]
    scratch = [pltpu.VMEM((SSM_WIDTH // LANES, TILE_T, LANES), F32),
               pltpu.VMEM((SSM_WIDTH // LANES, TILE_T, LANES), F32),
               pltpu.VMEM((SUBLANES + N_CHUNK, SSM_GROUPS * LANES), F32),
               pltpu.VMEM((3, POOL_TAIL + TILE_T, POOL_WIDTH), F32),
               pltpu.VMEM((8 + TILE_T, CONV_WIDTH), F32),
               pltpu.VMEM((N_EXPERTS, LANES), F32)]
    return pl.pallas_call(
        functools.partial(_mixer_kernel, first_layer=first_layer),
        grid=(batch, steps),
        in_specs=in_specs, out_specs=out_specs, out_shape=out_shape, scratch_shapes=scratch,
        compiler_params=pltpu.CompilerParams(dimension_semantics=("arbitrary", "arbitrary"),
                                             vmem_limit_bytes=VMEM_LIMIT),
        name="mixer_first" if first_layer else "mixer_next",
    )(*args)


def _route_kernel(slabt_ref, cnt_ref, dest_ref, meta_ref, *, n_steps):
    lane1 = lax.broadcasted_iota(I32, (1, LANES), 1)
    e_row = lax.broadcasted_iota(I32, (N_EXPERTS, LANES), 0)
    e_lane = lax.broadcasted_iota(I32, (N_EXPERTS, LANES), 1)
    cnt = cnt_ref[...].astype(I32)
    counts = jnp.sum(jnp.where(e_row == e_lane, cnt, 0), axis=0, keepdims=True)
    padded = ((counts + (STEP_ROWS - 1)) // STEP_ROWS) * STEP_ROWS
    pend = padded
    for sh in (1, 2, 4, 8, 16):
        pend = pend + jnp.where(lane1 >= sh, pltpu.roll(pend, sh, axis=1), 0)
    pstart = pend - padded

    @pl.when(pl.program_id(0) == 0)
    def _():
        rows = meta_ref.shape[0]
        bstart = lax.broadcasted_iota(I32, (rows, LANES), 0) * STEP_ROWS
        lane = lax.broadcasted_iota(I32, (rows, LANES), 1)
        done = jnp.where(jnp.logical_and(lane < N_EXPERTS, pend <= bstart), 1, 0)
        be = jnp.minimum(jnp.sum(done, axis=1, keepdims=True), N_EXPERTS - 1)
        mine = lane == be
        left = jnp.sum(jnp.where(mine, counts + pstart, 0), axis=1, keepdims=True) - bstart[:, 0:1]
        nblk = jnp.clip((left + (EXPERT_BLOCK - 1)) // EXPERT_BLOCK, 0, STEP_ROWS // EXPERT_BLOCK)
        total = jnp.sum(jnp.where(lane1 == N_EXPERTS - 1, pend, 0), axis=1, keepdims=True)
        row = lax.broadcasted_iota(I32, (rows, LANES), 0)
        table = jnp.where(lane == 0, be, jnp.where(lane == 1, nblk, 0))
        meta_ref[...] = jnp.where(row == n_steps, total // STEP_ROWS, table)

    pstart_col = jnp.sum(jnp.where(e_row == e_lane, pstart, 0), axis=1, keepdims=True)
    tokens = slabt_ref.shape[1]
    eio = lax.broadcasted_iota(I32, (N_EXPERTS, tokens), 0)
    row = lax.broadcasted_iota(I32, (SUBLANES, tokens), 0)
    out = jnp.zeros((SUBLANES, tokens), I32)
    for k in range(TOP_K):
        idx = slabt_ref[k:k + 1, :].astype(I32)
        rank = slabt_ref[2 * TOP_K + k:2 * TOP_K + k + 1, :].astype(I32)
        base = jnp.sum(jnp.where(eio == idx, pstart_col, 0), axis=0, keepdims=True)
        for h in range(2):
            out = jnp.where(row == 2 * k + h, base + rank + h * n_steps * STEP_ROWS, out)
    dest_ref[...] = out


def _route_call(slabt, counts, n_tok, n_steps):
    meta_rows = ((n_steps + 1 + SUBLANES - 1) // SUBLANES) * SUBLANES
    return pl.pallas_call(
        functools.partial(_route_kernel, n_steps=n_steps),
        grid=(n_tok // ROUTE_T,),
        in_specs=[pl.BlockSpec((2 * SUBLANES, ROUTE_T), lambda i: (0, i)),
                  pl.BlockSpec((N_EXPERTS, LANES), lambda i: (0, 0))],
        out_specs=[pl.BlockSpec((SUBLANES, ROUTE_T), lambda i: (0, i)),
                   pl.BlockSpec((meta_rows, LANES), lambda i: (0, 0))],
        out_shape=[jax.ShapeDtypeStruct((SUBLANES, n_tok), I32),
                   jax.ShapeDtypeStruct((meta_rows, LANES), I32)],
        compiler_params=pltpu.CompilerParams(dimension_semantics=("arbitrary",)),
        name="route",
    )(slabt, counts)


def _sc_mesh():
    return plsc.VectorSubcoreMesh(core_axis_name="core", subcore_axis_name="subcore")


def _dispatch(src, idx, n_rows):
    windows = src.shape[0] // SC_WINDOW

    @functools.partial(pl.kernel, out_type=jax.ShapeDtypeStruct((n_rows, SC_ROW), src.dtype),
                       mesh=_sc_mesh(), scratch_types=[])
    def scatter_rows(x_hbm, i_hbm, o_hbm):
        def body(x_vmem, i_vmem):
            pltpu.sync_copy(x_vmem, o_hbm.at[i_vmem.at[0]])

        pltpu.emit_pipeline(
            body,
            grid=(idx.shape[1] // SC_WINDOW,),
            in_specs=[pl.BlockSpec((SC_WINDOW, SC_ROW), lambda i: (lax.rem(i, windows), 0)),
                      pl.BlockSpec((1, SC_WINDOW), lambda i: (0, i))],
            out_specs=[],
            core_axis_name=("core", "subcore"),
            dimension_semantics=(pltpu.PARALLEL,),
        )(x_hbm, i_hbm)

    return scatter_rows(src, idx)


def _combine(src, idx):
    n_idx = idx.shape[1]

    @functools.partial(pl.kernel, out_type=jax.ShapeDtypeStruct((n_idx, SC_ROW), src.dtype),
                       mesh=_sc_mesh(), scratch_types=[])
    def gather_rows(y_hbm, i_hbm, o_hbm):
        def body(i_vmem, o_vmem):
            pltpu.sync_copy(y_hbm.at[i_vmem.at[0]], o_vmem)

        pltpu.emit_pipeline(
            body,
            grid=(n_idx // SC_WINDOW,),
            in_specs=[pl.BlockSpec((1, SC_WINDOW), lambda i: (0, i))],
            out_specs=[pl.BlockSpec((SC_WINDOW, SC_ROW), lambda i: (i, 0))],
            core_axis_name=("core", "subcore"),
            dimension_semantics=(pltpu.PARALLEL,),
        )(i_hbm, o_hbm)

    return gather_rows(src, idx)


def _expert_rows(xs_ref, ys_ref, row0, n, wgu_s, bgu_ref, wdn_s, bdn_ref, act_s):
    rows = slice(row0, row0 + n)
    arow = slice(0, n)
    w0 = xs_ref[0, rows, :]
    w1 = xs_ref[1, rows, :]
    xb = jnp.concatenate([_unpack_lo(w0).astype(BF16), _unpack_lo(w1).astype(BF16),
                          _unpack_hi(w0).astype(BF16), _unpack_hi(w1).astype(BF16)], axis=1)
    half = D_EXPERT // 2
    for j in range(2):
        cols = slice(j * half, (j + 1) * half)
        ucols = slice(D_EXPERT + j * half, D_EXPERT + (j + 1) * half)
        g = jnp.dot(xb, wgu_s[:, cols], preferred_element_type=F32) + bgu_ref[:, cols]
        up = jnp.dot(xb, wgu_s[:, ucols], preferred_element_type=F32) + bgu_ref[:, ucols]
        g = jnp.minimum(g, SWIGLU_LIMIT)
        up = jnp.clip(up, -SWIGLU_LIMIT, SWIGLU_LIMIT)
        act_s[arow, cols] = (g * jax.nn.sigmoid(SWIGLU_ALPHA * g) * (up + 1.0)).astype(BF16)
    y = jnp.dot(act_s[arow, :], wdn_s[...], preferred_element_type=F32) + bdn_ref[...]
    packed = _pack_rows(y)
    ys_ref[0, rows, :] = packed[:, :SC_ROW]
    ys_ref[1, rows, :] = packed[:, SC_ROW:]


def _expert_kernel(be_ref, nb_ref, nv_ref, xs_ref, wgu_ref, bgu_ref, wdn_ref, bdn_ref, ys_ref, wgu_s, wdn_s, act_s):
    blk = pl.program_id(0)
    prev = be_ref[jnp.maximum(blk - 1, 0)]
    changed = jnp.logical_or(blk == 0, be_ref[blk] != prev)
    valid = blk < nv_ref[0]

    @pl.when(jnp.logical_and(valid, changed))
    def _():
        chunk = 128

        def cast_gu(i, _):
            rows = pl.ds(pl.multiple_of(i * chunk, chunk), chunk)
            wgu_s[rows, :] = wgu_ref[rows, :].astype(BF16)
            return 0

        def cast_dn(i, _):
            rows = pl.ds(pl.multiple_of(i * chunk, chunk), chunk)
            wdn_s[rows, :] = wdn_ref[rows, :].astype(BF16)
            return 0

        lax.fori_loop(0, D_MODEL // chunk, cast_gu, 0)
        lax.fori_loop(0, D_EXPERT // chunk, cast_dn, 0)

    for nblk in range(1, STEP_ROWS // EXPERT_BLOCK + 1):
        @pl.when(jnp.logical_and(valid, nb_ref[blk] == nblk))
        def _(nblk=nblk):
            row0, left = 0, nblk
            while left:
                run = 1 << (left.bit_length() - 1)
                _expert_rows(xs_ref, ys_ref, row0, run * EXPERT_BLOCK, wgu_s, bgu_ref, wdn_s, bdn_ref, act_s)
                row0 += run * EXPERT_BLOCK
                left -= run


def _expert_call(xs, step_e, step_nblk, n_valid, w_gu, b_gu, w_dn, b_dn, layer, n_steps):
    def row_map(i, be, nb, nv):
        return (0, jnp.minimum(i, jnp.maximum(nv[0] - 1, 0)), 0)

    def w_map(i, be, nb, nv):
        return (layer, be[i], 0, 0)

    grid_spec = pltpu.PrefetchScalarGridSpec(
        num_scalar_prefetch=3,
        grid=(n_steps,),
        in_specs=[pl.BlockSpec((2, STEP_ROWS, SC_ROW), row_map),
                  pl.BlockSpec((None, None, D_MODEL, 2 * D_EXPERT), w_map),
                  pl.BlockSpec((None, None, 1, 2 * D_EXPERT), w_map),
                  pl.BlockSpec((None, None, D_EXPERT, D_MODEL), w_map),
                  pl.BlockSpec((None, None, 1, D_MODEL), w_map)],
        out_specs=pl.BlockSpec((2, STEP_ROWS, SC_ROW), row_map),
        scratch_shapes=[pltpu.VMEM((D_MODEL, 2 * D_EXPERT), BF16),
                        pltpu.VMEM((D_EXPERT, D_MODEL), BF16),
                        pltpu.VMEM((STEP_ROWS, D_EXPERT), BF16)])
    return pl.pallas_call(
        _expert_kernel,
        grid_spec=grid_spec,
        out_shape=jax.ShapeDtypeStruct((2, n_steps * STEP_ROWS, SC_ROW), I32),
        compiler_params=pltpu.CompilerParams(dimension_semantics=("arbitrary",),
                                             vmem_limit_bytes=VMEM_LIMIT),
        name="experts",
    )(step_e, step_nblk, n_valid, xs, w_gu, b_gu, w_dn, b_dn)


def _final_kernel(x_ref, y_ref, slab_ref, g_ref, *rest):
    o_ref = rest[-1]
    x = _moe_combine(x_ref[...], y_ref, slab_ref[...])
    o_ref[...] = _rms(x, g_ref[...])


def _final_call(x1, ysg, slab, g, out_row0, n_out, out_prev):
    t = FINAL_T
    out_tile0 = out_row0 // t
    in_specs = [pl.BlockSpec((t, D_MODEL), lambda i: (i, 0)),
                pl.BlockSpec((TOP_K, 2, t, SC_ROW), lambda i: (0, 0, i, 0)),
                pl.BlockSpec((t, LANES), lambda i: (i, 0)),
                pl.BlockSpec((1, D_MODEL), lambda i: (0, 0))]
    args = [x1, ysg, slab, g]
    aliases = {}
    if out_prev is not None:
        in_specs.append(pl.BlockSpec(memory_space=pl.ANY))
        args.append(out_prev)
        aliases = {4: 0}
    return pl.pallas_call(
        _final_kernel,
        grid=(x1.shape[0] // t,),
        in_specs=in_specs,
        out_specs=pl.BlockSpec((t, D_MODEL), lambda i: (out_tile0 + i, 0)),
        out_shape=jax.ShapeDtypeStruct((n_out, D_MODEL), F32),
        input_output_aliases=aliases,
        compiler_params=pltpu.CompilerParams(dimension_semantics=("arbitrary",)),
        name="final_norm",
    )(*args)


def _s5_tables(lam_re, lam_im, b_re, b_im, c_re, c_im, log_step):
    lr = jnp.minimum(lam_re.astype(F32), LAMBDA_RE_MAX)
    li = lam_im.astype(F32)
    step = jnp.exp(log_step.astype(F32))[:, None]
    mag = jnp.exp(lr * step)
    ar = mag * jnp.cos(li * step)
    ai = mag * jnp.sin(li * step)
    nr = ar - 1.0
    den = lr * lr + li * li
    kr = (nr * lr + ai * li) / den
    ki = (ai * lr - nr * li) / den
    bre = b_re.astype(F32)
    bim = b_im.astype(F32)
    bbar_r = kr[..., None] * bre - ki[..., None] * bim
    bbar_i = kr[..., None] * bim + ki[..., None] * bre
    cre = c_re.astype(F32)
    cim = c_im.astype(F32)

    def power(m):
        m = jnp.asarray(m, F32)[..., None, None]
        mg = jnp.exp(m * lr * step)
        return mg * jnp.cos(m * li * step), mg * jnp.sin(m * li * step)

    pw_r, pw_i = power(jnp.arange(CHUNK + 1))
    cp_r = cre[None] * pw_r[:, :, None, :] - cim[None] * pw_i[:, :, None, :]
    cp_i = cre[None] * pw_i[:, :, None, :] + cim[None] * pw_r[:, :, None, :]
    width = CHUNK * SSM_GROUP
    kcat = (jnp.einsum('tgap,gph->ghta', cp_r[:CHUNK], bbar_r)
            - jnp.einsum('tgap,gph->ghta', cp_i[:CHUNK], bbar_i)).astype(BF16)
    lag = jnp.arange(CHUNK)
    place = (lag[None, :, None] - lag[:, None, None] == lag[None, None, :]).astype(BF16)
    m_intra = jnp.einsum('stl,ghla->gshta', place, kcat,
                         preferred_element_type=F32).astype(BF16).reshape(SSM_GROUPS, width, width)
    mo_r = cp_r[1:].transpose(1, 3, 0, 2).reshape(SSM_GROUPS, SSM_STATE, width)
    mo_i = -cp_i[1:].transpose(1, 3, 0, 2).reshape(SSM_GROUPS, SSM_STATE, width)
    mcat = jnp.concatenate([m_intra, mo_r.astype(BF16), mo_i.astype(BF16)], axis=1)
    q_r = pw_r[CHUNK - 1::-1][:CHUNK][:, :, :, None]
    q_i = pw_i[CHUNK - 1::-1][:CHUNK][:, :, :, None]
    mn_r = q_r * bbar_r[None] - q_i * bbar_i[None]
    mn_i = q_r * bbar_i[None] + q_i * bbar_r[None]
    minw = jnp.concatenate([mn_r, mn_i], axis=2).transpose(1, 0, 3, 2).reshape(
        SSM_GROUPS, CHUNK * SSM_GROUP, 2 * SSM_STATE).astype(BF16)

    def packed(j):
        pr, pi = power(float(CHUNK * j))
        return (jnp.concatenate([pr, pr], axis=-1).reshape(1, SSM_GROUPS * LANES),
                jnp.concatenate([-pi, pi], axis=-1).reshape(1, SSM_GROUPS * LANES))

    rows = jnp.arange(SUBLANES)[:, None]
    kinds = []
    for d in (1, 2, 4):
        c1, c2 = packed(d)
        mask = (rows >= d).astype(F32)
        kinds += [mask * c1, mask * c2]
    pk = [packed(r + 1) for r in range(SUBLANES)]
    kinds += [jnp.concatenate([p[0] for p in pk], axis=0), jnp.concatenate([p[1] for p in pk], axis=0)]
    coef = jnp.stack(kinds, axis=0)
    return minw, mcat, coef


def _mixer_params(norm_mix, w_in, ssm_lam_re, ssm_lam_im, ssm_b_re, ssm_b_im, ssm_c_re, ssm_c_im, ssm_d,
                  ssm_log_step, ssm_w_glu, ssm_b_glu, pool_w, pool_scale, conv_w, branch_norm, w_out, norm_ffn,
                  router_w, router_b):
    depth = w_in.shape[0]
    minw, mcat, coef = jax.vmap(_s5_tables)(ssm_lam_re, ssm_lam_im, ssm_b_re, ssm_b_im, ssm_c_re, ssm_c_im,
                                            ssm_log_step)
    eye4 = jnp.eye(len(POOL_WINDOWS), dtype=F32)
    poolw = jnp.einsum('lgcd,gk->lgckd', pool_w.astype(F32), eye4).reshape(depth, POOL_WIDTH, POOL_WIDTH).astype(BF16)
    rw = jnp.pad(router_w.astype(F32), ((0, 0), (0, 0), (0, LANES - N_EXPERTS))).astype(BF16)
    rb = jnp.pad(router_b.astype(F32), ((0, 0), (0, LANES - N_EXPERTS))).reshape(depth, 1, LANES)
    row = lambda v: v.reshape(depth, 1, -1).astype(F32)
    utri = jnp.triu(jnp.ones((TILE_T, TILE_T), F32), 1).astype(BF16)
    return dict(gmix=row(norm_mix), win=w_in.astype(BF16), minw=minw, mcat=mcat, coef=coef, dskip=row(ssm_d),
                wglu=ssm_w_glu.astype(BF16), bglu=row(ssm_b_glu), poolw=poolw, pscale=row(pool_scale),
                convw=conv_w.astype(F32), gbr=row(branch_norm), wout=w_out.astype(BF16),
                gffn=row(norm_ffn), rw=rw, rb=rb, utri=utri[None])


def kernel(x, norm_mix, w_in, ssm_lam_re, ssm_lam_im, ssm_b_re, ssm_b_im, ssm_c_re, ssm_c_im, ssm_d, ssm_log_step, ssm_w_glu, ssm_b_glu, pool_w, pool_scale, conv_w, branch_norm, w_out, norm_ffn, router_w, router_b, w_gate_up, b_gate_up, w_down, b_down, final_norm):
    batch, seq, d_model = x.shape
    assert d_model == D_MODEL and seq % TILE_T == 0
    depth = w_in.shape[0]
    n_streams = 2 if batch % 2 == 0 else 1
    sb = batch // n_streams
    n_tok = sb * seq
    assert n_tok % ROUTE_T == 0 and n_tok % SC_WINDOW == 0
    n_steps = -(-(n_tok * TOP_K + N_EXPERTS * (STEP_ROWS - 1)) // STEP_ROWS)
    n_rows = n_steps * STEP_ROWS
    tiles = n_tok // TILE_T
    b_gu = b_gate_up.reshape(depth, N_EXPERTS, 1, 2 * D_EXPERT)
    b_dn = b_down.reshape(depth, N_EXPERTS, 1, D_MODEL)

    xin = [x.reshape(batch * seq, D_MODEL)] * n_streams
    tile0 = [s * tiles for s in range(n_streams)]
    ysg = [None] * n_streams
    slab = [None] * n_streams
    lp = _mixer_params(norm_mix, w_in, ssm_lam_re, ssm_lam_im, ssm_b_re, ssm_b_im, ssm_c_re, ssm_c_im, ssm_d,
                       ssm_log_step, ssm_w_glu, ssm_b_glu, pool_w, pool_scale, conv_w, branch_norm, w_out,
                       norm_ffn, router_w, router_b)
    for l in range(depth):
        routed = []
        for s in range(n_streams):
            xin[s], hp, slab[s], slabt, counts = _mixer_call(xin[s], tile0[s], ysg[s], slab[s], lp, l, sb, seq)
            tile0[s] = 0
            dest, meta = _route_call(slabt, counts, n_tok, n_steps)
            idx = dest.reshape(1, 2 * TOP_K * n_tok)
            xs = _dispatch(hp.reshape(2 * n_tok, SC_ROW), idx, 2 * n_rows).reshape(2, n_rows, SC_ROW)
            routed.append((xs, idx, meta[:n_steps, 0], meta[:n_steps, 1], meta[n_steps:n_steps + 1, 0]))
        for s, (xs, idx, step_e, step_nblk, n_valid) in enumerate(routed):
            ys = _expert_call(xs, step_e, step_nblk, n_valid, w_gate_up, b_gu, w_down, b_dn, l, n_steps)
            ysg[s] = _combine(ys.reshape(2 * n_rows, SC_ROW), idx).reshape(TOP_K, 2, n_tok, SC_ROW)
    g = final_norm.reshape(1, D_MODEL).astype(F32)
    out = None
    for s in range(n_streams):
        out = _final_call(xin[s], ysg[s], slab[s], g, s * n_tok, batch * seq, out)
    return out.reshape(batch, seq, D_MODEL)
```

```python
import functools

import jax
import jax.numpy as jnp
from jax import lax
from jax.experimental import pallas as pl
from jax.experimental.pallas import tpu as pltpu
from jax.experimental.pallas import tpu_sc as plsc

F32 = jnp.float32
BF16 = jnp.bfloat16
I32 = jnp.int32

D_MODEL = 1024
SSM_WIDTH = 512
POOL_WIDTH = 256
CONV_WIDTH = 256
SSM_GROUP = 16
SSM_GROUPS = 32
SSM_STATE = 64
LAMBDA_RE_MAX = -1e-4
POOL_WINDOWS = (2, 4, 8, 16)
POOL_GROUP = 64
N_EXPERTS = 32
TOP_K = 4
D_EXPERT = 1024
SWIGLU_LIMIT = 7.0
SWIGLU_ALPHA = 1.702
EXPERT_BLOCK = 256
STEP_ROWS = 1024
NORM_EPS = 1e-5

LANES = 128
SUBLANES = 8
TILE_T = 512
FINAL_T = 512
CHUNK = 16
N_CHUNK = TILE_T // CHUNK
POOL_TAIL = SUBLANES * len(POOL_WINDOWS)
assert POOL_WINDOWS == tuple(2 ** (k + 1) for k in range(len(POOL_WINDOWS))) and POOL_TAIL >= POOL_WINDOWS[-1]
PACKED = D_MODEL // 2
ROUTE_T = 2048
SC_WINDOW = 128
SC_ROW = PACKED // 2
HI_MASK = -65536
VMEM_LIMIT = 56 * 1024 * 1024


def _bf16_round(v):
    return v.astype(BF16).astype(F32)


def _pack_rows(v):
    lo = lax.shift_right_logical(lax.bitcast_convert_type(_bf16_round(v[:, :PACKED]), I32), 16)
    hi = lax.bitcast_convert_type(_bf16_round(v[:, PACKED:]), I32) & HI_MASK
    return hi | lo


def _unpack_lo(w):
    return lax.bitcast_convert_type(lax.shift_left(w, 16), F32)


def _unpack_hi(w):
    return lax.bitcast_convert_type(w & HI_MASK, F32)


def _rms(v, g):
    r = lax.rsqrt(jnp.mean(v * v, axis=-1, keepdims=True) + NORM_EPS)
    return (v * r) * g


def _moe_combine(x, y_ref, slab):
    parts = [x[:, i * SC_ROW:(i + 1) * SC_ROW] for i in range(4)]
    for k in range(TOP_K):
        g = slab[:, TOP_K + k:TOP_K + k + 1]
        for h in range(2):
            w = y_ref[k, h]
            parts[h] = parts[h] + g * _unpack_lo(w)
            parts[2 + h] = parts[2 + h] + g * _unpack_hi(w)
    return jnp.concatenate(parts, axis=1)


def _block_transpose8(vs, lane):
    vs = list(vs)
    for dist in (4, 2, 1):
        width = SSM_GROUP * dist
        low = (lane % (2 * width)) < width
        for j in range(8):
            if j & dist:
                continue
            a, b = vs[j], vs[j + dist]
            vs[j] = jnp.where(low, a, pltpu.roll(b, width, axis=1))
            vs[j + dist] = jnp.where(low, pltpu.roll(a, LANES - width, axis=1), b)
    return vs


def _cmul_packed(c1, c2, v):
    return c1 * v + c2 * pltpu.roll(v, SSM_STATE, axis=1)


def _chunk_scan(s, carry_in, cf):
    outs = []
    carry = jnp.broadcast_to(carry_in, (SUBLANES, LANES))
    for rg in range(N_CHUNK // SUBLANES):
        x = s[rg * SUBLANES:(rg + 1) * SUBLANES]
        for j, d in enumerate((1, 2, 4)):
            x = x + _cmul_packed(cf[2 * j], cf[2 * j + 1], pltpu.roll(x, d, axis=0))
        x = x + _cmul_packed(cf[6], cf[7], carry)
        outs.append(x)
        carry = jnp.broadcast_to(x[SUBLANES - 1:SUBLANES], (SUBLANES, LANES))
    return jnp.concatenate(outs, axis=0)


def _s5_mixer(u, first, minw_ref, mcat_ref, coef_ref, u_s, y_s, h_s):
    blocks = SSM_WIDTH // LANES
    for b4 in range(blocks):
        u_s[b4] = u[:, b4 * LANES:(b4 + 1) * LANES]
    lane = lax.broadcasted_iota(I32, (N_CHUNK, LANES), 1)

    halves = [[None, None] for _ in range(SSM_GROUPS)]
    for b4 in range(blocks):
        for hh in range(2):
            xs = [u_s[b4, pl.ds(8 * hh + j, N_CHUNK, stride=CHUNK), :].astype(BF16) for j in range(8)]
            ws = _block_transpose8(xs, lane)
            for gl in range(8):
                halves[8 * b4 + gl][hh] = ws[gl]
    ug = [jnp.concatenate(h2, axis=1) for h2 in halves]

    carry_in = jnp.where(first, 0.0, h_s[N_CHUNK + 7:N_CHUNK + 8, :])
    h_s[7:8, :] = carry_in
    for g in range(SSM_GROUPS):
        cols = slice(g * LANES, (g + 1) * LANES)
        s_g = jnp.dot(ug[g], minw_ref[g], preferred_element_type=F32)
        cf = [coef_ref[k, :, cols] for k in range(8)]
        h_s[8:8 + N_CHUNK, cols] = _chunk_scan(s_g, carry_in[:, cols], cf)

    yg = []
    for g in range(SSM_GROUPS):
        hprev = h_s[7:7 + N_CHUNK, g * LANES:(g + 1) * LANES].astype(BF16)
        lhs = jnp.concatenate([ug[g], hprev], axis=1)
        yg.append(jnp.dot(lhs, mcat_ref[g], preferred_element_type=F32).astype(BF16))

    for b4 in range(blocks):
        for hh in range(2):
            ws = [yg[8 * b4 + gl][:, hh * LANES:(hh + 1) * LANES] for gl in range(8)]
            xs = _block_transpose8(ws, lane)
            for j in range(8):
                y_s[b4, pl.ds(8 * hh + j, N_CHUNK, stride=CHUNK), :] = xs[j].astype(F32)
    return jnp.concatenate([y_s[b4] for b4 in range(blocks)], axis=1)


def _mixer_kernel(*refs, first_layer):
    if first_layer:
        x_ref = refs[0]
        refs = refs[1:]
    else:
        x_ref, yprev_ref, slabprev_ref = refs[:3]
        refs = refs[3:]
    (gmix_ref, win_ref, minw_ref, mcat_ref, coef_ref, dskip_ref, wglu_ref, bglu_ref, poolw_ref, pscale_ref,
     convw_ref, gbr_ref, wout_ref, gffn_ref, rw_ref, rb_ref, utri_ref,
     x1_ref, hp_ref, slab_ref, slabt_ref, cnt_ref,
     u_s, y_s, h_s, pbuf_ref, zbuf_ref, cntacc_ref) = refs

    b = pl.program_id(0)
    l = pl.program_id(1)
    seq_start = l == 0

    @pl.when(jnp.logical_and(b == 0, l == 0))
    def _():
        cntacc_ref[...] = jnp.zeros_like(cntacc_ref)

    if first_layer:
        x = x_ref[...]
    else:
        x = _moe_combine(x_ref[...], yprev_ref, slabprev_ref[...])

    hn = _rms(x, gmix_ref[...]).astype(BF16)
    proj = jnp.dot(hn, win_ref[...], preferred_element_type=F32)
    u = proj[:, :SSM_WIDTH]
    p = proj[:, SSM_WIDTH:SSM_WIDTH + POOL_WIDTH]
    c0 = SSM_WIDTH + POOL_WIDTH
    bg = proj[:, c0:c0 + CONV_WIDTH]
    cg = proj[:, c0 + CONV_WIDTH:c0 + 2 * CONV_WIDTH]
    hv = proj[:, c0 + 2 * CONV_WIDTH:]

    y = _s5_mixer(u, seq_start, minw_ref, mcat_ref, coef_ref, u_s, y_s, h_s) + dskip_ref[...] * u
    z = jax.nn.gelu(y)
    glu = jnp.dot(z.astype(BF16), wglu_ref[...], preferred_element_type=F32) + bglu_ref[...]
    y_ssm = z * jax.nn.sigmoid(glu)

    tail = POOL_TAIL
    pbuf_ref[0, 0:tail, :] = jnp.where(seq_start, 0.0, pbuf_ref[0, TILE_T:TILE_T + tail, :])
    pbuf_ref[0, tail:, :] = p
    sums = []
    for stage, w in enumerate(POOL_WINDOWS):
        src = 0 if stage == 0 else 1 + (stage - 1) % 2
        dst = 1 + stage % 2
        d = w // 2
        lo = SUBLANES * (stage + 1)
        n = tail + TILE_T - lo
        e = pbuf_ref[src, pl.ds(lo, n), :] + pbuf_ref[src, pl.ds(lo - d, n), :]
        if stage + 1 < len(POOL_WINDOWS):
            pbuf_ref[dst, pl.ds(lo, n), :] = e
        sums.append(e[tail - lo:, :])
    lane_p = lax.broadcasted_iota(I32, (TILE_T, POOL_WIDTH), 1)
    total = sums[-1]
    win = jnp.full((TILE_T, POOL_WIDTH), POOL_WINDOWS[-1], I32)
    for i in range(len(POOL_WINDOWS) - 2, -1, -1):
        sel = lane_p < (i + 1) * POOL_GROUP
        total = jnp.where(sel, sums[i], total)
        win = jnp.where(sel, POOL_WINDOWS[i], win)
    row_p = lax.broadcasted_iota(I32, (TILE_T, POOL_WIDTH), 0) + l * TILE_T
    count = jnp.minimum(row_p + 1, win).astype(F32)
    pooled = total / count - p
    mixed = jnp.dot(pooled.astype(BF16), poolw_ref[...], preferred_element_type=F32)
    y_pool = mixed * pscale_ref[...]

    zc = cg * hv
    zbuf_ref[0:8, :] = jnp.where(seq_start, 0.0, zbuf_ref[TILE_T:TILE_T + 8, :])
    zbuf_ref[8:, :] = zc
    yc = (convw_ref[0:1, :] * zbuf_ref[pl.ds(6, TILE_T), :]
          + convw_ref[1:2, :] * zbuf_ref[pl.ds(7, TILE_T), :]
          + convw_ref[2:3, :] * zc)
    y_conv = bg * yc

    gbr = gbr_ref[...]
    s1 = SSM_WIDTH + POOL_WIDTH
    mixed_all = jnp.concatenate([_rms(y_ssm, gbr[:, :SSM_WIDTH]).astype(BF16),
                                 _rms(y_pool, gbr[:, SSM_WIDTH:s1]).astype(BF16),
                                 _rms(y_conv, gbr[:, s1:]).astype(BF16)], axis=1)
    x1 = x + jnp.dot(mixed_all, wout_ref[...], preferred_element_type=F32)
    x1_ref[...] = x1

    hn2 = _rms(x1, gffn_ref[...])
    packed = _pack_rows(hn2)
    hp_ref[0] = packed[:, :SC_ROW]
    hp_ref[1] = packed[:, SC_ROW:]
    logits = jnp.dot(hn2.astype(BF16), rw_ref[...], preferred_element_type=F32) + rb_ref[...]
    work = logits.T[:N_EXPERTS, :]
    eio = lax.broadcasted_iota(I32, (N_EXPERTS, TILE_T), 0).astype(F32)
    neg = jnp.float32(-jnp.inf)
    vals, idxs = [], []
    for _ in range(TOP_K):
        m = jnp.max(work, axis=0, keepdims=True)
        idx = jnp.min(jnp.where(work == m, eio, float(N_EXPERTS)), axis=0, keepdims=True)
        vals.append(m)
        idxs.append(idx)
        work = jnp.where(eio == idx, neg, work)
    exps = [jnp.exp(v - vals[0]) for v in vals]
    denom = exps[0] + exps[1] + exps[2] + exps[3]
    gates = [e / denom for e in exps]

    onehot = jnp.zeros((N_EXPERTS, TILE_T), F32)
    for idx in idxs:
        onehot = jnp.where(eio == idx, 1.0, onehot)
    before = jnp.dot(onehot.astype(BF16), utri_ref[...], preferred_element_type=F32) + cntacc_ref[:, 0:1]
    ranks = [jnp.sum(jnp.where(eio == idx, before, 0.0), axis=0, keepdims=True) for idx in idxs]
    row = lax.broadcasted_iota(I32, (LANES, TILE_T), 0)
    slab_t = jnp.zeros((LANES, TILE_T), F32)
    for k in range(TOP_K):
        slab_t = jnp.where(row == k, idxs[k], slab_t)
        slab_t = jnp.where(row == TOP_K + k, gates[k], slab_t)
        slab_t = jnp.where(row == 2 * TOP_K + k, ranks[k], slab_t)
    slabt_ref[...] = slab_t[:2 * SUBLANES, :]
    slab_ref[...] = slab_t.T
    newcnt = cntacc_ref[:, 0:1] + jnp.sum(onehot, axis=1, keepdims=True)
    cntacc_ref[...] = jnp.broadcast_to(newcnt, cntacc_ref.shape)
    cnt_ref[...] = jnp.broadcast_to(newcnt, cnt_ref.shape)


def _layer_spec(shape, layer):
    nd = len(shape) - 1
    sel = layer if shape[0] > 1 else 0
    return pl.BlockSpec((None,) + tuple(shape[1:]), lambda b, l, nd=nd, sel=sel: (sel,) + (0,) * nd,
                        pipeline_mode=pl.Buffered(1))


def _mixer_call(x, x_tile0, yprev, slabprev, lp, layer, batch, seq):
    first_layer = yprev is None
    n_tok = batch * seq
    steps = seq // TILE_T
    tok_spec = lambda w: pl.BlockSpec((TILE_T, w), lambda b, l: (b * steps + l, 0))
    in_specs = [pl.BlockSpec((TILE_T, D_MODEL), lambda b, l: (x_tile0 + b * steps + l, 0))]
    args = [x]
    if not first_layer:
        in_specs += [pl.BlockSpec((TOP_K, 2, TILE_T, SC_ROW), lambda b, l: (0, 0, b * steps + l, 0)), tok_spec(LANES)]
        args += [yprev, slabprev]
    weights = [lp['gmix'], lp['win'], lp['minw'], lp['mcat'], lp['coef'], lp['dskip'], lp['wglu'], lp['bglu'],
               lp['poolw'], lp['pscale'], lp['convw'], lp['gbr'], lp['wout'], lp['gffn'], lp['rw'], lp['rb'],
               lp['utri']]
    in_specs += [_layer_spec(w.shape, layer) for w in weights]
    args += weights
    out_shape = [jax.ShapeDtypeStruct((n_tok, D_MODEL), F32),
                 jax.ShapeDtypeStruct((2, n_tok, SC_ROW), I32),
                 jax.ShapeDtypeStruct((n_tok, LANES), F32),
                 jax.ShapeDtypeStruct((2 * SUBLANES, n_tok), F32),
                 jax.ShapeDtypeStruct((N_EXPERTS, LANES), F32)]
    out_specs = [tok_spec(D_MODEL), pl.BlockSpec((2, TILE_T, SC_ROW), lambda b, l: (0, b * steps + l, 0)), tok_spec(LANES),
                 pl.BlockSpec((2 * SUBLANES, TILE_T), lambda b, l: (0, b * steps + l)),
                 pl.BlockSpec((N_EXPERTS, LANES), lambda b, l: (0, 0))]
    scratch = [pltpu.VMEM((SSM_WIDTH // LANES, TILE_T, LANES), F32),
               pltpu.VMEM((SSM_WIDTH // LANES, TILE_T, LANES), F32),
               pltpu.VMEM((SUBLANES + N_CHUNK, SSM_GROUPS * LANES), F32),
               pltpu.VMEM((3, POOL_TAIL + TILE_T, POOL_WIDTH), F32),
               pltpu.VMEM((8 + TILE_T, CONV_WIDTH), F32),
               pltpu.VMEM((N_EXPERTS, LANES), F32)]
    return pl.pallas_call(
        functools.partial(_mixer_kernel, first_layer=first_layer),
        grid=(batch, steps),
        in_specs=in_specs, out_specs=out_specs, out_shape=out_shape, scratch_shapes=scratch,
        compiler_params=pltpu.CompilerParams(dimension_semantics=("arbitrary", "arbitrary"),
                                             vmem_limit_bytes=VMEM_LIMIT),
        name="mixer_first" if first_layer else "mixer_next",
    )(*args)


def _route_kernel(slabt_ref, cnt_ref, dest_ref, meta_ref, *, n_steps):
    lane1 = lax.broadcasted_iota(I32, (1, LANES), 1)
    e_row = lax.broadcasted_iota(I32, (N_EXPERTS, LANES), 0)
    e_lane = lax.broadcasted_iota(I32, (N_EXPERTS, LANES), 1)
    cnt = cnt_ref[...].astype(I32)
    counts = jnp.sum(jnp.where(e_row == e_lane, cnt, 0), axis=0, keepdims=True)
    padded = ((counts + (STEP_ROWS - 1)) // STEP_ROWS) * STEP_ROWS
    pend = padded
    for sh in (1, 2, 4, 8, 16):
        pend = pend + jnp.where(lane1 >= sh, pltpu.roll(pend, sh, axis=1), 0)

    @pl.when(pl.program_id(0) == 0)
    def _():
        rows = meta_ref.shape[0]
        bstart = lax.broadcasted_iota(I32, (rows, LANES), 0) * STEP_ROWS
        lane = lax.broadcasted_iota(I32, (rows, LANES), 1)
        done = jnp.where(jnp.logical_and(lane < N_EXPERTS, pend <= bstart), 1, 0)
        be = jnp.minimum(jnp.sum(done, axis=1, keepdims=True), N_EXPERTS - 1)
        mine = lane == be
        first_row = jnp.sum(jnp.where(mine, pend - counts, 0), axis=1, keepdims=True)
        used = bstart[:, 0:1] + STEP_ROWS - jnp.maximum(bstart[:, 0:1], first_row)
        nblk = jnp.clip((used + (EXPERT_BLOCK - 1)) // EXPERT_BLOCK, 0, STEP_ROWS // EXPERT_BLOCK)
        total = jnp.sum(jnp.where(lane1 == N_EXPERTS - 1, pend, 0), axis=1, keepdims=True)
        row = lax.broadcasted_iota(I32, (rows, LANES), 0)
        table = jnp.where(lane == 0, be, jnp.where(lane == 1, nblk, 0))
        meta_ref[...] = jnp.where(row == n_steps, total // STEP_ROWS, table)

    last_col = jnp.sum(jnp.where(e_row == e_lane, pend - 1, 0), axis=1, keepdims=True)
    tokens = slabt_ref.shape[1]
    eio = lax.broadcasted_iota(I32, (N_EXPERTS, tokens), 0)
    row = lax.broadcasted_iota(I32, (SUBLANES, tokens), 0)
    out = jnp.zeros((SUBLANES, tokens), I32)
    for k in range(TOP_K):
        idx = slabt_ref[k:k + 1, :].astype(I32)
        rank = slabt_ref[2 * TOP_K + k:2 * TOP_K + k + 1, :].astype(I32)
        base = jnp.sum(jnp.where(eio == idx, last_col, 0), axis=0, keepdims=True)
        for h in range(2):
            out = jnp.where(row == 2 * k + h, base - rank + h * n_steps * STEP_ROWS, out)
    dest_ref[...] = out


def _route_call(slabt, counts, n_tok, n_steps):
    meta_rows = ((n_steps + 1 + SUBLANES - 1) // SUBLANES) * SUBLANES
    return pl.pallas_call(
        functools.partial(_route_kernel, n_steps=n_steps),
        grid=(n_tok // ROUTE_T,),
        in_specs=[pl.BlockSpec((2 * SUBLANES, ROUTE_T), lambda i: (0, i)),
                  pl.BlockSpec((N_EXPERTS, LANES), lambda i: (0, 0))],
        out_specs=[pl.BlockSpec((SUBLANES, ROUTE_T), lambda i: (0, i)),
                   pl.BlockSpec((meta_rows, LANES), lambda i: (0, 0))],
        out_shape=[jax.ShapeDtypeStruct((SUBLANES, n_tok), I32),
                   jax.ShapeDtypeStruct((meta_rows, LANES), I32)],
        compiler_params=pltpu.CompilerParams(dimension_semantics=("arbitrary",)),
        name="route",
    )(slabt, counts)


def _sc_mesh():
    return plsc.VectorSubcoreMesh(core_axis_name="core", subcore_axis_name="subcore")


def _dispatch(src, idx, n_rows):
    windows = src.shape[0] // SC_WINDOW

    @functools.partial(pl.kernel, out_type=jax.ShapeDtypeStruct((n_rows, SC_ROW), src.dtype),
                       mesh=_sc_mesh(), scratch_types=[])
    def scatter_rows(x_hbm, i_hbm, o_hbm):
        def body(x_vmem, i_vmem):
            pltpu.sync_copy(x_vmem, o_hbm.at[i_vmem.at[0]])

        pltpu.emit_pipeline(
            body,
            grid=(idx.shape[1] // SC_WINDOW,),
            in_specs=[pl.BlockSpec((SC_WINDOW, SC_ROW), lambda i: (lax.rem(i, windows), 0)),
                      pl.BlockSpec((1, SC_WINDOW), lambda i: (0, i))],
            out_specs=[],
            core_axis_name=("core", "subcore"),
            dimension_semantics=(pltpu.PARALLEL,),
        )(x_hbm, i_hbm)

    return scatter_rows(src, idx)


def _combine(src, idx):
    n_idx = idx.shape[1]

    @functools.partial(pl.kernel, out_type=jax.ShapeDtypeStruct((n_idx, SC_ROW), src.dtype),
                       mesh=_sc_mesh(), scratch_types=[])
    def gather_rows(y_hbm, i_hbm, o_hbm):
        def body(i_vmem, o_vmem):
            pltpu.sync_copy(y_hbm.at[i_vmem.at[0]], o_vmem)

        pltpu.emit_pipeline(
            body,
            grid=(n_idx // SC_WINDOW,),
            in_specs=[pl.BlockSpec((1, SC_WINDOW), lambda i: (0, i))],
            out_specs=[pl.BlockSpec((SC_WINDOW, SC_ROW), lambda i: (i, 0))],
            core_axis_name=("core", "subcore"),
            dimension_semantics=(pltpu.PARALLEL,),
        )(i_hbm, o_hbm)

    return gather_rows(src, idx)


def _expert_rows(xs_ref, ys_ref, row0, n, wgu_s, bgu_ref, wdn_s, bdn_ref, act_s):
    rows = slice(row0, row0 + n)
    arow = slice(0, n)
    w0 = xs_ref[0, rows, :]
    w1 = xs_ref[1, rows, :]
    xb = jnp.concatenate([_unpack_lo(w0).astype(BF16), _unpack_lo(w1).astype(BF16),
                          _unpack_hi(w0).astype(BF16), _unpack_hi(w1).astype(BF16)], axis=1)
    half = D_EXPERT // 2
    for j in range(2):
        cols = slice(j * half, (j + 1) * half)
        ucols = slice(D_EXPERT + j * half, D_EXPERT + (j + 1) * half)
        g = jnp.dot(xb, wgu_s[:, cols], preferred_element_type=F32) + bgu_ref[:, cols]
        up = jnp.dot(xb, wgu_s[:, ucols], preferred_element_type=F32) + bgu_ref[:, ucols]
        g = jnp.minimum(g, SWIGLU_LIMIT)
        up = jnp.clip(up, -SWIGLU_LIMIT, SWIGLU_LIMIT)
        act_s[arow, cols] = (g * jax.nn.sigmoid(SWIGLU_ALPHA * g) * (up + 1.0)).astype(BF16)
    y = jnp.dot(act_s[arow, :], wdn_s[...], preferred_element_type=F32) + bdn_ref[...]
    packed = _pack_rows(y)
    ys_ref[0, rows, :] = packed[:, :SC_ROW]
    ys_ref[1, rows, :] = packed[:, SC_ROW:]


def _expert_kernel(be_ref, nb_ref, nv_ref, xs_ref, wgu_ref, bgu_ref, wdn_ref, bdn_ref, ys_ref, wgu_s, wdn_s, act_s):
    blk = pl.program_id(0)
    prev = be_ref[jnp.maximum(blk - 1, 0)]
    changed = jnp.logical_or(blk == 0, be_ref[blk] != prev)
    valid = blk < nv_ref[0]

    @pl.when(jnp.logical_and(valid, changed))
    def _():
        chunk = 128

        def cast_gu(i, _):
            rows = pl.ds(pl.multiple_of(i * chunk, chunk), chunk)
            wgu_s[rows, :] = wgu_ref[rows, :].astype(BF16)
            return 0

        def cast_dn(i, _):
            rows = pl.ds(pl.multiple_of(i * chunk, chunk), chunk)
            wdn_s[rows, :] = wdn_ref[rows, :].astype(BF16)
            return 0

        lax.fori_loop(0, D_MODEL // chunk, cast_gu, 0)
        lax.fori_loop(0, D_EXPERT // chunk, cast_dn, 0)

    for nblk in range(1, STEP_ROWS // EXPERT_BLOCK + 1):
        @pl.when(jnp.logical_and(valid, nb_ref[blk] == nblk))
        def _(nblk=nblk):
            row0, left = STEP_ROWS - nblk * EXPERT_BLOCK, nblk
            while left:
                run = 1 << (left.bit_length() - 1)
                _expert_rows(xs_ref, ys_ref, row0, run * EXPERT_BLOCK, wgu_s, bgu_ref, wdn_s, bdn_ref, act_s)
                row0 += run * EXPERT_BLOCK
                left -= run


def _expert_call(xs, step_e, step_nblk, n_valid, w_gu, b_gu, w_dn, b_dn, layer, n_steps):
    def row_map(i, be, nb, nv):
        return (0, jnp.minimum(i, jnp.maximum(nv[0] - 1, 0)), 0)

    def w_map(i, be, nb, nv):
        return (layer, be[i], 0, 0)

    grid_spec = pltpu.PrefetchScalarGridSpec(
        num_scalar_prefetch=3,
        grid=(n_steps,),
        in_specs=[pl.BlockSpec((2, STEP_ROWS, SC_ROW), row_map),
                  pl.BlockSpec((None, None, D_MODEL, 2 * D_EXPERT), w_map),
                  pl.BlockSpec((None, None, 1, 2 * D_EXPERT), w_map),
                  pl.BlockSpec((None, None, D_EXPERT, D_MODEL), w_map),
                  pl.BlockSpec((None, None, 1, D_MODEL), w_map)],
        out_specs=pl.BlockSpec((2, STEP_ROWS, SC_ROW), row_map),
        scratch_shapes=[pltpu.VMEM((D_MODEL, 2 * D_EXPERT), BF16),
                        pltpu.VMEM((D_EXPERT, D_MODEL), BF16),
                        pltpu.VMEM((STEP_ROWS, D_EXPERT), BF16)])
    return pl.pallas_call(
        _expert_kernel,
        grid_spec=grid_spec,
        out_shape=jax.ShapeDtypeStruct((2, n_steps * STEP_ROWS, SC_ROW), I32),
        compiler_params=pltpu.CompilerParams(dimension_semantics=("arbitrary",),
                                             vmem_limit_bytes=VMEM_LIMIT),
        name="experts",
    )(step_e, step_nblk, n_valid, xs, w_gu, b_gu, w_dn, b_dn)


def _final_kernel(x_ref, y_ref, slab_ref, g_ref, *rest):
    o_ref = rest[-1]
    x = _moe_combine(x_ref[...], y_ref, slab_ref[...])
    o_ref[...] = _rms(x, g_ref[...])


def _final_call(x1, ysg, slab, g, out_row0, n_out, out_prev):
    t = FINAL_T
    out_tile0 = out_row0 // t
    in_specs = [pl.BlockSpec((t, D_MODEL), lambda i: (i, 0)),
                pl.BlockSpec((TOP_K, 2, t, SC_ROW), lambda i: (0, 0, i, 0)),
                pl.BlockSpec((t, LANES), lambda i: (i, 0)),
                pl.BlockSpec((1, D_MODEL), lambda i: (0, 0))]
    args = [x1, ysg, slab, g]
    aliases = {}
    if out_prev is not None:
        in_specs.append(pl.BlockSpec(memory_space=pl.ANY))
        args.append(out_prev)
        aliases = {4: 0}
    return pl.pallas_call(
        _final_kernel,
        grid=(x1.shape[0] // t,),
        in_specs=in_specs,
        out_specs=pl.BlockSpec((t, D_MODEL), lambda i: (out_tile0 + i, 0)),
        out_shape=jax.ShapeDtypeStruct((n_out, D_MODEL), F32),
        input_output_aliases=aliases,
        compiler_params=pltpu.CompilerParams(dimension_semantics=("arbitrary",)),
        name="final_norm",
    )(*args)


def _s5_tables(lam_re, lam_im, b_re, b_im, c_re, c_im, log_step):
    lr = jnp.minimum(lam_re.astype(F32), LAMBDA_RE_MAX)
    li = lam_im.astype(F32)
    step = jnp.exp(log_step.astype(F32))[:, None]
    mag = jnp.exp(lr * step)
    ar = mag * jnp.cos(li * step)
    ai = mag * jnp.sin(li * step)
    nr = ar - 1.0
    den = lr * lr + li * li
    kr = (nr * lr + ai * li) / den
    ki = (ai * lr - nr * li) / den
    bre = b_re.astype(F32)
    bim = b_im.astype(F32)
    bbar_r = kr[..., None] * bre - ki[..., None] * bim
    bbar_i = kr[..., None] * bim + ki[..., None] * bre
    cre = c_re.astype(F32)
    cim = c_im.astype(F32)

    def power(m):
        m = jnp.asarray(m, F32)[..., None, None]
        mg = jnp.exp(m * lr * step)
        return mg * jnp.cos(m * li * step), mg * jnp.sin(m * li * step)

    pw_r, pw_i = power(jnp.arange(CHUNK + 1))
    cp_r = cre[None] * pw_r[:, :, None, :] - cim[None] * pw_i[:, :, None, :]
    cp_i = cre[None] * pw_i[:, :, None, :] + cim[None] * pw_r[:, :, None, :]
    width = CHUNK * SSM_GROUP
    kcat = (jnp.einsum('tgap,gph->ghta', cp_r[:CHUNK], bbar_r)
            - jnp.einsum('tgap,gph->ghta', cp_i[:CHUNK], bbar_i)).astype(BF16)
    lag = jnp.arange(CHUNK)
    place = (lag[None, :, None] - lag[:, None, None] == lag[None, None, :]).astype(BF16)
    m_intra = jnp.einsum('stl,ghla->gshta', place, kcat,
                         preferred_element_type=F32).astype(BF16).reshape(SSM_GROUPS, width, width)
    mo_r = cp_r[1:].transpose(1, 3, 0, 2).reshape(SSM_GROUPS, SSM_STATE, width)
    mo_i = -cp_i[1:].transpose(1, 3, 0, 2).reshape(SSM_GROUPS, SSM_STATE, width)
    mcat = jnp.concatenate([m_intra, mo_r.astype(BF16), mo_i.astype(BF16)], axis=1)
    q_r = pw_r[CHUNK - 1::-1][:CHUNK][:, :, :, None]
    q_i = pw_i[CHUNK - 1::-1][:CHUNK][:, :, :, None]
    mn_r = q_r * bbar_r[None] - q_i * bbar_i[None]
    mn_i = q_r * bbar_i[None] + q_i * bbar_r[None]
    minw = jnp.concatenate([mn_r, mn_i], axis=2).transpose(1, 0, 3, 2).reshape(
        SSM_GROUPS, CHUNK * SSM_GROUP, 2 * SSM_STATE).astype(BF16)

    def packed(j):
        pr, pi = power(float(CHUNK * j))
        return (jnp.concatenate([pr, pr], axis=-1).reshape(1, SSM_GROUPS * LANES),
                jnp.concatenate([-pi, pi], axis=-1).reshape(1, SSM_GROUPS * LANES))

    rows = jnp.arange(SUBLANES)[:, None]
    kinds = []
    for d in (1, 2, 4):
        c1, c2 = packed(d)
        mask = (rows >= d).astype(F32)
        kinds += [mask * c1, mask * c2]
    pk = [packed(r + 1) for r in range(SUBLANES)]
    kinds += [jnp.concatenate([p[0] for p in pk], axis=0), jnp.concatenate([p[1] for p in pk], axis=0)]
    coef = jnp.stack(kinds, axis=0)
    return minw, mcat, coef


def _mixer_params(norm_mix, w_in, ssm_lam_re, ssm_lam_im, ssm_b_re, ssm_b_im, ssm_c_re, ssm_c_im, ssm_d,
                  ssm_log_step, ssm_w_glu, ssm_b_glu, pool_w, pool_scale, conv_w, branch_norm, w_out, norm_ffn,
                  router_w, router_b):
    depth = w_in.shape[0]
    minw, mcat, coef = jax.vmap(_s5_tables)(ssm_lam_re, ssm_lam_im, ssm_b_re, ssm_b_im, ssm_c_re, ssm_c_im,
                                            ssm_log_step)
    eye4 = jnp.eye(len(POOL_WINDOWS), dtype=F32)
    poolw = jnp.einsum('lgcd,gk->lgckd', pool_w.astype(F32), eye4).reshape(depth, POOL_WIDTH, POOL_WIDTH).astype(BF16)
    rw = jnp.pad(router_w.astype(F32), ((0, 0), (0, 0), (0, LANES - N_EXPERTS))).astype(BF16)
    rb = jnp.pad(router_b.astype(F32), ((0, 0), (0, LANES - N_EXPERTS))).reshape(depth, 1, LANES)
    row = lambda v: v.reshape(depth, 1, -1).astype(F32)
    utri = jnp.triu(jnp.ones((TILE_T, TILE_T), F32), 1).astype(BF16)
    return dict(gmix=row(norm_mix), win=w_in.astype(BF16), minw=minw, mcat=mcat, coef=coef, dskip=row(ssm_d),
                wglu=ssm_w_glu.astype(BF16), bglu=row(ssm_b_glu), poolw=poolw, pscale=row(pool_scale),
                convw=conv_w.astype(F32), gbr=row(branch_norm), wout=w_out.astype(BF16),
                gffn=row(norm_ffn), rw=rw, rb=rb, utri=utri[None])


def kernel(x, norm_mix, w_in, ssm_lam_re, ssm_lam_im, ssm_b_re, ssm_b_im, ssm_c_re, ssm_c_im, ssm_d, ssm_log_step, ssm_w_glu, ssm_b_glu, pool_w, pool_scale, conv_w, branch_norm, w_out, norm_ffn, router_w, router_b, w_gate_up, b_gate_up, w_down, b_down, final_norm):
    batch, seq, d_model = x.shape
    assert d_model == D_MODEL and seq % TILE_T == 0
    depth = w_in.shape[0]
    n_streams = 2 if batch % 2 == 0 else 1
    sb = batch // n_streams
    n_tok = sb * seq
    assert n_tok % ROUTE_T == 0 and n_tok % SC_WINDOW == 0
    n_steps = -(-(n_tok * TOP_K + N_EXPERTS * (STEP_ROWS - 1)) // STEP_ROWS)
    n_rows = n_steps * STEP_ROWS
    tiles = n_tok // TILE_T
    b_gu = b_gate_up.reshape(depth, N_EXPERTS, 1, 2 * D_EXPERT)
    b_dn = b_down.reshape(depth, N_EXPERTS, 1, D_MODEL)

    xin = [x.reshape(batch * seq, D_MODEL)] * n_streams
    tile0 = [s * tiles for s in range(n_streams)]
    ysg = [None] * n_streams
    slab = [None] * n_streams
    lp = _mixer_params(norm_mix, w_in, ssm_lam_re, ssm_lam_im, ssm_b_re, ssm_b_im, ssm_c_re, ssm_c_im, ssm_d,
                       ssm_log_step, ssm_w_glu, ssm_b_glu, pool_w, pool_scale, conv_w, branch_norm, w_out,
                       norm_ffn, router_w, router_b)
    for l in range(depth):
        routed = []
        for s in range(n_streams):
            xin[s], hp, slab[s], slabt, counts = _mixer_call(xin[s], tile0[s], ysg[s], slab[s], lp, l, sb, seq)
            tile0[s] = 0
            dest, meta = _route_call(slabt, counts, n_tok, n_steps)
            idx = dest.reshape(1, 2 * TOP_K * n_tok)
            xs = _dispatch(hp.reshape(2 * n_tok, SC_ROW), idx, 2 * n_rows).reshape(2, n_rows, SC_ROW)
            routed.append((xs, idx, meta[:n_steps, 0], meta[:n_steps, 1], meta[n_steps:n_steps + 1, 0]))
        for s, (xs, idx, step_e, step_nblk, n_valid) in enumerate(routed):
            ys = _expert_call(xs, step_e, step_nblk, n_valid, w_gate_up, b_gu, w_down, b_dn, l, n_steps)
            ysg[s] = _combine(ys.reshape(2 * n_rows, SC_ROW), idx).reshape(TOP_K, 2, n_tok, SC_ROW)
    g = final_norm.reshape(1, D_MODEL).astype(F32)
    out = None
    for s in range(n_streams):
        out = _final_call(xin[s], ysg[s], slab[s], g, s * n_tok, batch * seq, out)
    return out.reshape(batch, seq, D_MODEL)
```

```python
import functools

import jax
import jax.numpy as jnp
from jax import lax
from jax.experimental import pallas as pl
from jax.experimental.pallas import tpu as pltpu
from jax.experimental.pallas import tpu_sc as plsc

F32 = jnp.float32
BF16 = jnp.bfloat16
I32 = jnp.int32

D_MODEL = 1024
SSM_WIDTH = 512
POOL_WIDTH = 256
CONV_WIDTH = 256
SSM_GROUP = 16
SSM_GROUPS = 32
SSM_STATE = 64
LAMBDA_RE_MAX = -1e-4
POOL_WINDOWS = (2, 4, 8, 16)
POOL_GROUP = 64
N_EXPERTS = 32
TOP_K = 4
D_EXPERT = 1024
SWIGLU_LIMIT = 7.0
SWIGLU_ALPHA = 1.702
EXPERT_BLOCK = 256
STEP_ROWS = 1024
NORM_EPS = 1e-5

LANES = 128
SUBLANES = 8
TILE_T = 512
FINAL_T = 512
CHUNK = 16
N_CHUNK = TILE_T // CHUNK
POOL_TAIL = SUBLANES * len(POOL_WINDOWS)
assert POOL_WINDOWS == tuple(2 ** (k + 1) for k in range(len(POOL_WINDOWS))) and POOL_TAIL >= POOL_WINDOWS[-1]
PACKED = D_MODEL // 2
ROUTE_T = 2048
SC_WINDOW = 128
SC_ROW = PACKED // 2
HI_MASK = -65536
VMEM_LIMIT = 56 * 1024 * 1024


def _bf16_round(v):
    return v.astype(BF16).astype(F32)


def _pack_rows(v):
    lo = lax.shift_right_logical(lax.bitcast_convert_type(_bf16_round(v[:, :PACKED]), I32), 16)
    hi = lax.bitcast_convert_type(_bf16_round(v[:, PACKED:]), I32) & HI_MASK
    return hi | lo


def _unpack_lo(w):
    return lax.bitcast_convert_type(lax.shift_left(w, 16), F32)


def _unpack_hi(w):
    return lax.bitcast_convert_type(w & HI_MASK, F32)


def _rms(v, g):
    r = lax.rsqrt(jnp.mean(v * v, axis=-1, keepdims=True) + NORM_EPS)
    return (v * r) * g


def _moe_combine(x, y_ref, slab):
    parts = [x[:, i * SC_ROW:(i + 1) * SC_ROW] for i in range(4)]
    for k in range(TOP_K):
        g = slab[:, TOP_K + k:TOP_K + k + 1]
        for h in range(2):
            w = y_ref[k, h]
            parts[h] = parts[h] + g * _unpack_lo(w)
            parts[2 + h] = parts[2 + h] + g * _unpack_hi(w)
    return jnp.concatenate(parts, axis=1)


def _block_transpose8(vs, lane):
    vs = list(vs)
    for dist in (4, 2, 1):
        width = SSM_GROUP * dist
        low = (lane % (2 * width)) < width
        for j in range(8):
            if j & dist:
                continue
            a, b = vs[j], vs[j + dist]
            vs[j] = jnp.where(low, a, pltpu.roll(b, width, axis=1))
            vs[j + dist] = jnp.where(low, pltpu.roll(a, LANES - width, axis=1), b)
    return vs


def _cmul_packed(c1, c2, v):
    return c1 * v + c2 * pltpu.roll(v, SSM_STATE, axis=1)


def _chunk_scan(s, carry_in, cf):
    outs = []
    carry = jnp.broadcast_to(carry_in, (SUBLANES, LANES))
    for rg in range(N_CHUNK // SUBLANES):
        x = s[rg * SUBLANES:(rg + 1) * SUBLANES]
        for j, d in enumerate((1, 2, 4)):
            x = x + _cmul_packed(cf[2 * j], cf[2 * j + 1], pltpu.roll(x, d, axis=0))
        x = x + _cmul_packed(cf[6], cf[7], carry)
        outs.append(x)
        carry = jnp.broadcast_to(x[SUBLANES - 1:SUBLANES], (SUBLANES, LANES))
    return jnp.concatenate(outs, axis=0)


def _s5_mixer(u, first, minw_ref, mcat_ref, coef_ref, u_s, y_s, h_s):
    blocks = SSM_WIDTH // LANES
    for b4 in range(blocks):
        u_s[b4] = u[:, b4 * LANES:(b4 + 1) * LANES]
    lane = lax.broadcasted_iota(I32, (N_CHUNK, LANES), 1)

    halves = [[None, None] for _ in range(SSM_GROUPS)]
    for b4 in range(blocks):
        for hh in range(2):
            xs = [u_s[b4, pl.ds(8 * hh + j, N_CHUNK, stride=CHUNK), :].astype(BF16) for j in range(8)]
            ws = _block_transpose8(xs, lane)
            for gl in range(8):
                halves[8 * b4 + gl][hh] = ws[gl]
    ug = [jnp.concatenate(h2, axis=1) for h2 in halves]

    carry_in = jnp.where(first, 0.0, h_s[N_CHUNK + 7:N_CHUNK + 8, :])
    h_s[7:8, :] = carry_in
    for g in range(SSM_GROUPS):
        cols = slice(g * LANES, (g + 1) * LANES)
        s_g = jnp.dot(ug[g], minw_ref[g], preferred_element_type=F32)
        cf = [coef_ref[k, :, cols] for k in range(8)]
        h_s[8:8 + N_CHUNK, cols] = _chunk_scan(s_g, carry_in[:, cols], cf)

    yg = []
    for g in range(SSM_GROUPS):
        hprev = h_s[7:7 + N_CHUNK, g * LANES:(g + 1) * LANES].astype(BF16)
        lhs = jnp.concatenate([ug[g], hprev], axis=1)
        yg.append(jnp.dot(lhs, mcat_ref[g], preferred_element_type=F32).astype(BF16))

    for b4 in range(blocks):
        for hh in range(2):
            ws = [yg[8 * b4 + gl][:, hh * LANES:(hh + 1) * LANES] for gl in range(8)]
            xs = _block_transpose8(ws, lane)
            for j in range(8):
                y_s[b4, pl.ds(8 * hh + j, N_CHUNK, stride=CHUNK), :] = xs[j].astype(F32)
    return jnp.concatenate([y_s[b4] for b4 in range(blocks)], axis=1)


def _mixer_kernel(*refs, first_layer):
    if first_layer:
        x_ref = refs[0]
        refs = refs[1:]
    else:
        x_ref, yprev_ref, slabprev_ref = refs[:3]
        refs = refs[3:]
    (gmix_ref, win_ref, minw_ref, mcat_ref, coef_ref, dskip_ref, wglu_ref, bglu_ref, poolw_ref, pscale_ref,
     convw_ref, gbr_ref, wout_ref, gffn_ref, rw_ref, rb_ref, utri_ref,
     x1_ref, hp_ref, slab_ref, slabt_ref, cnt_ref,
     u_s, y_s, h_s, pbuf_ref, zbuf_ref, cntacc_ref) = refs

    b = pl.program_id(0)
    l = pl.program_id(1)
    seq_start = l == 0

    @pl.when(jnp.logical_and(b == 0, l == 0))
    def _():
        cntacc_ref[...] = jnp.zeros_like(cntacc_ref)

    if first_layer:
        x = x_ref[...]
    else:
        x = _moe_combine(x_ref[...], yprev_ref, slabprev_ref[...])

    hn = _rms(x, gmix_ref[...]).astype(BF16)
    proj = jnp.dot(hn, win_ref[...], preferred_element_type=F32)
    u = proj[:, :SSM_WIDTH]
    p = proj[:, SSM_WIDTH:SSM_WIDTH + POOL_WIDTH]
    c0 = SSM_WIDTH + POOL_WIDTH
    bg = proj[:, c0:c0 + CONV_WIDTH]
    cg = proj[:, c0 + CONV_WIDTH:c0 + 2 * CONV_WIDTH]
    hv = proj[:, c0 + 2 * CONV_WIDTH:]

    y = _s5_mixer(u, seq_start, minw_ref, mcat_ref, coef_ref, u_s, y_s, h_s) + dskip_ref[...] * u
    z = jax.nn.gelu(y)
    glu = jnp.dot(z.astype(BF16), wglu_ref[...], preferred_element_type=F32) + bglu_ref[...]
    y_ssm = z * jax.nn.sigmoid(glu)

    tail = POOL_TAIL
    pbuf_ref[0, 0:tail, :] = jnp.where(seq_start, 0.0, pbuf_ref[0, TILE_T:TILE_T + tail, :])
    pbuf_ref[0, tail:, :] = p
    sums = []
    for stage, w in enumerate(POOL_WINDOWS):
        src = 0 if stage == 0 else 1 + (stage - 1) % 2
        dst = 1 + stage % 2
        d = w // 2
        lo = SUBLANES * (stage + 1)
        n = tail + TILE_T - lo
        e = pbuf_ref[src, pl.ds(lo, n), :] + pbuf_ref[src, pl.ds(lo - d, n), :]
        if stage + 1 < len(POOL_WINDOWS):
            pbuf_ref[dst, pl.ds(lo, n), :] = e
        sums.append(e[tail - lo:, :])
    lane_p = lax.broadcasted_iota(I32, (TILE_T, POOL_WIDTH), 1)
    total = sums[-1]
    win = jnp.full((TILE_T, POOL_WIDTH), POOL_WINDOWS[-1], I32)
    for i in range(len(POOL_WINDOWS) - 2, -1, -1):
        sel = lane_p < (i + 1) * POOL_GROUP
        total = jnp.where(sel, sums[i], total)
        win = jnp.where(sel, POOL_WINDOWS[i], win)
    row_p = lax.broadcasted_iota(I32, (TILE_T, POOL_WIDTH), 0) + l * TILE_T
    count = jnp.minimum(row_p + 1, win).astype(F32)
    pooled = total / count - p
    mixed = jnp.dot(pooled.astype(BF16), poolw_ref[...], preferred_element_type=F32)
    y_pool = mixed * pscale_ref[...]

    zc = cg * hv
    zbuf_ref[0:8, :] = jnp.where(seq_start, 0.0, zbuf_ref[TILE_T:TILE_T + 8, :])
    zbuf_ref[8:, :] = zc
    yc = (convw_ref[0:1, :] * zbuf_ref[pl.ds(6, TILE_T), :]
          + convw_ref[1:2, :] * zbuf_ref[pl.ds(7, TILE_T), :]
          + convw_ref[2:3, :] * zc)
    y_conv = bg * yc

    gbr = gbr_ref[...]
    s1 = SSM_WIDTH + POOL_WIDTH
    mixed_all = jnp.concatenate([_rms(y_ssm, gbr[:, :SSM_WIDTH]).astype(BF16),
                                 _rms(y_pool, gbr[:, SSM_WIDTH:s1]).astype(BF16),
                                 _rms(y_conv, gbr[:, s1:]).astype(BF16)], axis=1)
    x1 = x + jnp.dot(mixed_all, wout_ref[...], preferred_element_type=F32)
    x1_ref[...] = x1

    hn2 = _rms(x1, gffn_ref[...])
    packed = _pack_rows(hn2)
    hp_ref[0] = packed[:, :SC_ROW]
    hp_ref[1] = packed[:, SC_ROW:]
    logits = jnp.dot(hn2.astype(BF16), rw_ref[...], preferred_element_type=F32) + rb_ref[...]
    work = logits.T[:N_EXPERTS, :]
    eio = lax.broadcasted_iota(I32, (N_EXPERTS, TILE_T), 0).astype(F32)
    neg = jnp.float32(-jnp.inf)
    vals, idxs = [], []
    for _ in range(TOP_K):
        m = jnp.max(work, axis=0, keepdims=True)
        idx = jnp.min(jnp.where(work == m, eio, float(N_EXPERTS)), axis=0, keepdims=True)
        vals.append(m)
        idxs.append(idx)
        work = jnp.where(eio == idx, neg, work)
    exps = [jnp.exp(v - vals[0]) for v in vals]
    denom = exps[0] + exps[1] + exps[2] + exps[3]
    gates = [e / denom for e in exps]

    onehot = jnp.zeros((N_EXPERTS, TILE_T), F32)
    for idx in idxs:
        onehot = jnp.where(eio == idx, 1.0, onehot)
    before = jnp.dot(onehot.astype(BF16), utri_ref[...], preferred_element_type=F32) + cntacc_ref[:, 0:1]
    ranks = [jnp.sum(jnp.where(eio == idx, before, 0.0), axis=0, keepdims=True) for idx in idxs]
    row = lax.broadcasted_iota(I32, (LANES, TILE_T), 0)
    slab_t = jnp.zeros((LANES, TILE_T), F32)
    for k in range(TOP_K):
        slab_t = jnp.where(row == k, idxs[k], slab_t)
        slab_t = jnp.where(row == TOP_K + k, gates[k], slab_t)
        slab_t = jnp.where(row == 2 * TOP_K + k, ranks[k], slab_t)
    slabt_ref[...] = slab_t[:2 * SUBLANES, :]
    slab_ref[...] = slab_t.T
    newcnt = cntacc_ref[:, 0:1] + jnp.sum(onehot, axis=1, keepdims=True)
    cntacc_ref[...] = jnp.broadcast_to(newcnt, cntacc_ref.shape)
    cnt_ref[...] = jnp.broadcast_to(newcnt, cnt_ref.shape)


def _layer_spec(shape, layer):
    nd = len(shape) - 1
    sel = layer if shape[0] > 1 else 0
    return pl.BlockSpec((None,) + tuple(shape[1:]), lambda b, l, nd=nd, sel=sel: (sel,) + (0,) * nd,
                        pipeline_mode=pl.Buffered(1))


def _mixer_call(x, x_tile0, yprev, slabprev, lp, layer, batch, seq):
    first_layer = yprev is None
    n_tok = batch * seq
    steps = seq // TILE_T
    tok_spec = lambda w: pl.BlockSpec((TILE_T, w), lambda b, l: (b * steps + l, 0))
    in_specs = [pl.BlockSpec((TILE_T, D_MODEL), lambda b, l: (x_tile0 + b * steps + l, 0))]
    args = [x]
    if not first_layer:
        in_specs += [pl.BlockSpec((TOP_K, 2, TILE_T, SC_ROW), lambda b, l: (0, 0, b * steps + l, 0)), tok_spec(LANES)]
        args += [yprev, slabprev]
    weights = [lp['gmix'], lp['win'], lp['minw'], lp['mcat'], lp['coef'], lp['dskip'], lp['wglu'], lp['bglu'],
               lp['poolw'], lp['pscale'], lp['convw'], lp['gbr'], lp['wout'], lp['gffn'], lp['rw'], lp['rb'],
               lp['utri']]
    in_specs += [_layer_spec(w.shape, layer) for w in weights]
    args += weights
    out_shape = [jax.ShapeDtypeStruct((n_tok, D_MODEL), F32),
                 jax.ShapeDtypeStruct((2, n_tok, SC_ROW), I32),
                 jax.ShapeDtypeStruct((n_tok, LANES), F32),
                 jax.ShapeDtypeStruct((2 * SUBLANES, n_tok), F32),
                 jax.ShapeDtypeStruct((N_EXPERTS, LANES), F32)]
    out_specs = [tok_spec(D_MODEL), pl.BlockSpec((2, TILE_T, SC_ROW), lambda b, l: (0, b * steps + l, 0)), tok_spec(LANES),
                 pl.BlockSpec((2 * SUBLANES, TILE_T), lambda b, l: (0, b * steps + l)),
                 pl.BlockSpec((N_EXPERTS, LANES), lambda b, l: (0, 0))]
    scratch = [pltpu.VMEM((SSM_WIDTH // LANES, TILE_T, LANES), F32),
               pltpu.VMEM((SSM_WIDTH // LANES, TILE_T, LANES), F32),
               pltpu.VMEM((SUBLANES + N_CHUNK, SSM_GROUPS * LANES), F32),
               pltpu.VMEM((3, POOL_TAIL + TILE_T, POOL_WIDTH), F32),
               pltpu.VMEM((8 + TILE_T, CONV_WIDTH), F32),
               pltpu.VMEM((N_EXPERTS, LANES), F32)]
    return pl.pallas_call(
        functools.partial(_mixer_kernel, first_layer=first_layer),
        grid=(batch, steps),
        in_specs=in_specs, out_specs=out_specs, out_shape=out_shape, scratch_shapes=scratch,
        compiler_params=pltpu.CompilerParams(dimension_semantics=("arbitrary", "arbitrary"),
                                             vmem_limit_bytes=VMEM_LIMIT),
        name="mixer_first" if first_layer else "mixer_next",
    )(*args)


def _route_kernel(slabt_ref, cnt_ref, dest_ref, meta_ref, *, n_steps):
    lane1 = lax.broadcasted_iota(I32, (1, LANES), 1)
    e_row = lax.broadcasted_iota(I32, (N_EXPERTS, LANES), 0)
    e_lane = lax.broadcasted_iota(I32, (N_EXPERTS, LANES), 1)
    cnt = cnt_ref[...].astype(I32)
    counts = jnp.sum(jnp.where(e_row == e_lane, cnt, 0), axis=0, keepdims=True)
    padded = ((counts + (STEP_ROWS - 1)) // STEP_ROWS) * STEP_ROWS
    pend = padded
    for sh in (1, 2, 4, 8, 16):
        pend = pend + jnp.where(lane1 >= sh, pltpu.roll(pend, sh, axis=1), 0)

    @pl.when(pl.program_id(0) == 0)
    def _():
        rows = meta_ref.shape[0]
        bstart = lax.broadcasted_iota(I32, (rows, LANES), 0) * STEP_ROWS
        lane = lax.broadcasted_iota(I32, (rows, LANES), 1)
        done = jnp.where(jnp.logical_and(lane < N_EXPERTS, pend <= bstart), 1, 0)
        be = jnp.minimum(jnp.sum(done, axis=1, keepdims=True), N_EXPERTS - 1)
        mine = lane == be
        first_row = jnp.sum(jnp.where(mine, pend - counts, 0), axis=1, keepdims=True)
        used = bstart[:, 0:1] + STEP_ROWS - jnp.maximum(bstart[:, 0:1], first_row)
        nblk = jnp.clip((used + (EXPERT_BLOCK - 1)) // EXPERT_BLOCK, 0, STEP_ROWS // EXPERT_BLOCK)
        total = jnp.sum(jnp.where(lane1 == N_EXPERTS - 1, pend, 0), axis=1, keepdims=True)
        row = lax.broadcasted_iota(I32, (rows, LANES), 0)
        table = jnp.where(lane == 0, be, jnp.where(lane == 1, nblk, 0))
        meta_ref[...] = jnp.where(row == n_steps, total // STEP_ROWS, table)

    last_col = jnp.sum(jnp.where(e_row == e_lane, pend - 1, 0), axis=1, keepdims=True)
    tokens = slabt_ref.shape[1]
    eio = lax.broadcasted_iota(I32, (N_EXPERTS, tokens), 0)
    row = lax.broadcasted_iota(I32, (SUBLANES, tokens), 0)
    out = jnp.zeros((SUBLANES, tokens), I32)
    for k in range(TOP_K):
        idx = slabt_ref[k:k + 1, :].astype(I32)
        rank = slabt_ref[2 * TOP_K + k:2 * TOP_K + k + 1, :].astype(I32)
        base = jnp.sum(jnp.where(eio == idx, last_col, 0), axis=0, keepdims=True)
        for h in range(2):
            out = jnp.where(row == 2 * k + h, base - rank + h * n_steps * STEP_ROWS, out)
    dest_ref[...] = out


def _route_call(slabt, counts, n_tok, n_steps):
    meta_rows = ((n_steps + 1 + SUBLANES - 1) // SUBLANES) * SUBLANES
    return pl.pallas_call(
        functools.partial(_route_kernel, n_steps=n_steps),
        grid=(n_tok // ROUTE_T,),
        in_specs=[pl.BlockSpec((2 * SUBLANES, ROUTE_T), lambda i: (0, i)),
                  pl.BlockSpec((N_EXPERTS, LANES), lambda i: (0, 0))],
        out_specs=[pl.BlockSpec((SUBLANES, ROUTE_T), lambda i: (0, i)),
                   pl.BlockSpec((meta_rows, LANES), lambda i: (0, 0))],
        out_shape=[jax.ShapeDtypeStruct((SUBLANES, n_tok), I32),
                   jax.ShapeDtypeStruct((meta_rows, LANES), I32)],
        compiler_params=pltpu.CompilerParams(dimension_semantics=("arbitrary",)),
        name="route",
    )(slabt, counts)


def _sc_mesh():
    return plsc.VectorSubcoreMesh(core_axis_name="core", subcore_axis_name="subcore")


def _dispatch(src, idx, n_rows):
    windows = src.shape[0] // SC_WINDOW

    @functools.partial(pl.kernel, out_type=jax.ShapeDtypeStruct((n_rows, SC_ROW), src.dtype),
                       mesh=_sc_mesh(), scratch_types=[])
    def scatter_rows(x_hbm, i_hbm, o_hbm):
        def body(x_vmem, i_vmem):
            pltpu.sync_copy(x_vmem, o_hbm.at[i_vmem.at[0]])

        pltpu.emit_pipeline(
            body,
            grid=(idx.shape[1] // SC_WINDOW,),
            in_specs=[pl.BlockSpec((SC_WINDOW, SC_ROW), lambda i: (lax.rem(i, windows), 0)),
                      pl.BlockSpec((1, SC_WINDOW), lambda i: (0, i))],
            out_specs=[],
            core_axis_name=("core", "subcore"),
            dimension_semantics=(pltpu.PARALLEL,),
        )(x_hbm, i_hbm)

    return scatter_rows(src, idx)


def _combine(src, idx):
    n_idx = idx.shape[1]

    @functools.partial(pl.kernel, out_type=jax.ShapeDtypeStruct((n_idx, SC_ROW), src.dtype),
                       mesh=_sc_mesh(), scratch_types=[])
    def gather_rows(y_hbm, i_hbm, o_hbm):
        def body(i_vmem, o_vmem):
            pltpu.sync_copy(y_hbm.at[i_vmem.at[0]], o_vmem)

        pltpu.emit_pipeline(
            body,
            grid=(n_idx // SC_WINDOW,),
            in_specs=[pl.BlockSpec((1, SC_WINDOW), lambda i: (0, i))],
            out_specs=[pl.BlockSpec((SC_WINDOW, SC_ROW), lambda i: (i, 0))],
            core_axis_name=("core", "subcore"),
            dimension_semantics=(pltpu.PARALLEL,),
        )(i_hbm, o_hbm)

    return gather_rows(src, idx)


def _expert_rows(xs_ref, ys_ref, row0, n, wgu_s, bgu, wdn_s, bdn, act_s):
    rows = slice(row0, row0 + n)
    arow = slice(0, n)
    w0 = xs_ref[0, rows, :]
    w1 = xs_ref[1, rows, :]
    xb = jnp.concatenate([_unpack_lo(w0).astype(BF16), _unpack_lo(w1).astype(BF16),
                          _unpack_hi(w0).astype(BF16), _unpack_hi(w1).astype(BF16)], axis=1)
    half = D_EXPERT // 2
    for j in range(2):
        cols = slice(j * half, (j + 1) * half)
        ucols = slice(D_EXPERT + j * half, D_EXPERT + (j + 1) * half)
        g = jnp.dot(xb, wgu_s[:, cols], preferred_element_type=F32) + bgu[:, cols]
        up = jnp.dot(xb, wgu_s[:, ucols], preferred_element_type=F32) + bgu[:, ucols]
        g = jnp.minimum(g, SWIGLU_LIMIT)
        up = jnp.clip(up, -SWIGLU_LIMIT, SWIGLU_LIMIT)
        act_s[arow, cols] = (g * jax.nn.sigmoid(SWIGLU_ALPHA * g) * (up + 1.0)).astype(BF16)
    y = jnp.dot(act_s[arow, :], wdn_s[...], preferred_element_type=F32) + bdn
    packed = _pack_rows(y)
    ys_ref[0, rows, :] = packed[:, :SC_ROW]
    ys_ref[1, rows, :] = packed[:, SC_ROW:]


def _expert_kernel(be_ref, nb_ref, nv_ref, xs_ref, wgu_ref, bgu_ref, wdn_ref, bdn_ref, ys_ref, wgu_s, wdn_s, act_s):
    j = pl.program_id(0)
    blk = jnp.maximum(j - 1, 0)
    n_valid = nv_ref[0]
    compute = jnp.logical_and(j >= 1, blk < n_valid)
    expert = be_ref[blk]

    for nblk in range(1, STEP_ROWS // EXPERT_BLOCK + 1):
        @pl.when(jnp.logical_and(compute, nb_ref[blk] == nblk))
        def _(nblk=nblk):
            bgu = bgu_ref[pl.ds(expert, 1), :]
            bdn = bdn_ref[pl.ds(expert, 1), :]
            row0, left = STEP_ROWS - nblk * EXPERT_BLOCK, nblk
            while left:
                run = 1 << (left.bit_length() - 1)
                _expert_rows(xs_ref, ys_ref, row0, run * EXPERT_BLOCK, wgu_s, bgu, wdn_s, bdn, act_s)
                row0 += run * EXPERT_BLOCK
                left -= run

    nxt = jnp.minimum(j, be_ref.shape[0] - 1)
    changed = jnp.logical_or(j == 0, be_ref[nxt] != expert)

    @pl.when(jnp.logical_and(j < n_valid, changed))
    def _():
        chunk = 128

        def cast_gu(i, _):
            rows = pl.ds(pl.multiple_of(i * chunk, chunk), chunk)
            wgu_s[rows, :] = wgu_ref[rows, :].astype(BF16)
            return 0

        def cast_dn(i, _):
            rows = pl.ds(pl.multiple_of(i * chunk, chunk), chunk)
            wdn_s[rows, :] = wdn_ref[rows, :].astype(BF16)
            return 0

        lax.fori_loop(0, D_MODEL // chunk, cast_gu, 0)
        lax.fori_loop(0, D_EXPERT // chunk, cast_dn, 0)


def _expert_call(xs, step_e, step_nblk, n_valid, w_gu, b_gu, w_dn, b_dn, layer, n_steps):
    def row_map(j, be, nb, nv):
        return (0, jnp.minimum(jnp.maximum(j - 1, 0), jnp.maximum(nv[0] - 1, 0)), 0)

    def w_map(j, be, nb, nv):
        return (layer, be[jnp.minimum(j, n_steps - 1)], 0, 0)

    def b_map(j, be, nb, nv):
        return (layer, 0, 0)

    grid_spec = pltpu.PrefetchScalarGridSpec(
        num_scalar_prefetch=3,
        grid=(n_steps + 1,),
        in_specs=[pl.BlockSpec((2, STEP_ROWS, SC_ROW), row_map),
                  pl.BlockSpec((None, None, D_MODEL, 2 * D_EXPERT), w_map),
                  pl.BlockSpec((None, N_EXPERTS, 2 * D_EXPERT), b_map),
                  pl.BlockSpec((None, None, D_EXPERT, D_MODEL), w_map),
                  pl.BlockSpec((None, N_EXPERTS, D_MODEL), b_map)],
        out_specs=pl.BlockSpec((2, STEP_ROWS, SC_ROW), row_map),
        scratch_shapes=[pltpu.VMEM((D_MODEL, 2 * D_EXPERT), BF16),
                        pltpu.VMEM((D_EXPERT, D_MODEL), BF16),
                        pltpu.VMEM((STEP_ROWS, D_EXPERT), BF16)])
    return pl.pallas_call(
        _expert_kernel,
        grid_spec=grid_spec,
        out_shape=jax.ShapeDtypeStruct((2, n_steps * STEP_ROWS, SC_ROW), I32),
        compiler_params=pltpu.CompilerParams(dimension_semantics=("arbitrary",),
                                             vmem_limit_bytes=VMEM_LIMIT),
        name="experts",
    )(step_e, step_nblk, n_valid, xs, w_gu, b_gu, w_dn, b_dn)


def _final_kernel(x_ref, y_ref, slab_ref, g_ref, *rest):
    o_ref = rest[-1]
    x = _moe_combine(x_ref[...], y_ref, slab_ref[...])
    o_ref[...] = _rms(x, g_ref[...])


def _final_call(x1, ysg, slab, g, out_row0, n_out, out_prev):
    t = FINAL_T
    out_tile0 = out_row0 // t
    in_specs = [pl.BlockSpec((t, D_MODEL), lambda i: (i, 0)),
                pl.BlockSpec((TOP_K, 2, t, SC_ROW), lambda i: (0, 0, i, 0)),
                pl.BlockSpec((t, LANES), lambda i: (i, 0)),
                pl.BlockSpec((1, D_MODEL), lambda i: (0, 0))]
    args = [x1, ysg, slab, g]
    aliases = {}
    if out_prev is not None:
        in_specs.append(pl.BlockSpec(memory_space=pl.ANY))
        args.append(out_prev)
        aliases = {4: 0}
    return pl.pallas_call(
        _final_kernel,
        grid=(x1.shape[0] // t,),
        in_specs=in_specs,
        out_specs=pl.BlockSpec((t, D_MODEL), lambda i: (out_tile0 + i, 0)),
        out_shape=jax.ShapeDtypeStruct((n_out, D_MODEL), F32),
        input_output_aliases=aliases,
        compiler_params=pltpu.CompilerParams(dimension_semantics=("arbitrary",)),
        name="final_norm",
    )(*args)


def _s5_tables(lam_re, lam_im, b_re, b_im, c_re, c_im, log_step):
    lr = jnp.minimum(lam_re.astype(F32), LAMBDA_RE_MAX)
    li = lam_im.astype(F32)
    step = jnp.exp(log_step.astype(F32))[:, None]
    mag = jnp.exp(lr * step)
    ar = mag * jnp.cos(li * step)
    ai = mag * jnp.sin(li * step)
    nr = ar - 1.0
    den = lr * lr + li * li
    kr = (nr * lr + ai * li) / den
    ki = (ai * lr - nr * li) / den
    bre = b_re.astype(F32)
    bim = b_im.astype(F32)
    bbar_r = kr[..., None] * bre - ki[..., None] * bim
    bbar_i = kr[..., None] * bim + ki[..., None] * bre
    cre = c_re.astype(F32)
    cim = c_im.astype(F32)

    def power(m):
        m = jnp.asarray(m, F32)[..., None, None]
        mg = jnp.exp(m * lr * step)
        return mg * jnp.cos(m * li * step), mg * jnp.sin(m * li * step)

    pw_r, pw_i = power(jnp.arange(CHUNK + 1))
    cp_r = cre[None] * pw_r[:, :, None, :] - cim[None] * pw_i[:, :, None, :]
    cp_i = cre[None] * pw_i[:, :, None, :] + cim[None] * pw_r[:, :, None, :]
    width = CHUNK * SSM_GROUP
    kcat = (jnp.einsum('tgap,gph->ghta', cp_r[:CHUNK], bbar_r)
            - jnp.einsum('tgap,gph->ghta', cp_i[:CHUNK], bbar_i)).astype(BF16)
    lag = jnp.arange(CHUNK)
    place = (lag[None, :, None] - lag[:, None, None] == lag[None, None, :]).astype(BF16)
    m_intra = jnp.einsum('stl,ghla->gshta', place, kcat,
                         preferred_element_type=F32).astype(BF16).reshape(SSM_GROUPS, width, width)
    mo_r = cp_r[1:].transpose(1, 3, 0, 2).reshape(SSM_GROUPS, SSM_STATE, width)
    mo_i = -cp_i[1:].transpose(1, 3, 0, 2).reshape(SSM_GROUPS, SSM_STATE, width)
    mcat = jnp.concatenate([m_intra, mo_r.astype(BF16), mo_i.astype(BF16)], axis=1)
    q_r = pw_r[CHUNK - 1::-1][:CHUNK][:, :, :, None]
    q_i = pw_i[CHUNK - 1::-1][:CHUNK][:, :, :, None]
    mn_r = q_r * bbar_r[None] - q_i * bbar_i[None]
    mn_i = q_r * bbar_i[None] + q_i * bbar_r[None]
    minw = jnp.concatenate([mn_r, mn_i], axis=2).transpose(1, 0, 3, 2).reshape(
        SSM_GROUPS, CHUNK * SSM_GROUP, 2 * SSM_STATE).astype(BF16)

    def packed(j):
        pr, pi = power(float(CHUNK * j))
        return (jnp.concatenate([pr, pr], axis=-1).reshape(1, SSM_GROUPS * LANES),
                jnp.concatenate([-pi, pi], axis=-1).reshape(1, SSM_GROUPS * LANES))

    rows = jnp.arange(SUBLANES)[:, None]
    kinds = []
    for d in (1, 2, 4):
        c1, c2 = packed(d)
        mask = (rows >= d).astype(F32)
        kinds += [mask * c1, mask * c2]
    pk = [packed(r + 1) for r in range(SUBLANES)]
    kinds += [jnp.concatenate([p[0] for p in pk], axis=0), jnp.concatenate([p[1] for p in pk], axis=0)]
    coef = jnp.stack(kinds, axis=0)
    return minw, mcat, coef


def _mixer_params(norm_mix, w_in, ssm_lam_re, ssm_lam_im, ssm_b_re, ssm_b_im, ssm_c_re, ssm_c_im, ssm_d,
                  ssm_log_step, ssm_w_glu, ssm_b_glu, pool_w, pool_scale, conv_w, branch_norm, w_out, norm_ffn,
                  router_w, router_b):
    depth = w_in.shape[0]
    minw, mcat, coef = jax.vmap(_s5_tables)(ssm_lam_re, ssm_lam_im, ssm_b_re, ssm_b_im, ssm_c_re, ssm_c_im,
                                            ssm_log_step)
    eye4 = jnp.eye(len(POOL_WINDOWS), dtype=F32)
    poolw = jnp.einsum('lgcd,gk->lgckd', pool_w.astype(F32), eye4).reshape(depth, POOL_WIDTH, POOL_WIDTH).astype(BF16)
    rw = jnp.pad(router_w.astype(F32), ((0, 0), (0, 0), (0, LANES - N_EXPERTS))).astype(BF16)
    rb = jnp.pad(router_b.astype(F32), ((0, 0), (0, LANES - N_EXPERTS))).reshape(depth, 1, LANES)
    row = lambda v: v.reshape(depth, 1, -1).astype(F32)
    utri = jnp.triu(jnp.ones((TILE_T, TILE_T), F32), 1).astype(BF16)
    return dict(gmix=row(norm_mix), win=w_in.astype(BF16), minw=minw, mcat=mcat, coef=coef, dskip=row(ssm_d),
                wglu=ssm_w_glu.astype(BF16), bglu=row(ssm_b_glu), poolw=poolw, pscale=row(pool_scale),
                convw=conv_w.astype(F32), gbr=row(branch_norm), wout=w_out.astype(BF16),
                gffn=row(norm_ffn), rw=rw, rb=rb, utri=utri[None])


def kernel(x, norm_mix, w_in, ssm_lam_re, ssm_lam_im, ssm_b_re, ssm_b_im, ssm_c_re, ssm_c_im, ssm_d, ssm_log_step, ssm_w_glu, ssm_b_glu, pool_w, pool_scale, conv_w, branch_norm, w_out, norm_ffn, router_w, router_b, w_gate_up, b_gate_up, w_down, b_down, final_norm):
    batch, seq, d_model = x.shape
    assert d_model == D_MODEL and seq % TILE_T == 0
    depth = w_in.shape[0]
    n_streams = 2 if batch % 2 == 0 else 1
    sb = batch // n_streams
    n_tok = sb * seq
    assert n_tok % ROUTE_T == 0 and n_tok % SC_WINDOW == 0
    n_steps = -(-(n_tok * TOP_K + N_EXPERTS * (STEP_ROWS - 1)) // STEP_ROWS)
    n_rows = n_steps * STEP_ROWS
    tiles = n_tok // TILE_T
    b_gu = b_gate_up.astype(F32)
    b_dn = b_down.astype(F32)

    xin = [x.reshape(batch * seq, D_MODEL)] * n_streams
    tile0 = [s * tiles for s in range(n_streams)]
    ysg = [None] * n_streams
    slab = [None] * n_streams
    lp = _mixer_params(norm_mix, w_in, ssm_lam_re, ssm_lam_im, ssm_b_re, ssm_b_im, ssm_c_re, ssm_c_im, ssm_d,
                       ssm_log_step, ssm_w_glu, ssm_b_glu, pool_w, pool_scale, conv_w, branch_norm, w_out,
                       norm_ffn, router_w, router_b)
    for l in range(depth):
        routed = []
        for s in range(n_streams):
            xin[s], hp, slab[s], slabt, counts = _mixer_call(xin[s], tile0[s], ysg[s], slab[s], lp, l, sb, seq)
            tile0[s] = 0
            dest, meta = _route_call(slabt, counts, n_tok, n_steps)
            idx = dest.reshape(1, 2 * TOP_K * n_tok)
            xs = _dispatch(hp.reshape(2 * n_tok, SC_ROW), idx, 2 * n_rows).reshape(2, n_rows, SC_ROW)
            routed.append((xs, idx, meta[:n_steps, 0], meta[:n_steps, 1], meta[n_steps:n_steps + 1, 0]))
        for s, (xs, idx, step_e, step_nblk, n_valid) in enumerate(routed):
            ys = _expert_call(xs, step_e, step_nblk, n_valid, w_gate_up, b_gu, w_down, b_dn, l, n_steps)
            ysg[s] = _combine(ys.reshape(2 * n_rows, SC_ROW), idx).reshape(TOP_K, 2, n_tok, SC_ROW)
    g = final_norm.reshape(1, D_MODEL).astype(F32)
    out = None
    for s in range(n_streams):
        out = _final_call(xin[s], ysg[s], slab[s], g, s * n_tok, batch * seq, out)
    return out.reshape(batch, seq, D_MODEL)
```

```python
import functools

import jax
import jax.numpy as jnp
from jax import lax
from jax.experimental import pallas as pl
from jax.experimental.pallas import tpu as pltpu
from jax.experimental.pallas import tpu_sc as plsc

F32 = jnp.float32
BF16 = jnp.bfloat16
I32 = jnp.int32

D_MODEL = 1024
SSM_WIDTH = 512
POOL_WIDTH = 256
CONV_WIDTH = 256
SSM_GROUP = 16
SSM_GROUPS = 32
SSM_STATE = 64
LAMBDA_RE_MAX = -1e-4
POOL_WINDOWS = (2, 4, 8, 16)
POOL_GROUP = 64
N_EXPERTS = 32
TOP_K = 4
D_EXPERT = 1024
SWIGLU_LIMIT = 7.0
SWIGLU_ALPHA = 1.702
EXPERT_BLOCK = 256
STEP_ROWS = 2048
RUN_BLOCKS = 4
NORM_EPS = 1e-5

LANES = 128
SUBLANES = 8
TILE_T = 512
FINAL_T = 512
CHUNK = 16
N_CHUNK = TILE_T // CHUNK
POOL_TAIL = SUBLANES * len(POOL_WINDOWS)
assert POOL_WINDOWS == tuple(2 ** (k + 1) for k in range(len(POOL_WINDOWS))) and POOL_TAIL >= POOL_WINDOWS[-1]
PACKED = D_MODEL // 2
ROUTE_T = 2048
SC_WINDOW = 128
SC_ROW = PACKED // 2
HI_MASK = -65536
VMEM_LIMIT = 56 * 1024 * 1024


def _bf16_round(v):
    return v.astype(BF16).astype(F32)


def _pack_rows(v):
    lo = lax.shift_right_logical(lax.bitcast_convert_type(_bf16_round(v[:, :PACKED]), I32), 16)
    hi = lax.bitcast_convert_type(_bf16_round(v[:, PACKED:]), I32) & HI_MASK
    return hi | lo


def _unpack_lo(w):
    return lax.bitcast_convert_type(lax.shift_left(w, 16), F32)


def _unpack_hi(w):
    return lax.bitcast_convert_type(w & HI_MASK, F32)


def _rms(v, g):
    r = lax.rsqrt(jnp.mean(v * v, axis=-1, keepdims=True) + NORM_EPS)
    return (v * r) * g


def _moe_combine(x, y_ref, slab):
    parts = [x[:, i * SC_ROW:(i + 1) * SC_ROW] for i in range(4)]
    for k in range(TOP_K):
        g = slab[:, TOP_K + k:TOP_K + k + 1]
        for h in range(2):
            w = y_ref[k, h]
            parts[h] = parts[h] + g * _unpack_lo(w)
            parts[2 + h] = parts[2 + h] + g * _unpack_hi(w)
    return jnp.concatenate(parts, axis=1)


def _block_transpose8(vs, lane):
    vs = list(vs)
    for dist in (4, 2, 1):
        width = SSM_GROUP * dist
        low = (lane % (2 * width)) < width
        for j in range(8):
            if j & dist:
                continue
            a, b = vs[j], vs[j + dist]
            vs[j] = jnp.where(low, a, pltpu.roll(b, width, axis=1))
            vs[j + dist] = jnp.where(low, pltpu.roll(a, LANES - width, axis=1), b)
    return vs


def _cmul_packed(c1, c2, v):
    return c1 * v + c2 * pltpu.roll(v, SSM_STATE, axis=1)


def _chunk_scan(s, carry_in, cf):
    outs = []
    carry = jnp.broadcast_to(carry_in, (SUBLANES, LANES))
    for rg in range(N_CHUNK // SUBLANES):
        x = s[rg * SUBLANES:(rg + 1) * SUBLANES]
        for j, d in enumerate((1, 2, 4)):
            x = x + _cmul_packed(cf[2 * j], cf[2 * j + 1], pltpu.roll(x, d, axis=0))
        x = x + _cmul_packed(cf[6], cf[7], carry)
        outs.append(x)
        carry = jnp.broadcast_to(x[SUBLANES - 1:SUBLANES], (SUBLANES, LANES))
    return jnp.concatenate(outs, axis=0)


def _s5_mixer(u, first, minw_ref, mcat_ref, coef_ref, u_s, y_s, h_s):
    blocks = SSM_WIDTH // LANES
    for b4 in range(blocks):
        u_s[b4] = u[:, b4 * LANES:(b4 + 1) * LANES]
    lane = lax.broadcasted_iota(I32, (N_CHUNK, LANES), 1)

    halves = [[None, None] for _ in range(SSM_GROUPS)]
    for b4 in range(blocks):
        for hh in range(2):
            xs = [u_s[b4, pl.ds(8 * hh + j, N_CHUNK, stride=CHUNK), :].astype(BF16) for j in range(8)]
            ws = _block_transpose8(xs, lane)
            for gl in range(8):
                halves[8 * b4 + gl][hh] = ws[gl]
    ug = [jnp.concatenate(h2, axis=1) for h2 in halves]

    carry_in = jnp.where(first, 0.0, h_s[N_CHUNK + 7:N_CHUNK + 8, :])
    h_s[7:8, :] = carry_in
    for g in range(SSM_GROUPS):
        cols = slice(g * LANES, (g + 1) * LANES)
        s_g = jnp.dot(ug[g], minw_ref[g], preferred_element_type=F32)
        cf = [coef_ref[k, :, cols] for k in range(8)]
        h_s[8:8 + N_CHUNK, cols] = _chunk_scan(s_g, carry_in[:, cols], cf)

    yg = []
    for g in range(SSM_GROUPS):
        hprev = h_s[7:7 + N_CHUNK, g * LANES:(g + 1) * LANES].astype(BF16)
        lhs = jnp.concatenate([ug[g], hprev], axis=1)
        yg.append(jnp.dot(lhs, mcat_ref[g], preferred_element_type=F32).astype(BF16))

    for b4 in range(blocks):
        for hh in range(2):
            ws = [yg[8 * b4 + gl][:, hh * LANES:(hh + 1) * LANES] for gl in range(8)]
            xs = _block_transpose8(ws, lane)
            for j in range(8):
                y_s[b4, pl.ds(8 * hh + j, N_CHUNK, stride=CHUNK), :] = xs[j].astype(F32)
    return jnp.concatenate([y_s[b4] for b4 in range(blocks)], axis=1)


def _mixer_kernel(*refs, first_layer):
    if first_layer:
        x_ref = refs[0]
        refs = refs[1:]
    else:
        x_ref, yprev_ref, slabprev_ref = refs[:3]
        refs = refs[3:]
    (gmix_ref, win_ref, minw_ref, mcat_ref, coef_ref, dskip_ref, wglu_ref, bglu_ref, poolw_ref, pscale_ref,
     convw_ref, gbr_ref, wout_ref, gffn_ref, rw_ref, rb_ref, utri_ref,
     x1_ref, hp_ref, slab_ref, slabt_ref, cnt_ref,
     u_s, y_s, h_s, pbuf_ref, zbuf_ref, cntacc_ref) = refs

    b = pl.program_id(0)
    l = pl.program_id(1)
    seq_start = l == 0

    @pl.when(jnp.logical_and(b == 0, l == 0))
    def _():
        cntacc_ref[...] = jnp.zeros_like(cntacc_ref)

    if first_layer:
        x = x_ref[...]
    else:
        x = _moe_combine(x_ref[...], yprev_ref, slabprev_ref[...])

    hn = _rms(x, gmix_ref[...]).astype(BF16)
    proj = jnp.dot(hn, win_ref[...], preferred_element_type=F32)
    u = proj[:, :SSM_WIDTH]
    p = proj[:, SSM_WIDTH:SSM_WIDTH + POOL_WIDTH]
    c0 = SSM_WIDTH + POOL_WIDTH
    bg = proj[:, c0:c0 + CONV_WIDTH]
    cg = proj[:, c0 + CONV_WIDTH:c0 + 2 * CONV_WIDTH]
    hv = proj[:, c0 + 2 * CONV_WIDTH:]

    y = _s5_mixer(u, seq_start, minw_ref, mcat_ref, coef_ref, u_s, y_s, h_s) + dskip_ref[...] * u
    z = jax.nn.gelu(y)
    glu = jnp.dot(z.astype(BF16), wglu_ref[...], preferred_element_type=F32) + bglu_ref[...]
    y_ssm = z * jax.nn.sigmoid(glu)

    tail = POOL_TAIL
    pbuf_ref[0, 0:tail, :] = jnp.where(seq_start, 0.0, pbuf_ref[0, TILE_T:TILE_T + tail, :])
    pbuf_ref[0, tail:, :] = p
    sums = []
    for stage, w in enumerate(POOL_WINDOWS):
        src = 0 if stage == 0 else 1 + (stage - 1) % 2
        dst = 1 + stage % 2
        d = w // 2
        lo = SUBLANES * (stage + 1)
        n = tail + TILE_T - lo
        e = pbuf_ref[src, pl.ds(lo, n), :] + pbuf_ref[src, pl.ds(lo - d, n), :]
        if stage + 1 < len(POOL_WINDOWS):
            pbuf_ref[dst, pl.ds(lo, n), :] = e
        sums.append(e[tail - lo:, :])
    lane_p = lax.broadcasted_iota(I32, (TILE_T, POOL_WIDTH), 1)
    total = sums[-1]
    win = jnp.full((TILE_T, POOL_WIDTH), POOL_WINDOWS[-1], I32)
    for i in range(len(POOL_WINDOWS) - 2, -1, -1):
        sel = lane_p < (i + 1) * POOL_GROUP
        total = jnp.where(sel, sums[i], total)
        win = jnp.where(sel, POOL_WINDOWS[i], win)
    row_p = lax.broadcasted_iota(I32, (TILE_T, POOL_WIDTH), 0) + l * TILE_T
    count = jnp.minimum(row_p + 1, win).astype(F32)
    pooled = total / count - p
    mixed = jnp.dot(pooled.astype(BF16), poolw_ref[...], preferred_element_type=F32)
    y_pool = mixed * pscale_ref[...]

    zc = cg * hv
    zbuf_ref[0:8, :] = jnp.where(seq_start, 0.0, zbuf_ref[TILE_T:TILE_T + 8, :])
    zbuf_ref[8:, :] = zc
    yc = (convw_ref[0:1, :] * zbuf_ref[pl.ds(6, TILE_T), :]
          + convw_ref[1:2, :] * zbuf_ref[pl.ds(7, TILE_T), :]
          + convw_ref[2:3, :] * zc)
    y_conv = bg * yc

    gbr = gbr_ref[...]
    s1 = SSM_WIDTH + POOL_WIDTH
    mixed_all = jnp.concatenate([_rms(y_ssm, gbr[:, :SSM_WIDTH]).astype(BF16),
                                 _rms(y_pool, gbr[:, SSM_WIDTH:s1]).astype(BF16),
                                 _rms(y_conv, gbr[:, s1:]).astype(BF16)], axis=1)
    x1 = x + jnp.dot(mixed_all, wout_ref[...], preferred_element_type=F32)
    x1_ref[...] = x1

    hn2 = _rms(x1, gffn_ref[...])
    packed = _pack_rows(hn2)
    hp_ref[0] = packed[:, :SC_ROW]
    hp_ref[1] = packed[:, SC_ROW:]
    logits = jnp.dot(hn2.astype(BF16), rw_ref[...], preferred_element_type=F32) + rb_ref[...]
    work = logits.T[:N_EXPERTS, :]
    eio = lax.broadcasted_iota(I32, (N_EXPERTS, TILE_T), 0).astype(F32)
    neg = jnp.float32(-jnp.inf)
    vals, idxs = [], []
    for _ in range(TOP_K):
        m = jnp.max(work, axis=0, keepdims=True)
        idx = jnp.min(jnp.where(work == m, eio, float(N_EXPERTS)), axis=0, keepdims=True)
        vals.append(m)
        idxs.append(idx)
        work = jnp.where(eio == idx, neg, work)
    exps = [jnp.exp(v - vals[0]) for v in vals]
    denom = exps[0] + exps[1] + exps[2] + exps[3]
    gates = [e / denom for e in exps]

    onehot = jnp.zeros((N_EXPERTS, TILE_T), F32)
    for idx in idxs:
        onehot = jnp.where(eio == idx, 1.0, onehot)
    before = jnp.dot(onehot.astype(BF16), utri_ref[...], preferred_element_type=F32) + cntacc_ref[:, 0:1]
    ranks = [jnp.sum(jnp.where(eio == idx, before, 0.0), axis=0, keepdims=True) for idx in idxs]
    row = lax.broadcasted_iota(I32, (LANES, TILE_T), 0)
    slab_t = jnp.zeros((LANES, TILE_T), F32)
    for k in range(TOP_K):
        slab_t = jnp.where(row == k, idxs[k], slab_t)
        slab_t = jnp.where(row == TOP_K + k, gates[k], slab_t)
        slab_t = jnp.where(row == 2 * TOP_K + k, ranks[k], slab_t)
    slabt_ref[...] = slab_t[:2 * SUBLANES, :]
    slab_ref[...] = slab_t.T
    newcnt = cntacc_ref[:, 0:1] + jnp.sum(onehot, axis=1, keepdims=True)
    cntacc_ref[...] = jnp.broadcast_to(newcnt, cntacc_ref.shape)
    cnt_ref[...] = jnp.broadcast_to(newcnt, cnt_ref.shape)


def _layer_spec(shape, layer):
    nd = len(shape) - 1
    sel = layer if shape[0] > 1 else 0
    return pl.BlockSpec((None,) + tuple(shape[1:]), lambda b, l, nd=nd, sel=sel: (sel,) + (0,) * nd,
                        pipeline_mode=pl.Buffered(1))


def _mixer_call(x, x_tile0, yprev, slabprev, lp, layer, batch, seq):
    first_layer = yprev is None
    n_tok = batch * seq
    steps = seq // TILE_T
    tok_spec = lambda w: pl.BlockSpec((TILE_T, w), lambda b, l: (b * steps + l, 0))
    in_specs = [pl.BlockSpec((TILE_T, D_MODEL), lambda b, l: (x_tile0 + b * steps + l, 0))]
    args = [x]
    if not first_layer:
        in_specs += [pl.BlockSpec((TOP_K, 2, TILE_T, SC_ROW), lambda b, l: (0, 0, b * steps + l, 0)), tok_spec(LANES)]
        args += [yprev, slabprev]
    weights = [lp['gmix'], lp['win'], lp['minw'], lp['mcat'], lp['coef'], lp['dskip'], lp['wglu'], lp['bglu'],
               lp['poolw'], lp['pscale'], lp['convw'], lp['gbr'], lp['wout'], lp['gffn'], lp['rw'], lp['rb'],
               lp['utri']]
    in_specs += [_layer_spec(w.shape, layer) for w in weights]
    args += weights
    out_shape = [jax.ShapeDtypeStruct((n_tok, D_MODEL), F32),
                 jax.ShapeDtypeStruct((2, n_tok, SC_ROW), I32),
                 jax.ShapeDtypeStruct((n_tok, LANES), F32),
                 jax.ShapeDtypeStruct((2 * SUBLANES, n_tok), F32),
                 jax.ShapeDtypeStruct((N_EXPERTS, LANES), F32)]
    out_specs = [tok_spec(D_MODEL), pl.BlockSpec((2, TILE_T, SC_ROW), lambda b, l: (0, b * steps + l, 0)), tok_spec(LANES),
                 pl.BlockSpec((2 * SUBLANES, TILE_T), lambda b, l: (0, b * steps + l)),
                 pl.BlockSpec((N_EXPERTS, LANES), lambda b, l: (0, 0))]
    scratch = [pltpu.VMEM((SSM_WIDTH // LANES, TILE_T, LANES), F32),
               pltpu.VMEM((SSM_WIDTH // LANES, TILE_T, LANES), F32),
               pltpu.VMEM((SUBLANES + N_CHUNK, SSM_GROUPS * LANES), F32),
               pltpu.VMEM((3, POOL_TAIL + TILE_T, POOL_WIDTH), F32),
               pltpu.VMEM((8 + TILE_T, CONV_WIDTH), F32),
               pltpu.VMEM((N_EXPERTS, LANES), F32)]
    return pl.pallas_call(
        functools.partial(_mixer_kernel, first_layer=first_layer),
        grid=(batch, steps),
        in_specs=in_specs, out_specs=out_specs, out_shape=out_shape, scratch_shapes=scratch,
        compiler_params=pltpu.CompilerParams(dimension_semantics=("arbitrary", "arbitrary"),
                                             vmem_limit_bytes=VMEM_LIMIT),
        name="mixer_first" if first_layer else "mixer_next",
    )(*args)


def _route_kernel(slabt_ref, cnt_ref, dest_ref, meta_ref, *, n_steps):
    lane1 = lax.broadcasted_iota(I32, (1, LANES), 1)
    e_row = lax.broadcasted_iota(I32, (N_EXPERTS, LANES), 0)
    e_lane = lax.broadcasted_iota(I32, (N_EXPERTS, LANES), 1)
    cnt = cnt_ref[...].astype(I32)
    counts = jnp.sum(jnp.where(e_row == e_lane, cnt, 0), axis=0, keepdims=True)
    padded = ((counts + (STEP_ROWS - 1)) // STEP_ROWS) * STEP_ROWS
    pend = padded
    for sh in (1, 2, 4, 8, 16):
        pend = pend + jnp.where(lane1 >= sh, pltpu.roll(pend, sh, axis=1), 0)

    @pl.when(pl.program_id(0) == 0)
    def _():
        rows = meta_ref.shape[0]
        bstart = lax.broadcasted_iota(I32, (rows, LANES), 0) * STEP_ROWS
        lane = lax.broadcasted_iota(I32, (rows, LANES), 1)
        done = jnp.where(jnp.logical_and(lane < N_EXPERTS, pend <= bstart), 1, 0)
        be = jnp.minimum(jnp.sum(done, axis=1, keepdims=True), N_EXPERTS - 1)
        mine = lane == be
        first_row = jnp.sum(jnp.where(mine, pend - counts, 0), axis=1, keepdims=True)
        used = bstart[:, 0:1] + STEP_ROWS - jnp.maximum(bstart[:, 0:1], first_row)
        nblk = jnp.clip((used + (EXPERT_BLOCK - 1)) // EXPERT_BLOCK, 0, STEP_ROWS // EXPERT_BLOCK)
        total = jnp.sum(jnp.where(lane1 == N_EXPERTS - 1, pend, 0), axis=1, keepdims=True)
        row = lax.broadcasted_iota(I32, (rows, LANES), 0)
        table = jnp.where(lane == 0, be, jnp.where(lane == 1, nblk, 0))
        meta_ref[...] = jnp.where(row == n_steps, total // STEP_ROWS, table)

    last_col = jnp.sum(jnp.where(e_row == e_lane, pend - 1, 0), axis=1, keepdims=True)
    tokens = slabt_ref.shape[1]
    eio = lax.broadcasted_iota(I32, (N_EXPERTS, tokens), 0)
    row = lax.broadcasted_iota(I32, (SUBLANES, tokens), 0)
    out = jnp.zeros((SUBLANES, tokens), I32)
    for k in range(TOP_K):
        idx = slabt_ref[k:k + 1, :].astype(I32)
        rank = slabt_ref[2 * TOP_K + k:2 * TOP_K + k + 1, :].astype(I32)
        base = jnp.sum(jnp.where(eio == idx, last_col, 0), axis=0, keepdims=True)
        for h in range(2):
            out = jnp.where(row == 2 * k + h, base - rank + h * n_steps * STEP_ROWS, out)
    dest_ref[...] = out


def _route_call(slabt, counts, n_tok, n_steps):
    meta_rows = ((n_steps + 1 + SUBLANES - 1) // SUBLANES) * SUBLANES
    return pl.pallas_call(
        functools.partial(_route_kernel, n_steps=n_steps),
        grid=(n_tok // ROUTE_T,),
        in_specs=[pl.BlockSpec((2 * SUBLANES, ROUTE_T), lambda i: (0, i)),
                  pl.BlockSpec((N_EXPERTS, LANES), lambda i: (0, 0))],
        out_specs=[pl.BlockSpec((SUBLANES, ROUTE_T), lambda i: (0, i)),
                   pl.BlockSpec((meta_rows, LANES), lambda i: (0, 0))],
        out_shape=[jax.ShapeDtypeStruct((SUBLANES, n_tok), I32),
                   jax.ShapeDtypeStruct((meta_rows, LANES), I32)],
        compiler_params=pltpu.CompilerParams(dimension_semantics=("arbitrary",)),
        name="route",
    )(slabt, counts)


def _sc_mesh():
    return plsc.VectorSubcoreMesh(core_axis_name="core", subcore_axis_name="subcore")


def _dispatch(src, idx, n_rows):
    windows = src.shape[0] // SC_WINDOW

    @functools.partial(pl.kernel, out_type=jax.ShapeDtypeStruct((n_rows, SC_ROW), src.dtype),
                       mesh=_sc_mesh(), scratch_types=[])
    def scatter_rows(x_hbm, i_hbm, o_hbm):
        def body(x_vmem, i_vmem):
            pltpu.sync_copy(x_vmem, o_hbm.at[i_vmem.at[0]])

        pltpu.emit_pipeline(
            body,
            grid=(idx.shape[1] // SC_WINDOW,),
            in_specs=[pl.BlockSpec((SC_WINDOW, SC_ROW), lambda i: (lax.rem(i, windows), 0)),
                      pl.BlockSpec((1, SC_WINDOW), lambda i: (0, i))],
            out_specs=[],
            core_axis_name=("core", "subcore"),
            dimension_semantics=(pltpu.PARALLEL,),
        )(x_hbm, i_hbm)

    return scatter_rows(src, idx)


def _combine(src, idx):
    n_idx = idx.shape[1]

    @functools.partial(pl.kernel, out_type=jax.ShapeDtypeStruct((n_idx, SC_ROW), src.dtype),
                       mesh=_sc_mesh(), scratch_types=[])
    def gather_rows(y_hbm, i_hbm, o_hbm):
        def body(i_vmem, o_vmem):
            pltpu.sync_copy(y_hbm.at[i_vmem.at[0]], o_vmem)

        pltpu.emit_pipeline(
            body,
            grid=(n_idx // SC_WINDOW,),
            in_specs=[pl.BlockSpec((1, SC_WINDOW), lambda i: (0, i))],
            out_specs=[pl.BlockSpec((SC_WINDOW, SC_ROW), lambda i: (i, 0))],
            core_axis_name=("core", "subcore"),
            dimension_semantics=(pltpu.PARALLEL,),
        )(i_hbm, o_hbm)

    return gather_rows(src, idx)


def _expert_rows(xs_ref, ys_ref, row0, n, wgu_s, bgu_ref, wdn_s, bdn_ref, act_s):
    rows = pl.ds(pl.multiple_of(row0, EXPERT_BLOCK), n)
    arow = slice(0, n)
    w0 = xs_ref[0, rows, :]
    w1 = xs_ref[1, rows, :]
    xb = jnp.concatenate([_unpack_lo(w0).astype(BF16), _unpack_lo(w1).astype(BF16),
                          _unpack_hi(w0).astype(BF16), _unpack_hi(w1).astype(BF16)], axis=1)
    half = D_EXPERT // 2
    for j in range(2):
        cols = slice(j * half, (j + 1) * half)
        ucols = slice(D_EXPERT + j * half, D_EXPERT + (j + 1) * half)
        g = jnp.dot(xb, wgu_s[:, cols], preferred_element_type=F32) + bgu_ref[:, cols]
        up = jnp.dot(xb, wgu_s[:, ucols], preferred_element_type=F32) + bgu_ref[:, ucols]
        g = jnp.minimum(g, SWIGLU_LIMIT)
        up = jnp.clip(up, -SWIGLU_LIMIT, SWIGLU_LIMIT)
        act_s[arow, cols] = (g * jax.nn.sigmoid(SWIGLU_ALPHA * g) * (up + 1.0)).astype(BF16)
    y = jnp.dot(act_s[arow, :], wdn_s[...], preferred_element_type=F32) + bdn_ref[...]
    packed = _pack_rows(y)
    ys_ref[0, rows, :] = packed[:, :SC_ROW]
    ys_ref[1, rows, :] = packed[:, SC_ROW:]


def _expert_kernel(be_ref, nb_ref, nv_ref, xs_ref, wgu_ref, bgu_ref, wdn_ref, bdn_ref, ys_ref, wgu_s, wdn_s, act_s):
    blk = pl.program_id(0)
    prev = be_ref[jnp.maximum(blk - 1, 0)]
    changed = jnp.logical_or(blk == 0, be_ref[blk] != prev)
    valid = blk < nv_ref[0]

    @pl.when(jnp.logical_and(valid, changed))
    def _():
        chunk = 128

        def cast_gu(i, _):
            rows = pl.ds(pl.multiple_of(i * chunk, chunk), chunk)
            wgu_s[rows, :] = wgu_ref[rows, :].astype(BF16)
            return 0

        def cast_dn(i, _):
            rows = pl.ds(pl.multiple_of(i * chunk, chunk), chunk)
            wdn_s[rows, :] = wdn_ref[rows, :].astype(BF16)
            return 0

        lax.fori_loop(0, D_MODEL // chunk, cast_gu, 0)
        lax.fori_loop(0, D_EXPERT // chunk, cast_dn, 0)

    nblk = nb_ref[blk]
    first = STEP_ROWS - nblk * EXPERT_BLOCK
    big = nblk // RUN_BLOCKS

    @pl.when(jnp.logical_and(valid, big > 0))
    def _():
        def one_run(i, _):
            row0 = first + (nblk - (i + 1) * RUN_BLOCKS) * EXPERT_BLOCK
            _expert_rows(xs_ref, ys_ref, row0, RUN_BLOCKS * EXPERT_BLOCK, wgu_s, bgu_ref, wdn_s, bdn_ref, act_s)
            return 0

        lax.fori_loop(0, big, one_run, 0)

    bit = RUN_BLOCKS // 2
    while bit:
        @pl.when(jnp.logical_and(valid, (nblk & bit) != 0))
        def _(bit=bit):
            row0 = first + (nblk & (bit - 1)) * EXPERT_BLOCK
            _expert_rows(xs_ref, ys_ref, row0, bit * EXPERT_BLOCK, wgu_s, bgu_ref, wdn_s, bdn_ref, act_s)
        bit //= 2


def _expert_call(xs, step_e, step_nblk, n_valid, w_gu, b_gu, w_dn, b_dn, layer, n_steps):
    def row_map(i, be, nb, nv):
        return (0, jnp.minimum(i, jnp.maximum(nv[0] - 1, 0)), 0)

    def w_map(i, be, nb, nv):
        return (layer, be[i], 0, 0)

    grid_spec = pltpu.PrefetchScalarGridSpec(
        num_scalar_prefetch=3,
        grid=(n_steps,),
        in_specs=[pl.BlockSpec((2, STEP_ROWS, SC_ROW), row_map),
                  pl.BlockSpec((None, None, D_MODEL, 2 * D_EXPERT), w_map),
                  pl.BlockSpec((None, None, 1, 2 * D_EXPERT), w_map),
                  pl.BlockSpec((None, None, D_EXPERT, D_MODEL), w_map),
                  pl.BlockSpec((None, None, 1, D_MODEL), w_map)],
        out_specs=pl.BlockSpec((2, STEP_ROWS, SC_ROW), row_map),
        scratch_shapes=[pltpu.VMEM((D_MODEL, 2 * D_EXPERT), BF16),
                        pltpu.VMEM((D_EXPERT, D_MODEL), BF16),
                        pltpu.VMEM((RUN_BLOCKS * EXPERT_BLOCK, D_EXPERT), BF16)])
    return pl.pallas_call(
        _expert_kernel,
        grid_spec=grid_spec,
        out_shape=jax.ShapeDtypeStruct((2, n_steps * STEP_ROWS, SC_ROW), I32),
        compiler_params=pltpu.CompilerParams(dimension_semantics=("arbitrary",),
                                             vmem_limit_bytes=VMEM_LIMIT),
        name="experts",
    )(step_e, step_nblk, n_valid, xs, w_gu, b_gu, w_dn, b_dn)


def _final_kernel(x_ref, y_ref, slab_ref, g_ref, *rest):
    o_ref = rest[-1]
    x = _moe_combine(x_ref[...], y_ref, slab_ref[...])
    o_ref[...] = _rms(x, g_ref[...])


def _final_call(x1, ysg, slab, g, out_row0, n_out, out_prev):
    t = FINAL_T
    out_tile0 = out_row0 // t
    in_specs = [pl.BlockSpec((t, D_MODEL), lambda i: (i, 0)),
                pl.BlockSpec((TOP_K, 2, t, SC_ROW), lambda i: (0, 0, i, 0)),
                pl.BlockSpec((t, LANES), lambda i: (i, 0)),
                pl.BlockSpec((1, D_MODEL), lambda i: (0, 0))]
    args = [x1, ysg, slab, g]
    aliases = {}
    if out_prev is not None:
        in_specs.append(pl.BlockSpec(memory_space=pl.ANY))
        args.append(out_prev)
        aliases = {4: 0}
    return pl.pallas_call(
        _final_kernel,
        grid=(x1.shape[0] // t,),
        in_specs=in_specs,
        out_specs=pl.BlockSpec((t, D_MODEL), lambda i: (out_tile0 + i, 0)),
        out_shape=jax.ShapeDtypeStruct((n_out, D_MODEL), F32),
        input_output_aliases=aliases,
        compiler_params=pltpu.CompilerParams(dimension_semantics=("arbitrary",)),
        name="final_norm",
    )(*args)


def _s5_tables(lam_re, lam_im, b_re, b_im, c_re, c_im, log_step):
    lr = jnp.minimum(lam_re.astype(F32), LAMBDA_RE_MAX)
    li = lam_im.astype(F32)
    step = jnp.exp(log_step.astype(F32))[:, None]
    mag = jnp.exp(lr * step)
    ar = mag * jnp.cos(li * step)
    ai = mag * jnp.sin(li * step)
    nr = ar - 1.0
    den = lr * lr + li * li
    kr = (nr * lr + ai * li) / den
    ki = (ai * lr - nr * li) / den
    bre = b_re.astype(F32)
    bim = b_im.astype(F32)
    bbar_r = kr[..., None] * bre - ki[..., None] * bim
    bbar_i = kr[..., None] * bim + ki[..., None] * bre
    cre = c_re.astype(F32)
    cim = c_im.astype(F32)

    def power(m):
        m = jnp.asarray(m, F32)[..., None, None]
        mg = jnp.exp(m * lr * step)
        return mg * jnp.cos(m * li * step), mg * jnp.sin(m * li * step)

    pw_r, pw_i = power(jnp.arange(CHUNK + 1))
    cp_r = cre[None] * pw_r[:, :, None, :] - cim[None] * pw_i[:, :, None, :]
    cp_i = cre[None] * pw_i[:, :, None, :] + cim[None] * pw_r[:, :, None, :]
    width = CHUNK * SSM_GROUP
    kcat = (jnp.einsum('tgap,gph->ghta', cp_r[:CHUNK], bbar_r)
            - jnp.einsum('tgap,gph->ghta', cp_i[:CHUNK], bbar_i)).astype(BF16)
    lag = jnp.arange(CHUNK)
    place = (lag[None, :, None] - lag[:, None, None] == lag[None, None, :]).astype(BF16)
    m_intra = jnp.einsum('stl,ghla->gshta', place, kcat,
                         preferred_element_type=F32).astype(BF16).reshape(SSM_GROUPS, width, width)
    mo_r = cp_r[1:].transpose(1, 3, 0, 2).reshape(SSM_GROUPS, SSM_STATE, width)
    mo_i = -cp_i[1:].transpose(1, 3, 0, 2).reshape(SSM_GROUPS, SSM_STATE, width)
    mcat = jnp.concatenate([m_intra, mo_r.astype(BF16), mo_i.astype(BF16)], axis=1)
    q_r = pw_r[CHUNK - 1::-1][:CHUNK][:, :, :, None]
    q_i = pw_i[CHUNK - 1::-1][:CHUNK][:, :, :, None]
    mn_r = q_r * bbar_r[None] - q_i * bbar_i[None]
    mn_i = q_r * bbar_i[None] + q_i * bbar_r[None]
    minw = jnp.concatenate([mn_r, mn_i], axis=2).transpose(1, 0, 3, 2).reshape(
        SSM_GROUPS, CHUNK * SSM_GROUP, 2 * SSM_STATE).astype(BF16)

    def packed(j):
        pr, pi = power(float(CHUNK * j))
        return (jnp.concatenate([pr, pr], axis=-1).reshape(1, SSM_GROUPS * LANES),
                jnp.concatenate([-pi, pi], axis=-1).reshape(1, SSM_GROUPS * LANES))

    rows = jnp.arange(SUBLANES)[:, None]
    kinds = []
    for d in (1, 2, 4):
        c1, c2 = packed(d)
        mask = (rows >= d).astype(F32)
        kinds += [mask * c1, mask * c2]
    pk = [packed(r + 1) for r in range(SUBLANES)]
    kinds += [jnp.concatenate([p[0] for p in pk], axis=0), jnp.concatenate([p[1] for p in pk], axis=0)]
    coef = jnp.stack(kinds, axis=0)
    return minw, mcat, coef


def _mixer_params(norm_mix, w_in, ssm_lam_re, ssm_lam_im, ssm_b_re, ssm_b_im, ssm_c_re, ssm_c_im, ssm_d,
                  ssm_log_step, ssm_w_glu, ssm_b_glu, pool_w, pool_scale, conv_w, branch_norm, w_out, norm_ffn,
                  router_w, router_b):
    depth = w_in.shape[0]
    minw, mcat, coef = jax.vmap(_s5_tables)(ssm_lam_re, ssm_lam_im, ssm_b_re, ssm_b_im, ssm_c_re, ssm_c_im,
                                            ssm_log_step)
    eye4 = jnp.eye(len(POOL_WINDOWS), dtype=F32)
    poolw = jnp.einsum('lgcd,gk->lgckd', pool_w.astype(F32), eye4).reshape(depth, POOL_WIDTH, POOL_WIDTH).astype(BF16)
    rw = jnp.pad(router_w.astype(F32), ((0, 0), (0, 0), (0, LANES - N_EXPERTS))).astype(BF16)
    rb = jnp.pad(router_b.astype(F32), ((0, 0), (0, LANES - N_EXPERTS))).reshape(depth, 1, LANES)
    row = lambda v: v.reshape(depth, 1, -1).astype(F32)
    utri = jnp.triu(jnp.ones((TILE_T, TILE_T), F32), 1).astype(BF16)
    return dict(gmix=row(norm_mix), win=w_in.astype(BF16), minw=minw, mcat=mcat, coef=coef, dskip=row(ssm_d),
                wglu=ssm_w_glu.astype(BF16), bglu=row(ssm_b_glu), poolw=poolw, pscale=row(pool_scale),
                convw=conv_w.astype(F32), gbr=row(branch_norm), wout=w_out.astype(BF16),
                gffn=row(norm_ffn), rw=rw, rb=rb, utri=utri[None])


def kernel(x, norm_mix, w_in, ssm_lam_re, ssm_lam_im, ssm_b_re, ssm_b_im, ssm_c_re, ssm_c_im, ssm_d, ssm_log_step, ssm_w_glu, ssm_b_glu, pool_w, pool_scale, conv_w, branch_norm, w_out, norm_ffn, router_w, router_b, w_gate_up, b_gate_up, w_down, b_down, final_norm):
    batch, seq, d_model = x.shape
    assert d_model == D_MODEL and seq % TILE_T == 0
    depth = w_in.shape[0]
    n_streams = 2 if batch % 2 == 0 else 1
    sb = batch // n_streams
    n_tok = sb * seq
    assert n_tok % ROUTE_T == 0 and n_tok % SC_WINDOW == 0
    n_steps = -(-(n_tok * TOP_K + N_EXPERTS * (STEP_ROWS - 1)) // STEP_ROWS)
    n_rows = n_steps * STEP_ROWS
    tiles = n_tok // TILE_T
    b_gu = b_gate_up.reshape(depth, N_EXPERTS, 1, 2 * D_EXPERT)
    b_dn = b_down.reshape(depth, N_EXPERTS, 1, D_MODEL)

    xin = [x.reshape(batch * seq, D_MODEL)] * n_streams
    tile0 = [s * tiles for s in range(n_streams)]
    ysg = [None] * n_streams
    slab = [None] * n_streams
    lp = _mixer_params(norm_mix, w_in, ssm_lam_re, ssm_lam_im, ssm_b_re, ssm_b_im, ssm_c_re, ssm_c_im, ssm_d,
                       ssm_log_step, ssm_w_glu, ssm_b_glu, pool_w, pool_scale, conv_w, branch_norm, w_out,
                       norm_ffn, router_w, router_b)
    for l in range(depth):
        routed = []
        for s in range(n_streams):
            xin[s], hp, slab[s], slabt, counts = _mixer_call(xin[s], tile0[s], ysg[s], slab[s], lp, l, sb, seq)
            tile0[s] = 0
            dest, meta = _route_call(slabt, counts, n_tok, n_steps)
            idx = dest.reshape(1, 2 * TOP_K * n_tok)
            xs = _dispatch(hp.reshape(2 * n_tok, SC_ROW), idx, 2 * n_rows).reshape(2, n_rows, SC_ROW)
            routed.append((xs, idx, meta[:n_steps, 0], meta[:n_steps, 1], meta[n_steps:n_steps + 1, 0]))
        for s, (xs, idx, step_e, step_nblk, n_valid) in enumerate(routed):
            ys = _expert_call(xs, step_e, step_nblk, n_valid, w_gate_up, b_gu, w_down, b_dn, l, n_steps)
            ysg[s] = _combine(ys.reshape(2 * n_rows, SC_ROW), idx).reshape(TOP_K, 2, n_tok, SC_ROW)
    g = final_norm.reshape(1, D_MODEL).astype(F32)
    out = None
    for s in range(n_streams):
        out = _final_call(xin[s], ysg[s], slab[s], g, s * n_tok, batch * seq, out)
    return out.reshape(batch, seq, D_MODEL)
```

```python
import functools

import jax
import jax.numpy as jnp
from jax import lax
from jax.experimental import pallas as pl
from jax.experimental.pallas import tpu as pltpu
from jax.experimental.pallas import tpu_sc as plsc

F32 = jnp.float32
BF16 = jnp.bfloat16
I32 = jnp.int32

D_MODEL = 1024
SSM_WIDTH = 512
POOL_WIDTH = 256
CONV_WIDTH = 256
SSM_GROUP = 16
SSM_GROUPS = 32
SSM_STATE = 64
LAMBDA_RE_MAX = -1e-4
POOL_WINDOWS = (2, 4, 8, 16)
POOL_GROUP = 64
N_EXPERTS = 32
TOP_K = 4
D_EXPERT = 1024
SWIGLU_LIMIT = 7.0
SWIGLU_ALPHA = 1.702
EXPERT_BLOCK = 256
STEP_ROWS = 2048
RUN_BLOCKS = 4
NORM_EPS = 1e-5

LANES = 128
SUBLANES = 8
TILE_T = 512
FINAL_T = 1024
CHUNK = 16
N_CHUNK = TILE_T // CHUNK
POOL_TAIL = SUBLANES * len(POOL_WINDOWS)
assert POOL_WINDOWS == tuple(2 ** (k + 1) for k in range(len(POOL_WINDOWS))) and POOL_TAIL >= POOL_WINDOWS[-1]
PACKED = D_MODEL // 2
ROUTE_T = 2048
SC_WINDOW = 128
SC_ROW = PACKED // 2
HI_MASK = -65536
VMEM_LIMIT = 56 * 1024 * 1024


def _bf16_round(v):
    return v.astype(BF16).astype(F32)


def _pack_rows(v):
    lo = lax.shift_right_logical(lax.bitcast_convert_type(_bf16_round(v[:, :PACKED]), I32), 16)
    hi = lax.bitcast_convert_type(_bf16_round(v[:, PACKED:]), I32) & HI_MASK
    return hi | lo


def _unpack_lo(w):
    return lax.bitcast_convert_type(lax.shift_left(w, 16), F32)


def _unpack_hi(w):
    return lax.bitcast_convert_type(w & HI_MASK, F32)


def _rms(v, g):
    r = lax.rsqrt(jnp.mean(v * v, axis=-1, keepdims=True) + NORM_EPS)
    return (v * r) * g


def _moe_combine(x, y_ref, slab):
    parts = [x[:, i * SC_ROW:(i + 1) * SC_ROW] for i in range(4)]
    for k in range(TOP_K):
        g = slab[:, TOP_K + k:TOP_K + k + 1]
        for h in range(2):
            w = y_ref[k, h]
            parts[h] = parts[h] + g * _unpack_lo(w)
            parts[2 + h] = parts[2 + h] + g * _unpack_hi(w)
    return jnp.concatenate(parts, axis=1)


def _block_transpose8(vs, lane):
    vs = list(vs)
    for dist in (4, 2, 1):
        width = SSM_GROUP * dist
        low = (lane % (2 * width)) < width
        for j in range(8):
            if j & dist:
                continue
            a, b = vs[j], vs[j + dist]
            vs[j] = jnp.where(low, a, pltpu.roll(b, width, axis=1))
            vs[j + dist] = jnp.where(low, pltpu.roll(a, LANES - width, axis=1), b)
    return vs


def _cmul_packed(c1, c2, v):
    return c1 * v + c2 * pltpu.roll(v, SSM_STATE, axis=1)


def _chunk_scan(s, carry_in, cf):
    outs = []
    carry = jnp.broadcast_to(carry_in, (SUBLANES, LANES))
    for rg in range(N_CHUNK // SUBLANES):
        x = s[rg * SUBLANES:(rg + 1) * SUBLANES]
        for j, d in enumerate((1, 2, 4)):
            x = x + _cmul_packed(cf[2 * j], cf[2 * j + 1], pltpu.roll(x, d, axis=0))
        x = x + _cmul_packed(cf[6], cf[7], carry)
        outs.append(x)
        carry = jnp.broadcast_to(x[SUBLANES - 1:SUBLANES], (SUBLANES, LANES))
    return jnp.concatenate(outs, axis=0)


def _s5_mixer(u, first, minw_ref, mcat_ref, coef_ref, u_s, y_s, h_s):
    blocks = SSM_WIDTH // LANES
    for b4 in range(blocks):
        u_s[b4] = u[:, b4 * LANES:(b4 + 1) * LANES]
    lane = lax.broadcasted_iota(I32, (N_CHUNK, LANES), 1)

    halves = [[None, None] for _ in range(SSM_GROUPS)]
    for b4 in range(blocks):
        for hh in range(2):
            xs = [u_s[b4, pl.ds(8 * hh + j, N_CHUNK, stride=CHUNK), :].astype(BF16) for j in range(8)]
            ws = _block_transpose8(xs, lane)
            for gl in range(8):
                halves[8 * b4 + gl][hh] = ws[gl]
    ug = [jnp.concatenate(h2, axis=1) for h2 in halves]

    carry_in = jnp.where(first, 0.0, h_s[N_CHUNK + 7:N_CHUNK + 8, :])
    h_s[7:8, :] = carry_in
    for g in range(SSM_GROUPS):
        cols = slice(g * LANES, (g + 1) * LANES)
        s_g = jnp.dot(ug[g], minw_ref[g], preferred_element_type=F32)
        cf = [coef_ref[k, :, cols] for k in range(8)]
        h_s[8:8 + N_CHUNK, cols] = _chunk_scan(s_g, carry_in[:, cols], cf)

    yg = []
    for g in range(SSM_GROUPS):
        hprev = h_s[7:7 + N_CHUNK, g * LANES:(g + 1) * LANES].astype(BF16)
        lhs = jnp.concatenate([ug[g], hprev], axis=1)
        yg.append(jnp.dot(lhs, mcat_ref[g], preferred_element_type=F32).astype(BF16))

    for b4 in range(blocks):
        for hh in range(2):
            ws = [yg[8 * b4 + gl][:, hh * LANES:(hh + 1) * LANES] for gl in range(8)]
            xs = _block_transpose8(ws, lane)
            for j in range(8):
                y_s[b4, pl.ds(8 * hh + j, N_CHUNK, stride=CHUNK), :] = xs[j].astype(F32)
    return jnp.concatenate([y_s[b4] for b4 in range(blocks)], axis=1)


def _mixer_kernel(*refs, first_layer):
    if first_layer:
        x_ref = refs[0]
        refs = refs[1:]
    else:
        x_ref, yprev_ref, slabprev_ref = refs[:3]
        refs = refs[3:]
    (gmix_ref, win_ref, minw_ref, mcat_ref, coef_ref, dskip_ref, wglu_ref, bglu_ref, poolw_ref, pscale_ref,
     convw_ref, gbr_ref, wout_ref, gffn_ref, rw_ref, rb_ref, utri_ref,
     x1_ref, hp_ref, slab_ref, slabt_ref, cnt_ref,
     u_s, y_s, h_s, pbuf_ref, zbuf_ref, cntacc_ref) = refs

    b = pl.program_id(0)
    l = pl.program_id(1)
    seq_start = l == 0

    @pl.when(jnp.logical_and(b == 0, l == 0))
    def _():
        cntacc_ref[...] = jnp.zeros_like(cntacc_ref)

    if first_layer:
        x = x_ref[...]
    else:
        x = _moe_combine(x_ref[...], yprev_ref, slabprev_ref[...])

    hn = _rms(x, gmix_ref[...]).astype(BF16)
    proj = jnp.dot(hn, win_ref[...], preferred_element_type=F32)
    u = proj[:, :SSM_WIDTH]
    p = proj[:, SSM_WIDTH:SSM_WIDTH + POOL_WIDTH]
    c0 = SSM_WIDTH + POOL_WIDTH
    bg = proj[:, c0:c0 + CONV_WIDTH]
    cg = proj[:, c0 + CONV_WIDTH:c0 + 2 * CONV_WIDTH]
    hv = proj[:, c0 + 2 * CONV_WIDTH:]

    y = _s5_mixer(u, seq_start, minw_ref, mcat_ref, coef_ref, u_s, y_s, h_s) + dskip_ref[...] * u
    z = jax.nn.gelu(y)
    glu = jnp.dot(z.astype(BF16), wglu_ref[...], preferred_element_type=F32) + bglu_ref[...]
    y_ssm = z * jax.nn.sigmoid(glu)

    tail = POOL_TAIL
    pbuf_ref[0, 0:tail, :] = jnp.where(seq_start, 0.0, pbuf_ref[0, TILE_T:TILE_T + tail, :])
    pbuf_ref[0, tail:, :] = p
    sums = []
    for stage, w in enumerate(POOL_WINDOWS):
        src = 0 if stage == 0 else 1 + (stage - 1) % 2
        dst = 1 + stage % 2
        d = w // 2
        lo = SUBLANES * (stage + 1)
        n = tail + TILE_T - lo
        e = pbuf_ref[src, pl.ds(lo, n), :] + pbuf_ref[src, pl.ds(lo - d, n), :]
        if stage + 1 < len(POOL_WINDOWS):
            pbuf_ref[dst, pl.ds(lo, n), :] = e
        sums.append(e[tail - lo:, :])
    lane_p = lax.broadcasted_iota(I32, (TILE_T, POOL_WIDTH), 1)
    total = sums[-1]
    win = jnp.full((TILE_T, POOL_WIDTH), POOL_WINDOWS[-1], I32)
    for i in range(len(POOL_WINDOWS) - 2, -1, -1):
        sel = lane_p < (i + 1) * POOL_GROUP
        total = jnp.where(sel, sums[i], total)
        win = jnp.where(sel, POOL_WINDOWS[i], win)
    row_p = lax.broadcasted_iota(I32, (TILE_T, POOL_WIDTH), 0) + l * TILE_T
    count = jnp.minimum(row_p + 1, win).astype(F32)
    pooled = total / count - p
    mixed = jnp.dot(pooled.astype(BF16), poolw_ref[...], preferred_element_type=F32)
    y_pool = mixed * pscale_ref[...]

    zc = cg * hv
    zbuf_ref[0:SUBLANES, :] = jnp.where(seq_start, 0.0, zbuf_ref[TILE_T:TILE_T + SUBLANES, :])
    zbuf_ref[SUBLANES:, :] = zc
    yc = (convw_ref[0:1, :] * zbuf_ref[pl.ds(SUBLANES - 2, TILE_T), :]
          + convw_ref[1:2, :] * zbuf_ref[pl.ds(SUBLANES - 1, TILE_T), :]
          + convw_ref[2:3, :] * zc)
    y_conv = bg * yc

    gbr = gbr_ref[...]
    s1 = SSM_WIDTH + POOL_WIDTH
    mixed_all = jnp.concatenate([_rms(y_ssm, gbr[:, :SSM_WIDTH]).astype(BF16),
                                 _rms(y_pool, gbr[:, SSM_WIDTH:s1]).astype(BF16),
                                 _rms(y_conv, gbr[:, s1:]).astype(BF16)], axis=1)
    x1 = x + jnp.dot(mixed_all, wout_ref[...], preferred_element_type=F32)
    x1_ref[...] = x1

    hn2 = _rms(x1, gffn_ref[...])
    packed = _pack_rows(hn2)
    hp_ref[0] = packed[:, :SC_ROW]
    hp_ref[1] = packed[:, SC_ROW:]
    logits = jnp.dot(hn2.astype(BF16), rw_ref[...], preferred_element_type=F32) + rb_ref[...]
    work = logits.T[:N_EXPERTS, :]
    eio = lax.broadcasted_iota(I32, (N_EXPERTS, TILE_T), 0).astype(F32)
    neg = jnp.float32(-jnp.inf)
    vals, idxs = [], []
    for _ in range(TOP_K):
        m = jnp.max(work, axis=0, keepdims=True)
        idx = jnp.min(jnp.where(work == m, eio, float(N_EXPERTS)), axis=0, keepdims=True)
        vals.append(m)
        idxs.append(idx)
        work = jnp.where(eio == idx, neg, work)
    exps = [jnp.exp(v - vals[0]) for v in vals]
    denom = exps[0] + exps[1] + exps[2] + exps[3]
    gates = [e / denom for e in exps]

    onehot = jnp.zeros((N_EXPERTS, TILE_T), F32)
    for idx in idxs:
        onehot = jnp.where(eio == idx, 1.0, onehot)
    before = jnp.dot(onehot.astype(BF16), utri_ref[...], preferred_element_type=F32) + cntacc_ref[:, 0:1]
    ranks = [jnp.sum(jnp.where(eio == idx, before, 0.0), axis=0, keepdims=True) for idx in idxs]
    row = lax.broadcasted_iota(I32, (LANES, TILE_T), 0)
    slab_t = jnp.zeros((LANES, TILE_T), F32)
    for k in range(TOP_K):
        slab_t = jnp.where(row == k, idxs[k], slab_t)
        slab_t = jnp.where(row == TOP_K + k, gates[k], slab_t)
        slab_t = jnp.where(row == 2 * TOP_K + k, ranks[k], slab_t)
    slabt_ref[...] = slab_t[:2 * SUBLANES, :]
    slab_ref[...] = slab_t.T
    newcnt = cntacc_ref[:, 0:1] + jnp.sum(onehot, axis=1, keepdims=True)
    cntacc_ref[...] = jnp.broadcast_to(newcnt, cntacc_ref.shape)
    cnt_ref[...] = jnp.broadcast_to(newcnt, cnt_ref.shape)


def _layer_spec(shape, layer):
    nd = len(shape) - 1
    sel = layer if shape[0] > 1 else 0
    return pl.BlockSpec((None,) + tuple(shape[1:]), lambda b, l, nd=nd, sel=sel: (sel,) + (0,) * nd,
                        pipeline_mode=pl.Buffered(1))


def _mixer_call(x, x_tile0, yprev, slabprev, lp, layer, batch, seq):
    first_layer = yprev is None
    n_tok = batch * seq
    steps = seq // TILE_T
    tok_spec = lambda w: pl.BlockSpec((TILE_T, w), lambda b, l: (b * steps + l, 0))
    in_specs = [pl.BlockSpec((TILE_T, D_MODEL), lambda b, l: (x_tile0 + b * steps + l, 0))]
    args = [x]
    if not first_layer:
        in_specs += [pl.BlockSpec((TOP_K, 2, TILE_T, SC_ROW), lambda b, l: (0, 0, b * steps + l, 0)), tok_spec(LANES)]
        args += [yprev, slabprev]
    weights = [lp['gmix'], lp['win'], lp['minw'], lp['mcat'], lp['coef'], lp['dskip'], lp['wglu'], lp['bglu'],
               lp['poolw'], lp['pscale'], lp['convw'], lp['gbr'], lp['wout'], lp['gffn'], lp['rw'], lp['rb'],
               lp['utri']]
    in_specs += [_layer_spec(w.shape, layer) for w in weights]
    args += weights
    out_shape = [jax.ShapeDtypeStruct((n_tok, D_MODEL), F32),
                 jax.ShapeDtypeStruct((2, n_tok, SC_ROW), I32),
                 jax.ShapeDtypeStruct((n_tok, LANES), F32),
                 jax.ShapeDtypeStruct((2 * SUBLANES, n_tok), F32),
                 jax.ShapeDtypeStruct((N_EXPERTS, LANES), F32)]
    out_specs = [tok_spec(D_MODEL), pl.BlockSpec((2, TILE_T, SC_ROW), lambda b, l: (0, b * steps + l, 0)), tok_spec(LANES),
                 pl.BlockSpec((2 * SUBLANES, TILE_T), lambda b, l: (0, b * steps + l)),
                 pl.BlockSpec((N_EXPERTS, LANES), lambda b, l: (0, 0))]
    scratch = [pltpu.VMEM((SSM_WIDTH // LANES, TILE_T, LANES), F32),
               pltpu.VMEM((SSM_WIDTH // LANES, TILE_T, LANES), F32),
               pltpu.VMEM((SUBLANES + N_CHUNK, SSM_GROUPS * LANES), F32),
               pltpu.VMEM((3, POOL_TAIL + TILE_T, POOL_WIDTH), F32),
               pltpu.VMEM((SUBLANES + TILE_T, CONV_WIDTH), F32),
               pltpu.VMEM((N_EXPERTS, LANES), F32)]
    return pl.pallas_call(
        functools.partial(_mixer_kernel, first_layer=first_layer),
        grid=(batch, steps),
        in_specs=in_specs, out_specs=out_specs, out_shape=out_shape, scratch_shapes=scratch,
        compiler_params=pltpu.CompilerParams(dimension_semantics=("arbitrary", "arbitrary"),
                                             vmem_limit_bytes=VMEM_LIMIT),
        name="mixer_first" if first_layer else "mixer_next",
    )(*args)


def _route_kernel(slabt_ref, cnt_ref, dest_ref, meta_ref, *, n_steps):
    lane1 = lax.broadcasted_iota(I32, (1, LANES), 1)
    e_row = lax.broadcasted_iota(I32, (N_EXPERTS, LANES), 0)
    e_lane = lax.broadcasted_iota(I32, (N_EXPERTS, LANES), 1)
    cnt = cnt_ref[...].astype(I32)
    counts = jnp.sum(jnp.where(e_row == e_lane, cnt, 0), axis=0, keepdims=True)
    padded = ((counts + (STEP_ROWS - 1)) // STEP_ROWS) * STEP_ROWS
    pend = padded
    sh = 1
    while sh < N_EXPERTS:
        pend = pend + jnp.where(lane1 >= sh, pltpu.roll(pend, sh, axis=1), 0)
        sh *= 2

    @pl.when(pl.program_id(0) == 0)
    def _():
        rows = meta_ref.shape[0]
        bstart = lax.broadcasted_iota(I32, (rows, LANES), 0) * STEP_ROWS
        lane = lax.broadcasted_iota(I32, (rows, LANES), 1)
        done = jnp.where(jnp.logical_and(lane < N_EXPERTS, pend <= bstart), 1, 0)
        be = jnp.minimum(jnp.sum(done, axis=1, keepdims=True), N_EXPERTS - 1)
        mine = lane == be
        first_row = jnp.sum(jnp.where(mine, pend - counts, 0), axis=1, keepdims=True)
        used = bstart[:, 0:1] + STEP_ROWS - jnp.maximum(bstart[:, 0:1], first_row)
        nblk = jnp.clip((used + (EXPERT_BLOCK - 1)) // EXPERT_BLOCK, 0, STEP_ROWS // EXPERT_BLOCK)
        total = jnp.sum(jnp.where(lane1 == N_EXPERTS - 1, pend, 0), axis=1, keepdims=True)
        row = lax.broadcasted_iota(I32, (rows, LANES), 0)
        table = jnp.where(lane == 0, be, jnp.where(lane == 1, nblk, 0))
        meta_ref[...] = jnp.where(row == n_steps, total // STEP_ROWS, table)

    last_col = jnp.sum(jnp.where(e_row == e_lane, pend - 1, 0), axis=1, keepdims=True)
    tokens = slabt_ref.shape[1]
    eio = lax.broadcasted_iota(I32, (N_EXPERTS, tokens), 0)
    row = lax.broadcasted_iota(I32, (SUBLANES, tokens), 0)
    out = jnp.zeros((SUBLANES, tokens), I32)
    for k in range(TOP_K):
        idx = slabt_ref[k:k + 1, :].astype(I32)
        rank = slabt_ref[2 * TOP_K + k:2 * TOP_K + k + 1, :].astype(I32)
        base = jnp.sum(jnp.where(eio == idx, last_col, 0), axis=0, keepdims=True)
        for h in range(2):
            out = jnp.where(row == 2 * k + h, base - rank + h * n_steps * STEP_ROWS, out)
    dest_ref[...] = out


def _route_call(slabt, counts, n_tok, n_steps):
    meta_rows = ((n_steps + 1 + SUBLANES - 1) // SUBLANES) * SUBLANES
    return pl.pallas_call(
        functools.partial(_route_kernel, n_steps=n_steps),
        grid=(n_tok // ROUTE_T,),
        in_specs=[pl.BlockSpec((2 * SUBLANES, ROUTE_T), lambda i: (0, i)),
                  pl.BlockSpec((N_EXPERTS, LANES), lambda i: (0, 0))],
        out_specs=[pl.BlockSpec((SUBLANES, ROUTE_T), lambda i: (0, i)),
                   pl.BlockSpec((meta_rows, LANES), lambda i: (0, 0))],
        out_shape=[jax.ShapeDtypeStruct((SUBLANES, n_tok), I32),
                   jax.ShapeDtypeStruct((meta_rows, LANES), I32)],
        compiler_params=pltpu.CompilerParams(dimension_semantics=("arbitrary",)),
        name="route",
    )(slabt, counts)


def _sc_mesh():
    return plsc.VectorSubcoreMesh(core_axis_name="core", subcore_axis_name="subcore")


def _dispatch(src, idx, n_rows):
    windows = src.shape[0] // SC_WINDOW

    @functools.partial(pl.kernel, out_type=jax.ShapeDtypeStruct((n_rows, SC_ROW), src.dtype),
                       mesh=_sc_mesh(), scratch_types=[])
    def scatter_rows(x_hbm, i_hbm, o_hbm):
        def body(x_vmem, i_vmem):
            pltpu.sync_copy(x_vmem, o_hbm.at[i_vmem.at[0]])

        pltpu.emit_pipeline(
            body,
            grid=(idx.shape[1] // SC_WINDOW,),
            in_specs=[pl.BlockSpec((SC_WINDOW, SC_ROW), lambda i: (lax.rem(i, windows), 0)),
                      pl.BlockSpec((1, SC_WINDOW), lambda i: (0, i))],
            out_specs=[],
            core_axis_name=("core", "subcore"),
            dimension_semantics=(pltpu.PARALLEL,),
        )(x_hbm, i_hbm)

    return scatter_rows(src, idx)


def _combine(src, idx):
    n_idx = idx.shape[1]

    @functools.partial(pl.kernel, out_type=jax.ShapeDtypeStruct((n_idx, SC_ROW), src.dtype),
                       mesh=_sc_mesh(), scratch_types=[])
    def gather_rows(y_hbm, i_hbm, o_hbm):
        def body(i_vmem, o_vmem):
            pltpu.sync_copy(y_hbm.at[i_vmem.at[0]], o_vmem)

        pltpu.emit_pipeline(
            body,
            grid=(n_idx // SC_WINDOW,),
            in_specs=[pl.BlockSpec((1, SC_WINDOW), lambda i: (0, i))],
            out_specs=[pl.BlockSpec((SC_WINDOW, SC_ROW), lambda i: (i, 0))],
            core_axis_name=("core", "subcore"),
            dimension_semantics=(pltpu.PARALLEL,),
        )(i_hbm, o_hbm)

    return gather_rows(src, idx)


def _expert_rows(xs_ref, ys_ref, row0, n, wgu_s, bgu_ref, wdn_s, bdn_ref, act_s):
    rows = pl.ds(pl.multiple_of(row0, EXPERT_BLOCK), n)
    arow = slice(0, n)
    w0 = xs_ref[0, rows, :]
    w1 = xs_ref[1, rows, :]
    xb = jnp.concatenate([_unpack_lo(w0).astype(BF16), _unpack_lo(w1).astype(BF16),
                          _unpack_hi(w0).astype(BF16), _unpack_hi(w1).astype(BF16)], axis=1)
    half = D_EXPERT // 2
    for j in range(2):
        cols = slice(j * half, (j + 1) * half)
        ucols = slice(D_EXPERT + j * half, D_EXPERT + (j + 1) * half)
        g = jnp.dot(xb, wgu_s[:, cols], preferred_element_type=F32) + bgu_ref[:, cols]
        up = jnp.dot(xb, wgu_s[:, ucols], preferred_element_type=F32) + bgu_ref[:, ucols]
        g = jnp.minimum(g, SWIGLU_LIMIT)
        up = jnp.clip(up, -SWIGLU_LIMIT, SWIGLU_LIMIT)
        act_s[arow, cols] = (g * jax.nn.sigmoid(SWIGLU_ALPHA * g) * (up + 1.0)).astype(BF16)
    y = jnp.dot(act_s[arow, :], wdn_s[...], preferred_element_type=F32) + bdn_ref[...]
    packed = _pack_rows(y)
    ys_ref[0, rows, :] = packed[:, :SC_ROW]
    ys_ref[1, rows, :] = packed[:, SC_ROW:]


def _expert_kernel(be_ref, nb_ref, nv_ref, xs_ref, wgu_ref, bgu_ref, wdn_ref, bdn_ref, ys_ref, wgu_s, wdn_s, act_s):
    blk = pl.program_id(0)
    prev = be_ref[jnp.maximum(blk - 1, 0)]
    changed = jnp.logical_or(blk == 0, be_ref[blk] != prev)
    valid = blk < nv_ref[0]

    @pl.when(jnp.logical_and(valid, changed))
    def _():
        chunk = 128

        def cast_gu(i, _):
            rows = pl.ds(pl.multiple_of(i * chunk, chunk), chunk)
            wgu_s[rows, :] = wgu_ref[rows, :].astype(BF16)
            return 0

        def cast_dn(i, _):
            rows = pl.ds(pl.multiple_of(i * chunk, chunk), chunk)
            wdn_s[rows, :] = wdn_ref[rows, :].astype(BF16)
            return 0

        lax.fori_loop(0, D_MODEL // chunk, cast_gu, 0)
        lax.fori_loop(0, D_EXPERT // chunk, cast_dn, 0)

    nblk = nb_ref[blk]
    first = STEP_ROWS - nblk * EXPERT_BLOCK
    big = nblk // RUN_BLOCKS

    @pl.when(jnp.logical_and(valid, big > 0))
    def _():
        def one_run(i, _):
            row0 = first + (nblk - (i + 1) * RUN_BLOCKS) * EXPERT_BLOCK
            _expert_rows(xs_ref, ys_ref, row0, RUN_BLOCKS * EXPERT_BLOCK, wgu_s, bgu_ref, wdn_s, bdn_ref, act_s)
            return 0

        lax.fori_loop(0, big, one_run, 0)

    bit = RUN_BLOCKS // 2
    while bit:
        @pl.when(jnp.logical_and(valid, (nblk & bit) != 0))
        def _(bit=bit):
            row0 = first + (nblk & (bit - 1)) * EXPERT_BLOCK
            _expert_rows(xs_ref, ys_ref, row0, bit * EXPERT_BLOCK, wgu_s, bgu_ref, wdn_s, bdn_ref, act_s)
        bit //= 2


def _expert_call(xs, step_e, step_nblk, n_valid, w_gu, b_gu, w_dn, b_dn, layer, n_steps):
    def row_map(i, be, nb, nv):
        return (0, jnp.minimum(i, jnp.maximum(nv[0] - 1, 0)), 0)

    def w_map(i, be, nb, nv):
        return (layer, be[i], 0, 0)

    grid_spec = pltpu.PrefetchScalarGridSpec(
        num_scalar_prefetch=3,
        grid=(n_steps,),
        in_specs=[pl.BlockSpec((2, STEP_ROWS, SC_ROW), row_map),
                  pl.BlockSpec((None, None, D_MODEL, 2 * D_EXPERT), w_map),
                  pl.BlockSpec((None, None, 1, 2 * D_EXPERT), w_map),
                  pl.BlockSpec((None, None, D_EXPERT, D_MODEL), w_map),
                  pl.BlockSpec((None, None, 1, D_MODEL), w_map)],
        out_specs=pl.BlockSpec((2, STEP_ROWS, SC_ROW), row_map),
        scratch_shapes=[pltpu.VMEM((D_MODEL, 2 * D_EXPERT), BF16),
                        pltpu.VMEM((D_EXPERT, D_MODEL), BF16),
                        pltpu.VMEM((RUN_BLOCKS * EXPERT_BLOCK, D_EXPERT), BF16)])
    return pl.pallas_call(
        _expert_kernel,
        grid_spec=grid_spec,
        out_shape=jax.ShapeDtypeStruct((2, n_steps * STEP_ROWS, SC_ROW), I32),
        compiler_params=pltpu.CompilerParams(dimension_semantics=("arbitrary",),
                                             vmem_limit_bytes=VMEM_LIMIT),
        name="experts",
    )(step_e, step_nblk, n_valid, xs, w_gu, b_gu, w_dn, b_dn)


def _final_kernel(x_ref, y_ref, slab_ref, g_ref, *rest):
    o_ref = rest[-1]
    x = _moe_combine(x_ref[...], y_ref, slab_ref[...])
    o_ref[...] = _rms(x, g_ref[...])


def _final_call(x1, ysg, slab, g, out_row0, n_out, out_prev):
    t = FINAL_T
    out_tile0 = out_row0 // t
    in_specs = [pl.BlockSpec((t, D_MODEL), lambda i: (i, 0)),
                pl.BlockSpec((TOP_K, 2, t, SC_ROW), lambda i: (0, 0, i, 0)),
                pl.BlockSpec((t, LANES), lambda i: (i, 0)),
                pl.BlockSpec((1, D_MODEL), lambda i: (0, 0))]
    args = [x1, ysg, slab, g]
    aliases = {}
    if out_prev is not None:
        in_specs.append(pl.BlockSpec(memory_space=pl.ANY))
        args.append(out_prev)
        aliases = {4: 0}
    return pl.pallas_call(
        _final_kernel,
        grid=(x1.shape[0] // t,),
        in_specs=in_specs,
        out_specs=pl.BlockSpec((t, D_MODEL), lambda i: (out_tile0 + i, 0)),
        out_shape=jax.ShapeDtypeStruct((n_out, D_MODEL), F32),
        input_output_aliases=aliases,
        compiler_params=pltpu.CompilerParams(dimension_semantics=("arbitrary",), vmem_limit_bytes=VMEM_LIMIT),
        name="final_norm",
    )(*args)


def _s5_tables(lam_re, lam_im, b_re, b_im, c_re, c_im, log_step):
    lr = jnp.minimum(lam_re.astype(F32), LAMBDA_RE_MAX)
    li = lam_im.astype(F32)
    step = jnp.exp(log_step.astype(F32))[:, None]
    mag = jnp.exp(lr * step)
    ar = mag * jnp.cos(li * step)
    ai = mag * jnp.sin(li * step)
    nr = ar - 1.0
    den = lr * lr + li * li
    kr = (nr * lr + ai * li) / den
    ki = (ai * lr - nr * li) / den
    bre = b_re.astype(F32)
    bim = b_im.astype(F32)
    bbar_r = kr[..., None] * bre - ki[..., None] * bim
    bbar_i = kr[..., None] * bim + ki[..., None] * bre
    cre = c_re.astype(F32)
    cim = c_im.astype(F32)

    def power(m):
        m = jnp.asarray(m, F32)[..., None, None]
        mg = jnp.exp(m * lr * step)
        return mg * jnp.cos(m * li * step), mg * jnp.sin(m * li * step)

    pw_r, pw_i = power(jnp.arange(CHUNK + 1))
    cp_r = cre[None] * pw_r[:, :, None, :] - cim[None] * pw_i[:, :, None, :]
    cp_i = cre[None] * pw_i[:, :, None, :] + cim[None] * pw_r[:, :, None, :]
    width = CHUNK * SSM_GROUP
    kcat = (jnp.einsum('tgap,gph->ghta', cp_r[:CHUNK], bbar_r)
            - jnp.einsum('tgap,gph->ghta', cp_i[:CHUNK], bbar_i)).astype(BF16)
    lag = jnp.arange(CHUNK)
    place = (lag[None, :, None] - lag[:, None, None] == lag[None, None, :]).astype(BF16)
    m_intra = jnp.einsum('stl,ghla->gshta', place, kcat,
                         preferred_element_type=F32).astype(BF16).reshape(SSM_GROUPS, width, width)
    mo_r = cp_r[1:].transpose(1, 3, 0, 2).reshape(SSM_GROUPS, SSM_STATE, width)
    mo_i = -cp_i[1:].transpose(1, 3, 0, 2).reshape(SSM_GROUPS, SSM_STATE, width)
    mcat = jnp.concatenate([m_intra, mo_r.astype(BF16), mo_i.astype(BF16)], axis=1)
    q_r = pw_r[CHUNK - 1::-1][:CHUNK][:, :, :, None]
    q_i = pw_i[CHUNK - 1::-1][:CHUNK][:, :, :, None]
    mn_r = q_r * bbar_r[None] - q_i * bbar_i[None]
    mn_i = q_r * bbar_i[None] + q_i * bbar_r[None]
    minw = jnp.concatenate([mn_r, mn_i], axis=2).transpose(1, 0, 3, 2).reshape(
        SSM_GROUPS, CHUNK * SSM_GROUP, 2 * SSM_STATE).astype(BF16)

    def packed(j):
        pr, pi = power(float(CHUNK * j))
        return (jnp.concatenate([pr, pr], axis=-1).reshape(1, SSM_GROUPS * LANES),
                jnp.concatenate([-pi, pi], axis=-1).reshape(1, SSM_GROUPS * LANES))

    rows = jnp.arange(SUBLANES)[:, None]
    kinds = []
    for d in (1, 2, 4):
        c1, c2 = packed(d)
        mask = (rows >= d).astype(F32)
        kinds += [mask * c1, mask * c2]
    pk = [packed(r + 1) for r in range(SUBLANES)]
    kinds += [jnp.concatenate([p[0] for p in pk], axis=0), jnp.concatenate([p[1] for p in pk], axis=0)]
    coef = jnp.stack(kinds, axis=0)
    return minw, mcat, coef


def _mixer_params(norm_mix, w_in, ssm_lam_re, ssm_lam_im, ssm_b_re, ssm_b_im, ssm_c_re, ssm_c_im, ssm_d,
                  ssm_log_step, ssm_w_glu, ssm_b_glu, pool_w, pool_scale, conv_w, branch_norm, w_out, norm_ffn,
                  router_w, router_b):
    depth = w_in.shape[0]
    minw, mcat, coef = jax.vmap(_s5_tables)(ssm_lam_re, ssm_lam_im, ssm_b_re, ssm_b_im, ssm_c_re, ssm_c_im,
                                            ssm_log_step)
    eye4 = jnp.eye(len(POOL_WINDOWS), dtype=F32)
    poolw = jnp.einsum('lgcd,gk->lgckd', pool_w.astype(F32), eye4).reshape(depth, POOL_WIDTH, POOL_WIDTH).astype(BF16)
    rw = jnp.pad(router_w.astype(F32), ((0, 0), (0, 0), (0, LANES - N_EXPERTS))).astype(BF16)
    rb = jnp.pad(router_b.astype(F32), ((0, 0), (0, LANES - N_EXPERTS))).reshape(depth, 1, LANES)
    row = lambda v: v.reshape(depth, 1, -1).astype(F32)
    utri = jnp.triu(jnp.ones((TILE_T, TILE_T), F32), 1).astype(BF16)
    return dict(gmix=row(norm_mix), win=w_in.astype(BF16), minw=minw, mcat=mcat, coef=coef, dskip=row(ssm_d),
                wglu=ssm_w_glu.astype(BF16), bglu=row(ssm_b_glu), poolw=poolw, pscale=row(pool_scale),
                convw=conv_w.astype(F32), gbr=row(branch_norm), wout=w_out.astype(BF16),
                gffn=row(norm_ffn), rw=rw, rb=rb, utri=utri[None])


def kernel(x, norm_mix, w_in, ssm_lam_re, ssm_lam_im, ssm_b_re, ssm_b_im, ssm_c_re, ssm_c_im, ssm_d, ssm_log_step, ssm_w_glu, ssm_b_glu, pool_w, pool_scale, conv_w, branch_norm, w_out, norm_ffn, router_w, router_b, w_gate_up, b_gate_up, w_down, b_down, final_norm):
    batch, seq, d_model = x.shape
    assert d_model == D_MODEL and seq % TILE_T == 0
    depth = w_in.shape[0]
    n_streams = 2 if batch % 2 == 0 else 1
    sb = batch // n_streams
    n_tok = sb * seq
    assert n_tok % ROUTE_T == 0 and n_tok % SC_WINDOW == 0
    n_steps = -(-(n_tok * TOP_K + N_EXPERTS * (STEP_ROWS - 1)) // STEP_ROWS)
    n_rows = n_steps * STEP_ROWS
    tiles = n_tok // TILE_T
    b_gu = b_gate_up.reshape(depth, N_EXPERTS, 1, 2 * D_EXPERT)
    b_dn = b_down.reshape(depth, N_EXPERTS, 1, D_MODEL)

    xin = [x.reshape(batch * seq, D_MODEL)] * n_streams
    tile0 = [s * tiles for s in range(n_streams)]
    ysg = [None] * n_streams
    slab = [None] * n_streams
    lp = _mixer_params(norm_mix, w_in, ssm_lam_re, ssm_lam_im, ssm_b_re, ssm_b_im, ssm_c_re, ssm_c_im, ssm_d,
                       ssm_log_step, ssm_w_glu, ssm_b_glu, pool_w, pool_scale, conv_w, branch_norm, w_out,
                       norm_ffn, router_w, router_b)
    for l in range(depth):
        routed = []
        for s in range(n_streams):
            xin[s], hp, slab[s], slabt, counts = _mixer_call(xin[s], tile0[s], ysg[s], slab[s], lp, l, sb, seq)
            tile0[s] = 0
            dest, meta = _route_call(slabt, counts, n_tok, n_steps)
            idx = dest.reshape(1, 2 * TOP_K * n_tok)
            xs = _dispatch(hp.reshape(2 * n_tok, SC_ROW), idx, 2 * n_rows).reshape(2, n_rows, SC_ROW)
            routed.append((xs, idx, meta[:n_steps, 0], meta[:n_steps, 1], meta[n_steps:n_steps + 1, 0]))
        for s, (xs, idx, step_e, step_nblk, n_valid) in enumerate(routed):
            ys = _expert_call(xs, step_e, step_nblk, n_valid, w_gate_up, b_gu, w_down, b_dn, l, n_steps)
            ysg[s] = _combine(ys.reshape(2 * n_rows, SC_ROW), idx).reshape(TOP_K, 2, n_tok, SC_ROW)
    g = final_norm.reshape(1, D_MODEL).astype(F32)
    out = None
    for s in range(n_streams):
        out = _final_call(xin[s], ysg[s], slab[s], g, s * n_tok, batch * seq, out)
    return out.reshape(batch, seq, D_MODEL)
```

```python
import functools

import jax
import jax.numpy as jnp
from jax import lax
from jax.experimental import pallas as pl
from jax.experimental.pallas import tpu as pltpu
from jax.experimental.pallas import tpu_sc as plsc

F32 = jnp.float32
BF16 = jnp.bfloat16
I32 = jnp.int32

D_MODEL = 1024
SSM_WIDTH = 512
POOL_WIDTH = 256
CONV_WIDTH = 256
SSM_GROUP = 16
SSM_GROUPS = 32
SSM_STATE = 64
LAMBDA_RE_MAX = -1e-4
POOL_WINDOWS = (2, 4, 8, 16)
POOL_GROUP = 64
N_EXPERTS = 32
TOP_K = 4
D_EXPERT = 1024
SWIGLU_LIMIT = 7.0
SWIGLU_ALPHA = 1.702
EXPERT_BLOCK = 256
STEP_ROWS = 2048
RUN_BLOCKS = 4
NORM_EPS = 1e-5

LANES = 128
SUBLANES = 8
TILE_T = 512
FINAL_T = 1024
CHUNK = 16
N_CHUNK = TILE_T // CHUNK
POOL_TAIL = SUBLANES * len(POOL_WINDOWS)
assert POOL_WINDOWS == tuple(2 ** (k + 1) for k in range(len(POOL_WINDOWS))) and POOL_TAIL >= POOL_WINDOWS[-1]
PACKED = D_MODEL // 2
ROUTE_T = 2048
SC_WINDOW = 128
SC_ROW = PACKED // 2
HI_MASK = -65536
VMEM_LIMIT = 56 * 1024 * 1024


def _bf16_round(v):
    return v.astype(BF16).astype(F32)


def _pack_rows(v):
    lo = lax.shift_right_logical(lax.bitcast_convert_type(_bf16_round(v[:, :PACKED]), I32), 16)
    hi = lax.bitcast_convert_type(_bf16_round(v[:, PACKED:]), I32) & HI_MASK
    return hi | lo


def _unpack_lo(w):
    return lax.bitcast_convert_type(lax.shift_left(w, 16), F32)


def _unpack_hi(w):
    return lax.bitcast_convert_type(w & HI_MASK, F32)


def _rms(v, g):
    r = lax.rsqrt(jnp.mean(v * v, axis=-1, keepdims=True) + NORM_EPS)
    return (v * r) * g


def _moe_combine(x, y_ref, slab):
    parts = [x[:, i * SC_ROW:(i + 1) * SC_ROW] for i in range(4)]
    for k in range(TOP_K):
        g = slab[:, TOP_K + k:TOP_K + k + 1]
        for h in range(2):
            w = y_ref[k, h]
            parts[h] = parts[h] + g * _unpack_lo(w)
            parts[2 + h] = parts[2 + h] + g * _unpack_hi(w)
    return jnp.concatenate(parts, axis=1)


def _block_transpose8(vs, lane):
    vs = list(vs)
    for dist in (4, 2, 1):
        width = SSM_GROUP * dist
        low = (lane % (2 * width)) < width
        for j in range(8):
            if j & dist:
                continue
            a, b = vs[j], vs[j + dist]
            vs[j] = jnp.where(low, a, pltpu.roll(b, width, axis=1))
            vs[j + dist] = jnp.where(low, pltpu.roll(a, LANES - width, axis=1), b)
    return vs


def _cmul_packed(c1, c2, v):
    return c1 * v + c2 * pltpu.roll(v, SSM_STATE, axis=1)


def _chunk_scan(s, carry_in, cf):
    outs = []
    carry = jnp.broadcast_to(carry_in, (SUBLANES, LANES))
    for rg in range(N_CHUNK // SUBLANES):
        x = s[rg * SUBLANES:(rg + 1) * SUBLANES]
        for j, d in enumerate((1, 2, 4)):
            x = x + _cmul_packed(cf[2 * j], cf[2 * j + 1], pltpu.roll(x, d, axis=0))
        x = x + _cmul_packed(cf[6], cf[7], carry)
        outs.append(x)
        carry = jnp.broadcast_to(x[SUBLANES - 1:SUBLANES], (SUBLANES, LANES))
    return jnp.concatenate(outs, axis=0)


def _expand_toeplitz(kcat_ref, mint_s):
    lane = lax.broadcasted_iota(I32, (SSM_GROUP, LANES), 1)
    zero = jnp.zeros((SSM_GROUP, LANES), BF16)

    def one_group(g, _):
        lo_src = kcat_ref[g, :, 0:LANES]
        hi_src = kcat_ref[g, :, LANES:2 * LANES]
        for s in range(CHUNK):
            d = SSM_GROUP * s
            rows = slice(SSM_GROUP * s, SSM_GROUP * (s + 1))
            if d == 0:
                lo, hi = lo_src, hi_src
            elif d < LANES:
                wrapped = pltpu.roll(lo_src, d, axis=1)
                lo = jnp.where(lane < d, zero, wrapped)
                hi = jnp.where(lane < d, wrapped, pltpu.roll(hi_src, d, axis=1))
            elif d == LANES:
                lo, hi = zero, lo_src
            else:
                lo = zero
                hi = jnp.where(lane < d - LANES, zero, pltpu.roll(lo_src, d - LANES, axis=1))
            mint_s[g, rows, 0:LANES] = lo
            mint_s[g, rows, LANES:2 * LANES] = hi
        return 0

    lax.fori_loop(0, SSM_GROUPS, one_group, 0)


def _s5_mixer(u, first, minw_ref, mint_s, mout_ref, coef_ref, u_s, y_s, h_s):
    blocks = SSM_WIDTH // LANES
    for b4 in range(blocks):
        u_s[b4] = u[:, b4 * LANES:(b4 + 1) * LANES]
    lane = lax.broadcasted_iota(I32, (N_CHUNK, LANES), 1)

    halves = [[None, None] for _ in range(SSM_GROUPS)]
    for b4 in range(blocks):
        for hh in range(2):
            xs = [u_s[b4, pl.ds(8 * hh + j, N_CHUNK, stride=CHUNK), :].astype(BF16) for j in range(8)]
            ws = _block_transpose8(xs, lane)
            for gl in range(8):
                halves[8 * b4 + gl][hh] = ws[gl]
    ug = [jnp.concatenate(h2, axis=1) for h2 in halves]

    carry_in = jnp.where(first, 0.0, h_s[N_CHUNK + 7:N_CHUNK + 8, :])
    h_s[7:8, :] = carry_in
    for g in range(SSM_GROUPS):
        cols = slice(g * LANES, (g + 1) * LANES)
        s_g = jnp.dot(ug[g], minw_ref[g], preferred_element_type=F32)
        cf = [coef_ref[k, :, cols] for k in range(8)]
        h_s[8:8 + N_CHUNK, cols] = _chunk_scan(s_g, carry_in[:, cols], cf)

    yg = []
    for g in range(SSM_GROUPS):
        hprev = h_s[7:7 + N_CHUNK, g * LANES:(g + 1) * LANES].astype(BF16)
        y_g = (jnp.dot(ug[g], mint_s[g], preferred_element_type=F32)
               + jnp.dot(hprev, mout_ref[g], preferred_element_type=F32))
        yg.append(y_g.astype(BF16))

    for b4 in range(blocks):
        for hh in range(2):
            ws = [yg[8 * b4 + gl][:, hh * LANES:(hh + 1) * LANES] for gl in range(8)]
            xs = _block_transpose8(ws, lane)
            for j in range(8):
                y_s[b4, pl.ds(8 * hh + j, N_CHUNK, stride=CHUNK), :] = xs[j].astype(F32)
    return jnp.concatenate([y_s[b4] for b4 in range(blocks)], axis=1)


def _mixer_kernel(*refs, first_layer):
    if first_layer:
        x_ref = refs[0]
        refs = refs[1:]
    else:
        x_ref, yprev_ref, slabprev_ref = refs[:3]
        refs = refs[3:]
    (gmix_ref, win_ref, minw_ref, kcat_ref, mout_ref, coef_ref, dskip_ref, wglu_ref, bglu_ref, poolw_ref, pscale_ref,
     convw_ref, gbr_ref, wout_ref, gffn_ref, rw_ref, rb_ref, utri_ref,
     x1_ref, hp_ref, slab_ref, slabt_ref, cnt_ref,
     u_s, y_s, h_s, pbuf_ref, zbuf_ref, cntacc_ref, mint_s) = refs

    b = pl.program_id(0)
    l = pl.program_id(1)
    seq_start = l == 0

    @pl.when(jnp.logical_and(b == 0, l == 0))
    def _():
        cntacc_ref[...] = jnp.zeros_like(cntacc_ref)
        _expand_toeplitz(kcat_ref, mint_s)

    if first_layer:
        x = x_ref[...]
    else:
        x = _moe_combine(x_ref[...], yprev_ref, slabprev_ref[...])

    hn = _rms(x, gmix_ref[...]).astype(BF16)
    proj = jnp.dot(hn, win_ref[...], preferred_element_type=F32)
    u = proj[:, :SSM_WIDTH]
    p = proj[:, SSM_WIDTH:SSM_WIDTH + POOL_WIDTH]
    c0 = SSM_WIDTH + POOL_WIDTH
    bg = proj[:, c0:c0 + CONV_WIDTH]
    cg = proj[:, c0 + CONV_WIDTH:c0 + 2 * CONV_WIDTH]
    hv = proj[:, c0 + 2 * CONV_WIDTH:]

    y = _s5_mixer(u, seq_start, minw_ref, mint_s, mout_ref, coef_ref, u_s, y_s, h_s) + dskip_ref[...] * u
    z = jax.nn.gelu(y)
    glu = jnp.dot(z.astype(BF16), wglu_ref[...], preferred_element_type=F32) + bglu_ref[...]
    y_ssm = z * jax.nn.sigmoid(glu)

    tail = POOL_TAIL
    pbuf_ref[0, 0:tail, :] = jnp.where(seq_start, 0.0, pbuf_ref[0, TILE_T:TILE_T + tail, :])
    pbuf_ref[0, tail:, :] = p
    sums = []
    for stage, w in enumerate(POOL_WINDOWS):
        src = 0 if stage == 0 else 1 + (stage - 1) % 2
        dst = 1 + stage % 2
        d = w // 2
        lo = SUBLANES * (stage + 1)
        n = tail + TILE_T - lo
        e = pbuf_ref[src, pl.ds(lo, n), :] + pbuf_ref[src, pl.ds(lo - d, n), :]
        if stage + 1 < len(POOL_WINDOWS):
            pbuf_ref[dst, pl.ds(lo, n), :] = e
        sums.append(e[tail - lo:, :])
    lane_p = lax.broadcasted_iota(I32, (TILE_T, POOL_WIDTH), 1)
    total = sums[-1]
    win = jnp.full((TILE_T, POOL_WIDTH), POOL_WINDOWS[-1], I32)
    for i in range(len(POOL_WINDOWS) - 2, -1, -1):
        sel = lane_p < (i + 1) * POOL_GROUP
        total = jnp.where(sel, sums[i], total)
        win = jnp.where(sel, POOL_WINDOWS[i], win)
    row_p = lax.broadcasted_iota(I32, (TILE_T, POOL_WIDTH), 0) + l * TILE_T
    count = jnp.minimum(row_p + 1, win).astype(F32)
    pooled = total / count - p
    mixed = jnp.dot(pooled.astype(BF16), poolw_ref[...], preferred_element_type=F32)
    y_pool = mixed * pscale_ref[...]

    zc = cg * hv
    zbuf_ref[0:SUBLANES, :] = jnp.where(seq_start, 0.0, zbuf_ref[TILE_T:TILE_T + SUBLANES, :])
    zbuf_ref[SUBLANES:, :] = zc
    yc = (convw_ref[0:1, :] * zbuf_ref[pl.ds(SUBLANES - 2, TILE_T), :]
          + convw_ref[1:2, :] * zbuf_ref[pl.ds(SUBLANES - 1, TILE_T), :]
          + convw_ref[2:3, :] * zc)
    y_conv = bg * yc

    gbr = gbr_ref[...]
    s1 = SSM_WIDTH + POOL_WIDTH
    mixed_all = jnp.concatenate([_rms(y_ssm, gbr[:, :SSM_WIDTH]).astype(BF16),
                                 _rms(y_pool, gbr[:, SSM_WIDTH:s1]).astype(BF16),
                                 _rms(y_conv, gbr[:, s1:]).astype(BF16)], axis=1)
    x1 = x + jnp.dot(mixed_all, wout_ref[...], preferred_element_type=F32)
    x1_ref[...] = x1

    hn2 = _rms(x1, gffn_ref[...])
    packed = _pack_rows(hn2)
    hp_ref[0] = packed[:, :SC_ROW]
    hp_ref[1] = packed[:, SC_ROW:]
    logits = jnp.dot(hn2.astype(BF16), rw_ref[...], preferred_element_type=F32) + rb_ref[...]
    work = logits.T[:N_EXPERTS, :]
    eio = lax.broadcasted_iota(I32, (N_EXPERTS, TILE_T), 0).astype(F32)
    neg = jnp.float32(-jnp.inf)
    vals, idxs = [], []
    for _ in range(TOP_K):
        m = jnp.max(work, axis=0, keepdims=True)
        idx = jnp.min(jnp.where(work == m, eio, float(N_EXPERTS)), axis=0, keepdims=True)
        vals.append(m)
        idxs.append(idx)
        work = jnp.where(eio == idx, neg, work)
    exps = [jnp.exp(v - vals[0]) for v in vals]
    denom = exps[0] + exps[1] + exps[2] + exps[3]
    gates = [e / denom for e in exps]

    onehot = jnp.zeros((N_EXPERTS, TILE_T), F32)
    for idx in idxs:
        onehot = jnp.where(eio == idx, 1.0, onehot)
    before = jnp.dot(onehot.astype(BF16), utri_ref[...], preferred_element_type=F32) + cntacc_ref[:, 0:1]
    ranks = [jnp.sum(jnp.where(eio == idx, before, 0.0), axis=0, keepdims=True) for idx in idxs]
    row = lax.broadcasted_iota(I32, (LANES, TILE_T), 0)
    slab_t = jnp.zeros((LANES, TILE_T), F32)
    for k in range(TOP_K):
        slab_t = jnp.where(row == k, idxs[k], slab_t)
        slab_t = jnp.where(row == TOP_K + k, gates[k], slab_t)
        slab_t = jnp.where(row == 2 * TOP_K + k, ranks[k], slab_t)
    slabt_ref[...] = slab_t[:2 * SUBLANES, :]
    slab_ref[...] = slab_t.T
    newcnt = cntacc_ref[:, 0:1] + jnp.sum(onehot, axis=1, keepdims=True)
    cntacc_ref[...] = jnp.broadcast_to(newcnt, cntacc_ref.shape)
    cnt_ref[...] = jnp.broadcast_to(newcnt, cnt_ref.shape)


def _layer_spec(shape, layer):
    nd = len(shape) - 1
    sel = layer if shape[0] > 1 else 0
    return pl.BlockSpec((None,) + tuple(shape[1:]), lambda b, l, nd=nd, sel=sel: (sel,) + (0,) * nd,
                        pipeline_mode=pl.Buffered(1))


def _mixer_call(x, x_tile0, yprev, slabprev, lp, layer, batch, seq):
    first_layer = yprev is None
    n_tok = batch * seq
    steps = seq // TILE_T
    tok_spec = lambda w: pl.BlockSpec((TILE_T, w), lambda b, l: (b * steps + l, 0))
    in_specs = [pl.BlockSpec((TILE_T, D_MODEL), lambda b, l: (x_tile0 + b * steps + l, 0))]
    args = [x]
    if not first_layer:
        in_specs += [pl.BlockSpec((TOP_K, 2, TILE_T, SC_ROW), lambda b, l: (0, 0, b * steps + l, 0)), tok_spec(LANES)]
        args += [yprev, slabprev]
    weights = [lp['gmix'], lp['win'], lp['minw'], lp['kcat'], lp['mout'], lp['coef'], lp['dskip'], lp['wglu'], lp['bglu'],
               lp['poolw'], lp['pscale'], lp['convw'], lp['gbr'], lp['wout'], lp['gffn'], lp['rw'], lp['rb'],
               lp['utri']]
    in_specs += [_layer_spec(w.shape, layer) for w in weights]
    args += weights
    out_shape = [jax.ShapeDtypeStruct((n_tok, D_MODEL), F32),
                 jax.ShapeDtypeStruct((2, n_tok, SC_ROW), I32),
                 jax.ShapeDtypeStruct((n_tok, LANES), F32),
                 jax.ShapeDtypeStruct((2 * SUBLANES, n_tok), F32),
                 jax.ShapeDtypeStruct((N_EXPERTS, LANES), F32)]
    out_specs = [tok_spec(D_MODEL), pl.BlockSpec((2, TILE_T, SC_ROW), lambda b, l: (0, b * steps + l, 0)), tok_spec(LANES),
                 pl.BlockSpec((2 * SUBLANES, TILE_T), lambda b, l: (0, b * steps + l)),
                 pl.BlockSpec((N_EXPERTS, LANES), lambda b, l: (0, 0))]
    scratch = [pltpu.VMEM((SSM_WIDTH // LANES, TILE_T, LANES), F32),
               pltpu.VMEM((SSM_WIDTH // LANES, TILE_T, LANES), F32),
               pltpu.VMEM((SUBLANES + N_CHUNK, SSM_GROUPS * LANES), F32),
               pltpu.VMEM((3, POOL_TAIL + TILE_T, POOL_WIDTH), F32),
               pltpu.VMEM((SUBLANES + TILE_T, CONV_WIDTH), F32),
               pltpu.VMEM((N_EXPERTS, LANES), F32),
               pltpu.VMEM((SSM_GROUPS, CHUNK * SSM_GROUP, CHUNK * SSM_GROUP), BF16)]
    return pl.pallas_call(
        functools.partial(_mixer_kernel, first_layer=first_layer),
        grid=(batch, steps),
        in_specs=in_specs, out_specs=out_specs, out_shape=out_shape, scratch_shapes=scratch,
        compiler_params=pltpu.CompilerParams(dimension_semantics=("arbitrary", "arbitrary"),
                                             vmem_limit_bytes=VMEM_LIMIT),
        name="mixer_first" if first_layer else "mixer_next",
    )(*args)


def _route_kernel(slabt_ref, cnt_ref, dest_ref, meta_ref, *, n_steps):
    lane1 = lax.broadcasted_iota(I32, (1, LANES), 1)
    e_row = lax.broadcasted_iota(I32, (N_EXPERTS, LANES), 0)
    e_lane = lax.broadcasted_iota(I32, (N_EXPERTS, LANES), 1)
    cnt = cnt_ref[...].astype(I32)
    counts = jnp.sum(jnp.where(e_row == e_lane, cnt, 0), axis=0, keepdims=True)
    padded = ((counts + (STEP_ROWS - 1)) // STEP_ROWS) * STEP_ROWS
    pend = padded
    sh = 1
    while sh < N_EXPERTS:
        pend = pend + jnp.where(lane1 >= sh, pltpu.roll(pend, sh, axis=1), 0)
        sh *= 2

    @pl.when(pl.program_id(0) == 0)
    def _():
        rows = meta_ref.shape[0]
        bstart = lax.broadcasted_iota(I32, (rows, LANES), 0) * STEP_ROWS
        lane = lax.broadcasted_iota(I32, (rows, LANES), 1)
        done = jnp.where(jnp.logical_and(lane < N_EXPERTS, pend <= bstart), 1, 0)
        be = jnp.minimum(jnp.sum(done, axis=1, keepdims=True), N_EXPERTS - 1)
        mine = lane == be
        first_row = jnp.sum(jnp.where(mine, pend - counts, 0), axis=1, keepdims=True)
        used = bstart[:, 0:1] + STEP_ROWS - jnp.maximum(bstart[:, 0:1], first_row)
        nblk = jnp.clip((used + (EXPERT_BLOCK - 1)) // EXPERT_BLOCK, 0, STEP_ROWS // EXPERT_BLOCK)
        total = jnp.sum(jnp.where(lane1 == N_EXPERTS - 1, pend, 0), axis=1, keepdims=True)
        row = lax.broadcasted_iota(I32, (rows, LANES), 0)
        table = jnp.where(lane == 0, be, jnp.where(lane == 1, nblk, 0))
        meta_ref[...] = jnp.where(row == n_steps, total // STEP_ROWS, table)

    last_col = jnp.sum(jnp.where(e_row == e_lane, pend - 1, 0), axis=1, keepdims=True)
    tokens = slabt_ref.shape[1]
    eio = lax.broadcasted_iota(I32, (N_EXPERTS, tokens), 0)
    row = lax.broadcasted_iota(I32, (SUBLANES, tokens), 0)
    out = jnp.zeros((SUBLANES, tokens), I32)
    for k in range(TOP_K):
        idx = slabt_ref[k:k + 1, :].astype(I32)
        rank = slabt_ref[2 * TOP_K + k:2 * TOP_K + k + 1, :].astype(I32)
        base = jnp.sum(jnp.where(eio == idx, last_col, 0), axis=0, keepdims=True)
        for h in range(2):
            out = jnp.where(row == 2 * k + h, base - rank + h * n_steps * STEP_ROWS, out)
    dest_ref[...] = out


def _route_call(slabt, counts, n_tok, n_steps):
    meta_rows = ((n_steps + 1 + SUBLANES - 1) // SUBLANES) * SUBLANES
    return pl.pallas_call(
        functools.partial(_route_kernel, n_steps=n_steps),
        grid=(n_tok // ROUTE_T,),
        in_specs=[pl.BlockSpec((2 * SUBLANES, ROUTE_T), lambda i: (0, i)),
                  pl.BlockSpec((N_EXPERTS, LANES), lambda i: (0, 0))],
        out_specs=[pl.BlockSpec((SUBLANES, ROUTE_T), lambda i: (0, i)),
                   pl.BlockSpec((meta_rows, LANES), lambda i: (0, 0))],
        out_shape=[jax.ShapeDtypeStruct((SUBLANES, n_tok), I32),
                   jax.ShapeDtypeStruct((meta_rows, LANES), I32)],
        compiler_params=pltpu.CompilerParams(dimension_semantics=("arbitrary",)),
        name="route",
    )(slabt, counts)


def _sc_mesh():
    return plsc.VectorSubcoreMesh(core_axis_name="core", subcore_axis_name="subcore")


def _dispatch(src, idx, n_rows):
    windows = src.shape[0] // SC_WINDOW

    @functools.partial(pl.kernel, out_type=jax.ShapeDtypeStruct((n_rows, SC_ROW), src.dtype),
                       mesh=_sc_mesh(), scratch_types=[])
    def scatter_rows(x_hbm, i_hbm, o_hbm):
        def body(x_vmem, i_vmem):
            pltpu.sync_copy(x_vmem, o_hbm.at[i_vmem.at[0]])

        pltpu.emit_pipeline(
            body,
            grid=(idx.shape[1] // SC_WINDOW,),
            in_specs=[pl.BlockSpec((SC_WINDOW, SC_ROW), lambda i: (lax.rem(i, windows), 0)),
                      pl.BlockSpec((1, SC_WINDOW), lambda i: (0, i))],
            out_specs=[],
            core_axis_name=("core", "subcore"),
            dimension_semantics=(pltpu.PARALLEL,),
        )(x_hbm, i_hbm)

    return scatter_rows(src, idx)


def _combine(src, idx):
    n_idx = idx.shape[1]

    @functools.partial(pl.kernel, out_type=jax.ShapeDtypeStruct((n_idx, SC_ROW), src.dtype),
                       mesh=_sc_mesh(), scratch_types=[])
    def gather_rows(y_hbm, i_hbm, o_hbm):
        def body(i_vmem, o_vmem):
            pltpu.sync_copy(y_hbm.at[i_vmem.at[0]], o_vmem)

        pltpu.emit_pipeline(
            body,
            grid=(n_idx // SC_WINDOW,),
            in_specs=[pl.BlockSpec((1, SC_WINDOW), lambda i: (0, i))],
            out_specs=[pl.BlockSpec((SC_WINDOW, SC_ROW), lambda i: (i, 0))],
            core_axis_name=("core", "subcore"),
            dimension_semantics=(pltpu.PARALLEL,),
        )(i_hbm, o_hbm)

    return gather_rows(src, idx)


def _expert_rows(xs_ref, ys_ref, row0, n, wgu_s, bgu_ref, wdn_s, bdn_ref, act_s):
    rows = pl.ds(pl.multiple_of(row0, EXPERT_BLOCK), n)
    arow = slice(0, n)
    w0 = xs_ref[0, rows, :]
    w1 = xs_ref[1, rows, :]
    xb = jnp.concatenate([_unpack_lo(w0).astype(BF16), _unpack_lo(w1).astype(BF16),
                          _unpack_hi(w0).astype(BF16), _unpack_hi(w1).astype(BF16)], axis=1)
    half = D_EXPERT // 2
    for j in range(2):
        cols = slice(j * half, (j + 1) * half)
        ucols = slice(D_EXPERT + j * half, D_EXPERT + (j + 1) * half)
        g = jnp.dot(xb, wgu_s[:, cols], preferred_element_type=F32) + bgu_ref[:, cols]
        up = jnp.dot(xb, wgu_s[:, ucols], preferred_element_type=F32) + bgu_ref[:, ucols]
        g = jnp.minimum(g, SWIGLU_LIMIT)
        up = jnp.clip(up, -SWIGLU_LIMIT, SWIGLU_LIMIT)
        act_s[arow, cols] = (g * jax.nn.sigmoid(SWIGLU_ALPHA * g) * (up + 1.0)).astype(BF16)
    y = jnp.dot(act_s[arow, :], wdn_s[...], preferred_element_type=F32) + bdn_ref[...]
    packed = _pack_rows(y)
    ys_ref[0, rows, :] = packed[:, :SC_ROW]
    ys_ref[1, rows, :] = packed[:, SC_ROW:]


def _expert_kernel(be_ref, nb_ref, nv_ref, xs_ref, wgu_ref, bgu_ref, wdn_ref, bdn_ref, ys_ref, wgu_s, wdn_s, act_s):
    blk = pl.program_id(0)
    prev = be_ref[jnp.maximum(blk - 1, 0)]
    changed = jnp.logical_or(blk == 0, be_ref[blk] != prev)
    valid = blk < nv_ref[0]

    @pl.when(jnp.logical_and(valid, changed))
    def _():
        chunk = 128

        def cast_gu(i, _):
            rows = pl.ds(pl.multiple_of(i * chunk, chunk), chunk)
            wgu_s[rows, :] = wgu_ref[rows, :].astype(BF16)
            return 0

        def cast_dn(i, _):
            rows = pl.ds(pl.multiple_of(i * chunk, chunk), chunk)
            wdn_s[rows, :] = wdn_ref[rows, :].astype(BF16)
            return 0

        lax.fori_loop(0, D_MODEL // chunk, cast_gu, 0)
        lax.fori_loop(0, D_EXPERT // chunk, cast_dn, 0)

    nblk = nb_ref[blk]
    first = STEP_ROWS - nblk * EXPERT_BLOCK
    big = nblk // RUN_BLOCKS

    @pl.when(jnp.logical_and(valid, big > 0))
    def _():
        def one_run(i, _):
            row0 = first + (nblk - (i + 1) * RUN_BLOCKS) * EXPERT_BLOCK
            _expert_rows(xs_ref, ys_ref, row0, RUN_BLOCKS * EXPERT_BLOCK, wgu_s, bgu_ref, wdn_s, bdn_ref, act_s)
            return 0

        lax.fori_loop(0, big, one_run, 0)

    bit = RUN_BLOCKS // 2
    while bit:
        @pl.when(jnp.logical_and(valid, (nblk & bit) != 0))
        def _(bit=bit):
            row0 = first + (nblk & (bit - 1)) * EXPERT_BLOCK
            _expert_rows(xs_ref, ys_ref, row0, bit * EXPERT_BLOCK, wgu_s, bgu_ref, wdn_s, bdn_ref, act_s)
        bit //= 2


def _expert_call(xs, step_e, step_nblk, n_valid, w_gu, b_gu, w_dn, b_dn, layer, n_steps):
    def row_map(i, be, nb, nv):
        return (0, jnp.minimum(i, jnp.maximum(nv[0] - 1, 0)), 0)

    def w_map(i, be, nb, nv):
        return (layer, be[i], 0, 0)

    grid_spec = pltpu.PrefetchScalarGridSpec(
        num_scalar_prefetch=3,
        grid=(n_steps,),
        in_specs=[pl.BlockSpec((2, STEP_ROWS, SC_ROW), row_map),
                  pl.BlockSpec((None, None, D_MODEL, 2 * D_EXPERT), w_map),
                  pl.BlockSpec((None, None, 1, 2 * D_EXPERT), w_map),
                  pl.BlockSpec((None, None, D_EXPERT, D_MODEL), w_map),
                  pl.BlockSpec((None, None, 1, D_MODEL), w_map)],
        out_specs=pl.BlockSpec((2, STEP_ROWS, SC_ROW), row_map),
        scratch_shapes=[pltpu.VMEM((D_MODEL, 2 * D_EXPERT), BF16),
                        pltpu.VMEM((D_EXPERT, D_MODEL), BF16),
                        pltpu.VMEM((RUN_BLOCKS * EXPERT_BLOCK, D_EXPERT), BF16)])
    return pl.pallas_call(
        _expert_kernel,
        grid_spec=grid_spec,
        out_shape=jax.ShapeDtypeStruct((2, n_steps * STEP_ROWS, SC_ROW), I32),
        compiler_params=pltpu.CompilerParams(dimension_semantics=("arbitrary",),
                                             vmem_limit_bytes=VMEM_LIMIT),
        name="experts",
    )(step_e, step_nblk, n_valid, xs, w_gu, b_gu, w_dn, b_dn)


def _final_kernel(x_ref, y_ref, slab_ref, g_ref, *rest):
    o_ref = rest[-1]
    x = _moe_combine(x_ref[...], y_ref, slab_ref[...])
    o_ref[...] = _rms(x, g_ref[...])


def _final_call(x1, ysg, slab, g, out_row0, n_out, out_prev):
    t = FINAL_T
    out_tile0 = out_row0 // t
    in_specs = [pl.BlockSpec((t, D_MODEL), lambda i: (i, 0)),
                pl.BlockSpec((TOP_K, 2, t, SC_ROW), lambda i: (0, 0, i, 0)),
                pl.BlockSpec((t, LANES), lambda i: (i, 0)),
                pl.BlockSpec((1, D_MODEL), lambda i: (0, 0))]
    args = [x1, ysg, slab, g]
    aliases = {}
    if out_prev is not None:
        in_specs.append(pl.BlockSpec(memory_space=pl.ANY))
        args.append(out_prev)
        aliases = {4: 0}
    return pl.pallas_call(
        _final_kernel,
        grid=(x1.shape[0] // t,),
        in_specs=in_specs,
        out_specs=pl.BlockSpec((t, D_MODEL), lambda i: (out_tile0 + i, 0)),
        out_shape=jax.ShapeDtypeStruct((n_out, D_MODEL), F32),
        input_output_aliases=aliases,
        compiler_params=pltpu.CompilerParams(dimension_semantics=("arbitrary",), vmem_limit_bytes=VMEM_LIMIT),
        name="final_norm",
    )(*args)


def _s5_tables(lam_re, lam_im, b_re, b_im, c_re, c_im, log_step):
    lr = jnp.minimum(lam_re.astype(F32), LAMBDA_RE_MAX)
    li = lam_im.astype(F32)
    step = jnp.exp(log_step.astype(F32))[:, None]
    mag = jnp.exp(lr * step)
    ar = mag * jnp.cos(li * step)
    ai = mag * jnp.sin(li * step)
    nr = ar - 1.0
    den = lr * lr + li * li
    kr = (nr * lr + ai * li) / den
    ki = (ai * lr - nr * li) / den
    bre = b_re.astype(F32)
    bim = b_im.astype(F32)
    bbar_r = kr[..., None] * bre - ki[..., None] * bim
    bbar_i = kr[..., None] * bim + ki[..., None] * bre
    cre = c_re.astype(F32)
    cim = c_im.astype(F32)

    def power(m):
        m = jnp.asarray(m, F32)[..., None, None]
        mg = jnp.exp(m * lr * step)
        return mg * jnp.cos(m * li * step), mg * jnp.sin(m * li * step)

    pw_r, pw_i = power(jnp.arange(CHUNK + 1))
    cp_r = cre[None] * pw_r[:, :, None, :] - cim[None] * pw_i[:, :, None, :]
    cp_i = cre[None] * pw_i[:, :, None, :] + cim[None] * pw_r[:, :, None, :]
    width = CHUNK * SSM_GROUP
    kcat = (jnp.einsum('tgap,gph->ghta', cp_r[:CHUNK], bbar_r)
            - jnp.einsum('tgap,gph->ghta', cp_i[:CHUNK], bbar_i)).reshape(SSM_GROUPS, SSM_GROUP, width).astype(BF16)
    mo_r = cp_r[1:].transpose(1, 3, 0, 2).reshape(SSM_GROUPS, SSM_STATE, width)
    mo_i = -cp_i[1:].transpose(1, 3, 0, 2).reshape(SSM_GROUPS, SSM_STATE, width)
    mout = jnp.concatenate([mo_r, mo_i], axis=1).astype(BF16)
    q_r = pw_r[CHUNK - 1::-1][:CHUNK][:, :, :, None]
    q_i = pw_i[CHUNK - 1::-1][:CHUNK][:, :, :, None]
    mn_r = q_r * bbar_r[None] - q_i * bbar_i[None]
    mn_i = q_r * bbar_i[None] + q_i * bbar_r[None]
    minw = jnp.concatenate([mn_r, mn_i], axis=2).transpose(1, 0, 3, 2).reshape(
        SSM_GROUPS, CHUNK * SSM_GROUP, 2 * SSM_STATE).astype(BF16)

    def packed(j):
        pr, pi = power(float(CHUNK * j))
        return (jnp.concatenate([pr, pr], axis=-1).reshape(1, SSM_GROUPS * LANES),
                jnp.concatenate([-pi, pi], axis=-1).reshape(1, SSM_GROUPS * LANES))

    rows = jnp.arange(SUBLANES)[:, None]
    kinds = []
    for d in (1, 2, 4):
        c1, c2 = packed(d)
        mask = (rows >= d).astype(F32)
        kinds += [mask * c1, mask * c2]
    pk = [packed(r + 1) for r in range(SUBLANES)]
    kinds += [jnp.concatenate([p[0] for p in pk], axis=0), jnp.concatenate([p[1] for p in pk], axis=0)]
    coef = jnp.stack(kinds, axis=0)
    return minw, kcat, mout, coef


def _mixer_params(norm_mix, w_in, ssm_lam_re, ssm_lam_im, ssm_b_re, ssm_b_im, ssm_c_re, ssm_c_im, ssm_d,
                  ssm_log_step, ssm_w_glu, ssm_b_glu, pool_w, pool_scale, conv_w, branch_norm, w_out, norm_ffn,
                  router_w, router_b):
    depth = w_in.shape[0]
    minw, kcat, mout, coef = jax.vmap(_s5_tables)(ssm_lam_re, ssm_lam_im, ssm_b_re, ssm_b_im, ssm_c_re, ssm_c_im,
                                                  ssm_log_step)
    eye4 = jnp.eye(len(POOL_WINDOWS), dtype=F32)
    poolw = jnp.einsum('lgcd,gk->lgckd', pool_w.astype(F32), eye4).reshape(depth, POOL_WIDTH, POOL_WIDTH).astype(BF16)
    rw = jnp.pad(router_w.astype(F32), ((0, 0), (0, 0), (0, LANES - N_EXPERTS))).astype(BF16)
    rb = jnp.pad(router_b.astype(F32), ((0, 0), (0, LANES - N_EXPERTS))).reshape(depth, 1, LANES)
    row = lambda v: v.reshape(depth, 1, -1).astype(F32)
    utri = jnp.triu(jnp.ones((TILE_T, TILE_T), F32), 1).astype(BF16)
    return dict(gmix=row(norm_mix), win=w_in.astype(BF16), minw=minw, kcat=kcat, mout=mout, coef=coef, dskip=row(ssm_d),
                wglu=ssm_w_glu.astype(BF16), bglu=row(ssm_b_glu), poolw=poolw, pscale=row(pool_scale),
                convw=conv_w.astype(F32), gbr=row(branch_norm), wout=w_out.astype(BF16),
                gffn=row(norm_ffn), rw=rw, rb=rb, utri=utri[None])


def kernel(x, norm_mix, w_in, ssm_lam_re, ssm_lam_im, ssm_b_re, ssm_b_im, ssm_c_re, ssm_c_im, ssm_d, ssm_log_step, ssm_w_glu, ssm_b_glu, pool_w, pool_scale, conv_w, branch_norm, w_out, norm_ffn, router_w, router_b, w_gate_up, b_gate_up, w_down, b_down, final_norm):
    batch, seq, d_model = x.shape
    assert d_model == D_MODEL and seq % TILE_T == 0
    depth = w_in.shape[0]
    n_streams = 2 if batch % 2 == 0 else 1
    sb = batch // n_streams
    n_tok = sb * seq
    assert n_tok % ROUTE_T == 0 and n_tok % SC_WINDOW == 0
    n_steps = -(-(n_tok * TOP_K + N_EXPERTS * (STEP_ROWS - 1)) // STEP_ROWS)
    n_rows = n_steps * STEP_ROWS
    tiles = n_tok // TILE_T
    b_gu = b_gate_up.reshape(depth, N_EXPERTS, 1, 2 * D_EXPERT)
    b_dn = b_down.reshape(depth, N_EXPERTS, 1, D_MODEL)

    xin = [x.reshape(batch * seq, D_MODEL)] * n_streams
    tile0 = [s * tiles for s in range(n_streams)]
    ysg = [None] * n_streams
    slab = [None] * n_streams
    lp = _mixer_params(norm_mix, w_in, ssm_lam_re, ssm_lam_im, ssm_b_re, ssm_b_im, ssm_c_re, ssm_c_im, ssm_d,
                       ssm_log_step, ssm_w_glu, ssm_b_glu, pool_w, pool_scale, conv_w, branch_norm, w_out,
                       norm_ffn, router_w, router_b)
    for l in range(depth):
        routed = []
        for s in range(n_streams):
            xin[s], hp, slab[s], slabt, counts = _mixer_call(xin[s], tile0[s], ysg[s], slab[s], lp, l, sb, seq)
            tile0[s] = 0
            dest, meta = _route_call(slabt, counts, n_tok, n_steps)
            idx = dest.reshape(1, 2 * TOP_K * n_tok)
            xs = _dispatch(hp.reshape(2 * n_tok, SC_ROW), idx, 2 * n_rows).reshape(2, n_rows, SC_ROW)
            routed.append((xs, idx, meta[:n_steps, 0], meta[:n_steps, 1], meta[n_steps:n_steps + 1, 0]))
        for s, (xs, idx, step_e, step_nblk, n_valid) in enumerate(routed):
            ys = _expert_call(xs, step_e, step_nblk, n_valid, w_gate_up, b_gu, w_down, b_dn, l, n_steps)
            ysg[s] = _combine(ys.reshape(2 * n_rows, SC_ROW), idx).reshape(TOP_K, 2, n_tok, SC_ROW)
    g = final_norm.reshape(1, D_MODEL).astype(F32)
    out = None
    for s in range(n_streams):
        out = _final_call(xin[s], ysg[s], slab[s], g, s * n_tok, batch * seq, out)
    return out.reshape(batch, seq, D_MODEL)
```

```python
import functools

import jax
import jax.numpy as jnp
from jax import lax
from jax.experimental import pallas as pl
from jax.experimental.pallas import tpu as pltpu
from jax.experimental.pallas import tpu_sc as plsc

F32 = jnp.float32
BF16 = jnp.bfloat16
I32 = jnp.int32

D_MODEL = 1024
SSM_WIDTH = 512
POOL_WIDTH = 256
CONV_WIDTH = 256
SSM_GROUP = 16
SSM_GROUPS = 32
SSM_STATE = 64
LAMBDA_RE_MAX = -1e-4
POOL_WINDOWS = (2, 4, 8, 16)
POOL_GROUP = 64
N_EXPERTS = 32
TOP_K = 4
D_EXPERT = 1024
SWIGLU_LIMIT = 7.0
SWIGLU_ALPHA = 1.702
EXPERT_BLOCK = 256
STEP_ROWS = 2048
RUN_BLOCKS = 4
NORM_EPS = 1e-5

LANES = 128
SUBLANES = 8
TILE_T = 512
FINAL_T = 1024
CHUNK = 16
N_CHUNK = TILE_T // CHUNK
POOL_TAIL = SUBLANES * len(POOL_WINDOWS)
assert POOL_WINDOWS == tuple(2 ** (k + 1) for k in range(len(POOL_WINDOWS))) and POOL_TAIL >= POOL_WINDOWS[-1]
PACKED = D_MODEL // 2
ROUTE_T = 2048
SC_WINDOW = 128
SC_ROW = PACKED // 2
HI_MASK = -65536
VMEM_LIMIT = 56 * 1024 * 1024


def _bf16_round(v):
    return v.astype(BF16).astype(F32)


def _pack_rows(v):
    lo = lax.shift_right_logical(lax.bitcast_convert_type(_bf16_round(v[:, :PACKED]), I32), 16)
    hi = lax.bitcast_convert_type(_bf16_round(v[:, PACKED:]), I32) & HI_MASK
    return hi | lo


def _unpack_lo(w):
    return lax.bitcast_convert_type(lax.shift_left(w, 16), F32)


def _unpack_hi(w):
    return lax.bitcast_convert_type(w & HI_MASK, F32)


def _rms(v, g):
    r = lax.rsqrt(jnp.mean(v * v, axis=-1, keepdims=True) + NORM_EPS)
    return (v * r) * g


def _moe_combine(x, y_ref, slab):
    parts = [x[:, i * SC_ROW:(i + 1) * SC_ROW] for i in range(4)]
    for k in range(TOP_K):
        g = slab[:, TOP_K + k:TOP_K + k + 1]
        for h in range(2):
            w = y_ref[k, h]
            parts[h] = parts[h] + g * _unpack_lo(w)
            parts[2 + h] = parts[2 + h] + g * _unpack_hi(w)
    return jnp.concatenate(parts, axis=1)


def _block_transpose8(vs, lane):
    vs = list(vs)
    for dist in (4, 2, 1):
        width = SSM_GROUP * dist
        low = (lane % (2 * width)) < width
        for j in range(8):
            if j & dist:
                continue
            a, b = vs[j], vs[j + dist]
            vs[j] = jnp.where(low, a, pltpu.roll(b, width, axis=1))
            vs[j + dist] = jnp.where(low, pltpu.roll(a, LANES - width, axis=1), b)
    return vs


def _cmul_packed(c1, c2, v):
    return c1 * v + c2 * pltpu.roll(v, SSM_STATE, axis=1)


def _chunk_scan(s, carry_in, cf):
    outs = []
    carry = jnp.broadcast_to(carry_in, (SUBLANES, LANES))
    for rg in range(N_CHUNK // SUBLANES):
        x = s[rg * SUBLANES:(rg + 1) * SUBLANES]
        for j, d in enumerate((1, 2, 4)):
            x = x + _cmul_packed(cf[2 * j], cf[2 * j + 1], pltpu.roll(x, d, axis=0))
        x = x + _cmul_packed(cf[6], cf[7], carry)
        outs.append(x)
        carry = jnp.broadcast_to(x[SUBLANES - 1:SUBLANES], (SUBLANES, LANES))
    return jnp.concatenate(outs, axis=0)


def _expand_toeplitz(kcat_ref, mint_s):
    lane = lax.broadcasted_iota(I32, (SSM_GROUP, LANES), 1)
    zero = jnp.zeros((SSM_GROUP, LANES), BF16)

    def one_group(g, _):
        lo_src = kcat_ref[g, :, 0:LANES]
        hi_src = kcat_ref[g, :, LANES:2 * LANES]
        for s in range(CHUNK):
            d = SSM_GROUP * s
            rows = slice(SSM_GROUP * s, SSM_GROUP * (s + 1))
            if d == 0:
                lo, hi = lo_src, hi_src
            elif d < LANES:
                wrapped = pltpu.roll(lo_src, d, axis=1)
                lo = jnp.where(lane < d, zero, wrapped)
                hi = jnp.where(lane < d, wrapped, pltpu.roll(hi_src, d, axis=1))
            elif d == LANES:
                lo, hi = zero, lo_src
            else:
                lo = zero
                hi = jnp.where(lane < d - LANES, zero, pltpu.roll(lo_src, d - LANES, axis=1))
            mint_s[g, rows, 0:LANES] = lo
            mint_s[g, rows, LANES:2 * LANES] = hi
        return 0

    lax.fori_loop(0, SSM_GROUPS, one_group, 0)


def _s5_mixer(u, first, minw_ref, mint_s, mout_ref, coef_ref, u_s, y_s, h_s):
    blocks = SSM_WIDTH // LANES
    top = SUBLANES - 1
    for b4 in range(blocks):
        u_s[b4] = u[:, b4 * LANES:(b4 + 1) * LANES]
    lane = lax.broadcasted_iota(I32, (N_CHUNK, LANES), 1)

    halves = [[None, None] for _ in range(SSM_GROUPS)]
    for b4 in range(blocks):
        for hh in range(2):
            xs = [u_s[b4, pl.ds(8 * hh + j, N_CHUNK, stride=CHUNK), :].astype(BF16) for j in range(8)]
            ws = _block_transpose8(xs, lane)
            for gl in range(8):
                halves[8 * b4 + gl][hh] = ws[gl]
    ug = [jnp.concatenate(h2, axis=1) for h2 in halves]

    carry_in = jnp.where(first, 0.0, h_s[top + N_CHUNK:top + N_CHUNK + 1, :])
    h_s[top:top + 1, :] = carry_in
    for g in range(SSM_GROUPS):
        cols = slice(g * LANES, (g + 1) * LANES)
        s_g = jnp.dot(ug[g], minw_ref[g], preferred_element_type=F32)
        cf = [coef_ref[k, :, cols] for k in range(8)]
        h_s[SUBLANES:SUBLANES + N_CHUNK, cols] = _chunk_scan(s_g, carry_in[:, cols], cf)

    yg = []
    for g in range(SSM_GROUPS):
        hprev = h_s[top:top + N_CHUNK, g * LANES:(g + 1) * LANES].astype(BF16)
        y_g = (jnp.dot(ug[g], mint_s[g], preferred_element_type=F32)
               + jnp.dot(hprev, mout_ref[g], preferred_element_type=F32))
        yg.append(y_g.astype(BF16))

    for b4 in range(blocks):
        for hh in range(2):
            ws = [yg[8 * b4 + gl][:, hh * LANES:(hh + 1) * LANES] for gl in range(8)]
            xs = _block_transpose8(ws, lane)
            for j in range(8):
                y_s[b4, pl.ds(8 * hh + j, N_CHUNK, stride=CHUNK), :] = xs[j].astype(F32)
    return jnp.concatenate([y_s[b4] for b4 in range(blocks)], axis=1)


def _mixer_kernel(*refs, first_layer):
    if first_layer:
        x_ref = refs[0]
        refs = refs[1:]
    else:
        x_ref, yprev_ref, slabprev_ref = refs[:3]
        refs = refs[3:]
    (gmix_ref, win_ref, minw_ref, kcat_ref, mout_ref, coef_ref, dskip_ref, wglu_ref, bglu_ref, poolw_ref, pscale_ref,
     convw_ref, gbr_ref, wout_ref, gffn_ref, rw_ref, rb_ref, utri_ref,
     x1_ref, hp_ref, slab_ref, slabt_ref, cnt_ref,
     u_s, y_s, h_s, pbuf_ref, zbuf_ref, cntacc_ref, mint_s) = refs

    b = pl.program_id(0)
    l = pl.program_id(1)
    seq_start = l == 0

    @pl.when(jnp.logical_and(b == 0, l == 0))
    def _():
        cntacc_ref[...] = jnp.zeros_like(cntacc_ref)
        _expand_toeplitz(kcat_ref, mint_s)

    if first_layer:
        x = x_ref[...]
    else:
        x = _moe_combine(x_ref[...], yprev_ref, slabprev_ref[...])

    hn = _rms(x, gmix_ref[...]).astype(BF16)
    proj = jnp.dot(hn, win_ref[...], preferred_element_type=F32)
    u = proj[:, :SSM_WIDTH]
    p = proj[:, SSM_WIDTH:SSM_WIDTH + POOL_WIDTH]
    c0 = SSM_WIDTH + POOL_WIDTH
    bg = proj[:, c0:c0 + CONV_WIDTH]
    cg = proj[:, c0 + CONV_WIDTH:c0 + 2 * CONV_WIDTH]
    hv = proj[:, c0 + 2 * CONV_WIDTH:]

    y = _s5_mixer(u, seq_start, minw_ref, mint_s, mout_ref, coef_ref, u_s, y_s, h_s) + dskip_ref[...] * u
    z = jax.nn.gelu(y)
    glu = jnp.dot(z.astype(BF16), wglu_ref[...], preferred_element_type=F32) + bglu_ref[...]
    y_ssm = z * jax.nn.sigmoid(glu)

    tail = POOL_TAIL
    pbuf_ref[0, 0:tail, :] = jnp.where(seq_start, 0.0, pbuf_ref[0, TILE_T:TILE_T + tail, :])
    pbuf_ref[0, tail:, :] = p
    sums = []
    for stage, w in enumerate(POOL_WINDOWS):
        src = 0 if stage == 0 else 1 + (stage - 1) % 2
        dst = 1 + stage % 2
        d = w // 2
        lo = SUBLANES * (stage + 1)
        n = tail + TILE_T - lo
        e = pbuf_ref[src, pl.ds(lo, n), :] + pbuf_ref[src, pl.ds(lo - d, n), :]
        if stage + 1 < len(POOL_WINDOWS):
            pbuf_ref[dst, pl.ds(lo, n), :] = e
        sums.append(e[tail - lo:, :])
    lane_p = lax.broadcasted_iota(I32, (TILE_T, POOL_WIDTH), 1)
    total = sums[-1]
    win = jnp.full((TILE_T, POOL_WIDTH), POOL_WINDOWS[-1], I32)
    for i in range(len(POOL_WINDOWS) - 2, -1, -1):
        sel = lane_p < (i + 1) * POOL_GROUP
        total = jnp.where(sel, sums[i], total)
        win = jnp.where(sel, POOL_WINDOWS[i], win)
    row_p = lax.broadcasted_iota(I32, (TILE_T, POOL_WIDTH), 0) + l * TILE_T
    count = jnp.minimum(row_p + 1, win).astype(F32)
    pooled = total / count - p
    mixed = jnp.dot(pooled.astype(BF16), poolw_ref[...], preferred_element_type=F32)
    y_pool = mixed * pscale_ref[...]

    zc = cg * hv
    zbuf_ref[0:SUBLANES, :] = jnp.where(seq_start, 0.0, zbuf_ref[TILE_T:TILE_T + SUBLANES, :])
    zbuf_ref[SUBLANES:, :] = zc
    yc = (convw_ref[0:1, :] * zbuf_ref[pl.ds(SUBLANES - 2, TILE_T), :]
          + convw_ref[1:2, :] * zbuf_ref[pl.ds(SUBLANES - 1, TILE_T), :]
          + convw_ref[2:3, :] * zc)
    y_conv = bg * yc

    gbr = gbr_ref[...]
    s1 = SSM_WIDTH + POOL_WIDTH
    mixed_all = jnp.concatenate([_rms(y_ssm, gbr[:, :SSM_WIDTH]).astype(BF16),
                                 _rms(y_pool, gbr[:, SSM_WIDTH:s1]).astype(BF16),
                                 _rms(y_conv, gbr[:, s1:]).astype(BF16)], axis=1)
    x1 = x + jnp.dot(mixed_all, wout_ref[...], preferred_element_type=F32)
    x1_ref[...] = x1

    hn2 = _rms(x1, gffn_ref[...])
    packed = _pack_rows(hn2)
    hp_ref[0] = packed[:, :SC_ROW]
    hp_ref[1] = packed[:, SC_ROW:]
    logits = jnp.dot(hn2.astype(BF16), rw_ref[...], preferred_element_type=F32) + rb_ref[...]
    work = logits.T[:N_EXPERTS, :]
    eio = lax.broadcasted_iota(I32, (N_EXPERTS, TILE_T), 0).astype(F32)
    neg = jnp.float32(-jnp.inf)
    vals, idxs = [], []
    for _ in range(TOP_K):
        m = jnp.max(work, axis=0, keepdims=True)
        idx = jnp.min(jnp.where(work == m, eio, float(N_EXPERTS)), axis=0, keepdims=True)
        vals.append(m)
        idxs.append(idx)
        work = jnp.where(eio == idx, neg, work)
    exps = [jnp.exp(v - vals[0]) for v in vals]
    denom = exps[0] + exps[1] + exps[2] + exps[3]
    gates = [e / denom for e in exps]

    onehot = jnp.zeros((N_EXPERTS, TILE_T), F32)
    for idx in idxs:
        onehot = jnp.where(eio == idx, 1.0, onehot)
    before = jnp.dot(onehot.astype(BF16), utri_ref[...], preferred_element_type=F32) + cntacc_ref[:, 0:1]
    ranks = [jnp.sum(jnp.where(eio == idx, before, 0.0), axis=0, keepdims=True) for idx in idxs]
    row = lax.broadcasted_iota(I32, (LANES, TILE_T), 0)
    slab_t = jnp.zeros((LANES, TILE_T), F32)
    for k in range(TOP_K):
        slab_t = jnp.where(row == k, idxs[k], slab_t)
        slab_t = jnp.where(row == TOP_K + k, gates[k], slab_t)
        slab_t = jnp.where(row == 2 * TOP_K + k, ranks[k], slab_t)
    slabt_ref[...] = slab_t[:2 * SUBLANES, :]
    slab_ref[...] = slab_t.T
    newcnt = cntacc_ref[:, 0:1] + jnp.sum(onehot, axis=1, keepdims=True)
    cntacc_ref[...] = jnp.broadcast_to(newcnt, cntacc_ref.shape)
    cnt_ref[...] = jnp.broadcast_to(newcnt, cnt_ref.shape)


def _layer_spec(shape, layer):
    nd = len(shape) - 1
    sel = layer if shape[0] > 1 else 0
    return pl.BlockSpec((None,) + tuple(shape[1:]), lambda b, l, nd=nd, sel=sel: (sel,) + (0,) * nd,
                        pipeline_mode=pl.Buffered(1))


def _mixer_call(x, x_tile0, yprev, slabprev, lp, layer, batch, seq):
    first_layer = yprev is None
    n_tok = batch * seq
    steps = seq // TILE_T
    tok_spec = lambda w: pl.BlockSpec((TILE_T, w), lambda b, l: (b * steps + l, 0))
    in_specs = [pl.BlockSpec((TILE_T, D_MODEL), lambda b, l: (x_tile0 + b * steps + l, 0))]
    args = [x]
    if not first_layer:
        in_specs += [pl.BlockSpec((TOP_K, 2, TILE_T, SC_ROW), lambda b, l: (0, 0, b * steps + l, 0)), tok_spec(LANES)]
        args += [yprev, slabprev]
    weights = [lp['gmix'], lp['win'], lp['minw'], lp['kcat'], lp['mout'], lp['coef'], lp['dskip'], lp['wglu'], lp['bglu'],
               lp['poolw'], lp['pscale'], lp['convw'], lp['gbr'], lp['wout'], lp['gffn'], lp['rw'], lp['rb'],
               lp['utri']]
    in_specs += [_layer_spec(w.shape, layer) for w in weights]
    args += weights
    out_shape = [jax.ShapeDtypeStruct((n_tok, D_MODEL), F32),
                 jax.ShapeDtypeStruct((2, n_tok, SC_ROW), I32),
                 jax.ShapeDtypeStruct((n_tok, LANES), F32),
                 jax.ShapeDtypeStruct((2 * SUBLANES, n_tok), F32),
                 jax.ShapeDtypeStruct((N_EXPERTS, LANES), F32)]
    out_specs = [tok_spec(D_MODEL), pl.BlockSpec((2, TILE_T, SC_ROW), lambda b, l: (0, b * steps + l, 0)), tok_spec(LANES),
                 pl.BlockSpec((2 * SUBLANES, TILE_T), lambda b, l: (0, b * steps + l)),
                 pl.BlockSpec((N_EXPERTS, LANES), lambda b, l: (0, 0))]
    scratch = [pltpu.VMEM((SSM_WIDTH // LANES, TILE_T, LANES), F32),
               pltpu.VMEM((SSM_WIDTH // LANES, TILE_T, LANES), F32),
               pltpu.VMEM((SUBLANES + N_CHUNK, SSM_GROUPS * LANES), F32),
               pltpu.VMEM((3, POOL_TAIL + TILE_T, POOL_WIDTH), F32),
               pltpu.VMEM((SUBLANES + TILE_T, CONV_WIDTH), F32),
               pltpu.VMEM((N_EXPERTS, LANES), F32),
               pltpu.VMEM((SSM_GROUPS, CHUNK * SSM_GROUP, CHUNK * SSM_GROUP), BF16)]
    return pl.pallas_call(
        functools.partial(_mixer_kernel, first_layer=first_layer),
        grid=(batch, steps),
        in_specs=in_specs, out_specs=out_specs, out_shape=out_shape, scratch_shapes=scratch,
        compiler_params=pltpu.CompilerParams(dimension_semantics=("arbitrary", "arbitrary"),
                                             vmem_limit_bytes=VMEM_LIMIT),
        name="mixer_first" if first_layer else "mixer_next",
    )(*args)


def _route_kernel(slabt_ref, cnt_ref, dest_ref, meta_ref, *, n_steps):
    lane1 = lax.broadcasted_iota(I32, (1, LANES), 1)
    e_row = lax.broadcasted_iota(I32, (N_EXPERTS, LANES), 0)
    e_lane = lax.broadcasted_iota(I32, (N_EXPERTS, LANES), 1)
    cnt = cnt_ref[...].astype(I32)
    counts = jnp.sum(jnp.where(e_row == e_lane, cnt, 0), axis=0, keepdims=True)
    padded = ((counts + (STEP_ROWS - 1)) // STEP_ROWS) * STEP_ROWS
    pend = padded
    sh = 1
    while sh < N_EXPERTS:
        pend = pend + jnp.where(lane1 >= sh, pltpu.roll(pend, sh, axis=1), 0)
        sh *= 2

    @pl.when(pl.program_id(0) == 0)
    def _():
        rows = meta_ref.shape[0]
        bstart = lax.broadcasted_iota(I32, (rows, LANES), 0) * STEP_ROWS
        lane = lax.broadcasted_iota(I32, (rows, LANES), 1)
        done = jnp.where(jnp.logical_and(lane < N_EXPERTS, pend <= bstart), 1, 0)
        be = jnp.minimum(jnp.sum(done, axis=1, keepdims=True), N_EXPERTS - 1)
        mine = lane == be
        first_row = jnp.sum(jnp.where(mine, pend - counts, 0), axis=1, keepdims=True)
        used = bstart[:, 0:1] + STEP_ROWS - jnp.maximum(bstart[:, 0:1], first_row)
        nblk = jnp.clip((used + (EXPERT_BLOCK - 1)) // EXPERT_BLOCK, 0, STEP_ROWS // EXPERT_BLOCK)
        total = jnp.sum(jnp.where(lane1 == N_EXPERTS - 1, pend, 0), axis=1, keepdims=True)
        row = lax.broadcasted_iota(I32, (rows, LANES), 0)
        table = jnp.where(lane == 0, be, jnp.where(lane == 1, nblk, 0))
        meta_ref[...] = jnp.where(row == n_steps, total // STEP_ROWS, table)

    last_col = jnp.sum(jnp.where(e_row == e_lane, pend - 1, 0), axis=1, keepdims=True)
    tokens = slabt_ref.shape[1]
    eio = lax.broadcasted_iota(I32, (N_EXPERTS, tokens), 0)
    row = lax.broadcasted_iota(I32, (SUBLANES, tokens), 0)
    out = jnp.zeros((SUBLANES, tokens), I32)
    for k in range(TOP_K):
        idx = slabt_ref[k:k + 1, :].astype(I32)
        rank = slabt_ref[2 * TOP_K + k:2 * TOP_K + k + 1, :].astype(I32)
        base = jnp.sum(jnp.where(eio == idx, last_col, 0), axis=0, keepdims=True)
        for h in range(2):
            out = jnp.where(row == 2 * k + h, base - rank + h * n_steps * STEP_ROWS, out)
    dest_ref[...] = out


def _route_call(slabt, counts, n_tok, n_steps):
    meta_rows = ((n_steps + 1 + SUBLANES - 1) // SUBLANES) * SUBLANES
    return pl.pallas_call(
        functools.partial(_route_kernel, n_steps=n_steps),
        grid=(n_tok // ROUTE_T,),
        in_specs=[pl.BlockSpec((2 * SUBLANES, ROUTE_T), lambda i: (0, i)),
                  pl.BlockSpec((N_EXPERTS, LANES), lambda i: (0, 0))],
        out_specs=[pl.BlockSpec((SUBLANES, ROUTE_T), lambda i: (0, i)),
                   pl.BlockSpec((meta_rows, LANES), lambda i: (0, 0))],
        out_shape=[jax.ShapeDtypeStruct((SUBLANES, n_tok), I32),
                   jax.ShapeDtypeStruct((meta_rows, LANES), I32)],
        compiler_params=pltpu.CompilerParams(dimension_semantics=("arbitrary",)),
        name="route",
    )(slabt, counts)


def _sc_mesh():
    return plsc.VectorSubcoreMesh(core_axis_name="core", subcore_axis_name="subcore")


def _dispatch(src, idx, n_rows):
    windows = src.shape[0] // SC_WINDOW

    @functools.partial(pl.kernel, out_type=jax.ShapeDtypeStruct((n_rows, SC_ROW), src.dtype),
                       mesh=_sc_mesh(), scratch_types=[])
    def scatter_rows(x_hbm, i_hbm, o_hbm):
        def body(x_vmem, i_vmem):
            pltpu.sync_copy(x_vmem, o_hbm.at[i_vmem.at[0]])

        pltpu.emit_pipeline(
            body,
            grid=(idx.shape[1] // SC_WINDOW,),
            in_specs=[pl.BlockSpec((SC_WINDOW, SC_ROW), lambda i: (lax.rem(i, windows), 0)),
                      pl.BlockSpec((1, SC_WINDOW), lambda i: (0, i))],
            out_specs=[],
            core_axis_name=("core", "subcore"),
            dimension_semantics=(pltpu.PARALLEL,),
        )(x_hbm, i_hbm)

    return scatter_rows(src, idx)


def _combine(src, idx):
    n_idx = idx.shape[1]

    @functools.partial(pl.kernel, out_type=jax.ShapeDtypeStruct((n_idx, SC_ROW), src.dtype),
                       mesh=_sc_mesh(), scratch_types=[])
    def gather_rows(y_hbm, i_hbm, o_hbm):
        def body(i_vmem, o_vmem):
            pltpu.sync_copy(y_hbm.at[i_vmem.at[0]], o_vmem)

        pltpu.emit_pipeline(
            body,
            grid=(n_idx // SC_WINDOW,),
            in_specs=[pl.BlockSpec((1, SC_WINDOW), lambda i: (0, i))],
            out_specs=[pl.BlockSpec((SC_WINDOW, SC_ROW), lambda i: (i, 0))],
            core_axis_name=("core", "subcore"),
            dimension_semantics=(pltpu.PARALLEL,),
        )(i_hbm, o_hbm)

    return gather_rows(src, idx)


def _expert_rows(xs_ref, ys_ref, row0, n, wgu_s, bgu_ref, wdn_s, bdn_ref, act_s):
    rows = pl.ds(pl.multiple_of(row0, EXPERT_BLOCK), n)
    arow = slice(0, n)
    w0 = xs_ref[0, rows, :]
    w1 = xs_ref[1, rows, :]
    xb = jnp.concatenate([_unpack_lo(w0).astype(BF16), _unpack_lo(w1).astype(BF16),
                          _unpack_hi(w0).astype(BF16), _unpack_hi(w1).astype(BF16)], axis=1)
    half = D_EXPERT // 2
    for j in range(2):
        cols = slice(j * half, (j + 1) * half)
        ucols = slice(D_EXPERT + j * half, D_EXPERT + (j + 1) * half)
        g = jnp.dot(xb, wgu_s[:, cols], preferred_element_type=F32) + bgu_ref[:, cols]
        up = jnp.dot(xb, wgu_s[:, ucols], preferred_element_type=F32) + bgu_ref[:, ucols]
        g = jnp.minimum(g, SWIGLU_LIMIT)
        up = jnp.clip(up, -SWIGLU_LIMIT, SWIGLU_LIMIT)
        act_s[arow, cols] = (g * jax.nn.sigmoid(SWIGLU_ALPHA * g) * (up + 1.0)).astype(BF16)
    y = jnp.dot(act_s[arow, :], wdn_s[...], preferred_element_type=F32) + bdn_ref[...]
    packed = _pack_rows(y)
    ys_ref[0, rows, :] = packed[:, :SC_ROW]
    ys_ref[1, rows, :] = packed[:, SC_ROW:]


def _expert_kernel(be_ref, nb_ref, nv_ref, xs_ref, wgu_ref, bgu_ref, wdn_ref, bdn_ref, ys_ref, wgu_s, wdn_s, act_s):
    blk = pl.program_id(0)
    prev = be_ref[jnp.maximum(blk - 1, 0)]
    changed = jnp.logical_or(blk == 0, be_ref[blk] != prev)
    valid = blk < nv_ref[0]

    @pl.when(jnp.logical_and(valid, changed))
    def _():
        chunk = 128

        def cast_gu(i, _):
            rows = pl.ds(pl.multiple_of(i * chunk, chunk), chunk)
            wgu_s[rows, :] = wgu_ref[rows, :].astype(BF16)
            return 0

        def cast_dn(i, _):
            rows = pl.ds(pl.multiple_of(i * chunk, chunk), chunk)
            wdn_s[rows, :] = wdn_ref[rows, :].astype(BF16)
            return 0

        lax.fori_loop(0, D_MODEL // chunk, cast_gu, 0)
        lax.fori_loop(0, D_EXPERT // chunk, cast_dn, 0)

    nblk = nb_ref[blk]
    first = STEP_ROWS - nblk * EXPERT_BLOCK
    big = nblk // RUN_BLOCKS

    @pl.when(jnp.logical_and(valid, big > 0))
    def _():
        def one_run(i, _):
            row0 = first + (nblk - (i + 1) * RUN_BLOCKS) * EXPERT_BLOCK
            _expert_rows(xs_ref, ys_ref, row0, RUN_BLOCKS * EXPERT_BLOCK, wgu_s, bgu_ref, wdn_s, bdn_ref, act_s)
            return 0

        lax.fori_loop(0, big, one_run, 0)

    bit = RUN_BLOCKS // 2
    while bit:
        @pl.when(jnp.logical_and(valid, (nblk & bit) != 0))
        def _(bit=bit):
            row0 = first + (nblk & (bit - 1)) * EXPERT_BLOCK
            _expert_rows(xs_ref, ys_ref, row0, bit * EXPERT_BLOCK, wgu_s, bgu_ref, wdn_s, bdn_ref, act_s)
        bit //= 2


def _expert_call(xs, step_e, step_nblk, n_valid, w_gu, b_gu, w_dn, b_dn, layer, n_steps):
    def row_map(i, be, nb, nv):
        return (0, jnp.minimum(i, jnp.maximum(nv[0] - 1, 0)), 0)

    def w_map(i, be, nb, nv):
        return (layer, be[i], 0, 0)

    grid_spec = pltpu.PrefetchScalarGridSpec(
        num_scalar_prefetch=3,
        grid=(n_steps,),
        in_specs=[pl.BlockSpec((2, STEP_ROWS, SC_ROW), row_map),
                  pl.BlockSpec((None, None, D_MODEL, 2 * D_EXPERT), w_map),
                  pl.BlockSpec((None, None, 1, 2 * D_EXPERT), w_map),
                  pl.BlockSpec((None, None, D_EXPERT, D_MODEL), w_map),
                  pl.BlockSpec((None, None, 1, D_MODEL), w_map)],
        out_specs=pl.BlockSpec((2, STEP_ROWS, SC_ROW), row_map),
        scratch_shapes=[pltpu.VMEM((D_MODEL, 2 * D_EXPERT), BF16),
                        pltpu.VMEM((D_EXPERT, D_MODEL), BF16),
                        pltpu.VMEM((RUN_BLOCKS * EXPERT_BLOCK, D_EXPERT), BF16)])
    return pl.pallas_call(
        _expert_kernel,
        grid_spec=grid_spec,
        out_shape=jax.ShapeDtypeStruct((2, n_steps * STEP_ROWS, SC_ROW), I32),
        compiler_params=pltpu.CompilerParams(dimension_semantics=("arbitrary",),
                                             vmem_limit_bytes=VMEM_LIMIT),
        name="experts",
    )(step_e, step_nblk, n_valid, xs, w_gu, b_gu, w_dn, b_dn)


def _final_kernel(x_ref, y_ref, slab_ref, g_ref, *rest):
    o_ref = rest[-1]
    x = _moe_combine(x_ref[...], y_ref, slab_ref[...])
    o_ref[...] = _rms(x, g_ref[...])


def _final_call(x1, ysg, slab, g, out_row0, n_out, out_prev):
    t = FINAL_T
    out_tile0 = out_row0 // t
    in_specs = [pl.BlockSpec((t, D_MODEL), lambda i: (i, 0)),
                pl.BlockSpec((TOP_K, 2, t, SC_ROW), lambda i: (0, 0, i, 0)),
                pl.BlockSpec((t, LANES), lambda i: (i, 0)),
                pl.BlockSpec((1, D_MODEL), lambda i: (0, 0))]
    args = [x1, ysg, slab, g]
    aliases = {}
    if out_prev is not None:
        in_specs.append(pl.BlockSpec(memory_space=pl.ANY))
        args.append(out_prev)
        aliases = {4: 0}
    return pl.pallas_call(
        _final_kernel,
        grid=(x1.shape[0] // t,),
        in_specs=in_specs,
        out_specs=pl.BlockSpec((t, D_MODEL), lambda i: (out_tile0 + i, 0)),
        out_shape=jax.ShapeDtypeStruct((n_out, D_MODEL), F32),
        input_output_aliases=aliases,
        compiler_params=pltpu.CompilerParams(dimension_semantics=("arbitrary",), vmem_limit_bytes=VMEM_LIMIT),
        name="final_norm",
    )(*args)


def _s5_tables(lam_re, lam_im, b_re, b_im, c_re, c_im, log_step):
    lr = jnp.minimum(lam_re.astype(F32), LAMBDA_RE_MAX)
    li = lam_im.astype(F32)
    step = jnp.exp(log_step.astype(F32))[:, None]
    mag = jnp.exp(lr * step)
    ar = mag * jnp.cos(li * step)
    ai = mag * jnp.sin(li * step)
    nr = ar - 1.0
    den = lr * lr + li * li
    kr = (nr * lr + ai * li) / den
    ki = (ai * lr - nr * li) / den
    bre = b_re.astype(F32)
    bim = b_im.astype(F32)
    bbar_r = kr[..., None] * bre - ki[..., None] * bim
    bbar_i = kr[..., None] * bim + ki[..., None] * bre
    cre = c_re.astype(F32)
    cim = c_im.astype(F32)

    def power(m):
        m = jnp.asarray(m, F32)[..., None, None]
        mg = jnp.exp(m * lr * step)
        return mg * jnp.cos(m * li * step), mg * jnp.sin(m * li * step)

    pw_r, pw_i = power(jnp.arange(CHUNK + 1))
    cp_r = cre[None] * pw_r[:, :, None, :] - cim[None] * pw_i[:, :, None, :]
    cp_i = cre[None] * pw_i[:, :, None, :] + cim[None] * pw_r[:, :, None, :]
    width = CHUNK * SSM_GROUP
    kcat = (jnp.einsum('tgap,gph->ghta', cp_r[:CHUNK], bbar_r)
            - jnp.einsum('tgap,gph->ghta', cp_i[:CHUNK], bbar_i)).reshape(SSM_GROUPS, SSM_GROUP, width).astype(BF16)
    mo_r = cp_r[1:].transpose(1, 3, 0, 2).reshape(SSM_GROUPS, SSM_STATE, width)
    mo_i = -cp_i[1:].transpose(1, 3, 0, 2).reshape(SSM_GROUPS, SSM_STATE, width)
    mout = jnp.concatenate([mo_r, mo_i], axis=1).astype(BF16)
    q_r = pw_r[CHUNK - 1::-1][:CHUNK][:, :, :, None]
    q_i = pw_i[CHUNK - 1::-1][:CHUNK][:, :, :, None]
    mn_r = q_r * bbar_r[None] - q_i * bbar_i[None]
    mn_i = q_r * bbar_i[None] + q_i * bbar_r[None]
    minw = jnp.concatenate([mn_r, mn_i], axis=2).transpose(1, 0, 3, 2).reshape(
        SSM_GROUPS, CHUNK * SSM_GROUP, 2 * SSM_STATE).astype(BF16)

    def packed(j):
        pr, pi = power(float(CHUNK * j))
        return (jnp.concatenate([pr, pr], axis=-1).reshape(1, SSM_GROUPS * LANES),
                jnp.concatenate([-pi, pi], axis=-1).reshape(1, SSM_GROUPS * LANES))

    rows = jnp.arange(SUBLANES)[:, None]
    kinds = []
    for d in (1, 2, 4):
        c1, c2 = packed(d)
        mask = (rows >= d).astype(F32)
        kinds += [mask * c1, mask * c2]
    pk = [packed(r + 1) for r in range(SUBLANES)]
    kinds += [jnp.concatenate([p[0] for p in pk], axis=0), jnp.concatenate([p[1] for p in pk], axis=0)]
    coef = jnp.stack(kinds, axis=0)
    return minw, kcat, mout, coef


def _mixer_params(norm_mix, w_in, ssm_lam_re, ssm_lam_im, ssm_b_re, ssm_b_im, ssm_c_re, ssm_c_im, ssm_d,
                  ssm_log_step, ssm_w_glu, ssm_b_glu, pool_w, pool_scale, conv_w, branch_norm, w_out, norm_ffn,
                  router_w, router_b):
    depth = w_in.shape[0]
    minw, kcat, mout, coef = jax.vmap(_s5_tables)(ssm_lam_re, ssm_lam_im, ssm_b_re, ssm_b_im, ssm_c_re, ssm_c_im,
                                                  ssm_log_step)
    eye4 = jnp.eye(len(POOL_WINDOWS), dtype=F32)
    poolw = jnp.einsum('lgcd,gk->lgckd', pool_w.astype(F32), eye4).reshape(depth, POOL_WIDTH, POOL_WIDTH).astype(BF16)
    rw = jnp.pad(router_w.astype(F32), ((0, 0), (0, 0), (0, LANES - N_EXPERTS))).astype(BF16)
    rb = jnp.pad(router_b.astype(F32), ((0, 0), (0, LANES - N_EXPERTS))).reshape(depth, 1, LANES)
    row = lambda v: v.reshape(depth, 1, -1).astype(F32)
    utri = jnp.triu(jnp.ones((TILE_T, TILE_T), F32), 1).astype(BF16)
    return dict(gmix=row(norm_mix), win=w_in.astype(BF16), minw=minw, kcat=kcat, mout=mout, coef=coef, dskip=row(ssm_d),
                wglu=ssm_w_glu.astype(BF16), bglu=row(ssm_b_glu), poolw=poolw, pscale=row(pool_scale),
                convw=conv_w.astype(F32), gbr=row(branch_norm), wout=w_out.astype(BF16),
                gffn=row(norm_ffn), rw=rw, rb=rb, utri=utri[None])


def kernel(x, norm_mix, w_in, ssm_lam_re, ssm_lam_im, ssm_b_re, ssm_b_im, ssm_c_re, ssm_c_im, ssm_d, ssm_log_step, ssm_w_glu, ssm_b_glu, pool_w, pool_scale, conv_w, branch_norm, w_out, norm_ffn, router_w, router_b, w_gate_up, b_gate_up, w_down, b_down, final_norm):
    batch, seq, d_model = x.shape
    assert d_model == D_MODEL and seq % TILE_T == 0
    depth = w_in.shape[0]
    n_streams = 2 if batch % 2 == 0 else 1
    sb = batch // n_streams
    n_tok = sb * seq
    assert n_tok % ROUTE_T == 0 and n_tok % SC_WINDOW == 0
    n_steps = -(-(n_tok * TOP_K + N_EXPERTS * (STEP_ROWS - 1)) // STEP_ROWS)
    n_rows = n_steps * STEP_ROWS
    tiles = n_tok // TILE_T
    b_gu = b_gate_up.reshape(depth, N_EXPERTS, 1, 2 * D_EXPERT)
    b_dn = b_down.reshape(depth, N_EXPERTS, 1, D_MODEL)

    xin = [x.reshape(batch * seq, D_MODEL)] * n_streams
    tile0 = [s * tiles for s in range(n_streams)]
    ysg = [None] * n_streams
    slab = [None] * n_streams
    lp = _mixer_params(norm_mix, w_in, ssm_lam_re, ssm_lam_im, ssm_b_re, ssm_b_im, ssm_c_re, ssm_c_im, ssm_d,
                       ssm_log_step, ssm_w_glu, ssm_b_glu, pool_w, pool_scale, conv_w, branch_norm, w_out,
                       norm_ffn, router_w, router_b)
    for l in range(depth):
        routed = []
        for s in range(n_streams):
            xin[s], hp, slab[s], slabt, counts = _mixer_call(xin[s], tile0[s], ysg[s], slab[s], lp, l, sb, seq)
            tile0[s] = 0
            dest, meta = _route_call(slabt, counts, n_tok, n_steps)
            idx = dest.reshape(1, 2 * TOP_K * n_tok)
            xs = _dispatch(hp.reshape(2 * n_tok, SC_ROW), idx, 2 * n_rows).reshape(2, n_rows, SC_ROW)
            routed.append((xs, idx, meta[:n_steps, 0], meta[:n_steps, 1], meta[n_steps:n_steps + 1, 0]))
        for s, (xs, idx, step_e, step_nblk, n_valid) in enumerate(routed):
            ys = _expert_call(xs, step_e, step_nblk, n_valid, w_gate_up, b_gu, w_down, b_dn, l, n_steps)
            ysg[s] = _combine(ys.reshape(2 * n_rows, SC_ROW), idx).reshape(TOP_K, 2, n_tok, SC_ROW)
    g = final_norm.reshape(1, D_MODEL).astype(F32)
    out = None
    for s in range(n_streams):
        out = _final_call(xin[s], ysg[s], slab[s], g, s * n_tok, batch * seq, out)
    return out.reshape(batch, seq, D_MODEL)
```

```python
import functools

import jax
import jax.numpy as jnp
from jax import lax
from jax.experimental import pallas as pl
from jax.experimental.pallas import tpu as pltpu
from jax.experimental.pallas import tpu_sc as plsc

F32 = jnp.float32
BF16 = jnp.bfloat16
I32 = jnp.int32

D_MODEL = 1024
SSM_WIDTH = 512
POOL_WIDTH = 256
CONV_WIDTH = 256
SSM_GROUP = 16
SSM_GROUPS = 32
SSM_STATE = 64
LAMBDA_RE_MAX = -1e-4
POOL_WINDOWS = (2, 4, 8, 16)
POOL_GROUP = 64
N_EXPERTS = 32
TOP_K = 4
D_EXPERT = 1024
SWIGLU_LIMIT = 7.0
SWIGLU_ALPHA = 1.702
EXPERT_BLOCK = 256
STEP_ROWS = 2048
RUN_BLOCKS = 4
NORM_EPS = 1e-5

LANES = 128
SUBLANES = 8
TILE_T = 512
FINAL_T = 1024
CHUNK = 16
N_CHUNK = TILE_T // CHUNK
POOL_TAIL = SUBLANES * len(POOL_WINDOWS)
assert POOL_WINDOWS == tuple(2 ** (k + 1) for k in range(len(POOL_WINDOWS))) and POOL_TAIL >= POOL_WINDOWS[-1]
PACKED = D_MODEL // 2
ROUTE_T = 2048
SC_WINDOW = 128
SC_ROW = PACKED // 2
HI_MASK = -65536
VMEM_LIMIT = 56 * 1024 * 1024


def _bf16_round(v):
    return v.astype(BF16).astype(F32)


def _pack_rows(v):
    lo = lax.shift_right_logical(lax.bitcast_convert_type(_bf16_round(v[:, :PACKED]), I32), 16)
    hi = lax.bitcast_convert_type(_bf16_round(v[:, PACKED:]), I32) & HI_MASK
    return hi | lo


def _unpack_lo(w):
    return lax.bitcast_convert_type(lax.shift_left(w, 16), F32)


def _unpack_hi(w):
    return lax.bitcast_convert_type(w & HI_MASK, F32)


def _rms(v, g):
    r = lax.rsqrt(jnp.mean(v * v, axis=-1, keepdims=True) + NORM_EPS)
    return (v * r) * g


def _moe_combine(x, y_ref, slab):
    parts = [x[:, i * SC_ROW:(i + 1) * SC_ROW] for i in range(4)]
    for k in range(TOP_K):
        g = slab[:, TOP_K + k:TOP_K + k + 1]
        for h in range(2):
            w = y_ref[k, h]
            parts[h] = parts[h] + g * _unpack_lo(w)
            parts[2 + h] = parts[2 + h] + g * _unpack_hi(w)
    return jnp.concatenate(parts, axis=1)


def _block_transpose8(vs, lane):
    vs = list(vs)
    for dist in (4, 2, 1):
        width = SSM_GROUP * dist
        low = (lane % (2 * width)) < width
        for j in range(8):
            if j & dist:
                continue
            a, b = vs[j], vs[j + dist]
            vs[j] = jnp.where(low, a, pltpu.roll(b, width, axis=1))
            vs[j + dist] = jnp.where(low, pltpu.roll(a, LANES - width, axis=1), b)
    return vs


def _cmul_packed(c1, c2, v):
    return c1 * v + c2 * pltpu.roll(v, SSM_STATE, axis=1)


def _chunk_scan(s, carry_in, cf):
    outs = []
    carry = jnp.broadcast_to(carry_in, (SUBLANES, LANES))
    for rg in range(N_CHUNK // SUBLANES):
        x = s[rg * SUBLANES:(rg + 1) * SUBLANES]
        for j, d in enumerate((1, 2, 4)):
            x = x + _cmul_packed(cf[2 * j], cf[2 * j + 1], pltpu.roll(x, d, axis=0))
        x = x + _cmul_packed(cf[6], cf[7], carry)
        outs.append(x)
        carry = jnp.broadcast_to(x[SUBLANES - 1:SUBLANES], (SUBLANES, LANES))
    return jnp.concatenate(outs, axis=0)


def _expand_toeplitz(kcat_ref, mint_s):
    lane = lax.broadcasted_iota(I32, (SSM_GROUP, LANES), 1)
    zero = jnp.zeros((SSM_GROUP, LANES), BF16)

    def one_group(g, _):
        lo_src = kcat_ref[g, :, 0:LANES]
        hi_src = kcat_ref[g, :, LANES:2 * LANES]
        for s in range(CHUNK):
            d = SSM_GROUP * s
            rows = slice(SSM_GROUP * s, SSM_GROUP * (s + 1))
            if d == 0:
                lo, hi = lo_src, hi_src
            elif d < LANES:
                wrapped = pltpu.roll(lo_src, d, axis=1)
                lo = jnp.where(lane < d, zero, wrapped)
                hi = jnp.where(lane < d, wrapped, pltpu.roll(hi_src, d, axis=1))
            elif d == LANES:
                lo, hi = zero, lo_src
            else:
                lo = zero
                hi = jnp.where(lane < d - LANES, zero, pltpu.roll(lo_src, d - LANES, axis=1))
            mint_s[g, rows, 0:LANES] = lo
            mint_s[g, rows, LANES:2 * LANES] = hi
        return 0

    lax.fori_loop(0, SSM_GROUPS, one_group, 0)


def _s5_mixer(u, first, minw_ref, mint_s, mout_ref, coef_ref, u_s, y_s, h_s, hooks):
    blocks = SSM_WIDTH // LANES
    top = SUBLANES - 1
    for b4 in range(blocks):
        u_s[b4] = u[:, b4 * LANES:(b4 + 1) * LANES]
    lane = lax.broadcasted_iota(I32, (N_CHUNK, LANES), 1)

    halves = [[None, None] for _ in range(SSM_GROUPS)]
    for b4 in range(blocks):
        for hh in range(2):
            xs = [u_s[b4, pl.ds(8 * hh + j, N_CHUNK, stride=CHUNK), :].astype(BF16) for j in range(8)]
            ws = _block_transpose8(xs, lane)
            for gl in range(8):
                halves[8 * b4 + gl][hh] = ws[gl]
    ug = [jnp.concatenate(h2, axis=1) for h2 in halves]
    hooks[0]()

    carry_in = jnp.where(first, 0.0, h_s[top + N_CHUNK:top + N_CHUNK + 1, :])
    h_s[top:top + 1, :] = carry_in
    for g in range(SSM_GROUPS):
        cols = slice(g * LANES, (g + 1) * LANES)
        s_g = jnp.dot(ug[g], minw_ref[g], preferred_element_type=F32)
        cf = [coef_ref[k, :, cols] for k in range(8)]
        h_s[SUBLANES:SUBLANES + N_CHUNK, cols] = _chunk_scan(s_g, carry_in[:, cols], cf)
        if g == SSM_GROUPS // 2:
            hooks[1]()
    hooks[2]()

    yg = []
    for g in range(SSM_GROUPS):
        hprev = h_s[top:top + N_CHUNK, g * LANES:(g + 1) * LANES].astype(BF16)
        y_g = (jnp.dot(ug[g], mint_s[g], preferred_element_type=F32)
               + jnp.dot(hprev, mout_ref[g], preferred_element_type=F32))
        yg.append(y_g.astype(BF16))
    hooks[3]()

    for b4 in range(blocks):
        for hh in range(2):
            ws = [yg[8 * b4 + gl][:, hh * LANES:(hh + 1) * LANES] for gl in range(8)]
            xs = _block_transpose8(ws, lane)
            for j in range(8):
                y_s[b4, pl.ds(8 * hh + j, N_CHUNK, stride=CHUNK), :] = xs[j].astype(F32)
    return jnp.concatenate([y_s[b4] for b4 in range(blocks)], axis=1)


def _mixer_kernel(*refs, first_layer, steps):
    if first_layer:
        x_ref = refs[0]
        refs = refs[1:]
    else:
        x_ref, yprev_ref, slabprev_ref = refs[:3]
        refs = refs[3:]
    (gmix_ref, win_ref, minw_ref, kcat_ref, mout_ref, coef_ref, dskip_ref, wglu_ref, bglu_ref, poolw_ref, pscale_ref,
     convw_ref, gbr_ref, wout_ref, gffn_ref, rw_ref, rb_ref, utri_ref,
     x1_ref, hp_ref, slab_ref, slabt_ref, cnt_ref,
     u_s, y_s, h_s, pbuf_ref, zbuf_ref, cntacc_ref, mint_s, mix_s, xres_s) = refs

    j = pl.program_id(0)
    l = lax.rem(j, steps)
    seq_start = l == 0

    def out_proj_quarter(q):
        def run():
            cols = slice(q * (D_MODEL // 4), (q + 1) * (D_MODEL // 4))
            x1_ref[:, cols] = xres_s[:, cols] + jnp.dot(mix_s[...], wout_ref[:, cols], preferred_element_type=F32)
        return run

    hooks = [out_proj_quarter(q) for q in range(4)]

    @pl.when(j == 0)
    def _():
        cntacc_ref[...] = jnp.zeros_like(cntacc_ref)
        _expand_toeplitz(kcat_ref, mint_s)

    if first_layer:
        x = x_ref[...]
    else:
        x = _moe_combine(x_ref[...], yprev_ref, slabprev_ref[...])

    hn = _rms(x, gmix_ref[...]).astype(BF16)
    proj = jnp.dot(hn, win_ref[...], preferred_element_type=F32)
    u = proj[:, :SSM_WIDTH]
    p = proj[:, SSM_WIDTH:SSM_WIDTH + POOL_WIDTH]
    c0 = SSM_WIDTH + POOL_WIDTH
    bg = proj[:, c0:c0 + CONV_WIDTH]
    cg = proj[:, c0 + CONV_WIDTH:c0 + 2 * CONV_WIDTH]
    hv = proj[:, c0 + 2 * CONV_WIDTH:]

    y = _s5_mixer(u, seq_start, minw_ref, mint_s, mout_ref, coef_ref, u_s, y_s, h_s, hooks) + dskip_ref[...] * u
    _mixer_second_half(x1_ref[...], j >= 1, gffn_ref, rw_ref, rb_ref, utri_ref,
                       hp_ref, slab_ref, slabt_ref, cnt_ref, cntacc_ref)
    z = jax.nn.gelu(y)
    glu = jnp.dot(z.astype(BF16), wglu_ref[...], preferred_element_type=F32) + bglu_ref[...]
    y_ssm = z * jax.nn.sigmoid(glu)

    tail = POOL_TAIL
    pbuf_ref[0, 0:tail, :] = jnp.where(seq_start, 0.0, pbuf_ref[0, TILE_T:TILE_T + tail, :])
    pbuf_ref[0, tail:, :] = p
    sums = []
    for stage, w in enumerate(POOL_WINDOWS):
        src = 0 if stage == 0 else 1 + (stage - 1) % 2
        dst = 1 + stage % 2
        d = w // 2
        lo = SUBLANES * (stage + 1)
        n = tail + TILE_T - lo
        e = pbuf_ref[src, pl.ds(lo, n), :] + pbuf_ref[src, pl.ds(lo - d, n), :]
        if stage + 1 < len(POOL_WINDOWS):
            pbuf_ref[dst, pl.ds(lo, n), :] = e
        sums.append(e[tail - lo:, :])
    lane_p = lax.broadcasted_iota(I32, (TILE_T, POOL_WIDTH), 1)
    total = sums[-1]
    win = jnp.full((TILE_T, POOL_WIDTH), POOL_WINDOWS[-1], I32)
    for i in range(len(POOL_WINDOWS) - 2, -1, -1):
        sel = lane_p < (i + 1) * POOL_GROUP
        total = jnp.where(sel, sums[i], total)
        win = jnp.where(sel, POOL_WINDOWS[i], win)
    row_p = lax.broadcasted_iota(I32, (TILE_T, POOL_WIDTH), 0) + l * TILE_T
    count = jnp.minimum(row_p + 1, win).astype(F32)
    pooled = total / count - p
    mixed = jnp.dot(pooled.astype(BF16), poolw_ref[...], preferred_element_type=F32)
    y_pool = mixed * pscale_ref[...]

    zc = cg * hv
    zbuf_ref[0:SUBLANES, :] = jnp.where(seq_start, 0.0, zbuf_ref[TILE_T:TILE_T + SUBLANES, :])
    zbuf_ref[SUBLANES:, :] = zc
    yc = (convw_ref[0:1, :] * zbuf_ref[pl.ds(SUBLANES - 2, TILE_T), :]
          + convw_ref[1:2, :] * zbuf_ref[pl.ds(SUBLANES - 1, TILE_T), :]
          + convw_ref[2:3, :] * zc)
    y_conv = bg * yc

    gbr = gbr_ref[...]
    s1 = SSM_WIDTH + POOL_WIDTH
    mixed_all = jnp.concatenate([_rms(y_ssm, gbr[:, :SSM_WIDTH]).astype(BF16),
                                 _rms(y_pool, gbr[:, SSM_WIDTH:s1]).astype(BF16),
                                 _rms(y_conv, gbr[:, s1:]).astype(BF16)], axis=1)
    mix_s[...] = mixed_all
    xres_s[...] = x


def _mixer_second_half(x1, live, gffn_ref, rw_ref, rb_ref, utri_ref, hp_ref, slab_ref, slabt_ref, cnt_ref, cntacc_ref):
    hn2 = _rms(x1, gffn_ref[...])
    packed = _pack_rows(hn2)
    hp_ref[0] = packed[:, :SC_ROW]
    hp_ref[1] = packed[:, SC_ROW:]
    logits = jnp.dot(hn2.astype(BF16), rw_ref[...], preferred_element_type=F32) + rb_ref[...]
    work = logits.T[:N_EXPERTS, :]
    eio = lax.broadcasted_iota(I32, (N_EXPERTS, TILE_T), 0).astype(F32)
    neg = jnp.float32(-jnp.inf)
    vals, idxs = [], []
    for _ in range(TOP_K):
        m = jnp.max(work, axis=0, keepdims=True)
        idx = jnp.min(jnp.where(work == m, eio, float(N_EXPERTS)), axis=0, keepdims=True)
        vals.append(m)
        idxs.append(idx)
        work = jnp.where(eio == idx, neg, work)
    exps = [jnp.exp(v - vals[0]) for v in vals]
    denom = exps[0] + exps[1] + exps[2] + exps[3]
    gates = [e / denom for e in exps]

    onehot = jnp.zeros((N_EXPERTS, TILE_T), F32)
    for idx in idxs:
        onehot = jnp.where(eio == idx, 1.0, onehot)
    before = jnp.dot(onehot.astype(BF16), utri_ref[...], preferred_element_type=F32) + cntacc_ref[:, 0:1]
    ranks = [jnp.sum(jnp.where(eio == idx, before, 0.0), axis=0, keepdims=True) for idx in idxs]
    row = lax.broadcasted_iota(I32, (LANES, TILE_T), 0)
    slab_t = jnp.zeros((LANES, TILE_T), F32)
    for k in range(TOP_K):
        slab_t = jnp.where(row == k, idxs[k], slab_t)
        slab_t = jnp.where(row == TOP_K + k, gates[k], slab_t)
        slab_t = jnp.where(row == 2 * TOP_K + k, ranks[k], slab_t)
    slabt_ref[...] = slab_t[:2 * SUBLANES, :]
    slab_ref[...] = slab_t.T
    newcnt = jnp.where(live, cntacc_ref[:, 0:1] + jnp.sum(onehot, axis=1, keepdims=True), 0.0)
    cntacc_ref[...] = jnp.broadcast_to(newcnt, cntacc_ref.shape)
    cnt_ref[...] = jnp.broadcast_to(newcnt, cnt_ref.shape)


def _layer_spec(shape, layer):
    nd = len(shape) - 1
    sel = layer if shape[0] > 1 else 0
    return pl.BlockSpec((None,) + tuple(shape[1:]), lambda j, nd=nd, sel=sel: (sel,) + (0,) * nd,
                        pipeline_mode=pl.Buffered(1))


def _mixer_call(x, x_tile0, yprev, slabprev, lp, layer, batch, seq):
    first_layer = yprev is None
    n_tok = batch * seq
    steps = seq // TILE_T
    tiles = batch * steps
    cur = lambda j: jnp.minimum(j, tiles - 1)
    prv = lambda j: jnp.maximum(j - 1, 0)
    tok_spec = lambda w: pl.BlockSpec((TILE_T, w), lambda j: (prv(j), 0))
    in_specs = [pl.BlockSpec((TILE_T, D_MODEL), lambda j: (x_tile0 + cur(j), 0))]
    args = [x]
    if not first_layer:
        in_specs += [pl.BlockSpec((TOP_K, 2, TILE_T, SC_ROW), lambda j: (0, 0, cur(j), 0)),
                     pl.BlockSpec((TILE_T, LANES), lambda j: (cur(j), 0))]
        args += [yprev, slabprev]
    weights = [lp['gmix'], lp['win'], lp['minw'], lp['kcat'], lp['mout'], lp['coef'], lp['dskip'], lp['wglu'], lp['bglu'],
               lp['poolw'], lp['pscale'], lp['convw'], lp['gbr'], lp['wout'], lp['gffn'], lp['rw'], lp['rb'],
               lp['utri']]
    in_specs += [_layer_spec(w.shape, layer) for w in weights]
    args += weights
    out_shape = [jax.ShapeDtypeStruct((n_tok, D_MODEL), F32),
                 jax.ShapeDtypeStruct((2, n_tok, SC_ROW), I32),
                 jax.ShapeDtypeStruct((n_tok, LANES), F32),
                 jax.ShapeDtypeStruct((2 * SUBLANES, n_tok), F32),
                 jax.ShapeDtypeStruct((N_EXPERTS, LANES), F32)]
    out_specs = [tok_spec(D_MODEL), pl.BlockSpec((2, TILE_T, SC_ROW), lambda j: (0, prv(j), 0)), tok_spec(LANES),
                 pl.BlockSpec((2 * SUBLANES, TILE_T), lambda j: (0, prv(j))),
                 pl.BlockSpec((N_EXPERTS, LANES), lambda j: (0, 0))]
    scratch = [pltpu.VMEM((SSM_WIDTH // LANES, TILE_T, LANES), F32),
               pltpu.VMEM((SSM_WIDTH // LANES, TILE_T, LANES), F32),
               pltpu.VMEM((SUBLANES + N_CHUNK, SSM_GROUPS * LANES), F32),
               pltpu.VMEM((3, POOL_TAIL + TILE_T, POOL_WIDTH), F32),
               pltpu.VMEM((SUBLANES + TILE_T, CONV_WIDTH), F32),
               pltpu.VMEM((N_EXPERTS, LANES), F32),
               pltpu.VMEM((SSM_GROUPS, CHUNK * SSM_GROUP, CHUNK * SSM_GROUP), BF16),
               pltpu.VMEM((TILE_T, D_MODEL), BF16),
               pltpu.VMEM((TILE_T, D_MODEL), F32)]
    return pl.pallas_call(
        functools.partial(_mixer_kernel, first_layer=first_layer, steps=steps),
        grid=(tiles + 1,),
        in_specs=in_specs, out_specs=out_specs, out_shape=out_shape, scratch_shapes=scratch,
        compiler_params=pltpu.CompilerParams(dimension_semantics=("arbitrary",),
                                             vmem_limit_bytes=VMEM_LIMIT),
        name="mixer_first" if first_layer else "mixer_next",
    )(*args)


def _route_kernel(slabt_ref, cnt_ref, dest_ref, meta_ref, *, n_steps):
    lane1 = lax.broadcasted_iota(I32, (1, LANES), 1)
    e_row = lax.broadcasted_iota(I32, (N_EXPERTS, LANES), 0)
    e_lane = lax.broadcasted_iota(I32, (N_EXPERTS, LANES), 1)
    cnt = cnt_ref[...].astype(I32)
    counts = jnp.sum(jnp.where(e_row == e_lane, cnt, 0), axis=0, keepdims=True)
    padded = ((counts + (STEP_ROWS - 1)) // STEP_ROWS) * STEP_ROWS
    pend = padded
    sh = 1
    while sh < N_EXPERTS:
        pend = pend + jnp.where(lane1 >= sh, pltpu.roll(pend, sh, axis=1), 0)
        sh *= 2

    @pl.when(pl.program_id(0) == 0)
    def _():
        rows = meta_ref.shape[0]
        bstart = lax.broadcasted_iota(I32, (rows, LANES), 0) * STEP_ROWS
        lane = lax.broadcasted_iota(I32, (rows, LANES), 1)
        done = jnp.where(jnp.logical_and(lane < N_EXPERTS, pend <= bstart), 1, 0)
        be = jnp.minimum(jnp.sum(done, axis=1, keepdims=True), N_EXPERTS - 1)
        mine = lane == be
        first_row = jnp.sum(jnp.where(mine, pend - counts, 0), axis=1, keepdims=True)
        used = bstart[:, 0:1] + STEP_ROWS - jnp.maximum(bstart[:, 0:1], first_row)
        nblk = jnp.clip((used + (EXPERT_BLOCK - 1)) // EXPERT_BLOCK, 0, STEP_ROWS // EXPERT_BLOCK)
        total = jnp.sum(jnp.where(lane1 == N_EXPERTS - 1, pend, 0), axis=1, keepdims=True)
        row = lax.broadcasted_iota(I32, (rows, LANES), 0)
        table = jnp.where(lane == 0, be, jnp.where(lane == 1, nblk, 0))
        meta_ref[...] = jnp.where(row == n_steps, total // STEP_ROWS, table)

    last_col = jnp.sum(jnp.where(e_row == e_lane, pend - 1, 0), axis=1, keepdims=True)
    tokens = slabt_ref.shape[1]
    eio = lax.broadcasted_iota(I32, (N_EXPERTS, tokens), 0)
    row = lax.broadcasted_iota(I32, (SUBLANES, tokens), 0)
    out = jnp.zeros((SUBLANES, tokens), I32)
    for k in range(TOP_K):
        idx = slabt_ref[k:k + 1, :].astype(I32)
        rank = slabt_ref[2 * TOP_K + k:2 * TOP_K + k + 1, :].astype(I32)
        base = jnp.sum(jnp.where(eio == idx, last_col, 0), axis=0, keepdims=True)
        for h in range(2):
            out = jnp.where(row == 2 * k + h, base - rank + h * n_steps * STEP_ROWS, out)
    dest_ref[...] = out


def _route_call(slabt, counts, n_tok, n_steps):
    meta_rows = ((n_steps + 1 + SUBLANES - 1) // SUBLANES) * SUBLANES
    return pl.pallas_call(
        functools.partial(_route_kernel, n_steps=n_steps),
        grid=(n_tok // ROUTE_T,),
        in_specs=[pl.BlockSpec((2 * SUBLANES, ROUTE_T), lambda i: (0, i)),
                  pl.BlockSpec((N_EXPERTS, LANES), lambda i: (0, 0))],
        out_specs=[pl.BlockSpec((SUBLANES, ROUTE_T), lambda i: (0, i)),
                   pl.BlockSpec((meta_rows, LANES), lambda i: (0, 0))],
        out_shape=[jax.ShapeDtypeStruct((SUBLANES, n_tok), I32),
                   jax.ShapeDtypeStruct((meta_rows, LANES), I32)],
        compiler_params=pltpu.CompilerParams(dimension_semantics=("arbitrary",)),
        name="route",
    )(slabt, counts)


def _sc_mesh():
    return plsc.VectorSubcoreMesh(core_axis_name="core", subcore_axis_name="subcore")


def _dispatch(src, idx, n_rows):
    windows = src.shape[0] // SC_WINDOW

    @functools.partial(pl.kernel, out_type=jax.ShapeDtypeStruct((n_rows, SC_ROW), src.dtype),
                       mesh=_sc_mesh(), scratch_types=[])
    def scatter_rows(x_hbm, i_hbm, o_hbm):
        def body(x_vmem, i_vmem):
            pltpu.sync_copy(x_vmem, o_hbm.at[i_vmem.at[0]])

        pltpu.emit_pipeline(
            body,
            grid=(idx.shape[1] // SC_WINDOW,),
            in_specs=[pl.BlockSpec((SC_WINDOW, SC_ROW), lambda i: (lax.rem(i, windows), 0)),
                      pl.BlockSpec((1, SC_WINDOW), lambda i: (0, i))],
            out_specs=[],
            core_axis_name=("core", "subcore"),
            dimension_semantics=(pltpu.PARALLEL,),
        )(x_hbm, i_hbm)

    return scatter_rows(src, idx)


def _combine(src, idx):
    n_idx = idx.shape[1]

    @functools.partial(pl.kernel, out_type=jax.ShapeDtypeStruct((n_idx, SC_ROW), src.dtype),
                       mesh=_sc_mesh(), scratch_types=[])
    def gather_rows(y_hbm, i_hbm, o_hbm):
        def body(i_vmem, o_vmem):
            pltpu.sync_copy(y_hbm.at[i_vmem.at[0]], o_vmem)

        pltpu.emit_pipeline(
            body,
            grid=(n_idx // SC_WINDOW,),
            in_specs=[pl.BlockSpec((1, SC_WINDOW), lambda i: (0, i))],
            out_specs=[pl.BlockSpec((SC_WINDOW, SC_ROW), lambda i: (i, 0))],
            core_axis_name=("core", "subcore"),
            dimension_semantics=(pltpu.PARALLEL,),
        )(i_hbm, o_hbm)

    return gather_rows(src, idx)


def _expert_rows(xs_ref, ys_ref, row0, n, wgu_s, bgu_ref, wdn_s, bdn_ref, act_s):
    rows = pl.ds(pl.multiple_of(row0, EXPERT_BLOCK), n)
    arow = slice(0, n)
    w0 = xs_ref[0, rows, :]
    w1 = xs_ref[1, rows, :]
    xb = jnp.concatenate([_unpack_lo(w0).astype(BF16), _unpack_lo(w1).astype(BF16),
                          _unpack_hi(w0).astype(BF16), _unpack_hi(w1).astype(BF16)], axis=1)
    half = D_EXPERT // 2
    for j in range(2):
        cols = slice(j * half, (j + 1) * half)
        ucols = slice(D_EXPERT + j * half, D_EXPERT + (j + 1) * half)
        g = jnp.dot(xb, wgu_s[:, cols], preferred_element_type=F32) + bgu_ref[:, cols]
        up = jnp.dot(xb, wgu_s[:, ucols], preferred_element_type=F32) + bgu_ref[:, ucols]
        g = jnp.minimum(g, SWIGLU_LIMIT)
        up = jnp.clip(up, -SWIGLU_LIMIT, SWIGLU_LIMIT)
        act_s[arow, cols] = (g * jax.nn.sigmoid(SWIGLU_ALPHA * g) * (up + 1.0)).astype(BF16)
    y = jnp.dot(act_s[arow, :], wdn_s[...], preferred_element_type=F32) + bdn_ref[...]
    packed = _pack_rows(y)
    ys_ref[0, rows, :] = packed[:, :SC_ROW]
    ys_ref[1, rows, :] = packed[:, SC_ROW:]


def _expert_kernel(be_ref, nb_ref, nv_ref, xs_ref, wgu_ref, bgu_ref, wdn_ref, bdn_ref, ys_ref, wgu_s, wdn_s, act_s):
    blk = pl.program_id(0)
    prev = be_ref[jnp.maximum(blk - 1, 0)]
    changed = jnp.logical_or(blk == 0, be_ref[blk] != prev)
    valid = blk < nv_ref[0]

    @pl.when(jnp.logical_and(valid, changed))
    def _():
        chunk = 128

        def cast_gu(i, _):
            rows = pl.ds(pl.multiple_of(i * chunk, chunk), chunk)
            wgu_s[rows, :] = wgu_ref[rows, :].astype(BF16)
            return 0

        def cast_dn(i, _):
            rows = pl.ds(pl.multiple_of(i * chunk, chunk), chunk)
            wdn_s[rows, :] = wdn_ref[rows, :].astype(BF16)
            return 0

        lax.fori_loop(0, D_MODEL // chunk, cast_gu, 0)
        lax.fori_loop(0, D_EXPERT // chunk, cast_dn, 0)

    nblk = nb_ref[blk]
    first = STEP_ROWS - nblk * EXPERT_BLOCK
    big = nblk // RUN_BLOCKS

    @pl.when(jnp.logical_and(valid, big > 0))
    def _():
        def one_run(i, _):
            row0 = first + (nblk - (i + 1) * RUN_BLOCKS) * EXPERT_BLOCK
            _expert_rows(xs_ref, ys_ref, row0, RUN_BLOCKS * EXPERT_BLOCK, wgu_s, bgu_ref, wdn_s, bdn_ref, act_s)
            return 0

        lax.fori_loop(0, big, one_run, 0)

    bit = RUN_BLOCKS // 2
    while bit:
        @pl.when(jnp.logical_and(valid, (nblk & bit) != 0))
        def _(bit=bit):
            row0 = first + (nblk & (bit - 1)) * EXPERT_BLOCK
            _expert_rows(xs_ref, ys_ref, row0, bit * EXPERT_BLOCK, wgu_s, bgu_ref, wdn_s, bdn_ref, act_s)
        bit //= 2


def _expert_call(xs, step_e, step_nblk, n_valid, w_gu, b_gu, w_dn, b_dn, layer, n_steps):
    def row_map(i, be, nb, nv):
        return (0, jnp.minimum(i, jnp.maximum(nv[0] - 1, 0)), 0)

    def w_map(i, be, nb, nv):
        return (layer, be[i], 0, 0)

    grid_spec = pltpu.PrefetchScalarGridSpec(
        num_scalar_prefetch=3,
        grid=(n_steps,),
        in_specs=[pl.BlockSpec((2, STEP_ROWS, SC_ROW), row_map),
                  pl.BlockSpec((None, None, D_MODEL, 2 * D_EXPERT), w_map),
                  pl.BlockSpec((None, None, 1, 2 * D_EXPERT), w_map),
                  pl.BlockSpec((None, None, D_EXPERT, D_MODEL), w_map),
                  pl.BlockSpec((None, None, 1, D_MODEL), w_map)],
        out_specs=pl.BlockSpec((2, STEP_ROWS, SC_ROW), row_map),
        scratch_shapes=[pltpu.VMEM((D_MODEL, 2 * D_EXPERT), BF16),
                        pltpu.VMEM((D_EXPERT, D_MODEL), BF16),
                        pltpu.VMEM((RUN_BLOCKS * EXPERT_BLOCK, D_EXPERT), BF16)])
    return pl.pallas_call(
        _expert_kernel,
        grid_spec=grid_spec,
        out_shape=jax.ShapeDtypeStruct((2, n_steps * STEP_ROWS, SC_ROW), I32),
        compiler_params=pltpu.CompilerParams(dimension_semantics=("arbitrary",),
                                             vmem_limit_bytes=VMEM_LIMIT),
        name="experts",
    )(step_e, step_nblk, n_valid, xs, w_gu, b_gu, w_dn, b_dn)


def _final_kernel(x_ref, y_ref, slab_ref, g_ref, *rest):
    o_ref = rest[-1]
    x = _moe_combine(x_ref[...], y_ref, slab_ref[...])
    o_ref[...] = _rms(x, g_ref[...])


def _final_call(x1, ysg, slab, g, out_row0, n_out, out_prev):
    t = FINAL_T
    out_tile0 = out_row0 // t
    in_specs = [pl.BlockSpec((t, D_MODEL), lambda i: (i, 0)),
                pl.BlockSpec((TOP_K, 2, t, SC_ROW), lambda i: (0, 0, i, 0)),
                pl.BlockSpec((t, LANES), lambda i: (i, 0)),
                pl.BlockSpec((1, D_MODEL), lambda i: (0, 0))]
    args = [x1, ysg, slab, g]
    aliases = {}
    if out_prev is not None:
        in_specs.append(pl.BlockSpec(memory_space=pl.ANY))
        args.append(out_prev)
        aliases = {4: 0}
    return pl.pallas_call(
        _final_kernel,
        grid=(x1.shape[0] // t,),
        in_specs=in_specs,
        out_specs=pl.BlockSpec((t, D_MODEL), lambda i: (out_tile0 + i, 0)),
        out_shape=jax.ShapeDtypeStruct((n_out, D_MODEL), F32),
        input_output_aliases=aliases,
        compiler_params=pltpu.CompilerParams(dimension_semantics=("arbitrary",), vmem_limit_bytes=VMEM_LIMIT),
        name="final_norm",
    )(*args)


def _s5_tables(lam_re, lam_im, b_re, b_im, c_re, c_im, log_step):
    lr = jnp.minimum(lam_re.astype(F32), LAMBDA_RE_MAX)
    li = lam_im.astype(F32)
    step = jnp.exp(log_step.astype(F32))[:, None]
    mag = jnp.exp(lr * step)
    ar = mag * jnp.cos(li * step)
    ai = mag * jnp.sin(li * step)
    nr = ar - 1.0
    den = lr * lr + li * li
    kr = (nr * lr + ai * li) / den
    ki = (ai * lr - nr * li) / den
    bre = b_re.astype(F32)
    bim = b_im.astype(F32)
    bbar_r = kr[..., None] * bre - ki[..., None] * bim
    bbar_i = kr[..., None] * bim + ki[..., None] * bre
    cre = c_re.astype(F32)
    cim = c_im.astype(F32)

    def power(m):
        m = jnp.asarray(m, F32)[..., None, None]
        mg = jnp.exp(m * lr * step)
        return mg * jnp.cos(m * li * step), mg * jnp.sin(m * li * step)

    pw_r, pw_i = power(jnp.arange(CHUNK + 1))
    cp_r = cre[None] * pw_r[:, :, None, :] - cim[None] * pw_i[:, :, None, :]
    cp_i = cre[None] * pw_i[:, :, None, :] + cim[None] * pw_r[:, :, None, :]
    width = CHUNK * SSM_GROUP
    kcat = (jnp.einsum('tgap,gph->ghta', cp_r[:CHUNK], bbar_r)
            - jnp.einsum('tgap,gph->ghta', cp_i[:CHUNK], bbar_i)).reshape(SSM_GROUPS, SSM_GROUP, width).astype(BF16)
    mo_r = cp_r[1:].transpose(1, 3, 0, 2).reshape(SSM_GROUPS, SSM_STATE, width)
    mo_i = -cp_i[1:].transpose(1, 3, 0, 2).reshape(SSM_GROUPS, SSM_STATE, width)
    mout = jnp.concatenate([mo_r, mo_i], axis=1).astype(BF16)
    q_r = pw_r[CHUNK - 1::-1][:CHUNK][:, :, :, None]
    q_i = pw_i[CHUNK - 1::-1][:CHUNK][:, :, :, None]
    mn_r = q_r * bbar_r[None] - q_i * bbar_i[None]
    mn_i = q_r * bbar_i[None] + q_i * bbar_r[None]
    minw = jnp.concatenate([mn_r, mn_i], axis=2).transpose(1, 0, 3, 2).reshape(
        SSM_GROUPS, CHUNK * SSM_GROUP, 2 * SSM_STATE).astype(BF16)

    def packed(j):
        pr, pi = power(float(CHUNK * j))
        return (jnp.concatenate([pr, pr], axis=-1).reshape(1, SSM_GROUPS * LANES),
                jnp.concatenate([-pi, pi], axis=-1).reshape(1, SSM_GROUPS * LANES))

    rows = jnp.arange(SUBLANES)[:, None]
    kinds = []
    for d in (1, 2, 4):
        c1, c2 = packed(d)
        mask = (rows >= d).astype(F32)
        kinds += [mask * c1, mask * c2]
    pk = [packed(r + 1) for r in range(SUBLANES)]
    kinds += [jnp.concatenate([p[0] for p in pk], axis=0), jnp.concatenate([p[1] for p in pk], axis=0)]
    coef = jnp.stack(kinds, axis=0)
    return minw, kcat, mout, coef


def _mixer_params(norm_mix, w_in, ssm_lam_re, ssm_lam_im, ssm_b_re, ssm_b_im, ssm_c_re, ssm_c_im, ssm_d,
                  ssm_log_step, ssm_w_glu, ssm_b_glu, pool_w, pool_scale, conv_w, branch_norm, w_out, norm_ffn,
                  router_w, router_b):
    depth = w_in.shape[0]
    minw, kcat, mout, coef = jax.vmap(_s5_tables)(ssm_lam_re, ssm_lam_im, ssm_b_re, ssm_b_im, ssm_c_re, ssm_c_im,
                                                  ssm_log_step)
    eye4 = jnp.eye(len(POOL_WINDOWS), dtype=F32)
    poolw = jnp.einsum('lgcd,gk->lgckd', pool_w.astype(F32), eye4).reshape(depth, POOL_WIDTH, POOL_WIDTH).astype(BF16)
    rw = jnp.pad(router_w.astype(F32), ((0, 0), (0, 0), (0, LANES - N_EXPERTS))).astype(BF16)
    rb = jnp.pad(router_b.astype(F32), ((0, 0), (0, LANES - N_EXPERTS))).reshape(depth, 1, LANES)
    row = lambda v: v.reshape(depth, 1, -1).astype(F32)
    utri = jnp.triu(jnp.ones((TILE_T, TILE_T), F32), 1).astype(BF16)
    return dict(gmix=row(norm_mix), win=w_in.astype(BF16), minw=minw, kcat=kcat, mout=mout, coef=coef, dskip=row(ssm_d),
                wglu=ssm_w_glu.astype(BF16), bglu=row(ssm_b_glu), poolw=poolw, pscale=row(pool_scale),
                convw=conv_w.astype(F32), gbr=row(branch_norm), wout=w_out.astype(BF16),
                gffn=row(norm_ffn), rw=rw, rb=rb, utri=utri[None])


def kernel(x, norm_mix, w_in, ssm_lam_re, ssm_lam_im, ssm_b_re, ssm_b_im, ssm_c_re, ssm_c_im, ssm_d, ssm_log_step, ssm_w_glu, ssm_b_glu, pool_w, pool_scale, conv_w, branch_norm, w_out, norm_ffn, router_w, router_b, w_gate_up, b_gate_up, w_down, b_down, final_norm):
    batch, seq, d_model = x.shape
    assert d_model == D_MODEL and seq % TILE_T == 0
    depth = w_in.shape[0]
    n_streams = 2 if batch % 2 == 0 else 1
    sb = batch // n_streams
    n_tok = sb * seq
    assert n_tok % ROUTE_T == 0 and n_tok % SC_WINDOW == 0
    n_steps = -(-(n_tok * TOP_K + N_EXPERTS * (STEP_ROWS - 1)) // STEP_ROWS)
    n_rows = n_steps * STEP_ROWS
    tiles = n_tok // TILE_T
    b_gu = b_gate_up.reshape(depth, N_EXPERTS, 1, 2 * D_EXPERT)
    b_dn = b_down.reshape(depth, N_EXPERTS, 1, D_MODEL)

    xin = [x.reshape(batch * seq, D_MODEL)] * n_streams
    tile0 = [s * tiles for s in range(n_streams)]
    ysg = [None] * n_streams
    slab = [None] * n_streams
    lp = _mixer_params(norm_mix, w_in, ssm_lam_re, ssm_lam_im, ssm_b_re, ssm_b_im, ssm_c_re, ssm_c_im, ssm_d,
                       ssm_log_step, ssm_w_glu, ssm_b_glu, pool_w, pool_scale, conv_w, branch_norm, w_out,
                       norm_ffn, router_w, router_b)
    for l in range(depth):
        routed = []
        for s in range(n_streams):
            xin[s], hp, slab[s], slabt, counts = _mixer_call(xin[s], tile0[s], ysg[s], slab[s], lp, l, sb, seq)
            tile0[s] = 0
            dest, meta = _route_call(slabt, counts, n_tok, n_steps)
            idx = dest.reshape(1, 2 * TOP_K * n_tok)
            xs = _dispatch(hp.reshape(2 * n_tok, SC_ROW), idx, 2 * n_rows).reshape(2, n_rows, SC_ROW)
            routed.append((xs, idx, meta[:n_steps, 0], meta[:n_steps, 1], meta[n_steps:n_steps + 1, 0]))
        for s, (xs, idx, step_e, step_nblk, n_valid) in enumerate(routed):
            ys = _expert_call(xs, step_e, step_nblk, n_valid, w_gate_up, b_gu, w_down, b_dn, l, n_steps)
            ysg[s] = _combine(ys.reshape(2 * n_rows, SC_ROW), idx).reshape(TOP_K, 2, n_tok, SC_ROW)
    g = final_norm.reshape(1, D_MODEL).astype(F32)
    out = None
    for s in range(n_streams):
        out = _final_call(xin[s], ysg[s], slab[s], g, s * n_tok, batch * seq, out)
    return out.reshape(batch, seq, D_MODEL)
```

```python
import functools

import jax
import jax.numpy as jnp
from jax import lax
from jax.experimental import pallas as pl
from jax.experimental.pallas import tpu as pltpu
from jax.experimental.pallas import tpu_sc as plsc

F32 = jnp.float32
BF16 = jnp.bfloat16
I32 = jnp.int32

D_MODEL = 1024
SSM_WIDTH = 512
POOL_WIDTH = 256
CONV_WIDTH = 256
SSM_GROUP = 16
SSM_GROUPS = 32
SSM_STATE = 64
LAMBDA_RE_MAX = -1e-4
POOL_WINDOWS = (2, 4, 8, 16)
POOL_GROUP = 64
N_EXPERTS = 32
TOP_K = 4
D_EXPERT = 1024
SWIGLU_LIMIT = 7.0
SWIGLU_ALPHA = 1.702
EXPERT_BLOCK = 256
STEP_ROWS = 2048
RUN_BLOCKS = 4
NORM_EPS = 1e-5

LANES = 128
SUBLANES = 8
TILE_T = 512
FINAL_T = 1024
CHUNK = 16
N_CHUNK = TILE_T // CHUNK
POOL_TAIL = SUBLANES * len(POOL_WINDOWS)
assert POOL_WINDOWS == tuple(2 ** (k + 1) for k in range(len(POOL_WINDOWS))) and POOL_TAIL >= POOL_WINDOWS[-1]
PACKED = D_MODEL // 2
ROUTE_T = 2048
SC_WINDOW = 128
SC_ROW = PACKED // 2
HI_MASK = -65536
VMEM_LIMIT = 56 * 1024 * 1024


def _bf16_round(v):
    return v.astype(BF16).astype(F32)


def _pack_rows(v):
    lo = lax.shift_right_logical(lax.bitcast_convert_type(_bf16_round(v[:, :PACKED]), I32), 16)
    hi = lax.bitcast_convert_type(_bf16_round(v[:, PACKED:]), I32) & HI_MASK
    return hi | lo


def _unpack_lo(w):
    return lax.bitcast_convert_type(lax.shift_left(w, 16), F32)


def _unpack_hi(w):
    return lax.bitcast_convert_type(w & HI_MASK, F32)


def _rms(v, g):
    r = lax.rsqrt(jnp.mean(v * v, axis=-1, keepdims=True) + NORM_EPS)
    return (v * r) * g


def _moe_combine(x, y_ref, slab):
    parts = [x[:, i * SC_ROW:(i + 1) * SC_ROW] for i in range(4)]
    for k in range(TOP_K):
        g = slab[:, TOP_K + k:TOP_K + k + 1]
        for h in range(2):
            w = y_ref[k, h]
            parts[h] = parts[h] + g * _unpack_lo(w)
            parts[2 + h] = parts[2 + h] + g * _unpack_hi(w)
    return jnp.concatenate(parts, axis=1)


def _block_transpose8(vs, lane):
    vs = list(vs)
    for dist in (4, 2, 1):
        width = SSM_GROUP * dist
        low = (lane % (2 * width)) < width
        for j in range(8):
            if j & dist:
                continue
            a, b = vs[j], vs[j + dist]
            vs[j] = jnp.where(low, a, pltpu.roll(b, width, axis=1))
            vs[j + dist] = jnp.where(low, pltpu.roll(a, LANES - width, axis=1), b)
    return vs


def _cmul_packed(c1, c2, v):
    return c1 * v + c2 * pltpu.roll(v, SSM_STATE, axis=1)


def _chunk_scan(s, carry_in, cf):
    outs = []
    carry = jnp.broadcast_to(carry_in, (SUBLANES, LANES))
    for rg in range(N_CHUNK // SUBLANES):
        x = s[rg * SUBLANES:(rg + 1) * SUBLANES]
        for j, d in enumerate((1, 2, 4)):
            x = x + _cmul_packed(cf[2 * j], cf[2 * j + 1], pltpu.roll(x, d, axis=0))
        x = x + _cmul_packed(cf[6], cf[7], carry)
        outs.append(x)
        carry = jnp.broadcast_to(x[SUBLANES - 1:SUBLANES], (SUBLANES, LANES))
    return jnp.concatenate(outs, axis=0)


def _expand_toeplitz(kcat_ref, mint_s):
    lane = lax.broadcasted_iota(I32, (SSM_GROUP, LANES), 1)
    zero = jnp.zeros((SSM_GROUP, LANES), BF16)

    def one_group(g, _):
        lo_src = kcat_ref[g, :, 0:LANES]
        hi_src = kcat_ref[g, :, LANES:2 * LANES]
        for s in range(CHUNK):
            d = SSM_GROUP * s
            rows = slice(SSM_GROUP * s, SSM_GROUP * (s + 1))
            if d == 0:
                lo, hi = lo_src, hi_src
            elif d < LANES:
                wrapped = pltpu.roll(lo_src, d, axis=1)
                lo = jnp.where(lane < d, zero, wrapped)
                hi = jnp.where(lane < d, wrapped, pltpu.roll(hi_src, d, axis=1))
            elif d == LANES:
                lo, hi = zero, lo_src
            else:
                lo = zero
                hi = jnp.where(lane < d - LANES, zero, pltpu.roll(lo_src, d - LANES, axis=1))
            mint_s[g, rows, 0:LANES] = lo
            mint_s[g, rows, LANES:2 * LANES] = hi
        return 0

    lax.fori_loop(0, SSM_GROUPS, one_group, 0)


def _s5_mixer(u, first, minw_ref, mint_s, mout_ref, coef_ref, u_s, y_s, h_s, hooks):
    blocks = SSM_WIDTH // LANES
    top = SUBLANES - 1
    for b4 in range(blocks):
        u_s[b4] = u[:, b4 * LANES:(b4 + 1) * LANES]
    lane = lax.broadcasted_iota(I32, (N_CHUNK, LANES), 1)

    halves = [[None, None] for _ in range(SSM_GROUPS)]
    for b4 in range(blocks):
        for hh in range(2):
            xs = [u_s[b4, pl.ds(8 * hh + j, N_CHUNK, stride=CHUNK), :].astype(BF16) for j in range(8)]
            ws = _block_transpose8(xs, lane)
            for gl in range(8):
                halves[8 * b4 + gl][hh] = ws[gl]
    ug = [jnp.concatenate(h2, axis=1) for h2 in halves]
    hooks[0]()

    carry_in = jnp.where(first, 0.0, h_s[top + N_CHUNK:top + N_CHUNK + 1, :])
    h_s[top:top + 1, :] = carry_in
    for g in range(SSM_GROUPS):
        cols = slice(g * LANES, (g + 1) * LANES)
        s_g = jnp.dot(ug[g], minw_ref[g], preferred_element_type=F32)
        cf = [coef_ref[k, :, cols] for k in range(8)]
        h_s[SUBLANES:SUBLANES + N_CHUNK, cols] = _chunk_scan(s_g, carry_in[:, cols], cf)
        if g == SSM_GROUPS // 2:
            hooks[1]()
    hooks[2]()

    yg = []
    for g in range(SSM_GROUPS):
        hprev = h_s[top:top + N_CHUNK, g * LANES:(g + 1) * LANES].astype(BF16)
        y_g = (jnp.dot(ug[g], mint_s[g], preferred_element_type=F32)
               + jnp.dot(hprev, mout_ref[g], preferred_element_type=F32))
        yg.append(y_g.astype(BF16))
    hooks[3]()

    for b4 in range(blocks):
        for hh in range(2):
            ws = [yg[8 * b4 + gl][:, hh * LANES:(hh + 1) * LANES] for gl in range(8)]
            xs = _block_transpose8(ws, lane)
            for j in range(8):
                y_s[b4, pl.ds(8 * hh + j, N_CHUNK, stride=CHUNK), :] = xs[j].astype(F32)
    return jnp.concatenate([y_s[b4] for b4 in range(blocks)], axis=1)


def _mixer_kernel(*refs, first_layer, steps):
    if first_layer:
        x_ref = refs[0]
        refs = refs[1:]
    else:
        x_ref, yprev_ref, slabprev_ref = refs[:3]
        refs = refs[3:]
    (gmix_ref, win_ref, minw_ref, kcat_ref, mout_ref, coef_ref, dskip_ref, wglu_ref, bglu_ref, poolw_ref, pscale_ref,
     convw_ref, gbr_ref, wout_ref, gffn_ref, rw_ref, rb_ref, utri_ref,
     x1_ref, hp_ref, slab_ref, slabt_ref, cnt_ref,
     u_s, y_s, h_s, pbuf_ref, zbuf_ref, cntacc_ref, mint_s, mix_s, xres_s) = refs

    j = pl.program_id(0)
    l = lax.rem(j, steps)
    seq_start = l == 0

    def out_proj_quarter(q):
        def run():
            cols = slice(q * (D_MODEL // 4), (q + 1) * (D_MODEL // 4))
            x1_ref[:, cols] = xres_s[:, cols] + jnp.dot(mix_s[...], wout_ref[:, cols], preferred_element_type=F32)
        return run

    hooks = [out_proj_quarter(q) for q in range(4)]

    @pl.when(j == 0)
    def _():
        cntacc_ref[...] = jnp.zeros_like(cntacc_ref)
        _expand_toeplitz(kcat_ref, mint_s)

    if first_layer:
        x = x_ref[...]
    else:
        x = _moe_combine(x_ref[...], yprev_ref, slabprev_ref[...])

    hn = _rms(x, gmix_ref[...]).astype(BF16)
    proj = jnp.dot(hn, win_ref[...], preferred_element_type=F32)
    u = proj[:, :SSM_WIDTH]
    p = proj[:, SSM_WIDTH:SSM_WIDTH + POOL_WIDTH]
    c0 = SSM_WIDTH + POOL_WIDTH
    bg = proj[:, c0:c0 + CONV_WIDTH]
    cg = proj[:, c0 + CONV_WIDTH:c0 + 2 * CONV_WIDTH]
    hv = proj[:, c0 + 2 * CONV_WIDTH:]

    y = _s5_mixer(u, seq_start, minw_ref, mint_s, mout_ref, coef_ref, u_s, y_s, h_s, hooks) + dskip_ref[...] * u
    _mixer_second_half(x1_ref[...], j >= 1, gffn_ref, rw_ref, rb_ref, utri_ref,
                       hp_ref, slab_ref, slabt_ref, cnt_ref, cntacc_ref)
    z = jax.nn.gelu(y)
    glu = jnp.dot(z.astype(BF16), wglu_ref[...], preferred_element_type=F32) + bglu_ref[...]
    y_ssm = z * jax.nn.sigmoid(glu)

    tail = POOL_TAIL
    pbuf_ref[0, 0:tail, :] = jnp.where(seq_start, 0.0, pbuf_ref[0, TILE_T:TILE_T + tail, :])
    pbuf_ref[0, tail:, :] = p
    sums = []
    for stage, w in enumerate(POOL_WINDOWS):
        src = 0 if stage == 0 else 1 + (stage - 1) % 2
        dst = 1 + stage % 2
        d = w // 2
        lo = SUBLANES * (stage + 1)
        n = tail + TILE_T - lo
        e = pbuf_ref[src, pl.ds(lo, n), :] + pbuf_ref[src, pl.ds(lo - d, n), :]
        if stage + 1 < len(POOL_WINDOWS):
            pbuf_ref[dst, pl.ds(lo, n), :] = e
        sums.append(e[tail - lo:, :])
    lane_p = lax.broadcasted_iota(I32, (TILE_T, POOL_WIDTH), 1)
    total = sums[-1]
    win = jnp.full((TILE_T, POOL_WIDTH), POOL_WINDOWS[-1], I32)
    for i in range(len(POOL_WINDOWS) - 2, -1, -1):
        sel = lane_p < (i + 1) * POOL_GROUP
        total = jnp.where(sel, sums[i], total)
        win = jnp.where(sel, POOL_WINDOWS[i], win)
    row_p = lax.broadcasted_iota(I32, (TILE_T, POOL_WIDTH), 0) + l * TILE_T
    count = jnp.minimum(row_p + 1, win).astype(F32)
    pooled = total / count - p
    mixed = jnp.dot(pooled.astype(BF16), poolw_ref[...], preferred_element_type=F32)
    y_pool = mixed * pscale_ref[...]

    zc = cg * hv
    zbuf_ref[0:SUBLANES, :] = jnp.where(seq_start, 0.0, zbuf_ref[TILE_T:TILE_T + SUBLANES, :])
    zbuf_ref[SUBLANES:, :] = zc
    yc = (convw_ref[0:1, :] * zbuf_ref[pl.ds(SUBLANES - 2, TILE_T), :]
          + convw_ref[1:2, :] * zbuf_ref[pl.ds(SUBLANES - 1, TILE_T), :]
          + convw_ref[2:3, :] * zc)
    y_conv = bg * yc

    gbr = gbr_ref[...]
    s1 = SSM_WIDTH + POOL_WIDTH
    mixed_all = jnp.concatenate([_rms(y_ssm, gbr[:, :SSM_WIDTH]).astype(BF16),
                                 _rms(y_pool, gbr[:, SSM_WIDTH:s1]).astype(BF16),
                                 _rms(y_conv, gbr[:, s1:]).astype(BF16)], axis=1)
    mix_s[...] = mixed_all
    xres_s[...] = x


def _mixer_second_half(x1, live, gffn_ref, rw_ref, rb_ref, utri_ref, hp_ref, slab_ref, slabt_ref, cnt_ref, cntacc_ref):
    hn2 = _rms(x1, gffn_ref[...])
    packed = _pack_rows(hn2)
    hp_ref[0] = packed[:, :SC_ROW]
    hp_ref[1] = packed[:, SC_ROW:]
    logits = jnp.dot(hn2.astype(BF16), rw_ref[...], preferred_element_type=F32) + rb_ref[...]
    work = logits.T[:N_EXPERTS, :]
    eio = lax.broadcasted_iota(I32, (N_EXPERTS, TILE_T), 0).astype(F32)
    neg = jnp.float32(-jnp.inf)
    vals, idxs = [], []
    for _ in range(TOP_K):
        m = jnp.max(work, axis=0, keepdims=True)
        idx = jnp.min(jnp.where(work == m, eio, float(N_EXPERTS)), axis=0, keepdims=True)
        vals.append(m)
        idxs.append(idx)
        work = jnp.where(eio == idx, neg, work)
    exps = [jnp.exp(v - vals[0]) for v in vals]
    denom = exps[0] + exps[1] + exps[2] + exps[3]
    gates = [e / denom for e in exps]

    onehot = jnp.zeros((N_EXPERTS, TILE_T), F32)
    for idx in idxs:
        onehot = jnp.where(eio == idx, 1.0, onehot)
    before = jnp.dot(onehot.astype(BF16), utri_ref[...], preferred_element_type=F32) + cntacc_ref[:, 0:1]
    ranks = [jnp.sum(jnp.where(eio == idx, before, 0.0), axis=0, keepdims=True) for idx in idxs]
    row = lax.broadcasted_iota(I32, (LANES, TILE_T), 0)
    slab_t = jnp.zeros((LANES, TILE_T), F32)
    for k in range(TOP_K):
        slab_t = jnp.where(row == k, idxs[k], slab_t)
        slab_t = jnp.where(row == TOP_K + k, gates[k], slab_t)
        slab_t = jnp.where(row == 2 * TOP_K + k, ranks[k], slab_t)
    slabt_ref[...] = slab_t[:2 * SUBLANES, :]
    slab_ref[...] = slab_t.T
    newcnt = jnp.where(live, cntacc_ref[:, 0:1] + jnp.sum(onehot, axis=1, keepdims=True), 0.0)
    cntacc_ref[...] = jnp.broadcast_to(newcnt, cntacc_ref.shape)
    cnt_ref[...] = jnp.broadcast_to(newcnt, cnt_ref.shape)


def _layer_spec(shape, layer):
    nd = len(shape) - 1
    sel = layer if shape[0] > 1 else 0
    return pl.BlockSpec((None,) + tuple(shape[1:]), lambda j, nd=nd, sel=sel: (sel,) + (0,) * nd,
                        pipeline_mode=pl.Buffered(1))


def _mixer_call(x, x_tile0, yprev, slabprev, lp, layer, batch, seq):
    first_layer = yprev is None
    n_tok = batch * seq
    steps = seq // TILE_T
    tiles = batch * steps
    cur = lambda j: jnp.minimum(j, tiles - 1)
    prv = lambda j: jnp.maximum(j - 1, 0)
    tok_spec = lambda w: pl.BlockSpec((TILE_T, w), lambda j: (prv(j), 0))
    in_specs = [pl.BlockSpec((TILE_T, D_MODEL), lambda j: (x_tile0 + cur(j), 0))]
    args = [x]
    if not first_layer:
        in_specs += [pl.BlockSpec((TOP_K, 2, TILE_T, SC_ROW), lambda j: (0, 0, cur(j), 0)),
                     pl.BlockSpec((TILE_T, LANES), lambda j: (cur(j), 0))]
        args += [yprev, slabprev]
    weights = [lp['gmix'], lp['win'], lp['minw'], lp['kcat'], lp['mout'], lp['coef'], lp['dskip'], lp['wglu'], lp['bglu'],
               lp['poolw'], lp['pscale'], lp['convw'], lp['gbr'], lp['wout'], lp['gffn'], lp['rw'], lp['rb'],
               lp['utri']]
    in_specs += [_layer_spec(w.shape, layer) for w in weights]
    args += weights
    out_shape = [jax.ShapeDtypeStruct((n_tok, D_MODEL), F32),
                 jax.ShapeDtypeStruct((2, n_tok, SC_ROW), I32),
                 jax.ShapeDtypeStruct((n_tok, LANES), F32),
                 jax.ShapeDtypeStruct((2 * SUBLANES, n_tok), F32),
                 jax.ShapeDtypeStruct((N_EXPERTS, LANES), F32)]
    out_specs = [tok_spec(D_MODEL), pl.BlockSpec((2, TILE_T, SC_ROW), lambda j: (0, prv(j), 0)), tok_spec(LANES),
                 pl.BlockSpec((2 * SUBLANES, TILE_T), lambda j: (0, prv(j))),
                 pl.BlockSpec((N_EXPERTS, LANES), lambda j: (0, 0))]
    scratch = [pltpu.VMEM((SSM_WIDTH // LANES, TILE_T, LANES), F32),
               pltpu.VMEM((SSM_WIDTH // LANES, TILE_T, LANES), F32),
               pltpu.VMEM((SUBLANES + N_CHUNK, SSM_GROUPS * LANES), F32),
               pltpu.VMEM((3, POOL_TAIL + TILE_T, POOL_WIDTH), F32),
               pltpu.VMEM((SUBLANES + TILE_T, CONV_WIDTH), F32),
               pltpu.VMEM((N_EXPERTS, LANES), F32),
               pltpu.VMEM((SSM_GROUPS, CHUNK * SSM_GROUP, CHUNK * SSM_GROUP), BF16),
               pltpu.VMEM((TILE_T, D_MODEL), BF16),
               pltpu.VMEM((TILE_T, D_MODEL), F32)]
    return pl.pallas_call(
        functools.partial(_mixer_kernel, first_layer=first_layer, steps=steps),
        grid=(tiles + 1,),
        in_specs=in_specs, out_specs=out_specs, out_shape=out_shape, scratch_shapes=scratch,
        compiler_params=pltpu.CompilerParams(dimension_semantics=("arbitrary",),
                                             vmem_limit_bytes=VMEM_LIMIT),
        name="mixer_first" if first_layer else "mixer_next",
    )(*args)


def _route_kernel(slabt_ref, cnt_ref, dest_ref, meta_ref, *, n_steps):
    lane1 = lax.broadcasted_iota(I32, (1, LANES), 1)
    e_row = lax.broadcasted_iota(I32, (N_EXPERTS, LANES), 0)
    e_lane = lax.broadcasted_iota(I32, (N_EXPERTS, LANES), 1)
    cnt = cnt_ref[...].astype(I32)
    counts = jnp.sum(jnp.where(e_row == e_lane, cnt, 0), axis=0, keepdims=True)
    padded = ((counts + (STEP_ROWS - 1)) // STEP_ROWS) * STEP_ROWS
    pend = padded
    sh = 1
    while sh < N_EXPERTS:
        pend = pend + jnp.where(lane1 >= sh, pltpu.roll(pend, sh, axis=1), 0)
        sh *= 2

    @pl.when(pl.program_id(0) == 0)
    def _():
        rows = meta_ref.shape[0]
        bstart = lax.broadcasted_iota(I32, (rows, LANES), 0) * STEP_ROWS
        lane = lax.broadcasted_iota(I32, (rows, LANES), 1)
        done = jnp.where(jnp.logical_and(lane < N_EXPERTS, pend <= bstart), 1, 0)
        be = jnp.minimum(jnp.sum(done, axis=1, keepdims=True), N_EXPERTS - 1)
        mine = lane == be
        first_row = jnp.sum(jnp.where(mine, pend - counts, 0), axis=1, keepdims=True)
        used = bstart[:, 0:1] + STEP_ROWS - jnp.maximum(bstart[:, 0:1], first_row)
        nblk = jnp.clip((used + (EXPERT_BLOCK - 1)) // EXPERT_BLOCK, 0, STEP_ROWS // EXPERT_BLOCK)
        total = jnp.sum(jnp.where(lane1 == N_EXPERTS - 1, pend, 0), axis=1, keepdims=True)
        row = lax.broadcasted_iota(I32, (rows, LANES), 0)
        table = jnp.where(lane == 0, be, jnp.where(lane == 1, nblk, 0))
        meta_ref[...] = jnp.where(row == n_steps, total // STEP_ROWS, table)

    last_col = jnp.sum(jnp.where(e_row == e_lane, pend - 1, 0), axis=1, keepdims=True)
    tokens = slabt_ref.shape[1]
    eio = lax.broadcasted_iota(I32, (N_EXPERTS, tokens), 0)
    row = lax.broadcasted_iota(I32, (SUBLANES, tokens), 0)
    out = jnp.zeros((SUBLANES, tokens), I32)
    for k in range(TOP_K):
        idx = slabt_ref[k:k + 1, :].astype(I32)
        rank = slabt_ref[2 * TOP_K + k:2 * TOP_K + k + 1, :].astype(I32)
        base = jnp.sum(jnp.where(eio == idx, last_col, 0), axis=0, keepdims=True)
        for h in range(2):
            out = jnp.where(row == 2 * k + h, base - rank + h * n_steps * STEP_ROWS, out)
    dest_ref[...] = out


def _route_call(slabt, counts, n_tok, n_steps):
    meta_rows = ((n_steps + 1 + SUBLANES - 1) // SUBLANES) * SUBLANES
    return pl.pallas_call(
        functools.partial(_route_kernel, n_steps=n_steps),
        grid=(n_tok // ROUTE_T,),
        in_specs=[pl.BlockSpec((2 * SUBLANES, ROUTE_T), lambda i: (0, i)),
                  pl.BlockSpec((N_EXPERTS, LANES), lambda i: (0, 0))],
        out_specs=[pl.BlockSpec((SUBLANES, ROUTE_T), lambda i: (0, i)),
                   pl.BlockSpec((meta_rows, LANES), lambda i: (0, 0))],
        out_shape=[jax.ShapeDtypeStruct((SUBLANES, n_tok), I32),
                   jax.ShapeDtypeStruct((meta_rows, LANES), I32)],
        compiler_params=pltpu.CompilerParams(dimension_semantics=("arbitrary",)),
        name="route",
    )(slabt, counts)


def _sc_mesh():
    return plsc.VectorSubcoreMesh(core_axis_name="core", subcore_axis_name="subcore")


def _dispatch(src, idx, n_rows):
    windows = src.shape[0] // SC_WINDOW
    copies = idx.shape[1] // src.shape[0]

    @functools.partial(pl.kernel, out_type=jax.ShapeDtypeStruct((n_rows, SC_ROW), src.dtype),
                       mesh=_sc_mesh(), scratch_types=[])
    def scatter_rows(x_hbm, i_hbm, o_hbm):
        def body(x_vmem, *i_vmems):
            for i_vmem in i_vmems:
                pltpu.sync_copy(x_vmem, o_hbm.at[i_vmem.at[0]])

        pltpu.emit_pipeline(
            body,
            grid=(windows,),
            in_specs=[pl.BlockSpec((SC_WINDOW, SC_ROW), lambda i: (i, 0))]
            + [pl.BlockSpec((1, SC_WINDOW), lambda i, k=k: (0, k * windows + i)) for k in range(copies)],
            out_specs=[],
            core_axis_name=("core", "subcore"),
            dimension_semantics=(pltpu.PARALLEL,),
        )(x_hbm, *([i_hbm] * copies))

    return scatter_rows(src, idx)


def _combine(src, idx):
    n_idx = idx.shape[1]

    @functools.partial(pl.kernel, out_type=jax.ShapeDtypeStruct((n_idx, SC_ROW), src.dtype),
                       mesh=_sc_mesh(), scratch_types=[])
    def gather_rows(y_hbm, i_hbm, o_hbm):
        def body(i_vmem, o_vmem):
            pltpu.sync_copy(y_hbm.at[i_vmem.at[0]], o_vmem)

        pltpu.emit_pipeline(
            body,
            grid=(n_idx // SC_WINDOW,),
            in_specs=[pl.BlockSpec((1, SC_WINDOW), lambda i: (0, i))],
            out_specs=[pl.BlockSpec((SC_WINDOW, SC_ROW), lambda i: (i, 0))],
            core_axis_name=("core", "subcore"),
            dimension_semantics=(pltpu.PARALLEL,),
        )(i_hbm, o_hbm)

    return gather_rows(src, idx)


def _expert_rows(xs_ref, ys_ref, row0, n, wgu_s, bgu_ref, wdn_s, bdn_ref, act_s):
    rows = pl.ds(pl.multiple_of(row0, EXPERT_BLOCK), n)
    arow = slice(0, n)
    w0 = xs_ref[0, rows, :]
    w1 = xs_ref[1, rows, :]
    xb = jnp.concatenate([_unpack_lo(w0).astype(BF16), _unpack_lo(w1).astype(BF16),
                          _unpack_hi(w0).astype(BF16), _unpack_hi(w1).astype(BF16)], axis=1)
    half = D_EXPERT // 2
    for j in range(2):
        cols = slice(j * half, (j + 1) * half)
        ucols = slice(D_EXPERT + j * half, D_EXPERT + (j + 1) * half)
        g = jnp.dot(xb, wgu_s[:, cols], preferred_element_type=F32) + bgu_ref[:, cols]
        up = jnp.dot(xb, wgu_s[:, ucols], preferred_element_type=F32) + bgu_ref[:, ucols]
        g = jnp.minimum(g, SWIGLU_LIMIT)
        up = jnp.clip(up, -SWIGLU_LIMIT, SWIGLU_LIMIT)
        act_s[arow, cols] = (g * jax.nn.sigmoid(SWIGLU_ALPHA * g) * (up + 1.0)).astype(BF16)
    y = jnp.dot(act_s[arow, :], wdn_s[...], preferred_element_type=F32) + bdn_ref[...]
    packed = _pack_rows(y)
    ys_ref[0, rows, :] = packed[:, :SC_ROW]
    ys_ref[1, rows, :] = packed[:, SC_ROW:]


def _expert_kernel(be_ref, nb_ref, nv_ref, xs_ref, wgu_ref, bgu_ref, wdn_ref, bdn_ref, ys_ref, wgu_s, wdn_s, act_s):
    blk = pl.program_id(0)
    prev = be_ref[jnp.maximum(blk - 1, 0)]
    changed = jnp.logical_or(blk == 0, be_ref[blk] != prev)
    valid = blk < nv_ref[0]

    @pl.when(jnp.logical_and(valid, changed))
    def _():
        chunk = 128

        def cast_gu(i, _):
            rows = pl.ds(pl.multiple_of(i * chunk, chunk), chunk)
            wgu_s[rows, :] = wgu_ref[rows, :].astype(BF16)
            return 0

        def cast_dn(i, _):
            rows = pl.ds(pl.multiple_of(i * chunk, chunk), chunk)
            wdn_s[rows, :] = wdn_ref[rows, :].astype(BF16)
            return 0

        lax.fori_loop(0, D_MODEL // chunk, cast_gu, 0)
        lax.fori_loop(0, D_EXPERT // chunk, cast_dn, 0)

    nblk = nb_ref[blk]
    first = STEP_ROWS - nblk * EXPERT_BLOCK
    big = nblk // RUN_BLOCKS

    @pl.when(jnp.logical_and(valid, big > 0))
    def _():
        def one_run(i, _):
            row0 = first + (nblk - (i + 1) * RUN_BLOCKS) * EXPERT_BLOCK
            _expert_rows(xs_ref, ys_ref, row0, RUN_BLOCKS * EXPERT_BLOCK, wgu_s, bgu_ref, wdn_s, bdn_ref, act_s)
            return 0

        lax.fori_loop(0, big, one_run, 0)

    bit = RUN_BLOCKS // 2
    while bit:
        @pl.when(jnp.logical_and(valid, (nblk & bit) != 0))
        def _(bit=bit):
            row0 = first + (nblk & (bit - 1)) * EXPERT_BLOCK
            _expert_rows(xs_ref, ys_ref, row0, bit * EXPERT_BLOCK, wgu_s, bgu_ref, wdn_s, bdn_ref, act_s)
        bit //= 2


def _expert_call(xs, step_e, step_nblk, n_valid, w_gu, b_gu, w_dn, b_dn, layer, n_steps):
    def row_map(i, be, nb, nv):
        return (0, jnp.minimum(i, jnp.maximum(nv[0] - 1, 0)), 0)

    def w_map(i, be, nb, nv):
        return (layer, be[i], 0, 0)

    grid_spec = pltpu.PrefetchScalarGridSpec(
        num_scalar_prefetch=3,
        grid=(n_steps,),
        in_specs=[pl.BlockSpec((2, STEP_ROWS, SC_ROW), row_map),
                  pl.BlockSpec((None, None, D_MODEL, 2 * D_EXPERT), w_map),
                  pl.BlockSpec((None, None, 1, 2 * D_EXPERT), w_map),
                  pl.BlockSpec((None, None, D_EXPERT, D_MODEL), w_map),
                  pl.BlockSpec((None, None, 1, D_MODEL), w_map)],
        out_specs=pl.BlockSpec((2, STEP_ROWS, SC_ROW), row_map),
        scratch_shapes=[pltpu.VMEM((D_MODEL, 2 * D_EXPERT), BF16),
                        pltpu.VMEM((D_EXPERT, D_MODEL), BF16),
                        pltpu.VMEM((RUN_BLOCKS * EXPERT_BLOCK, D_EXPERT), BF16)])
    return pl.pallas_call(
        _expert_kernel,
        grid_spec=grid_spec,
        out_shape=jax.ShapeDtypeStruct((2, n_steps * STEP_ROWS, SC_ROW), I32),
        compiler_params=pltpu.CompilerParams(dimension_semantics=("arbitrary",),
                                             vmem_limit_bytes=VMEM_LIMIT),
        name="experts",
    )(step_e, step_nblk, n_valid, xs, w_gu, b_gu, w_dn, b_dn)


def _final_kernel(x_ref, y_ref, slab_ref, g_ref, *rest):
    o_ref = rest[-1]
    x = _moe_combine(x_ref[...], y_ref, slab_ref[...])
    o_ref[...] = _rms(x, g_ref[...])


def _final_call(x1, ysg, slab, g, out_row0, n_out, out_prev):
    t = FINAL_T
    out_tile0 = out_row0 // t
    in_specs = [pl.BlockSpec((t, D_MODEL), lambda i: (i, 0)),
                pl.BlockSpec((TOP_K, 2, t, SC_ROW), lambda i: (0, 0, i, 0)),
                pl.BlockSpec((t, LANES), lambda i: (i, 0)),
                pl.BlockSpec((1, D_MODEL), lambda i: (0, 0))]
    args = [x1, ysg, slab, g]
    aliases = {}
    if out_prev is not None:
        in_specs.append(pl.BlockSpec(memory_space=pl.ANY))
        args.append(out_prev)
        aliases = {4: 0}
    return pl.pallas_call(
        _final_kernel,
        grid=(x1.shape[0] // t,),
        in_specs=in_specs,
        out_specs=pl.BlockSpec((t, D_MODEL), lambda i: (out_tile0 + i, 0)),
        out_shape=jax.ShapeDtypeStruct((n_out, D_MODEL), F32),
        input_output_aliases=aliases,
        compiler_params=pltpu.CompilerParams(dimension_semantics=("arbitrary",), vmem_limit_bytes=VMEM_LIMIT),
        name="final_norm",
    )(*args)


def _s5_tables(lam_re, lam_im, b_re, b_im, c_re, c_im, log_step):
    lr = jnp.minimum(lam_re.astype(F32), LAMBDA_RE_MAX)
    li = lam_im.astype(F32)
    step = jnp.exp(log_step.astype(F32))[:, None]
    mag = jnp.exp(lr * step)
    ar = mag * jnp.cos(li * step)
    ai = mag * jnp.sin(li * step)
    nr = ar - 1.0
    den = lr * lr + li * li
    kr = (nr * lr + ai * li) / den
    ki = (ai * lr - nr * li) / den
    bre = b_re.astype(F32)
    bim = b_im.astype(F32)
    bbar_r = kr[..., None] * bre - ki[..., None] * bim
    bbar_i = kr[..., None] * bim + ki[..., None] * bre
    cre = c_re.astype(F32)
    cim = c_im.astype(F32)

    def power(m):
        m = jnp.asarray(m, F32)[..., None, None]
        mg = jnp.exp(m * lr * step)
        return mg * jnp.cos(m * li * step), mg * jnp.sin(m * li * step)

    pw_r, pw_i = power(jnp.arange(CHUNK + 1))
    cp_r = cre[None] * pw_r[:, :, None, :] - cim[None] * pw_i[:, :, None, :]
    cp_i = cre[None] * pw_i[:, :, None, :] + cim[None] * pw_r[:, :, None, :]
    width = CHUNK * SSM_GROUP
    kcat = (jnp.einsum('tgap,gph->ghta', cp_r[:CHUNK], bbar_r)
            - jnp.einsum('tgap,gph->ghta', cp_i[:CHUNK], bbar_i)).reshape(SSM_GROUPS, SSM_GROUP, width).astype(BF16)
    mo_r = cp_r[1:].transpose(1, 3, 0, 2).reshape(SSM_GROUPS, SSM_STATE, width)
    mo_i = -cp_i[1:].transpose(1, 3, 0, 2).reshape(SSM_GROUPS, SSM_STATE, width)
    mout = jnp.concatenate([mo_r, mo_i], axis=1).astype(BF16)
    q_r = pw_r[CHUNK - 1::-1][:CHUNK][:, :, :, None]
    q_i = pw_i[CHUNK - 1::-1][:CHUNK][:, :, :, None]
    mn_r = q_r * bbar_r[None] - q_i * bbar_i[None]
    mn_i = q_r * bbar_i[None] + q_i * bbar_r[None]
    minw = jnp.concatenate([mn_r, mn_i], axis=2).transpose(1, 0, 3, 2).reshape(
        SSM_GROUPS, CHUNK * SSM_GROUP, 2 * SSM_STATE).astype(BF16)

    def packed(j):
        pr, pi = power(float(CHUNK * j))
        return (jnp.concatenate([pr, pr], axis=-1).reshape(1, SSM_GROUPS * LANES),
                jnp.concatenate([-pi, pi], axis=-1).reshape(1, SSM_GROUPS * LANES))

    rows = jnp.arange(SUBLANES)[:, None]
    kinds = []
    for d in (1, 2, 4):
        c1, c2 = packed(d)
        mask = (rows >= d).astype(F32)
        kinds += [mask * c1, mask * c2]
    pk = [packed(r + 1) for r in range(SUBLANES)]
    kinds += [jnp.concatenate([p[0] for p in pk], axis=0), jnp.concatenate([p[1] for p in pk], axis=0)]
    coef = jnp.stack(kinds, axis=0)
    return minw, kcat, mout, coef


def _mixer_params(norm_mix, w_in, ssm_lam_re, ssm_lam_im, ssm_b_re, ssm_b_im, ssm_c_re, ssm_c_im, ssm_d,
                  ssm_log_step, ssm_w_glu, ssm_b_glu, pool_w, pool_scale, conv_w, branch_norm, w_out, norm_ffn,
                  router_w, router_b):
    depth = w_in.shape[0]
    minw, kcat, mout, coef = jax.vmap(_s5_tables)(ssm_lam_re, ssm_lam_im, ssm_b_re, ssm_b_im, ssm_c_re, ssm_c_im,
                                                  ssm_log_step)
    eye4 = jnp.eye(len(POOL_WINDOWS), dtype=F32)
    poolw = jnp.einsum('lgcd,gk->lgckd', pool_w.astype(F32), eye4).reshape(depth, POOL_WIDTH, POOL_WIDTH).astype(BF16)
    rw = jnp.pad(router_w.astype(F32), ((0, 0), (0, 0), (0, LANES - N_EXPERTS))).astype(BF16)
    rb = jnp.pad(router_b.astype(F32), ((0, 0), (0, LANES - N_EXPERTS))).reshape(depth, 1, LANES)
    row = lambda v: v.reshape(depth, 1, -1).astype(F32)
    utri = jnp.triu(jnp.ones((TILE_T, TILE_T), F32), 1).astype(BF16)
    return dict(gmix=row(norm_mix), win=w_in.astype(BF16), minw=minw, kcat=kcat, mout=mout, coef=coef, dskip=row(ssm_d),
                wglu=ssm_w_glu.astype(BF16), bglu=row(ssm_b_glu), poolw=poolw, pscale=row(pool_scale),
                convw=conv_w.astype(F32), gbr=row(branch_norm), wout=w_out.astype(BF16),
                gffn=row(norm_ffn), rw=rw, rb=rb, utri=utri[None])


def kernel(x, norm_mix, w_in, ssm_lam_re, ssm_lam_im, ssm_b_re, ssm_b_im, ssm_c_re, ssm_c_im, ssm_d, ssm_log_step, ssm_w_glu, ssm_b_glu, pool_w, pool_scale, conv_w, branch_norm, w_out, norm_ffn, router_w, router_b, w_gate_up, b_gate_up, w_down, b_down, final_norm):
    batch, seq, d_model = x.shape
    assert d_model == D_MODEL and seq % TILE_T == 0
    depth = w_in.shape[0]
    n_streams = 2 if batch % 2 == 0 else 1
    sb = batch // n_streams
    n_tok = sb * seq
    assert n_tok % ROUTE_T == 0 and n_tok % SC_WINDOW == 0
    n_steps = -(-(n_tok * TOP_K + N_EXPERTS * (STEP_ROWS - 1)) // STEP_ROWS)
    n_rows = n_steps * STEP_ROWS
    tiles = n_tok // TILE_T
    b_gu = b_gate_up.reshape(depth, N_EXPERTS, 1, 2 * D_EXPERT)
    b_dn = b_down.reshape(depth, N_EXPERTS, 1, D_MODEL)

    xin = [x.reshape(batch * seq, D_MODEL)] * n_streams
    tile0 = [s * tiles for s in range(n_streams)]
    ysg = [None] * n_streams
    slab = [None] * n_streams
    lp = _mixer_params(norm_mix, w_in, ssm_lam_re, ssm_lam_im, ssm_b_re, ssm_b_im, ssm_c_re, ssm_c_im, ssm_d,
                       ssm_log_step, ssm_w_glu, ssm_b_glu, pool_w, pool_scale, conv_w, branch_norm, w_out,
                       norm_ffn, router_w, router_b)
    for l in range(depth):
        routed = []
        for s in range(n_streams):
            xin[s], hp, slab[s], slabt, counts = _mixer_call(xin[s], tile0[s], ysg[s], slab[s], lp, l, sb, seq)
            tile0[s] = 0
            dest, meta = _route_call(slabt, counts, n_tok, n_steps)
            idx = dest.reshape(1, 2 * TOP_K * n_tok)
            xs = _dispatch(hp.reshape(2 * n_tok, SC_ROW), idx, 2 * n_rows).reshape(2, n_rows, SC_ROW)
            routed.append((xs, idx, meta[:n_steps, 0], meta[:n_steps, 1], meta[n_steps:n_steps + 1, 0]))
        for s, (xs, idx, step_e, step_nblk, n_valid) in enumerate(routed):
            ys = _expert_call(xs, step_e, step_nblk, n_valid, w_gate_up, b_gu, w_down, b_dn, l, n_steps)
            ysg[s] = _combine(ys.reshape(2 * n_rows, SC_ROW), idx).reshape(TOP_K, 2, n_tok, SC_ROW)
    g = final_norm.reshape(1, D_MODEL).astype(F32)
    out = None
    for s in range(n_streams):
        out = _final_call(xin[s], ysg[s], slab[s], g, s * n_tok, batch * seq, out)
    return out.reshape(batch, seq, D_MODEL)
```

```python
import functools

import jax
import jax.numpy as jnp
from jax import lax
from jax.experimental import pallas as pl
from jax.experimental.pallas import tpu as pltpu
from jax.experimental.pallas import tpu_sc as plsc

F32 = jnp.float32
BF16 = jnp.bfloat16
I32 = jnp.int32

D_MODEL = 1024
SSM_WIDTH = 512
POOL_WIDTH = 256
CONV_WIDTH = 256
SSM_GROUP = 16
SSM_GROUPS = 32
SSM_STATE = 64
LAMBDA_RE_MAX = -1e-4
POOL_WINDOWS = (2, 4, 8, 16)
POOL_GROUP = 64
N_EXPERTS = 32
TOP_K = 4
D_EXPERT = 1024
SWIGLU_LIMIT = 7.0
SWIGLU_ALPHA = 1.702
EXPERT_BLOCK = 256
STEP_ROWS = 2048
RUN_BLOCKS = 4
NORM_EPS = 1e-5

LANES = 128
SUBLANES = 8
TILE_T = 512
FINAL_T = 1024
CHUNK = 16
N_CHUNK = TILE_T // CHUNK
POOL_TAIL = SUBLANES * len(POOL_WINDOWS)
assert POOL_WINDOWS == tuple(2 ** (k + 1) for k in range(len(POOL_WINDOWS))) and POOL_TAIL >= POOL_WINDOWS[-1]
PACKED = D_MODEL // 2
ROUTE_T = 4096
SC_WINDOW = 128
SC_ROW = PACKED // 2
HI_MASK = -65536
VMEM_LIMIT = 56 * 1024 * 1024


def _bf16_round(v):
    return v.astype(BF16).astype(F32)


def _pack_rows(v):
    lo = lax.shift_right_logical(lax.bitcast_convert_type(_bf16_round(v[:, :PACKED]), I32), 16)
    hi = lax.bitcast_convert_type(_bf16_round(v[:, PACKED:]), I32) & HI_MASK
    return hi | lo


def _unpack_lo(w):
    return lax.bitcast_convert_type(lax.shift_left(w, 16), F32)


def _unpack_hi(w):
    return lax.bitcast_convert_type(w & HI_MASK, F32)


def _rms(v, g):
    r = lax.rsqrt(jnp.mean(v * v, axis=-1, keepdims=True) + NORM_EPS)
    return (v * r) * g


def _moe_combine(x, y_ref, slab):
    parts = [x[:, i * SC_ROW:(i + 1) * SC_ROW] for i in range(4)]
    for k in range(TOP_K):
        g = slab[:, TOP_K + k:TOP_K + k + 1]
        for h in range(2):
            w = y_ref[k, h]
            parts[h] = parts[h] + g * _unpack_lo(w)
            parts[2 + h] = parts[2 + h] + g * _unpack_hi(w)
    return jnp.concatenate(parts, axis=1)


def _block_transpose8(vs, lane):
    vs = list(vs)
    for dist in (4, 2, 1):
        width = SSM_GROUP * dist
        low = (lane % (2 * width)) < width
        for j in range(8):
            if j & dist:
                continue
            a, b = vs[j], vs[j + dist]
            vs[j] = jnp.where(low, a, pltpu.roll(b, width, axis=1))
            vs[j + dist] = jnp.where(low, pltpu.roll(a, LANES - width, axis=1), b)
    return vs


def _cmul_packed(c1, c2, v):
    return c1 * v + c2 * pltpu.roll(v, SSM_STATE, axis=1)


def _chunk_scan(s, carry_in, cf):
    outs = []
    carry = jnp.broadcast_to(carry_in, (SUBLANES, LANES))
    for rg in range(N_CHUNK // SUBLANES):
        x = s[rg * SUBLANES:(rg + 1) * SUBLANES]
        for j, d in enumerate((1, 2, 4)):
            x = x + _cmul_packed(cf[2 * j], cf[2 * j + 1], pltpu.roll(x, d, axis=0))
        x = x + _cmul_packed(cf[6], cf[7], carry)
        outs.append(x)
        carry = jnp.broadcast_to(x[SUBLANES - 1:SUBLANES], (SUBLANES, LANES))
    return jnp.concatenate(outs, axis=0)


def _expand_toeplitz(kcat_ref, mint_s):
    lane = lax.broadcasted_iota(I32, (SSM_GROUP, LANES), 1)
    zero = jnp.zeros((SSM_GROUP, LANES), BF16)

    def one_group(g, _):
        lo_src = kcat_ref[g, :, 0:LANES]
        hi_src = kcat_ref[g, :, LANES:2 * LANES]
        for s in range(CHUNK):
            d = SSM_GROUP * s
            rows = slice(SSM_GROUP * s, SSM_GROUP * (s + 1))
            if d == 0:
                lo, hi = lo_src, hi_src
            elif d < LANES:
                wrapped = pltpu.roll(lo_src, d, axis=1)
                lo = jnp.where(lane < d, zero, wrapped)
                hi = jnp.where(lane < d, wrapped, pltpu.roll(hi_src, d, axis=1))
            elif d == LANES:
                lo, hi = zero, lo_src
            else:
                lo = zero
                hi = jnp.where(lane < d - LANES, zero, pltpu.roll(lo_src, d - LANES, axis=1))
            mint_s[g, rows, 0:LANES] = lo
            mint_s[g, rows, LANES:2 * LANES] = hi
        return 0

    lax.fori_loop(0, SSM_GROUPS, one_group, 0)


def _s5_mixer(u, first, minw_ref, mint_s, mout_ref, coef_ref, u_s, y_s, h_s, hooks):
    blocks = SSM_WIDTH // LANES
    top = SUBLANES - 1
    for b4 in range(blocks):
        u_s[b4] = u[:, b4 * LANES:(b4 + 1) * LANES]
    lane = lax.broadcasted_iota(I32, (N_CHUNK, LANES), 1)

    halves = [[None, None] for _ in range(SSM_GROUPS)]
    for b4 in range(blocks):
        for hh in range(2):
            xs = [u_s[b4, pl.ds(8 * hh + j, N_CHUNK, stride=CHUNK), :].astype(BF16) for j in range(8)]
            ws = _block_transpose8(xs, lane)
            for gl in range(8):
                halves[8 * b4 + gl][hh] = ws[gl]
    ug = [jnp.concatenate(h2, axis=1) for h2 in halves]
    hooks[0]()

    carry_in = jnp.where(first, 0.0, h_s[top + N_CHUNK:top + N_CHUNK + 1, :])
    h_s[top:top + 1, :] = carry_in
    for g in range(SSM_GROUPS):
        cols = slice(g * LANES, (g + 1) * LANES)
        s_g = jnp.dot(ug[g], minw_ref[g], preferred_element_type=F32)
        cf = [coef_ref[k, :, cols] for k in range(8)]
        h_s[SUBLANES:SUBLANES + N_CHUNK, cols] = _chunk_scan(s_g, carry_in[:, cols], cf)
        if g == SSM_GROUPS // 2:
            hooks[1]()
    hooks[2]()

    yg = []
    for g in range(SSM_GROUPS):
        hprev = h_s[top:top + N_CHUNK, g * LANES:(g + 1) * LANES].astype(BF16)
        y_g = (jnp.dot(ug[g], mint_s[g], preferred_element_type=F32)
               + jnp.dot(hprev, mout_ref[g], preferred_element_type=F32))
        yg.append(y_g.astype(BF16))
    hooks[3]()

    for b4 in range(blocks):
        for hh in range(2):
            ws = [yg[8 * b4 + gl][:, hh * LANES:(hh + 1) * LANES] for gl in range(8)]
            xs = _block_transpose8(ws, lane)
            for j in range(8):
                y_s[b4, pl.ds(8 * hh + j, N_CHUNK, stride=CHUNK), :] = xs[j].astype(F32)
    return jnp.concatenate([y_s[b4] for b4 in range(blocks)], axis=1)


def _mixer_kernel(*refs, first_layer, steps):
    if first_layer:
        x_ref = refs[0]
        refs = refs[1:]
    else:
        x_ref, yprev_ref, slabprev_ref = refs[:3]
        refs = refs[3:]
    (gmix_ref, win_ref, minw_ref, kcat_ref, mout_ref, coef_ref, dskip_ref, wglu_ref, bglu_ref, poolw_ref, pscale_ref,
     convw_ref, gbr_ref, wout_ref, gffn_ref, rw_ref, rb_ref, utri_ref,
     x1_ref, hp_ref, slab_ref, slabt_ref, cnt_ref,
     u_s, y_s, h_s, pbuf_ref, zbuf_ref, cntacc_ref, mint_s, mix_s, xres_s) = refs

    j = pl.program_id(0)
    l = lax.rem(j, steps)
    seq_start = l == 0

    def out_proj_quarter(q):
        def run():
            cols = slice(q * (D_MODEL // 4), (q + 1) * (D_MODEL // 4))
            x1_ref[:, cols] = xres_s[:, cols] + jnp.dot(mix_s[...], wout_ref[:, cols], preferred_element_type=F32)
        return run

    hooks = [out_proj_quarter(q) for q in range(4)]

    @pl.when(j == 0)
    def _():
        cntacc_ref[...] = jnp.zeros_like(cntacc_ref)
        _expand_toeplitz(kcat_ref, mint_s)

    if first_layer:
        x = x_ref[...]
    else:
        x = _moe_combine(x_ref[...], yprev_ref, slabprev_ref[...])

    hn = _rms(x, gmix_ref[...]).astype(BF16)
    proj = jnp.dot(hn, win_ref[...], preferred_element_type=F32)
    u = proj[:, :SSM_WIDTH]
    p = proj[:, SSM_WIDTH:SSM_WIDTH + POOL_WIDTH]
    c0 = SSM_WIDTH + POOL_WIDTH
    bg = proj[:, c0:c0 + CONV_WIDTH]
    cg = proj[:, c0 + CONV_WIDTH:c0 + 2 * CONV_WIDTH]
    hv = proj[:, c0 + 2 * CONV_WIDTH:]

    y = _s5_mixer(u, seq_start, minw_ref, mint_s, mout_ref, coef_ref, u_s, y_s, h_s, hooks) + dskip_ref[...] * u
    _mixer_second_half(x1_ref[...], j >= 1, gffn_ref, rw_ref, rb_ref, utri_ref,
                       hp_ref, slab_ref, slabt_ref, cnt_ref, cntacc_ref)
    z = jax.nn.gelu(y)
    glu = jnp.dot(z.astype(BF16), wglu_ref[...], preferred_element_type=F32) + bglu_ref[...]
    y_ssm = z * jax.nn.sigmoid(glu)

    tail = POOL_TAIL
    pbuf_ref[0, 0:tail, :] = jnp.where(seq_start, 0.0, pbuf_ref[0, TILE_T:TILE_T + tail, :])
    pbuf_ref[0, tail:, :] = p
    sums = []
    for stage, w in enumerate(POOL_WINDOWS):
        src = 0 if stage == 0 else 1 + (stage - 1) % 2
        dst = 1 + stage % 2
        d = w // 2
        lo = SUBLANES * (stage + 1)
        n = tail + TILE_T - lo
        e = pbuf_ref[src, pl.ds(lo, n), :] + pbuf_ref[src, pl.ds(lo - d, n), :]
        if stage + 1 < len(POOL_WINDOWS):
            pbuf_ref[dst, pl.ds(lo, n), :] = e
        sums.append(e[tail - lo:, :])
    lane_p = lax.broadcasted_iota(I32, (TILE_T, POOL_WIDTH), 1)
    total = sums[-1]
    win = jnp.full((TILE_T, POOL_WIDTH), POOL_WINDOWS[-1], I32)
    for i in range(len(POOL_WINDOWS) - 2, -1, -1):
        sel = lane_p < (i + 1) * POOL_GROUP
        total = jnp.where(sel, sums[i], total)
        win = jnp.where(sel, POOL_WINDOWS[i], win)
    row_p = lax.broadcasted_iota(I32, (TILE_T, POOL_WIDTH), 0) + l * TILE_T
    count = jnp.minimum(row_p + 1, win).astype(F32)
    pooled = total / count - p
    mixed = jnp.dot(pooled.astype(BF16), poolw_ref[...], preferred_element_type=F32)
    y_pool = mixed * pscale_ref[...]

    zc = cg * hv
    zbuf_ref[0:SUBLANES, :] = jnp.where(seq_start, 0.0, zbuf_ref[TILE_T:TILE_T + SUBLANES, :])
    zbuf_ref[SUBLANES:, :] = zc
    yc = (convw_ref[0:1, :] * zbuf_ref[pl.ds(SUBLANES - 2, TILE_T), :]
          + convw_ref[1:2, :] * zbuf_ref[pl.ds(SUBLANES - 1, TILE_T), :]
          + convw_ref[2:3, :] * zc)
    y_conv = bg * yc

    gbr = gbr_ref[...]
    s1 = SSM_WIDTH + POOL_WIDTH
    mixed_all = jnp.concatenate([_rms(y_ssm, gbr[:, :SSM_WIDTH]).astype(BF16),
                                 _rms(y_pool, gbr[:, SSM_WIDTH:s1]).astype(BF16),
                                 _rms(y_conv, gbr[:, s1:]).astype(BF16)], axis=1)
    mix_s[...] = mixed_all
    xres_s[...] = x


def _mixer_second_half(x1, live, gffn_ref, rw_ref, rb_ref, utri_ref, hp_ref, slab_ref, slabt_ref, cnt_ref, cntacc_ref):
    hn2 = _rms(x1, gffn_ref[...])
    packed = _pack_rows(hn2)
    hp_ref[0] = packed[:, :SC_ROW]
    hp_ref[1] = packed[:, SC_ROW:]
    logits = jnp.dot(hn2.astype(BF16), rw_ref[...], preferred_element_type=F32) + rb_ref[...]
    work = logits.T[:N_EXPERTS, :]
    eio = lax.broadcasted_iota(I32, (N_EXPERTS, TILE_T), 0).astype(F32)
    neg = jnp.float32(-jnp.inf)
    vals, idxs = [], []
    for _ in range(TOP_K):
        m = jnp.max(work, axis=0, keepdims=True)
        idx = jnp.min(jnp.where(work == m, eio, float(N_EXPERTS)), axis=0, keepdims=True)
        vals.append(m)
        idxs.append(idx)
        work = jnp.where(eio == idx, neg, work)
    exps = [jnp.exp(v - vals[0]) for v in vals]
    denom = exps[0] + exps[1] + exps[2] + exps[3]
    gates = [e / denom for e in exps]

    onehot = jnp.zeros((N_EXPERTS, TILE_T), F32)
    for idx in idxs:
        onehot = jnp.where(eio == idx, 1.0, onehot)
    before = jnp.dot(onehot.astype(BF16), utri_ref[...], preferred_element_type=F32) + cntacc_ref[:, 0:1]
    ranks = [jnp.sum(jnp.where(eio == idx, before, 0.0), axis=0, keepdims=True) for idx in idxs]
    row = lax.broadcasted_iota(I32, (LANES, TILE_T), 0)
    slab_t = jnp.zeros((LANES, TILE_T), F32)
    for k in range(TOP_K):
        slab_t = jnp.where(row == k, idxs[k], slab_t)
        slab_t = jnp.where(row == TOP_K + k, gates[k], slab_t)
        slab_t = jnp.where(row == 2 * TOP_K + k, ranks[k], slab_t)
    slabt_ref[...] = slab_t[:2 * SUBLANES, :]
    slab_ref[...] = slab_t.T
    newcnt = jnp.where(live, cntacc_ref[:, 0:1] + jnp.sum(onehot, axis=1, keepdims=True), 0.0)
    cntacc_ref[...] = jnp.broadcast_to(newcnt, cntacc_ref.shape)
    cnt_ref[...] = jnp.broadcast_to(newcnt, cnt_ref.shape)


def _layer_spec(shape, layer):
    nd = len(shape) - 1
    sel = layer if shape[0] > 1 else 0
    return pl.BlockSpec((None,) + tuple(shape[1:]), lambda j, nd=nd, sel=sel: (sel,) + (0,) * nd,
                        pipeline_mode=pl.Buffered(1))


def _mixer_call(x, x_tile0, yprev, slabprev, lp, layer, batch, seq):
    first_layer = yprev is None
    n_tok = batch * seq
    steps = seq // TILE_T
    tiles = batch * steps
    cur = lambda j: jnp.minimum(j, tiles - 1)
    prv = lambda j: jnp.maximum(j - 1, 0)
    tok_spec = lambda w: pl.BlockSpec((TILE_T, w), lambda j: (prv(j), 0))
    in_specs = [pl.BlockSpec((TILE_T, D_MODEL), lambda j: (x_tile0 + cur(j), 0))]
    args = [x]
    if not first_layer:
        in_specs += [pl.BlockSpec((TOP_K, 2, TILE_T, SC_ROW), lambda j: (0, 0, cur(j), 0)),
                     pl.BlockSpec((TILE_T, LANES), lambda j: (cur(j), 0))]
        args += [yprev, slabprev]
    weights = [lp['gmix'], lp['win'], lp['minw'], lp['kcat'], lp['mout'], lp['coef'], lp['dskip'], lp['wglu'], lp['bglu'],
               lp['poolw'], lp['pscale'], lp['convw'], lp['gbr'], lp['wout'], lp['gffn'], lp['rw'], lp['rb'],
               lp['utri']]
    in_specs += [_layer_spec(w.shape, layer) for w in weights]
    args += weights
    out_shape = [jax.ShapeDtypeStruct((n_tok, D_MODEL), F32),
                 jax.ShapeDtypeStruct((2, n_tok, SC_ROW), I32),
                 jax.ShapeDtypeStruct((n_tok, LANES), F32),
                 jax.ShapeDtypeStruct((2 * SUBLANES, n_tok), F32),
                 jax.ShapeDtypeStruct((N_EXPERTS, LANES), F32)]
    out_specs = [tok_spec(D_MODEL), pl.BlockSpec((2, TILE_T, SC_ROW), lambda j: (0, prv(j), 0)), tok_spec(LANES),
                 pl.BlockSpec((2 * SUBLANES, TILE_T), lambda j: (0, prv(j))),
                 pl.BlockSpec((N_EXPERTS, LANES), lambda j: (0, 0))]
    scratch = [pltpu.VMEM((SSM_WIDTH // LANES, TILE_T, LANES), F32),
               pltpu.VMEM((SSM_WIDTH // LANES, TILE_T, LANES), F32),
               pltpu.VMEM((SUBLANES + N_CHUNK, SSM_GROUPS * LANES), F32),
               pltpu.VMEM((3, POOL_TAIL + TILE_T, POOL_WIDTH), F32),
               pltpu.VMEM((SUBLANES + TILE_T, CONV_WIDTH), F32),
               pltpu.VMEM((N_EXPERTS, LANES), F32),
               pltpu.VMEM((SSM_GROUPS, CHUNK * SSM_GROUP, CHUNK * SSM_GROUP), BF16),
               pltpu.VMEM((TILE_T, D_MODEL), BF16),
               pltpu.VMEM((TILE_T, D_MODEL), F32)]
    return pl.pallas_call(
        functools.partial(_mixer_kernel, first_layer=first_layer, steps=steps),
        grid=(tiles + 1,),
        in_specs=in_specs, out_specs=out_specs, out_shape=out_shape, scratch_shapes=scratch,
        compiler_params=pltpu.CompilerParams(dimension_semantics=("arbitrary",),
                                             vmem_limit_bytes=VMEM_LIMIT),
        name="mixer_first" if first_layer else "mixer_next",
    )(*args)


def _route_kernel(slabt_ref, cnt_ref, dest_ref, meta_ref, *, n_steps):
    lane1 = lax.broadcasted_iota(I32, (1, LANES), 1)
    e_row = lax.broadcasted_iota(I32, (N_EXPERTS, LANES), 0)
    e_lane = lax.broadcasted_iota(I32, (N_EXPERTS, LANES), 1)
    cnt = cnt_ref[...].astype(I32)
    counts = jnp.sum(jnp.where(e_row == e_lane, cnt, 0), axis=0, keepdims=True)
    padded = ((counts + (STEP_ROWS - 1)) // STEP_ROWS) * STEP_ROWS
    pend = padded
    sh = 1
    while sh < N_EXPERTS:
        pend = pend + jnp.where(lane1 >= sh, pltpu.roll(pend, sh, axis=1), 0)
        sh *= 2

    @pl.when(pl.program_id(0) == 0)
    def _():
        rows = meta_ref.shape[0]
        bstart = lax.broadcasted_iota(I32, (rows, LANES), 0) * STEP_ROWS
        lane = lax.broadcasted_iota(I32, (rows, LANES), 1)
        done = jnp.where(jnp.logical_and(lane < N_EXPERTS, pend <= bstart), 1, 0)
        be = jnp.minimum(jnp.sum(done, axis=1, keepdims=True), N_EXPERTS - 1)
        mine = lane == be
        first_row = jnp.sum(jnp.where(mine, pend - counts, 0), axis=1, keepdims=True)
        used = bstart[:, 0:1] + STEP_ROWS - jnp.maximum(bstart[:, 0:1], first_row)
        nblk = jnp.clip((used + (EXPERT_BLOCK - 1)) // EXPERT_BLOCK, 0, STEP_ROWS // EXPERT_BLOCK)
        total = jnp.sum(jnp.where(lane1 == N_EXPERTS - 1, pend, 0), axis=1, keepdims=True)
        row = lax.broadcasted_iota(I32, (rows, LANES), 0)
        table = jnp.where(lane == 0, be, jnp.where(lane == 1, nblk, 0))
        meta_ref[...] = jnp.where(row == n_steps, total // STEP_ROWS, table)

    last_col = jnp.sum(jnp.where(e_row == e_lane, pend - 1, 0), axis=1, keepdims=True)
    tokens = slabt_ref.shape[1]
    eio = lax.broadcasted_iota(I32, (N_EXPERTS, tokens), 0)
    row = lax.broadcasted_iota(I32, (SUBLANES, tokens), 0)
    out = jnp.zeros((SUBLANES, tokens), I32)
    for k in range(TOP_K):
        idx = slabt_ref[k:k + 1, :].astype(I32)
        rank = slabt_ref[2 * TOP_K + k:2 * TOP_K + k + 1, :].astype(I32)
        base = jnp.sum(jnp.where(eio == idx, last_col, 0), axis=0, keepdims=True)
        for h in range(2):
            out = jnp.where(row == 2 * k + h, base - rank + h * n_steps * STEP_ROWS, out)
    dest_ref[...] = out


def _route_call(slabt, counts, n_tok, n_steps):
    meta_rows = ((n_steps + 1 + SUBLANES - 1) // SUBLANES) * SUBLANES
    return pl.pallas_call(
        functools.partial(_route_kernel, n_steps=n_steps),
        grid=(n_tok // ROUTE_T,),
        in_specs=[pl.BlockSpec((2 * SUBLANES, ROUTE_T), lambda i: (0, i)),
                  pl.BlockSpec((N_EXPERTS, LANES), lambda i: (0, 0))],
        out_specs=[pl.BlockSpec((SUBLANES, ROUTE_T), lambda i: (0, i)),
                   pl.BlockSpec((meta_rows, LANES), lambda i: (0, 0))],
        out_shape=[jax.ShapeDtypeStruct((SUBLANES, n_tok), I32),
                   jax.ShapeDtypeStruct((meta_rows, LANES), I32)],
        compiler_params=pltpu.CompilerParams(dimension_semantics=("arbitrary",)),
        name="route",
    )(slabt, counts)


def _sc_mesh():
    return plsc.VectorSubcoreMesh(core_axis_name="core", subcore_axis_name="subcore")


def _dispatch(src, idx, n_rows):
    windows = src.shape[0] // SC_WINDOW
    copies = idx.shape[1] // src.shape[0]

    @functools.partial(pl.kernel, out_type=jax.ShapeDtypeStruct((n_rows, SC_ROW), src.dtype),
                       mesh=_sc_mesh(), scratch_types=[])
    def scatter_rows(x_hbm, i_hbm, o_hbm):
        def body(x_vmem, *i_vmems):
            for i_vmem in i_vmems:
                pltpu.sync_copy(x_vmem, o_hbm.at[i_vmem.at[0]])

        pltpu.emit_pipeline(
            body,
            grid=(windows,),
            in_specs=[pl.BlockSpec((SC_WINDOW, SC_ROW), lambda i: (i, 0))]
            + [pl.BlockSpec((1, SC_WINDOW), lambda i, k=k: (0, k * windows + i)) for k in range(copies)],
            out_specs=[],
            core_axis_name=("core", "subcore"),
            dimension_semantics=(pltpu.PARALLEL,),
        )(x_hbm, *([i_hbm] * copies))

    return scatter_rows(src, idx)


def _combine(src, idx):
    n_idx = idx.shape[1]

    @functools.partial(pl.kernel, out_type=jax.ShapeDtypeStruct((n_idx, SC_ROW), src.dtype),
                       mesh=_sc_mesh(), scratch_types=[])
    def gather_rows(y_hbm, i_hbm, o_hbm):
        def body(i_vmem, o_vmem):
            pltpu.sync_copy(y_hbm.at[i_vmem.at[0]], o_vmem)

        pltpu.emit_pipeline(
            body,
            grid=(n_idx // SC_WINDOW,),
            in_specs=[pl.BlockSpec((1, SC_WINDOW), lambda i: (0, i))],
            out_specs=[pl.BlockSpec((SC_WINDOW, SC_ROW), lambda i: (i, 0))],
            core_axis_name=("core", "subcore"),
            dimension_semantics=(pltpu.PARALLEL,),
        )(i_hbm, o_hbm)

    return gather_rows(src, idx)


def _expert_rows(xs_ref, ys_ref, row0, n, wgu_s, bgu_ref, wdn_s, bdn_ref, act_s):
    rows = pl.ds(pl.multiple_of(row0, EXPERT_BLOCK), n)
    arow = slice(0, n)
    w0 = xs_ref[0, rows, :]
    w1 = xs_ref[1, rows, :]
    xb = jnp.concatenate([_unpack_lo(w0).astype(BF16), _unpack_lo(w1).astype(BF16),
                          _unpack_hi(w0).astype(BF16), _unpack_hi(w1).astype(BF16)], axis=1)
    half = D_EXPERT // 2
    for j in range(2):
        cols = slice(j * half, (j + 1) * half)
        ucols = slice(D_EXPERT + j * half, D_EXPERT + (j + 1) * half)
        g = jnp.dot(xb, wgu_s[:, cols], preferred_element_type=F32) + bgu_ref[:, cols]
        up = jnp.dot(xb, wgu_s[:, ucols], preferred_element_type=F32) + bgu_ref[:, ucols]
        g = jnp.minimum(g, SWIGLU_LIMIT)
        up = jnp.clip(up, -SWIGLU_LIMIT, SWIGLU_LIMIT)
        act_s[arow, cols] = (g * jax.nn.sigmoid(SWIGLU_ALPHA * g) * (up + 1.0)).astype(BF16)
    y = jnp.dot(act_s[arow, :], wdn_s[...], preferred_element_type=F32) + bdn_ref[...]
    packed = _pack_rows(y)
    ys_ref[0, rows, :] = packed[:, :SC_ROW]
    ys_ref[1, rows, :] = packed[:, SC_ROW:]


def _expert_kernel(be_ref, nb_ref, nv_ref, xs_ref, wgu_ref, bgu_ref, wdn_ref, bdn_ref, ys_ref, wgu_s, wdn_s, act_s):
    blk = pl.program_id(0)
    prev = be_ref[jnp.maximum(blk - 1, 0)]
    changed = jnp.logical_or(blk == 0, be_ref[blk] != prev)
    valid = blk < nv_ref[0]

    @pl.when(jnp.logical_and(valid, changed))
    def _():
        chunk = 128

        def cast_gu(i, _):
            rows = pl.ds(pl.multiple_of(i * chunk, chunk), chunk)
            wgu_s[rows, :] = wgu_ref[rows, :].astype(BF16)
            return 0

        def cast_dn(i, _):
            rows = pl.ds(pl.multiple_of(i * chunk, chunk), chunk)
            wdn_s[rows, :] = wdn_ref[rows, :].astype(BF16)
            return 0

        lax.fori_loop(0, D_MODEL // chunk, cast_gu, 0)
        lax.fori_loop(0, D_EXPERT // chunk, cast_dn, 0)

    nblk = nb_ref[blk]
    first = STEP_ROWS - nblk * EXPERT_BLOCK
    big = nblk // RUN_BLOCKS

    @pl.when(jnp.logical_and(valid, big > 0))
    def _():
        def one_run(i, _):
            row0 = first + (nblk - (i + 1) * RUN_BLOCKS) * EXPERT_BLOCK
            _expert_rows(xs_ref, ys_ref, row0, RUN_BLOCKS * EXPERT_BLOCK, wgu_s, bgu_ref, wdn_s, bdn_ref, act_s)
            return 0

        lax.fori_loop(0, big, one_run, 0)

    bit = RUN_BLOCKS // 2
    while bit:
        @pl.when(jnp.logical_and(valid, (nblk & bit) != 0))
        def _(bit=bit):
            row0 = first + (nblk & (bit - 1)) * EXPERT_BLOCK
            _expert_rows(xs_ref, ys_ref, row0, bit * EXPERT_BLOCK, wgu_s, bgu_ref, wdn_s, bdn_ref, act_s)
        bit //= 2


def _expert_call(xs, step_e, step_nblk, n_valid, w_gu, b_gu, w_dn, b_dn, layer, n_steps):
    def row_map(i, be, nb, nv):
        return (0, jnp.minimum(i, jnp.maximum(nv[0] - 1, 0)), 0)

    def w_map(i, be, nb, nv):
        return (layer, be[i], 0, 0)

    grid_spec = pltpu.PrefetchScalarGridSpec(
        num_scalar_prefetch=3,
        grid=(n_steps,),
        in_specs=[pl.BlockSpec((2, STEP_ROWS, SC_ROW), row_map),
                  pl.BlockSpec((None, None, D_MODEL, 2 * D_EXPERT), w_map),
                  pl.BlockSpec((None, None, 1, 2 * D_EXPERT), w_map),
                  pl.BlockSpec((None, None, D_EXPERT, D_MODEL), w_map),
                  pl.BlockSpec((None, None, 1, D_MODEL), w_map)],
        out_specs=pl.BlockSpec((2, STEP_ROWS, SC_ROW), row_map),
        scratch_shapes=[pltpu.VMEM((D_MODEL, 2 * D_EXPERT), BF16),
                        pltpu.VMEM((D_EXPERT, D_MODEL), BF16),
                        pltpu.VMEM((RUN_BLOCKS * EXPERT_BLOCK, D_EXPERT), BF16)])
    return pl.pallas_call(
        _expert_kernel,
        grid_spec=grid_spec,
        out_shape=jax.ShapeDtypeStruct((2, n_steps * STEP_ROWS, SC_ROW), I32),
        compiler_params=pltpu.CompilerParams(dimension_semantics=("arbitrary",),
                                             vmem_limit_bytes=VMEM_LIMIT),
        name="experts",
    )(step_e, step_nblk, n_valid, xs, w_gu, b_gu, w_dn, b_dn)


def _final_kernel(x_ref, y_ref, slab_ref, g_ref, *rest):
    o_ref = rest[-1]
    x = _moe_combine(x_ref[...], y_ref, slab_ref[...])
    o_ref[...] = _rms(x, g_ref[...])


def _final_call(x1, ysg, slab, g, out_row0, n_out, out_prev):
    t = FINAL_T
    out_tile0 = out_row0 // t
    in_specs = [pl.BlockSpec((t, D_MODEL), lambda i: (i, 0)),
                pl.BlockSpec((TOP_K, 2, t, SC_ROW), lambda i: (0, 0, i, 0)),
                pl.BlockSpec((t, LANES), lambda i: (i, 0)),
                pl.BlockSpec((1, D_MODEL), lambda i: (0, 0))]
    args = [x1, ysg, slab, g]
    aliases = {}
    if out_prev is not None:
        in_specs.append(pl.BlockSpec(memory_space=pl.ANY))
        args.append(out_prev)
        aliases = {4: 0}
    return pl.pallas_call(
        _final_kernel,
        grid=(x1.shape[0] // t,),
        in_specs=in_specs,
        out_specs=pl.BlockSpec((t, D_MODEL), lambda i: (out_tile0 + i, 0)),
        out_shape=jax.ShapeDtypeStruct((n_out, D_MODEL), F32),
        input_output_aliases=aliases,
        compiler_params=pltpu.CompilerParams(dimension_semantics=("arbitrary",), vmem_limit_bytes=VMEM_LIMIT),
        name="final_norm",
    )(*args)


def _s5_tables(lam_re, lam_im, b_re, b_im, c_re, c_im, log_step):
    lr = jnp.minimum(lam_re.astype(F32), LAMBDA_RE_MAX)
    li = lam_im.astype(F32)
    step = jnp.exp(log_step.astype(F32))[:, None]
    mag = jnp.exp(lr * step)
    ar = mag * jnp.cos(li * step)
    ai = mag * jnp.sin(li * step)
    nr = ar - 1.0
    den = lr * lr + li * li
    kr = (nr * lr + ai * li) / den
    ki = (ai * lr - nr * li) / den
    bre = b_re.astype(F32)
    bim = b_im.astype(F32)
    bbar_r = kr[..., None] * bre - ki[..., None] * bim
    bbar_i = kr[..., None] * bim + ki[..., None] * bre
    cre = c_re.astype(F32)
    cim = c_im.astype(F32)

    def power(m):
        m = jnp.asarray(m, F32)[..., None, None]
        mg = jnp.exp(m * lr * step)
        return mg * jnp.cos(m * li * step), mg * jnp.sin(m * li * step)

    pw_r, pw_i = power(jnp.arange(CHUNK + 1))
    cp_r = cre[None] * pw_r[:, :, None, :] - cim[None] * pw_i[:, :, None, :]
    cp_i = cre[None] * pw_i[:, :, None, :] + cim[None] * pw_r[:, :, None, :]
    width = CHUNK * SSM_GROUP
    kcat = (jnp.einsum('tgap,gph->ghta', cp_r[:CHUNK], bbar_r)
            - jnp.einsum('tgap,gph->ghta', cp_i[:CHUNK], bbar_i)).reshape(SSM_GROUPS, SSM_GROUP, width).astype(BF16)
    mo_r = cp_r[1:].transpose(1, 3, 0, 2).reshape(SSM_GROUPS, SSM_STATE, width)
    mo_i = -cp_i[1:].transpose(1, 3, 0, 2).reshape(SSM_GROUPS, SSM_STATE, width)
    mout = jnp.concatenate([mo_r, mo_i], axis=1).astype(BF16)
    q_r = pw_r[CHUNK - 1::-1][:CHUNK][:, :, :, None]
    q_i = pw_i[CHUNK - 1::-1][:CHUNK][:, :, :, None]
    mn_r = q_r * bbar_r[None] - q_i * bbar_i[None]
    mn_i = q_r * bbar_i[None] + q_i * bbar_r[None]
    minw = jnp.concatenate([mn_r, mn_i], axis=2).transpose(1, 0, 3, 2).reshape(
        SSM_GROUPS, CHUNK * SSM_GROUP, 2 * SSM_STATE).astype(BF16)

    def packed(j):
        pr, pi = power(float(CHUNK * j))
        return (jnp.concatenate([pr, pr], axis=-1).reshape(1, SSM_GROUPS * LANES),
                jnp.concatenate([-pi, pi], axis=-1).reshape(1, SSM_GROUPS * LANES))

    rows = jnp.arange(SUBLANES)[:, None]
    kinds = []
    for d in (1, 2, 4):
        c1, c2 = packed(d)
        mask = (rows >= d).astype(F32)
        kinds += [mask * c1, mask * c2]
    pk = [packed(r + 1) for r in range(SUBLANES)]
    kinds += [jnp.concatenate([p[0] for p in pk], axis=0), jnp.concatenate([p[1] for p in pk], axis=0)]
    coef = jnp.stack(kinds, axis=0)
    return minw, kcat, mout, coef


def _mixer_params(norm_mix, w_in, ssm_lam_re, ssm_lam_im, ssm_b_re, ssm_b_im, ssm_c_re, ssm_c_im, ssm_d,
                  ssm_log_step, ssm_w_glu, ssm_b_glu, pool_w, pool_scale, conv_w, branch_norm, w_out, norm_ffn,
                  router_w, router_b):
    depth = w_in.shape[0]
    minw, kcat, mout, coef = jax.vmap(_s5_tables)(ssm_lam_re, ssm_lam_im, ssm_b_re, ssm_b_im, ssm_c_re, ssm_c_im,
                                                  ssm_log_step)
    eye4 = jnp.eye(len(POOL_WINDOWS), dtype=F32)
    poolw = jnp.einsum('lgcd,gk->lgckd', pool_w.astype(F32), eye4).reshape(depth, POOL_WIDTH, POOL_WIDTH).astype(BF16)
    rw = jnp.pad(router_w.astype(F32), ((0, 0), (0, 0), (0, LANES - N_EXPERTS))).astype(BF16)
    rb = jnp.pad(router_b.astype(F32), ((0, 0), (0, LANES - N_EXPERTS))).reshape(depth, 1, LANES)
    row = lambda v: v.reshape(depth, 1, -1).astype(F32)
    utri = jnp.triu(jnp.ones((TILE_T, TILE_T), F32), 1).astype(BF16)
    return dict(gmix=row(norm_mix), win=w_in.astype(BF16), minw=minw, kcat=kcat, mout=mout, coef=coef, dskip=row(ssm_d),
                wglu=ssm_w_glu.astype(BF16), bglu=row(ssm_b_glu), poolw=poolw, pscale=row(pool_scale),
                convw=conv_w.astype(F32), gbr=row(branch_norm), wout=w_out.astype(BF16),
                gffn=row(norm_ffn), rw=rw, rb=rb, utri=utri[None])


def kernel(x, norm_mix, w_in, ssm_lam_re, ssm_lam_im, ssm_b_re, ssm_b_im, ssm_c_re, ssm_c_im, ssm_d, ssm_log_step, ssm_w_glu, ssm_b_glu, pool_w, pool_scale, conv_w, branch_norm, w_out, norm_ffn, router_w, router_b, w_gate_up, b_gate_up, w_down, b_down, final_norm):
    batch, seq, d_model = x.shape
    assert d_model == D_MODEL and seq % TILE_T == 0
    depth = w_in.shape[0]
    n_streams = 2 if batch % 2 == 0 else 1
    sb = batch // n_streams
    n_tok = sb * seq
    assert n_tok % ROUTE_T == 0 and n_tok % SC_WINDOW == 0
    n_steps = -(-(n_tok * TOP_K + N_EXPERTS * (STEP_ROWS - 1)) // STEP_ROWS)
    n_rows = n_steps * STEP_ROWS
    tiles = n_tok // TILE_T
    b_gu = b_gate_up.reshape(depth, N_EXPERTS, 1, 2 * D_EXPERT)
    b_dn = b_down.reshape(depth, N_EXPERTS, 1, D_MODEL)

    xin = [x.reshape(batch * seq, D_MODEL)] * n_streams
    tile0 = [s * tiles for s in range(n_streams)]
    ysg = [None] * n_streams
    slab = [None] * n_streams
    lp = _mixer_params(norm_mix, w_in, ssm_lam_re, ssm_lam_im, ssm_b_re, ssm_b_im, ssm_c_re, ssm_c_im, ssm_d,
                       ssm_log_step, ssm_w_glu, ssm_b_glu, pool_w, pool_scale, conv_w, branch_norm, w_out,
                       norm_ffn, router_w, router_b)
    for l in range(depth):
        routed = []
        for s in range(n_streams):
            xin[s], hp, slab[s], slabt, counts = _mixer_call(xin[s], tile0[s], ysg[s], slab[s], lp, l, sb, seq)
            tile0[s] = 0
            dest, meta = _route_call(slabt, counts, n_tok, n_steps)
            idx = dest.reshape(1, 2 * TOP_K * n_tok)
            xs = _dispatch(hp.reshape(2 * n_tok, SC_ROW), idx, 2 * n_rows).reshape(2, n_rows, SC_ROW)
            routed.append((xs, idx, meta[:n_steps, 0], meta[:n_steps, 1], meta[n_steps:n_steps + 1, 0]))
        for s, (xs, idx, step_e, step_nblk, n_valid) in enumerate(routed):
            ys = _expert_call(xs, step_e, step_nblk, n_valid, w_gate_up, b_gu, w_down, b_dn, l, n_steps)
            ysg[s] = _combine(ys.reshape(2 * n_rows, SC_ROW), idx).reshape(TOP_K, 2, n_tok, SC_ROW)
    g = final_norm.reshape(1, D_MODEL).astype(F32)
    out = None
    for s in range(n_streams):
        out = _final_call(xin[s], ysg[s], slab[s], g, s * n_tok, batch * seq, out)
    return out.reshape(batch, seq, D_MODEL)
```
